```python
import math
import jax, jax.numpy as jnp
from jax import lax
import numpy as np

D_MODEL = 1024
BATCH = 8
SEQ = 4096
DEPTH = 2

PLE_DIM = 256
D_FF = 2816
RMS_EPS = 1e-6
N_NORMS = 8

A_GROUPS = 4
A_WIDTH = 256
A_GROUP_DIM = A_WIDTH // A_GROUPS
A_CHUNK = 128

B_GROUPS = 4
B_WIDTH = 256
B_GROUP_DIM = B_WIDTH // B_GROUPS
CONV_W = 4
LRU_C = 8.0

C_HEADS = 8
C_KV_GROUPS = 2
C_HPG = C_HEADS // C_KV_GROUPS
HEAD_DIM = 64
C_WIDTH = C_HEADS * HEAD_DIM
KV_W = C_KV_GROUPS * HEAD_DIM
CMP_LEN = 32
CMP_STRIDE = 16
CMP_HIDDEN = 128
SEL_LEN = 64
SEL_TOP = 16
WINDOW = 512
Q_BLOCK = 64
FORCE_SCORE = 1e4
NEG = -1e30

N_BUCKETS = 32
MAX_DISTANCE = 1024

D_MIX = A_WIDTH + B_WIDTH + C_WIDTH
IN_SPLITS = (A_WIDTH, A_WIDTH, B_WIDTH, B_WIDTH, C_WIDTH, KV_W, KV_W, KV_W, KV_W, KV_W, KV_W, C_HEADS, C_HEADS, C_HEADS)
N_IN = 2 * A_WIDTH + 2 * B_WIDTH + C_WIDTH + 6 * KV_W + 3 * C_HEADS

kernel_name = "hymba_style_gmlp_rglru_nsa_hybrid"


def rmsnorm(x, g):
    xf = x.astype(jnp.float32)
    y = xf * lax.rsqrt(jnp.mean(xf * xf, axis=-1, keepdims=True) + RMS_EPS)
    return (y * g.astype(jnp.float32)).astype(x.dtype)


def swiglu(x, wg, wu, wd):
    return (jax.nn.silu(x @ wg) * (x @ wu)) @ wd


def t5_bucket(dist):
    n = jnp.maximum(dist, 0)
    max_exact = N_BUCKETS // 2
    nf = jnp.maximum(n, max_exact).astype(jnp.float32)
    large = max_exact + (jnp.log(nf / max_exact) / math.log(MAX_DISTANCE / max_exact)
                         * (N_BUCKETS - max_exact)).astype(jnp.int32)
    large = jnp.minimum(large, N_BUCKETS - 1)
    return jnp.where(n < max_exact, n, large)


def spatial_gating(u, v, norm_g, w_s, b_s):
    bn, s, _ = u.shape
    nc = s // A_CHUNK
    v = rmsnorm(v, norm_g).reshape(bn, nc, A_CHUNK, A_GROUPS, A_GROUP_DIM)
    w = jnp.where(jnp.tril(jnp.ones((A_CHUNK, A_CHUNK), bool)), w_s, 0)
    mixed = jnp.einsum('gts,bcsgd->bctgd', w, v) + b_s.T[None, None, :, :, None]
    return u * mixed.reshape(bn, s, A_WIDTH)


def rg_lru_block(xb, gate, conv_w, conv_b, wa, ba, wx, bx, lam):
    bn, s, w = xb.shape
    xp = jnp.pad(xb, ((0, 0), (CONV_W - 1, 0), (0, 0)))
    xc = sum(xp[:, k:k + s] * conv_w[k] for k in range(CONV_W)) + conv_b
    xg = xc.reshape(bn, s, B_GROUPS, B_GROUP_DIM)
    r = jax.nn.sigmoid(jnp.einsum('bsgi,gij->bsgj', xg, wa).reshape(bn, s, w) + ba)
    i = jax.nn.sigmoid(jnp.einsum('bsgi,gij->bsgj', xg, wx).reshape(bn, s, w) + bx)
    log_a = -LRU_C * r.astype(jnp.float32) * jax.nn.softplus(-lam.astype(jnp.float32))
    a = jnp.exp(log_a)
    b_in = jnp.sqrt(-jnp.expm1(2.0 * log_a)) * (i * xc).astype(jnp.float32)

    def combine(left, right):
        a1, b1 = left
        a2, b2 = right
        return a1 * a2, a2 * b1 + b2

    _, hs = lax.associative_scan(combine, (a, b_in), axis=1)
    return hs.astype(xb.dtype) * jax.nn.gelu(gate)


def compress(k, pos, w1, b1, w2, b2):
    bn, s, g, d = k.shape
    n_cmp = (s - CMP_LEN) // CMP_STRIDE + 1
    idx = jnp.arange(n_cmp)[:, None] * CMP_STRIDE + jnp.arange(CMP_LEN)[None, :]
    blk = k[:, idx] + pos[None, None, :, None, :]
    blk = jnp.moveaxis(blk, 3, 2).reshape(bn, n_cmp, g, CMP_LEN * d)
    return jax.nn.gelu(blk @ w1 + b1) @ w2 + b2


def nsa(q, kc_raw, vc_raw, ks, vs, kw, vw, gc, gs, gw, rel_bias, cmp_pos, cmp_w1, cmp_b1, cmp_w2, cmp_b2):
    bn, s = q.shape[:2]
    G, R, D = C_KV_GROUPS, C_HPG, HEAD_DIM
    n_cmp = (s - CMP_LEN) // CMP_STRIDE + 1
    n_sel = s // SEL_LEN
    top = min(SEL_TOP, n_sel)
    nq = s // Q_BLOCK
    scale = HEAD_DIM ** -0.5

    kc = compress(kc_raw, cmp_pos[0], cmp_w1[0], cmp_b1[0], cmp_w2[0], cmp_b2[0])
    vc = compress(vc_raw, cmp_pos[1], cmp_w1[1], cmp_b1[1], cmp_w2[1], cmp_b2[1])
    cs = jnp.arange(n_cmp) * CMP_STRIDE
    cmp_end = cs + CMP_LEN - 1
    ss = jnp.arange(n_sel) * SEL_LEN
    overlap = jnp.clip(jnp.minimum(cs[:, None] + CMP_LEN, ss[None] + SEL_LEN)
                       - jnp.maximum(cs[:, None], ss[None]), 0, None).astype(jnp.float32) / CMP_LEN
    ks_blk = ks.reshape(bn, n_sel, SEL_LEN, G, D).transpose(0, 3, 1, 2, 4)
    vs_blk = vs.reshape(bn, n_sel, SEL_LEN, G, D).transpose(0, 3, 1, 2, 4)
    kw_pad = jnp.pad(kw, ((0, 0), (WINDOW, 0), (0, 0), (0, 0)))
    vw_pad = jnp.pad(vw, ((0, 0), (WINDOW, 0), (0, 0), (0, 0)))
    rb_heads = rel_bias.astype(jnp.float32).reshape(N_BUCKETS, G, R)
    rb_grp = rb_heads.transpose(1, 0, 2)
    bi = jnp.arange(bn)[:, None, None, None]
    gi = jnp.arange(G)[None, None, :, None]
    gi5 = jnp.arange(G)[None, None, :, None, None]
    j_sel = jnp.arange(n_sel)

    def block_fn(args):
        c, qc = args
        t = c * Q_BLOCK + jnp.arange(Q_BLOCK)
        d_c = t[:, None] - cmp_end[None]
        ok_c = d_c >= 0
        lg = (jnp.einsum('btgrd,bngd->bgrtn', qc, kc).astype(jnp.float32) * scale
              + rb_heads[t5_bucket(d_c)].transpose(2, 3, 0, 1))
        p_c = jax.nn.softmax(jnp.where(ok_c, lg, NEG), axis=-1) * ok_c
        o_c = jnp.einsum('bgrtn,bngd->btgrd', p_c.astype(vc.dtype), vc)
        imp = jnp.einsum('bgrtn,nj->btgj', p_c, overlap)
        blk_t = (t // SEL_LEN)[None, :, None, None]
        forced = (j_sel == 0) | (j_sel == blk_t) | (j_sel == blk_t - 1)
        score = jnp.where(j_sel <= blk_t, jnp.where(forced, FORCE_SCORE, imp), -1.0)
        top_v, top_i = lax.top_k(score, top)
        kg = ks_blk[bi, gi, top_i]
        vg = vs_blk[bi, gi, top_i]
        tok = top_i[..., None] * SEL_LEN + jnp.arange(SEL_LEN)
        d_s = t[None, :, None, None, None] - tok
        ok_s = (d_s >= 0) & (top_v >= 0.0)[..., None]
        bias_s = jnp.moveaxis(rb_grp[gi5, t5_bucket(d_s)], -1, 3)
        lg = jnp.einsum('btgrd,btgkld->btgrkl', qc, kg).astype(jnp.float32) * scale + bias_s
        lg = jnp.where(ok_s[:, :, :, None], lg, NEG)
        p_s = jax.nn.softmax(lg.reshape(bn, Q_BLOCK, G, R, top * SEL_LEN), axis=-1).reshape(lg.shape)
        o_s = jnp.einsum('btgrkl,btgkld->btgrd', p_s.astype(vg.dtype), vg)
        kwc = lax.dynamic_slice_in_dim(kw_pad, c * Q_BLOCK, Q_BLOCK + WINDOW, axis=1)
        vwc = lax.dynamic_slice_in_dim(vw_pad, c * Q_BLOCK, Q_BLOCK + WINDOW, axis=1)
        spos = c * Q_BLOCK - WINDOW + jnp.arange(Q_BLOCK + WINDOW)
        d_w = t[:, None] - spos[None]
        ok_w = (d_w >= 0) & (d_w < WINDOW) & (spos[None] >= 0)
        lg = (jnp.einsum('btgrd,bsgd->bgrts', qc, kwc).astype(jnp.float32) * scale
              + rb_heads[t5_bucket(d_w)].transpose(2, 3, 0, 1))
        p_w = jax.nn.softmax(jnp.where(ok_w, lg, NEG), axis=-1)
        o_w = jnp.einsum('bgrts,bsgd->btgrd', p_w.astype(vwc.dtype), vwc)
        return o_c, o_s, o_w

    q_blocks = jnp.moveaxis(q.reshape(bn, nq, Q_BLOCK, G, R, D), 1, 0)
    o_c, o_s, o_w = lax.map(block_fn, (jnp.arange(nq), q_blocks))

    def unblock(o):
        return jnp.moveaxis(o, 0, 1).reshape(bn, s, C_HEADS, D)

    out = (jax.nn.sigmoid(gc)[..., None] * unblock(o_c)
           + jax.nn.sigmoid(gs)[..., None] * unblock(o_s)
           + jax.nn.sigmoid(gw)[..., None] * unblock(o_w))
    return out.reshape(bn, s, C_WIDTH)


def setup_inputs(seed: int = 0) -> dict:
    key = jax.random.key(seed)
    ks = jax.random.split(key, 26)
    L = DEPTH

    def nrm(k, shape, scale):
        return jax.random.normal(k, shape, jnp.float32) * scale

    lam_u = jax.random.uniform(ks[17], (L, B_WIDTH), jnp.float32, 0.9, 0.999)
    lam_s = lam_u ** (1.0 / LRU_C)
    return {
        "x": nrm(ks[0], (BATCH, SEQ, D_MODEL), 1.0),
        "p": nrm(ks[1], (L, BATCH, SEQ, PLE_DIM), 1.0),
        "rel_bias": nrm(ks[2], (N_BUCKETS, C_HEADS), 0.2),
        "norm_g": 1.0 + nrm(ks[3], (L, N_NORMS, D_MODEL), 0.05),
        "ffn_w_gate": nrm(ks[4], (L, 2, D_MODEL, D_FF), D_MODEL ** -0.5),
        "ffn_w_up": nrm(ks[5], (L, 2, D_MODEL, D_FF), D_MODEL ** -0.5),
        "ffn_w_down": nrm(ks[6], (L, 2, D_FF, D_MODEL), D_FF ** -0.5),
        "w_in": nrm(ks[7], (L, D_MODEL, N_IN), D_MODEL ** -0.5),
        "w_out": nrm(ks[8], (L, D_MIX, D_MODEL), D_MIX ** -0.5),
        "sgu_norm_g": 1.0 + nrm(ks[9], (L, A_WIDTH), 0.05),
        "sgu_w": nrm(ks[10], (L, A_GROUPS, A_CHUNK, A_CHUNK), A_CHUNK ** -0.5),
        "sgu_b": 1.0 + nrm(ks[11], (L, A_GROUPS, A_CHUNK), 0.1),
        "conv_w": nrm(ks[12], (L, CONV_W, B_WIDTH), CONV_W ** -0.5),
        "conv_b": nrm(ks[13], (L, B_WIDTH), 0.01),
        "lru_wa": nrm(ks[14], (L, B_GROUPS, B_GROUP_DIM, B_GROUP_DIM), B_GROUP_DIM ** -0.5),
        "lru_ba": nrm(ks[15], (L, B_WIDTH), 0.01),
        "lru_wx": nrm(ks[16], (L, B_GROUPS, B_GROUP_DIM, B_GROUP_DIM), B_GROUP_DIM ** -0.5),
        "lru_bx": nrm(ks[18], (L, B_WIDTH), 0.01),
        "lru_lambda": jnp.log(lam_s) - jnp.log1p(-lam_s),
        "cmp_pos": nrm(ks[19], (L, 2, CMP_LEN, HEAD_DIM), 0.02),
        "cmp_w1": nrm(ks[20], (L, 2, CMP_LEN * HEAD_DIM, CMP_HIDDEN), (CMP_LEN * HEAD_DIM) ** -0.5),
        "cmp_b1": nrm(ks[21], (L, 2, CMP_HIDDEN), 0.01),
        "cmp_w2": nrm(ks[22], (L, 2, CMP_HIDDEN, HEAD_DIM), CMP_HIDDEN ** -0.5),
        "cmp_b2": nrm(ks[23], (L, 2, HEAD_DIM), 0.01),
        "ple_w_gate": nrm(ks[24], (L, D_MODEL, D_MODEL), D_MODEL ** -0.5),
        "ple_w_proj": nrm(ks[25], (L, PLE_DIM, D_MODEL), PLE_DIM ** -0.5),
    }


def reference(x, p, rel_bias, norm_g, ffn_w_gate, ffn_w_up, ffn_w_down, w_in, w_out,
              sgu_norm_g, sgu_w, sgu_b, conv_w, conv_b, lru_wa, lru_ba, lru_wx, lru_bx, lru_lambda,
              cmp_pos, cmp_w1, cmp_b1, cmp_w2, cmp_b2, ple_w_gate, ple_w_proj):
    bn, s, _ = x.shape
    split_points = np.cumsum(IN_SPLITS)[:-1].tolist()
    h = x
    for i in range(DEPTH):
        g = norm_g[i]
        f = swiglu(rmsnorm(h, g[0]), ffn_w_gate[i, 0], ffn_w_up[i, 0], ffn_w_down[i, 0])
        h = h + 0.5 * rmsnorm(f, g[1])
        z = rmsnorm(h, g[2]) @ w_in[i]
        (a_u, a_v, b_x, b_gate, c_q, c_kc, c_vc, c_ks, c_vs, c_kw, c_vw,
         c_gc, c_gs, c_gw) = jnp.split(z, split_points, axis=-1)
        y_a = spatial_gating(jax.nn.gelu(a_u), jax.nn.gelu(a_v), sgu_norm_g[i], sgu_w[i], sgu_b[i])
        y_b = rg_lru_block(b_x, b_gate, conv_w[i], conv_b[i], lru_wa[i], lru_ba[i],
                           lru_wx[i], lru_bx[i], lru_lambda[i])
        kv = [t.reshape(bn, s, C_KV_GROUPS, HEAD_DIM) for t in (c_kc, c_vc, c_ks, c_vs, c_kw, c_vw)]
        y_c = nsa(c_q.reshape(bn, s, C_KV_GROUPS, C_HPG, HEAD_DIM), kv[0], kv[1], kv[2], kv[3], kv[4], kv[5],
                  c_gc, c_gs, c_gw, rel_bias, cmp_pos[i], cmp_w1[i], cmp_b1[i], cmp_w2[i], cmp_b2[i])
        mix = jnp.concatenate([y_a, y_b, y_c], axis=-1) @ w_out[i]
        h = h + rmsnorm(mix, g[3])
        f = swiglu(rmsnorm(h, g[4]), ffn_w_gate[i, 1], ffn_w_up[i, 1], ffn_w_down[i, 1])
        h = h + 0.5 * rmsnorm(f, g[5])
        gate = jax.nn.sigmoid(rmsnorm(h, g[6]) @ ple_w_gate[i])
        h = h + rmsnorm(gate * (p[i] @ ple_w_proj[i]), g[7])
    return h
```

```python
import functools
import math

import jax
import jax.numpy as jnp
from jax import lax
from jax.experimental import pallas as pl
from jax.experimental.pallas import tpu as pltpu

F32 = jnp.float32
BF16 = jnp.bfloat16

RMS_EPS = 1e-6
A_GROUPS = 4
A_WIDTH = 256
A_CHUNK = 128
B_GROUPS = 4
B_WIDTH = 256
CONV_W = 4
LRU_C = 8.0
C_HEADS = 8
C_KV_GROUPS = 2
C_HPG = C_HEADS // C_KV_GROUPS
HEAD_DIM = 64
C_WIDTH = C_HEADS * HEAD_DIM
KV_W = C_KV_GROUPS * HEAD_DIM
CMP_LEN = 32
CMP_STRIDE = 16
SEL_LEN = 64
SEL_SHIFT = 6
SEL_TOP = 16
WINDOW = 512
FORCE_SCORE = 1e4
NEG = -1e30
N_BUCKETS = 32
MAX_DISTANCE = 1024

LANES = 128
TOKEN_TILE = 512
FFN_CHUNK = 256
SCAN_TILE = 512
SUBLANES = 8
Q_TILE = 128
SEL_KEYS = 256
WIN_KEYS = 128
QL = C_HPG * Q_TILE
TABLE_ROWS = 128
VMEM_LIMIT = 56 * 1024 * 1024


def _cparams(n_axes):
    return pltpu.CompilerParams(dimension_semantics=("arbitrary",) * n_axes,
                                vmem_limit_bytes=VMEM_LIMIT)


def _resident(shape):
    nd = len(shape)
    return pl.BlockSpec(shape, lambda *_: (0,) * nd, pipeline_mode=pl.Buffered(1))


def _rms(x, g):
    return x * lax.rsqrt(jnp.mean(x * x, axis=-1, keepdims=True) + RMS_EPS) * g


def _sigmoid(x):
    return 1.0 / (1.0 + jnp.exp(-x))


def _dot(a, b):
    return jnp.dot(a, b, preferred_element_type=F32)


def _bias_table_kernel(rbx_ref, o_ref, *, stride, offset, dmax):
    i = pl.program_id(1)
    shape = (TABLE_ROWS, QL)
    x = lax.broadcasted_iota(jnp.int32, shape, 0) + i * TABLE_ROWS
    t = lax.broadcasted_iota(jnp.int32, shape, 1) & (Q_TILE - 1)
    d = t - stride * x + offset
    n = jnp.maximum(d, 0)
    max_exact = N_BUCKETS // 2
    nf = jnp.maximum(n, max_exact).astype(F32)
    large = max_exact + (jnp.log(nf / max_exact) / math.log(MAX_DISTANCE / max_exact)
                         * (N_BUCKETS - max_exact)).astype(jnp.int32)
    large = jnp.minimum(large, N_BUCKETS - 1)
    bucket = jnp.where(n < max_exact, n, large)
    acc = jnp.zeros(shape, F32)
    for k in range(N_BUCKETS):
        acc = jnp.where(bucket == k, rbx_ref[0, k:k + 1, :], acc)
    o_ref[0] = jnp.where((d >= 0) & (d < dmax), acc, NEG)


def _bias_table(rbx, rows, stride, offset, dmax):
    rows_p = -(-rows // TABLE_ROWS) * TABLE_ROWS
    return pl.pallas_call(
        functools.partial(_bias_table_kernel, stride=stride, offset=offset, dmax=dmax),
        grid=(C_KV_GROUPS, rows_p // TABLE_ROWS),
        in_specs=[pl.BlockSpec((1, N_BUCKETS, QL), lambda g, i: (g, 0, 0))],
        out_specs=pl.BlockSpec((1, TABLE_ROWS, QL), lambda g, i: (g, i, 0)),
        out_shape=jax.ShapeDtypeStruct((C_KV_GROUPS, rows_p, QL), F32),
        compiler_params=_cparams(2),
        name="bias_table",
    )(rbx)


def _ffn_kernel(h_ref, gpre_ref, gpost_ref, wg_ref, wu_ref, wd_ref, o_ref, acc_ref):
    x = h_ref[...]
    xn = _rms(x, gpre_ref[...]).astype(BF16)
    acc_ref[...] = jnp.zeros_like(acc_ref)

    def body(j, carry):
        gate = _dot(xn, wg_ref[j])
        up = _dot(xn, wu_ref[j])
        hid = (gate * _sigmoid(gate) * up).astype(BF16)
        acc_ref[...] += _dot(hid, wd_ref[j])
        return carry

    lax.fori_loop(0, wg_ref.shape[0], body, 0)
    o_ref[...] = x + 0.5 * _rms(acc_ref[...], gpost_ref[...])


def _ffn(h, g_pre, g_post, wg, wu, wd):
    n, d = h.shape
    d_ff = wg.shape[1]
    nch = d_ff // FFN_CHUNK
    wg3 = wg.astype(BF16).reshape(d, nch, FFN_CHUNK).transpose(1, 0, 2)
    wu3 = wu.astype(BF16).reshape(d, nch, FFN_CHUNK).transpose(1, 0, 2)
    wd3 = wd.astype(BF16).reshape(nch, FFN_CHUNK, d)
    tile = pl.BlockSpec((TOKEN_TILE, d), lambda i: (i, 0))
    return pl.pallas_call(
        _ffn_kernel,
        grid=(n // TOKEN_TILE,),
        in_specs=[tile, _resident((1, d)), _resident((1, d)),
                  _resident(wg3.shape), _resident(wu3.shape), _resident(wd3.shape)],
        out_specs=tile,
        out_shape=jax.ShapeDtypeStruct((n, d), F32),
        scratch_shapes=[pltpu.VMEM((TOKEN_TILE, d), F32)],
        compiler_params=_cparams(1),
        name="ffn",
    )(h, g_pre.reshape(1, d), g_post.reshape(1, d), wg3, wu3, wd3)


def _inproj_kernel(h_ref, g_ref, wa_ref, wb_ref, wq_ref, wkv_ref, wgt_ref, sgn_ref, sgw_ref, sgb_ref,
                   ya_ref, bx_ref, qt_ref, raw_ref, ks_ref, kw_ref, vst_ref, vwt_ref, gt_ref):
    xn = _rms(h_ref[0], g_ref[...]).astype(BF16)

    za = _dot(xn, wa_ref[...])
    u = jax.nn.gelu(za[:, :A_WIDTH])
    v = _rms(jax.nn.gelu(za[:, A_WIDTH:]), sgn_ref[...]).astype(BF16)
    row = lax.broadcasted_iota(jnp.int32, (A_CHUNK, A_CHUNK), 0)
    col = lax.broadcasted_iota(jnp.int32, (A_CHUNK, A_CHUNK), 1)
    lane_group = lax.shift_right_logical(lax.broadcasted_iota(jnp.int32, (A_CHUNK, A_WIDTH), 1),
                                          (A_WIDTH // A_GROUPS).bit_length() - 1)
    w_tril = [jnp.where(row >= col, sgw_ref[g], 0.0).astype(BF16) for g in range(A_GROUPS)]
    for c in range(TOKEN_TILE // A_CHUNK):
        rows = slice(c * A_CHUNK, (c + 1) * A_CHUNK)
        mixed = jnp.zeros((A_CHUNK, A_WIDTH), F32)
        for g in range(A_GROUPS):
            mixed = jnp.where(lane_group == g, _dot(w_tril[g], v[rows]), mixed)
        ya_ref[0, rows, :] = (u[rows] * (mixed + sgb_ref[...])).astype(ya_ref.dtype)

    bx_ref[0] = _dot(xn, wb_ref[...])

    zq_t = (_dot(xn, wq_ref[...]) * (HEAD_DIM ** -0.5)).T
    for g in range(C_KV_GROUPS):
        for c in range(TOKEN_TILE // Q_TILE):
            parts = []
            for r in range(C_HPG):
                base = (g * C_HPG + r) * HEAD_DIM
                parts.append(zq_t[base:base + HEAD_DIM, c * Q_TILE:(c + 1) * Q_TILE])
            qt_ref[0, g, c] = jnp.concatenate(parts, axis=1).astype(qt_ref.dtype)

    zkv = _dot(xn, wkv_ref[...])
    vs_t = zkv[:, 3 * KV_W:4 * KV_W].T
    vw_t = zkv[:, 5 * KV_W:6 * KV_W].T
    for g in range(C_KV_GROUPS):
        lo, hi = g * HEAD_DIM, (g + 1) * HEAD_DIM
        raw_ref[0, 0, g] = zkv[:, lo:hi]
        raw_ref[1, 0, g] = zkv[:, KV_W + lo:KV_W + hi]
        ks_ref[0, g] = zkv[:, 2 * KV_W + lo:2 * KV_W + hi].astype(ks_ref.dtype)
        kw_ref[0, g] = zkv[:, 4 * KV_W + lo:4 * KV_W + hi].astype(kw_ref.dtype)
        for c in range(TOKEN_TILE // SEL_KEYS):
            vst_ref[0, g, c] = vs_t[lo:hi, c * SEL_KEYS:(c + 1) * SEL_KEYS].astype(vst_ref.dtype)
        for c in range(TOKEN_TILE // WIN_KEYS):
            vwt_ref[0, g, c] = vw_t[lo:hi, c * WIN_KEYS:(c + 1) * WIN_KEYS].astype(vwt_ref.dtype)

    sg_t = _sigmoid(_dot(xn, wgt_ref[...])).T
    for br in range(3):
        for g in range(C_KV_GROUPS):
            base = br * C_HEADS + g * C_HPG
            gt_ref[0, br, g] = sg_t[base:base + C_HPG, :]


def _inproj(h, g_norm, w_in, sgu_norm_g, sgu_w, sgu_b):
    b, s, d = h.shape
    wb16 = w_in.astype(BF16)
    o = 0
    wa = wb16[:, o:o + 2 * A_WIDTH]; o += 2 * A_WIDTH
    wb = wb16[:, o:o + 2 * B_WIDTH]; o += 2 * B_WIDTH
    wq = wb16[:, o:o + C_WIDTH]; o += C_WIDTH
    wkv = wb16[:, o:o + 6 * KV_W]; o += 6 * KV_W
    wgt = jnp.pad(wb16[:, o:o + 3 * C_HEADS], ((0, 0), (0, LANES - 3 * C_HEADS)))
    sgb = jnp.repeat(sgu_b.T, A_WIDTH // A_GROUPS, axis=1)
    nt = s // TOKEN_TILE
    grid = (b, nt)
    G = C_KV_GROUPS
    out_shape = [
        jax.ShapeDtypeStruct((b, s, A_WIDTH), BF16),
        jax.ShapeDtypeStruct((b, s, 2 * B_WIDTH), F32),
        jax.ShapeDtypeStruct((b, G, s // Q_TILE, HEAD_DIM, QL), BF16),
        jax.ShapeDtypeStruct((2, b, G, s, HEAD_DIM), F32),
        jax.ShapeDtypeStruct((b, G, s, HEAD_DIM), BF16),
        jax.ShapeDtypeStruct((b, G, s, HEAD_DIM), BF16),
        jax.ShapeDtypeStruct((b, G, s // SEL_KEYS, HEAD_DIM, SEL_KEYS), BF16),
        jax.ShapeDtypeStruct((b, G, s // WIN_KEYS, HEAD_DIM, WIN_KEYS), BF16),
        jax.ShapeDtypeStruct((b, 3, G, C_HPG, s), F32),
    ]
    out_specs = [
        pl.BlockSpec((1, TOKEN_TILE, A_WIDTH), lambda bi, i: (bi, i, 0)),
        pl.BlockSpec((1, TOKEN_TILE, 2 * B_WIDTH), lambda bi, i: (bi, i, 0)),
        pl.BlockSpec((1, G, TOKEN_TILE // Q_TILE, HEAD_DIM, QL), lambda bi, i: (bi, 0, i, 0, 0)),
        pl.BlockSpec((2, 1, G, TOKEN_TILE, HEAD_DIM), lambda bi, i: (0, bi, 0, i, 0)),
        pl.BlockSpec((1, G, TOKEN_TILE, HEAD_DIM), lambda bi, i: (bi, 0, i, 0)),
        pl.BlockSpec((1, G, TOKEN_TILE, HEAD_DIM), lambda bi, i: (bi, 0, i, 0)),
        pl.BlockSpec((1, G, TOKEN_TILE // SEL_KEYS, HEAD_DIM, SEL_KEYS), lambda bi, i: (bi, 0, i, 0, 0)),
        pl.BlockSpec((1, G, TOKEN_TILE // WIN_KEYS, HEAD_DIM, WIN_KEYS), lambda bi, i: (bi, 0, i, 0, 0)),
        pl.BlockSpec((1, 3, G, C_HPG, TOKEN_TILE), lambda bi, i: (bi, 0, 0, 0, i)),
    ]
    in_specs = [
        pl.BlockSpec((1, TOKEN_TILE, d), lambda bi, i: (bi, i, 0)),
        _resident((1, d)), _resident(wa.shape), _resident(wb.shape), _resident(wq.shape),
        _resident(wkv.shape), _resident(wgt.shape), _resident((1, A_WIDTH)),
        _resident(sgu_w.shape), _resident(sgb.shape),
    ]
    return pl.pallas_call(
        _inproj_kernel, grid=grid, in_specs=in_specs, out_specs=out_specs, out_shape=out_shape,
        compiler_params=_cparams(2), name="inproj",
    )(h, g_norm.reshape(1, d), wa, wb, wq, wkv, wgt, sgu_norm_g.reshape(1, A_WIDTH), sgu_w, sgb)


def _rglru_kernel(bx_ref, cw_ref, cb_ref, wa_ref, ba_ref, wx_ref, bxb_ref, lam_ref, o_ref,
                  tail_ref, h_ref, a_ref, b_ref):
    @pl.when(pl.program_id(1) == 0)
    def _():
        tail_ref[...] = jnp.zeros_like(tail_ref)
        h_ref[...] = jnp.zeros_like(h_ref)

    xb = bx_ref[0, :, :B_WIDTH]
    gate = bx_ref[0, :, B_WIDTH:]
    ext = jnp.concatenate([tail_ref[...], xb], axis=0)
    xc = cb_ref[...] + xb * cw_ref[CONV_W - 1:CONV_W, :]
    for k in range(CONV_W - 1):
        shift = CONV_W - 1 - k
        xc = xc + ext[SUBLANES - shift:SUBLANES - shift + SCAN_TILE] * cw_ref[k:k + 1, :]
    tail_ref[...] = xb[SCAN_TILE - SUBLANES:]

    xcb = xc.astype(BF16)
    r = _sigmoid(_dot(xcb, wa_ref[...]) + ba_ref[...])
    i = _sigmoid(_dot(xcb, wx_ref[...]) + bxb_ref[...])
    z = -lam_ref[...]
    e = jnp.exp(-jnp.abs(z))
    softplus = jnp.maximum(z, 0.0) + jnp.log1p(e)
    log_a = -LRU_C * r * softplus
    a = jnp.exp(log_a)
    b = jnp.sqrt(jnp.tanh(-log_a) * (a * a + 1.0)) * (i * xc)

    row = lax.broadcasted_iota(jnp.int32, a.shape, 0) & (SUBLANES - 1)
    for dist in (1, 2, 4):
        a_prev = jnp.where(row >= dist, pltpu.roll(a, dist, 0), 1.0)
        b_prev = jnp.where(row >= dist, pltpu.roll(b, dist, 0), 0.0)
        b = a * b_prev + b
        a = a * a_prev
    a_ref[...] = a
    b_ref[...] = b

    def body(k, h):
        off = pl.multiple_of(k * SUBLANES, SUBLANES)
        rows = pl.ds(off, SUBLANES)
        hs = b_ref[rows, :] + a_ref[rows, :] * h
        b_ref[rows, :] = hs
        return jnp.broadcast_to(hs[SUBLANES - 1:SUBLANES, :], hs.shape)

    h_ref[...] = lax.fori_loop(0, SCAN_TILE // SUBLANES, body, h_ref[...])
    o_ref[0] = (b_ref[...] * jax.nn.gelu(gate)).astype(o_ref.dtype)


def _block_diag(w):
    g, n, _ = w.shape
    out = jnp.zeros((g * n, g * n), w.dtype)
    for k in range(g):
        out = out.at[k * n:(k + 1) * n, k * n:(k + 1) * n].set(w[k])
    return out


def _rglru(bx, conv_w, conv_b, wa, ba, wx, bxb, lam):
    b, s, _ = bx.shape
    w = B_WIDTH
    row = lambda v: v.reshape(1, w)
    return pl.pallas_call(
        _rglru_kernel,
        grid=(b, s // SCAN_TILE),
        in_specs=[pl.BlockSpec((1, SCAN_TILE, 2 * w), lambda bi, i: (bi, i, 0)),
                  _resident((CONV_W, w)), _resident((1, w)), _resident((w, w)), _resident((1, w)),
                  _resident((w, w)), _resident((1, w)), _resident((1, w))],
        out_specs=pl.BlockSpec((1, SCAN_TILE, w), lambda bi, i: (bi, i, 0)),
        out_shape=jax.ShapeDtypeStruct((b, s, w), BF16),
        scratch_shapes=[pltpu.VMEM((SUBLANES, w), F32), pltpu.VMEM((SUBLANES, w), F32),
                        pltpu.VMEM((SCAN_TILE, w), F32), pltpu.VMEM((SCAN_TILE, w), F32)],
        compiler_params=_cparams(2),
        name="rglru",
    )(bx, conv_w, row(conv_b), _block_diag(wa).astype(BF16), row(ba),
      _block_diag(wx).astype(BF16), row(bxb), row(lam))


def _compress_kernel(raw_ref, pos_ref, w1_ref, b1_ref, w2_ref, b2_ref, w2t_ref, b2c_ref, kc_ref, kct_ref):
    g, ncp, half = raw_ref.shape[2:]
    x = raw_ref[0, 0].reshape(g * ncp, half)
    top = (x + pos_ref[0, :, :half]).astype(BF16)
    bot = (x + pos_ref[0, :, half:]).astype(BF16)
    u = _dot(top, w1_ref[0, :half, :])
    v = _dot(bot, w1_ref[0, half:, :])
    hid = jax.nn.gelu(u + pltpu.roll(v, g * ncp - 1, 0) + b1_ref[0]).astype(BF16)
    kc_ref[0, 0] = (_dot(hid, w2_ref[0]) + b2_ref[0]).reshape(g, ncp, HEAD_DIM).astype(kc_ref.dtype)
    for gi in range(g):
        t = lax.dot_general(w2t_ref[0], hid[gi * ncp:(gi + 1) * ncp], (((1,), (1,)), ((), ())),
                            preferred_element_type=F32)
        kct_ref[0, 0, gi] = (t + b2c_ref[0]).astype(kct_ref.dtype)


def _compress(raw, cmp_pos, cmp_w1, cmp_b1, cmp_w2, cmp_b2):
    _, b, g, s, hd = raw.shape
    ncp = s // CMP_STRIDE
    half = CMP_STRIDE * hd
    hid = cmp_w1.shape[-1]
    raw16 = raw.reshape(2, b, g, ncp, half)
    sel = lambda *shape: pl.BlockSpec((1,) + shape, lambda kv, bi: (kv,) + (0,) * len(shape))
    return pl.pallas_call(
        _compress_kernel,
        grid=(2, b),
        in_specs=[pl.BlockSpec((1, 1, g, ncp, half), lambda kv, bi: (kv, bi, 0, 0, 0)),
                  sel(1, 2 * half), sel(2 * half, hid), sel(1, hid), sel(hid, hd), sel(1, hd),
                  sel(hd, hid), sel(hd, 1)],
        out_specs=[pl.BlockSpec((1, 1, g, ncp, hd), lambda kv, bi: (kv, bi, 0, 0, 0)),
                   pl.BlockSpec((1, 1, g, hd, ncp), lambda kv, bi: (kv, bi, 0, 0, 0))],
        out_shape=[jax.ShapeDtypeStruct((2, b, g, ncp, hd), BF16),
                   jax.ShapeDtypeStruct((2, b, g, hd, ncp), BF16)],
        compiler_params=_cparams(2),
        name="compress",
    )(raw16, cmp_pos.reshape(2, 1, 2 * half), cmp_w1.astype(BF16), cmp_b1.reshape(2, 1, hid),
      cmp_w2.astype(BF16), cmp_b2.reshape(2, 1, hd),
      cmp_w2.astype(BF16).transpose(0, 2, 1), cmp_b2.reshape(2, hd, 1))


def _softmax_step(carry, s, v_t):
    m, l, acc = carry
    m_new = jnp.maximum(m, jnp.max(s, axis=0, keepdims=True))
    alpha = jnp.exp(m - m_new)
    p = jnp.exp(s - m_new)
    l = l * alpha + jnp.sum(p, axis=0, keepdims=True)
    acc = acc * alpha + _dot(v_t, p.astype(BF16))
    return m_new, l, acc


def _nsa_kernel(qt_ref, kc_ref, vct_ref, ks_ref, vst_ref, kw_ref, vwt_ref, gate_ref,
                gsel_ref, gwin_ref, gcmp_ref, ovt_ref, o_ref, *, n_tiles):
    c = pl.program_id(2)
    qt = qt_ref[0, 0, 0]
    ncp = kc_ref.shape[2]
    nsel = ovt_ref.shape[0]
    init = (jnp.full((1, QL), NEG, F32), jnp.zeros((1, QL), F32), jnp.zeros((HEAD_DIM, QL), F32))
    lane = lax.broadcasted_iota(jnp.int32, (1, QL), 1)
    tq = lane & (Q_TILE - 1)

    y0 = pl.multiple_of((n_tiles - 1 - c) * (Q_TILE // CMP_STRIDE), SUBLANES)
    s = _dot(kc_ref[0, 0], qt) + gcmp_ref[0, pl.ds(y0, ncp), :]
    m = jnp.max(s, axis=0, keepdims=True)
    e = jnp.exp(s - m)
    p = e / jnp.sum(e, axis=0, keepdims=True)
    p = p * (c * Q_TILE + tq >= CMP_LEN - 1).astype(F32)
    o_cmp = _dot(vct_ref[0, 0], p.astype(BF16))

    p_heads = p[:, 0:Q_TILE]
    for r in range(1, C_HPG):
        p_heads = p_heads + p[:, r * Q_TILE:(r + 1) * Q_TILE]
    p_hi = p_heads.astype(BF16)
    p_lo = (p_heads - p_hi.astype(F32)).astype(BF16)
    imp = _dot(ovt_ref[...], p_hi) + _dot(ovt_ref[...], p_lo)
    j = lax.broadcasted_iota(jnp.int32, (nsel, Q_TILE), 0)
    blk = c * (Q_TILE // SEL_LEN) + lax.shift_right_logical(
        lax.broadcasted_iota(jnp.int32, (nsel, Q_TILE), 1), SEL_SHIFT)
    forced = (j == 0) | (j == blk) | (j == blk - 1)
    score = jnp.where(j <= blk, jnp.where(forced, FORCE_SCORE, imp), -1.0)
    rank = jnp.zeros((nsel, Q_TILE), jnp.int32)
    for jp in range(nsel):
        other = score[jp:jp + 1, :]
        ge = jnp.where(other >= score, 1, 0)
        gt = jnp.where(other > score, 1, 0)
        rank = rank + jnp.where(j > jp, ge, gt)
    sel_t = jnp.where((rank < SEL_TOP) & (score >= 0.0), 1.0, 0.0).astype(BF16)

    x0 = (n_tiles - 1 - c) * Q_TILE

    def sel_body(i, carry):
        off = pl.multiple_of(i * SEL_KEYS, SEL_KEYS)
        s = _dot(ks_ref[0, 0, pl.ds(off, SEL_KEYS), :], qt)
        s = s + gsel_ref[0, pl.ds(pl.multiple_of(x0 + off, Q_TILE), SEL_KEYS), :]
        key_blk = lax.shift_right_logical(lax.broadcasted_iota(jnp.int32, (SEL_KEYS, nsel), 0) + off, SEL_SHIFT)
        expand = jnp.where(key_blk == lax.broadcasted_iota(jnp.int32, (SEL_KEYS, nsel), 1), 1.0, 0.0)
        mask = _dot(expand.astype(BF16), sel_t)
        s = s + jnp.concatenate([(mask - 1.0) * (-NEG)] * C_HPG, axis=1)
        return _softmax_step(carry, s, vst_ref[0, 0, i])

    n_sel_steps = (c * Q_TILE + Q_TILE + SEL_KEYS - 1) // SEL_KEYS
    _, l_sel, acc_sel = lax.fori_loop(0, n_sel_steps, sel_body, init)
    o_sel = acc_sel / l_sel

    back = WINDOW // WIN_KEYS

    def win_body(i, carry):
        off = pl.multiple_of(i * WIN_KEYS, WIN_KEYS)
        s = _dot(kw_ref[0, 0, pl.ds(off, WIN_KEYS), :], qt)
        rel = pl.multiple_of((i - c + back) * WIN_KEYS, WIN_KEYS)
        s = s + gwin_ref[0, pl.ds(rel, WIN_KEYS), :]
        return _softmax_step(carry, s, vwt_ref[0, 0, i])

    _, l_win, acc_win = lax.fori_loop(jnp.maximum(c - back, 0), c + 1, win_body, init)
    o_win = acc_win / l_win

    cols = []
    for r in range(C_HPG):
        ln = slice(r * Q_TILE, (r + 1) * Q_TILE)
        cols.append(gate_ref[0, 0, 0, r:r + 1, :] * o_cmp[:, ln]
                    + gate_ref[0, 1, 0, r:r + 1, :] * o_sel[:, ln]
                    + gate_ref[0, 2, 0, r:r + 1, :] * o_win[:, ln])
    pairs = [jnp.concatenate(cols[2 * k:2 * k + 2], axis=0).T for k in range(C_HPG // 2)]
    o_ref[0] = jnp.concatenate(pairs, axis=1).astype(o_ref.dtype)


def _nsa(qt, kc, vct, ks, vst, kw, vwt, gates, gsel, gwin, gcmp, ovt):
    b, G, n_tiles = qt.shape[:3]
    s = n_tiles * Q_TILE
    per_bg = lambda *shape: pl.BlockSpec((1, 1) + shape, lambda g, bi, c: (bi, g) + (0,) * len(shape))
    table = lambda a: pl.BlockSpec((1,) + a.shape[1:], lambda g, bi, c: (g, 0, 0),
                                   pipeline_mode=pl.Buffered(1))
    return pl.pallas_call(
        functools.partial(_nsa_kernel, n_tiles=n_tiles),
        grid=(G, b, n_tiles),
        in_specs=[pl.BlockSpec((1, 1, 1, HEAD_DIM, QL), lambda g, bi, c: (bi, g, c, 0, 0)),
                  per_bg(*kc.shape[2:]), per_bg(*vct.shape[2:]),
                  per_bg(*ks.shape[2:]), per_bg(*vst.shape[2:]),
                  per_bg(*kw.shape[2:]), per_bg(*vwt.shape[2:]),
                  pl.BlockSpec((1, 3, 1, C_HPG, Q_TILE), lambda g, bi, c: (bi, 0, g, 0, c)),
                  table(gsel), table(gwin), table(gcmp), _resident(ovt.shape)],
        out_specs=pl.BlockSpec((1, Q_TILE, C_HPG * HEAD_DIM), lambda g, bi, c: (bi, c, g)),
        out_shape=jax.ShapeDtypeStruct((b, s, C_WIDTH), BF16),
        compiler_params=_cparams(3),
        name="nsa",
    )(qt, kc, vct, ks, vst, kw, vwt, gates, gsel, gwin, gcmp, ovt)


def _outproj_kernel(h_ref, ya_ref, yb_ref, yc_ref, wa_ref, wb_ref, wc_ref, g_ref, o_ref):
    mix = _dot(ya_ref[...], wa_ref[...]) + _dot(yb_ref[...], wb_ref[...]) + _dot(yc_ref[...], wc_ref[...])
    o_ref[...] = h_ref[...] + _rms(mix, g_ref[...])


def _outproj(h, ya, yb, yc, w_out, g_norm):
    n, d = h.shape
    w = w_out.astype(BF16)
    wa, wb, wc = w[:A_WIDTH], w[A_WIDTH:A_WIDTH + B_WIDTH], w[A_WIDTH + B_WIDTH:]
    tile = lambda width: pl.BlockSpec((TOKEN_TILE, width), lambda i: (i, 0))
    return pl.pallas_call(
        _outproj_kernel,
        grid=(n // TOKEN_TILE,),
        in_specs=[tile(d), tile(A_WIDTH), tile(B_WIDTH), tile(C_WIDTH),
                  _resident(wa.shape), _resident(wb.shape), _resident(wc.shape), _resident((1, d))],
        out_specs=tile(d),
        out_shape=jax.ShapeDtypeStruct((n, d), F32),
        compiler_params=_cparams(1),
        name="outproj",
    )(h, ya, yb, yc, wa, wb, wc, g_norm.reshape(1, d))


def _ple_kernel(h_ref, p_ref, gpre_ref, gpost_ref, wg_ref, wp_ref, o_ref):
    x = h_ref[...]
    gate = _sigmoid(_dot(_rms(x, gpre_ref[...]).astype(BF16), wg_ref[...]))
    emb = _dot(p_ref[...].astype(BF16), wp_ref[...])
    o_ref[...] = x + _rms(gate * emb, gpost_ref[...])


def _ple(h, p, g_pre, g_post, w_gate, w_proj):
    n, d = h.shape
    dp = p.shape[-1]
    tile = lambda width: pl.BlockSpec((TOKEN_TILE, width), lambda i: (i, 0))
    return pl.pallas_call(
        _ple_kernel,
        grid=(n // TOKEN_TILE,),
        in_specs=[tile(d), tile(dp), _resident((1, d)), _resident((1, d)),
                  _resident((d, d)), _resident((dp, d))],
        out_specs=tile(d),
        out_shape=jax.ShapeDtypeStruct((n, d), F32),
        compiler_params=_cparams(1),
        name="ple",
    )(h, p, g_pre.reshape(1, d), g_post.reshape(1, d), w_gate.astype(BF16), w_proj.astype(BF16))


def _overlap_t(s):
    ncp = s // CMP_STRIDE
    n_cmp = (s - CMP_LEN) // CMP_STRIDE + 1
    cs = jnp.arange(ncp) * CMP_STRIDE
    ss = jnp.arange(s // SEL_LEN) * SEL_LEN
    ov = jnp.clip(jnp.minimum(cs[None] + CMP_LEN, ss[:, None] + SEL_LEN)
                  - jnp.maximum(cs[None], ss[:, None]), 0, None).astype(F32) / CMP_LEN
    return jnp.where(jnp.arange(ncp)[None] < n_cmp, ov, 0.0).astype(BF16)


def kernel(x, p, rel_bias, norm_g, ffn_w_gate, ffn_w_up, ffn_w_down, w_in, w_out, sgu_norm_g, sgu_w, sgu_b,
           conv_w, conv_b, lru_wa, lru_ba, lru_wx, lru_bx, lru_lambda, cmp_pos, cmp_w1, cmp_b1, cmp_w2,
           cmp_b2, ple_w_gate, ple_w_proj):
    b, s, d = x.shape
    depth = norm_g.shape[0]
    assert s % TOKEN_TILE == 0 and s % SCAN_TILE == 0 and s >= WINDOW and s // SEL_LEN >= SEL_TOP
    n_tiles = s // Q_TILE

    rbx = jnp.repeat(rel_bias.reshape(N_BUCKETS, C_KV_GROUPS, C_HPG).transpose(1, 0, 2), Q_TILE, axis=2)
    no_limit = 1 << 30
    gsel = _bias_table(rbx, s + SEL_KEYS - Q_TILE, 1, s - Q_TILE, no_limit)
    gwin = _bias_table(rbx, WINDOW + Q_TILE, 1, WINDOW, WINDOW)
    per_tile = Q_TILE // CMP_STRIDE
    gcmp = _bias_table(rbx, per_tile * (n_tiles - 1) + s // CMP_STRIDE, CMP_STRIDE,
                       CMP_STRIDE * per_tile * (n_tiles - 1) - (CMP_LEN - 1), no_limit)
    ovt = _overlap_t(s)

    h = x.reshape(b * s, d)
    for i in range(depth):
        g = norm_g[i]
        h = _ffn(h, g[0], g[1], ffn_w_gate[i, 0], ffn_w_up[i, 0], ffn_w_down[i, 0])
        ya, bx, qt, raw, ks, kw, vst, vwt, gates = _inproj(
            h.reshape(b, s, d), g[2], w_in[i], sgu_norm_g[i], sgu_w[i], sgu_b[i])
        yb = _rglru(bx, conv_w[i], conv_b[i], lru_wa[i], lru_ba[i], lru_wx[i], lru_bx[i], lru_lambda[i])
        kc, kct = _compress(raw, cmp_pos[i], cmp_w1[i], cmp_b1[i], cmp_w2[i], cmp_b2[i])
        yc = _nsa(qt, kc[0], kct[1], ks, vst, kw, vwt, gates, gsel, gwin, gcmp, ovt)
        h = _outproj(h, ya.reshape(b * s, -1), yb.reshape(b * s, -1), yc.reshape(b * s, -1), w_out[i], g[3])
        h = _ffn(h, g[4], g[5], ffn_w_gate[i, 1], ffn_w_up[i, 1], ffn_w_down[i, 1])
        h = _ple(h, p[i].reshape(b * s, -1), g[6], g[7], ple_w_gate[i], ple_w_proj[i])
    return h.reshape(b, s, d)
```

```python
import functools
import math

import jax
import jax.numpy as jnp
from jax import lax
from jax.experimental import pallas as pl
from jax.experimental.pallas import tpu as pltpu

F32 = jnp.float32
BF16 = jnp.bfloat16

RMS_EPS = 1e-6
A_GROUPS = 4
A_WIDTH = 256
A_CHUNK = 128
B_GROUPS = 4
B_WIDTH = 256
CONV_W = 4
LRU_C = 8.0
C_HEADS = 8
C_KV_GROUPS = 2
C_HPG = C_HEADS // C_KV_GROUPS
HEAD_DIM = 64
C_WIDTH = C_HEADS * HEAD_DIM
KV_W = C_KV_GROUPS * HEAD_DIM
CMP_LEN = 32
CMP_STRIDE = 16
SEL_LEN = 64
SEL_SHIFT = 6
SEL_SLOTS = 64
SEL_TOP = 16
WINDOW = 512
FORCE_SCORE = 1e4
NEG = -1e30
N_BUCKETS = 32
MAX_DISTANCE = 1024

LANES = 128
TOKEN_TILE = 512
FFN_CHUNK = 256
SCAN_TILE = 512
SUBLANES = 8
Q_TILE = 128
SEL_KEYS = 512
WIN_KEYS = 128
QL = C_HPG * Q_TILE
TABLE_ROWS = 128
VMEM_LIMIT = 56 * 1024 * 1024


def _cparams(n_axes):
    return pltpu.CompilerParams(dimension_semantics=("arbitrary",) * n_axes,
                                vmem_limit_bytes=VMEM_LIMIT)


def _resident(shape):
    nd = len(shape)
    return pl.BlockSpec(shape, lambda *_: (0,) * nd, pipeline_mode=pl.Buffered(1))


def _rms(x, g):
    return x * lax.rsqrt(jnp.mean(x * x, axis=-1, keepdims=True) + RMS_EPS) * g


def _sigmoid(x):
    return 1.0 / (1.0 + jnp.exp(-x))


def _dot(a, b):
    return jnp.dot(a, b, preferred_element_type=F32)


def _bias_table_kernel(rbx_ref, o_ref, *, stride, offset, dmax):
    i = pl.program_id(1)
    shape = (TABLE_ROWS, QL)
    x = lax.broadcasted_iota(jnp.int32, shape, 0) + i * TABLE_ROWS
    t = lax.broadcasted_iota(jnp.int32, shape, 1) & (Q_TILE - 1)
    d = t - stride * x + offset
    n = jnp.maximum(d, 0)
    max_exact = N_BUCKETS // 2
    nf = jnp.maximum(n, max_exact).astype(F32)
    large = max_exact + (jnp.log(nf / max_exact) / math.log(MAX_DISTANCE / max_exact)
                         * (N_BUCKETS - max_exact)).astype(jnp.int32)
    large = jnp.minimum(large, N_BUCKETS - 1)
    bucket = jnp.where(n < max_exact, n, large)
    acc = jnp.zeros(shape, F32)
    for k in range(N_BUCKETS):
        acc = jnp.where(bucket == k, rbx_ref[0, k:k + 1, :], acc)
    o_ref[0] = jnp.where((d >= 0) & (d < dmax), acc, NEG)


def _bias_table(rbx, rows, stride, offset, dmax):
    rows_p = -(-rows // TABLE_ROWS) * TABLE_ROWS
    return pl.pallas_call(
        functools.partial(_bias_table_kernel, stride=stride, offset=offset, dmax=dmax),
        grid=(C_KV_GROUPS, rows_p // TABLE_ROWS),
        in_specs=[pl.BlockSpec((1, N_BUCKETS, QL), lambda g, i: (g, 0, 0))],
        out_specs=pl.BlockSpec((1, TABLE_ROWS, QL), lambda g, i: (g, i, 0)),
        out_shape=jax.ShapeDtypeStruct((C_KV_GROUPS, rows_p, QL), F32),
        compiler_params=_cparams(2),
        name="bias_table",
    )(rbx)


def _ffn_kernel(h_ref, gpre_ref, gpost_ref, wg_ref, wu_ref, wd_ref, o_ref, acc_ref):
    x = h_ref[...]
    xn = _rms(x, gpre_ref[...]).astype(BF16)
    acc_ref[...] = jnp.zeros_like(acc_ref)

    def body(j, carry):
        gate = _dot(xn, wg_ref[j])
        up = _dot(xn, wu_ref[j])
        hid = (gate * _sigmoid(gate) * up).astype(BF16)
        acc_ref[...] += _dot(hid, wd_ref[j])
        return carry

    lax.fori_loop(0, wg_ref.shape[0], body, 0)
    o_ref[...] = x + 0.5 * _rms(acc_ref[...], gpost_ref[...])


def _ffn(h, g_pre, g_post, wg, wu, wd):
    n, d = h.shape
    d_ff = wg.shape[1]
    nch = d_ff // FFN_CHUNK
    wg3 = wg.astype(BF16).reshape(d, nch, FFN_CHUNK).transpose(1, 0, 2)
    wu3 = wu.astype(BF16).reshape(d, nch, FFN_CHUNK).transpose(1, 0, 2)
    wd3 = wd.astype(BF16).reshape(nch, FFN_CHUNK, d)
    tile = pl.BlockSpec((TOKEN_TILE, d), lambda i: (i, 0))
    return pl.pallas_call(
        _ffn_kernel,
        grid=(n // TOKEN_TILE,),
        in_specs=[tile, _resident((1, d)), _resident((1, d)),
                  _resident(wg3.shape), _resident(wu3.shape), _resident(wd3.shape)],
        out_specs=tile,
        out_shape=jax.ShapeDtypeStruct((n, d), F32),
        scratch_shapes=[pltpu.VMEM((TOKEN_TILE, d), F32)],
        compiler_params=_cparams(1),
        name="ffn",
    )(h, g_pre.reshape(1, d), g_post.reshape(1, d), wg3, wu3, wd3)


def _inproj_kernel(h_ref, g_ref, wa_ref, wb_ref, wq_ref, wkv_ref, wgt_ref, sgn_ref, sgw_ref, sgb_ref,
                   ya_ref, bx_ref, qt_ref, raw_ref, ks_ref, kw_ref, vst_ref, vwt_ref, gt_ref):
    xn = _rms(h_ref[0], g_ref[...]).astype(BF16)

    za = _dot(xn, wa_ref[...])
    u = jax.nn.gelu(za[:, :A_WIDTH])
    v = _rms(jax.nn.gelu(za[:, A_WIDTH:]), sgn_ref[...]).astype(BF16)
    row = lax.broadcasted_iota(jnp.int32, (A_CHUNK, A_CHUNK), 0)
    col = lax.broadcasted_iota(jnp.int32, (A_CHUNK, A_CHUNK), 1)
    lane_group = lax.shift_right_logical(lax.broadcasted_iota(jnp.int32, (A_CHUNK, A_WIDTH), 1),
                                          (A_WIDTH // A_GROUPS).bit_length() - 1)
    w_tril = [jnp.where(row >= col, sgw_ref[g], 0.0).astype(BF16) for g in range(A_GROUPS)]
    for c in range(TOKEN_TILE // A_CHUNK):
        rows = slice(c * A_CHUNK, (c + 1) * A_CHUNK)
        mixed = jnp.zeros((A_CHUNK, A_WIDTH), F32)
        for g in range(A_GROUPS):
            mixed = jnp.where(lane_group == g, _dot(w_tril[g], v[rows]), mixed)
        ya_ref[0, rows, :] = (u[rows] * (mixed + sgb_ref[...])).astype(ya_ref.dtype)

    bx_ref[0] = _dot(xn, wb_ref[...])

    zq_t = (_dot(xn, wq_ref[...]) * (HEAD_DIM ** -0.5)).T
    for g in range(C_KV_GROUPS):
        for c in range(TOKEN_TILE // Q_TILE):
            parts = []
            for r in range(C_HPG):
                base = (g * C_HPG + r) * HEAD_DIM
                parts.append(zq_t[base:base + HEAD_DIM, c * Q_TILE:(c + 1) * Q_TILE])
            qt_ref[0, g, c] = jnp.concatenate(parts, axis=1).astype(qt_ref.dtype)

    zkv = _dot(xn, wkv_ref[...])
    vs_t = zkv[:, 3 * KV_W:4 * KV_W].T
    vw_t = zkv[:, 5 * KV_W:6 * KV_W].T
    key_blk = lax.shift_right_logical(
        lax.broadcasted_iota(jnp.int32, (TOKEN_TILE, SEL_SLOTS), 0) + pl.program_id(1) * TOKEN_TILE, SEL_SHIFT)
    blk_onehot = jnp.where(key_blk == lax.broadcasted_iota(jnp.int32, (TOKEN_TILE, SEL_SLOTS), 1), 1.0, 0.0)
    for g in range(C_KV_GROUPS):
        lo, hi = g * HEAD_DIM, (g + 1) * HEAD_DIM
        raw_ref[0, 0, g] = zkv[:, lo:hi]
        raw_ref[1, 0, g] = zkv[:, KV_W + lo:KV_W + hi]
        ks_ref[0, g] = jnp.concatenate([zkv[:, 2 * KV_W + lo:2 * KV_W + hi], blk_onehot],
                                       axis=1).astype(ks_ref.dtype)
        kw_ref[0, g] = zkv[:, 4 * KV_W + lo:4 * KV_W + hi].astype(kw_ref.dtype)
        for c in range(TOKEN_TILE // SEL_KEYS):
            vst_ref[0, g, c] = vs_t[lo:hi, c * SEL_KEYS:(c + 1) * SEL_KEYS].astype(vst_ref.dtype)
        for c in range(TOKEN_TILE // WIN_KEYS):
            vwt_ref[0, g, c] = vw_t[lo:hi, c * WIN_KEYS:(c + 1) * WIN_KEYS].astype(vwt_ref.dtype)

    sg_t = _sigmoid(_dot(xn, wgt_ref[...])).T
    for br in range(3):
        for g in range(C_KV_GROUPS):
            base = br * C_HEADS + g * C_HPG
            gt_ref[0, br, g] = sg_t[base:base + C_HPG, :]


def _inproj(h, g_norm, w_in, sgu_norm_g, sgu_w, sgu_b):
    b, s, d = h.shape
    wb16 = w_in.astype(BF16)
    o = 0
    wa = wb16[:, o:o + 2 * A_WIDTH]; o += 2 * A_WIDTH
    wb = wb16[:, o:o + 2 * B_WIDTH]; o += 2 * B_WIDTH
    wq = wb16[:, o:o + C_WIDTH]; o += C_WIDTH
    wkv = wb16[:, o:o + 6 * KV_W]; o += 6 * KV_W
    wgt = jnp.pad(wb16[:, o:o + 3 * C_HEADS], ((0, 0), (0, LANES - 3 * C_HEADS)))
    sgb = jnp.repeat(sgu_b.T, A_WIDTH // A_GROUPS, axis=1)
    nt = s // TOKEN_TILE
    grid = (b, nt)
    G = C_KV_GROUPS
    out_shape = [
        jax.ShapeDtypeStruct((b, s, A_WIDTH), BF16),
        jax.ShapeDtypeStruct((b, s, 2 * B_WIDTH), F32),
        jax.ShapeDtypeStruct((b, G, s // Q_TILE, HEAD_DIM, QL), BF16),
        jax.ShapeDtypeStruct((2, b, G, s, HEAD_DIM), F32),
        jax.ShapeDtypeStruct((b, G, s, HEAD_DIM + SEL_SLOTS), BF16),
        jax.ShapeDtypeStruct((b, G, s, HEAD_DIM), BF16),
        jax.ShapeDtypeStruct((b, G, s // SEL_KEYS, HEAD_DIM, SEL_KEYS), BF16),
        jax.ShapeDtypeStruct((b, G, s // WIN_KEYS, HEAD_DIM, WIN_KEYS), BF16),
        jax.ShapeDtypeStruct((b, 3, G, C_HPG, s), F32),
    ]
    out_specs = [
        pl.BlockSpec((1, TOKEN_TILE, A_WIDTH), lambda bi, i: (bi, i, 0)),
        pl.BlockSpec((1, TOKEN_TILE, 2 * B_WIDTH), lambda bi, i: (bi, i, 0)),
        pl.BlockSpec((1, G, TOKEN_TILE // Q_TILE, HEAD_DIM, QL), lambda bi, i: (bi, 0, i, 0, 0)),
        pl.BlockSpec((2, 1, G, TOKEN_TILE, HEAD_DIM), lambda bi, i: (0, bi, 0, i, 0)),
        pl.BlockSpec((1, G, TOKEN_TILE, HEAD_DIM + SEL_SLOTS), lambda bi, i: (bi, 0, i, 0)),
        pl.BlockSpec((1, G, TOKEN_TILE, HEAD_DIM), lambda bi, i: (bi, 0, i, 0)),
        pl.BlockSpec((1, G, TOKEN_TILE // SEL_KEYS, HEAD_DIM, SEL_KEYS), lambda bi, i: (bi, 0, i, 0, 0)),
        pl.BlockSpec((1, G, TOKEN_TILE // WIN_KEYS, HEAD_DIM, WIN_KEYS), lambda bi, i: (bi, 0, i, 0, 0)),
        pl.BlockSpec((1, 3, G, C_HPG, TOKEN_TILE), lambda bi, i: (bi, 0, 0, 0, i)),
    ]
    in_specs = [
        pl.BlockSpec((1, TOKEN_TILE, d), lambda bi, i: (bi, i, 0)),
        _resident((1, d)), _resident(wa.shape), _resident(wb.shape), _resident(wq.shape),
        _resident(wkv.shape), _resident(wgt.shape), _resident((1, A_WIDTH)),
        _resident(sgu_w.shape), _resident(sgb.shape),
    ]
    return pl.pallas_call(
        _inproj_kernel, grid=grid, in_specs=in_specs, out_specs=out_specs, out_shape=out_shape,
        compiler_params=_cparams(2), name="inproj",
    )(h, g_norm.reshape(1, d), wa, wb, wq, wkv, wgt, sgu_norm_g.reshape(1, A_WIDTH), sgu_w, sgb)


def _rglru_kernel(bx_ref, cw_ref, cb_ref, wa_ref, ba_ref, wx_ref, bxb_ref, lam_ref, o_ref,
                  tail_ref, h_ref, a_ref, b_ref):
    @pl.when(pl.program_id(1) == 0)
    def _():
        tail_ref[...] = jnp.zeros_like(tail_ref)
        h_ref[...] = jnp.zeros_like(h_ref)

    xb = bx_ref[0, :, :B_WIDTH]
    gate = bx_ref[0, :, B_WIDTH:]
    ext = jnp.concatenate([tail_ref[...], xb], axis=0)
    xc = cb_ref[...] + xb * cw_ref[CONV_W - 1:CONV_W, :]
    for k in range(CONV_W - 1):
        shift = CONV_W - 1 - k
        xc = xc + ext[SUBLANES - shift:SUBLANES - shift + SCAN_TILE] * cw_ref[k:k + 1, :]
    tail_ref[...] = xb[SCAN_TILE - SUBLANES:]

    xcb = xc.astype(BF16)
    r = _sigmoid(_dot(xcb, wa_ref[...]) + ba_ref[...])
    i = _sigmoid(_dot(xcb, wx_ref[...]) + bxb_ref[...])
    z = -lam_ref[...]
    e = jnp.exp(-jnp.abs(z))
    softplus = jnp.maximum(z, 0.0) + jnp.log1p(e)
    log_a = -LRU_C * r * softplus
    a = jnp.exp(log_a)
    b = jnp.sqrt(jnp.tanh(-log_a) * (a * a + 1.0)) * (i * xc)

    row = lax.broadcasted_iota(jnp.int32, a.shape, 0) & (SUBLANES - 1)
    for dist in (1, 2, 4):
        a_prev = jnp.where(row >= dist, pltpu.roll(a, dist, 0), 1.0)
        b_prev = jnp.where(row >= dist, pltpu.roll(b, dist, 0), 0.0)
        b = a * b_prev + b
        a = a * a_prev
    a_ref[...] = a
    b_ref[...] = b

    def body(k, h):
        off = pl.multiple_of(k * SUBLANES, SUBLANES)
        rows = pl.ds(off, SUBLANES)
        hs = b_ref[rows, :] + a_ref[rows, :] * h
        b_ref[rows, :] = hs
        return jnp.broadcast_to(hs[SUBLANES - 1:SUBLANES, :], hs.shape)

    h_ref[...] = lax.fori_loop(0, SCAN_TILE // SUBLANES, body, h_ref[...])
    o_ref[0] = (b_ref[...] * jax.nn.gelu(gate)).astype(o_ref.dtype)


def _block_diag(w):
    g, n, _ = w.shape
    out = jnp.zeros((g * n, g * n), w.dtype)
    for k in range(g):
        out = out.at[k * n:(k + 1) * n, k * n:(k + 1) * n].set(w[k])
    return out


def _rglru(bx, conv_w, conv_b, wa, ba, wx, bxb, lam):
    b, s, _ = bx.shape
    w = B_WIDTH
    row = lambda v: v.reshape(1, w)
    return pl.pallas_call(
        _rglru_kernel,
        grid=(b, s // SCAN_TILE),
        in_specs=[pl.BlockSpec((1, SCAN_TILE, 2 * w), lambda bi, i: (bi, i, 0)),
                  _resident((CONV_W, w)), _resident((1, w)), _resident((w, w)), _resident((1, w)),
                  _resident((w, w)), _resident((1, w)), _resident((1, w))],
        out_specs=pl.BlockSpec((1, SCAN_TILE, w), lambda bi, i: (bi, i, 0)),
        out_shape=jax.ShapeDtypeStruct((b, s, w), BF16),
        scratch_shapes=[pltpu.VMEM((SUBLANES, w), F32), pltpu.VMEM((SUBLANES, w), F32),
                        pltpu.VMEM((SCAN_TILE, w), F32), pltpu.VMEM((SCAN_TILE, w), F32)],
        compiler_params=_cparams(2),
        name="rglru",
    )(bx, conv_w, row(conv_b), _block_diag(wa).astype(BF16), row(ba),
      _block_diag(wx).astype(BF16), row(bxb), row(lam))


def _compress_kernel(raw_ref, pos_ref, w1_ref, b1_ref, w2_ref, b2_ref, w2t_ref, b2c_ref, kc_ref, kct_ref):
    g, ncp, half = raw_ref.shape[2:]
    x = raw_ref[0, 0].reshape(g * ncp, half)
    top = (x + pos_ref[0, :, :half]).astype(BF16)
    bot = (x + pos_ref[0, :, half:]).astype(BF16)
    u = _dot(top, w1_ref[0, :half, :])
    v = _dot(bot, w1_ref[0, half:, :])
    hid = jax.nn.gelu(u + pltpu.roll(v, g * ncp - 1, 0) + b1_ref[0]).astype(BF16)
    kc_ref[0, 0] = (_dot(hid, w2_ref[0]) + b2_ref[0]).reshape(g, ncp, HEAD_DIM).astype(kc_ref.dtype)
    for gi in range(g):
        t = lax.dot_general(w2t_ref[0], hid[gi * ncp:(gi + 1) * ncp], (((1,), (1,)), ((), ())),
                            preferred_element_type=F32)
        kct_ref[0, 0, gi] = (t + b2c_ref[0]).astype(kct_ref.dtype)


def _compress(raw, cmp_pos, cmp_w1, cmp_b1, cmp_w2, cmp_b2):
    _, b, g, s, hd = raw.shape
    ncp = s // CMP_STRIDE
    half = CMP_STRIDE * hd
    hid = cmp_w1.shape[-1]
    raw16 = raw.reshape(2, b, g, ncp, half)
    sel = lambda *shape: pl.BlockSpec((1,) + shape, lambda kv, bi: (kv,) + (0,) * len(shape))
    return pl.pallas_call(
        _compress_kernel,
        grid=(2, b),
        in_specs=[pl.BlockSpec((1, 1, g, ncp, half), lambda kv, bi: (kv, bi, 0, 0, 0)),
                  sel(1, 2 * half), sel(2 * half, hid), sel(1, hid), sel(hid, hd), sel(1, hd),
                  sel(hd, hid), sel(hd, 1)],
        out_specs=[pl.BlockSpec((1, 1, g, ncp, hd), lambda kv, bi: (kv, bi, 0, 0, 0)),
                   pl.BlockSpec((1, 1, g, hd, ncp), lambda kv, bi: (kv, bi, 0, 0, 0))],
        out_shape=[jax.ShapeDtypeStruct((2, b, g, ncp, hd), BF16),
                   jax.ShapeDtypeStruct((2, b, g, hd, ncp), BF16)],
        compiler_params=_cparams(2),
        name="compress",
    )(raw16, cmp_pos.reshape(2, 1, 2 * half), cmp_w1.astype(BF16), cmp_b1.reshape(2, 1, hid),
      cmp_w2.astype(BF16), cmp_b2.reshape(2, 1, hd),
      cmp_w2.astype(BF16).transpose(0, 2, 1), cmp_b2.reshape(2, hd, 1))


def _softmax_step(carry, s, v_t):
    m, l, acc = carry
    m_new = jnp.maximum(m, jnp.max(s, axis=0, keepdims=True))
    alpha = jnp.exp(m - m_new)
    p = jnp.exp(s - m_new)
    l = l * alpha + jnp.sum(p, axis=0, keepdims=True)
    acc = acc * alpha + _dot(v_t, p.astype(BF16))
    return m_new, l, acc


def _nsa_kernel(qt_ref, kc_ref, vct_ref, ks_ref, vst_ref, kw_ref, vwt_ref, gate_ref,
                gsel_ref, gwin_ref, gcmp_ref, ovt_ref, o_ref, *, n_tiles):
    c = pl.program_id(2)
    qt = qt_ref[0, 0, 0]
    ncp = kc_ref.shape[2]
    nsel = ovt_ref.shape[0]
    init = (jnp.full((1, QL), NEG, F32), jnp.zeros((1, QL), F32), jnp.zeros((HEAD_DIM, QL), F32))
    lane = lax.broadcasted_iota(jnp.int32, (1, QL), 1)
    tq = lane & (Q_TILE - 1)

    y0 = pl.multiple_of((n_tiles - 1 - c) * (Q_TILE // CMP_STRIDE), SUBLANES)
    s = _dot(kc_ref[0, 0], qt) + gcmp_ref[0, pl.ds(y0, ncp), :]
    m = jnp.max(s, axis=0, keepdims=True)
    e = jnp.exp(s - m)
    p = e / jnp.sum(e, axis=0, keepdims=True)
    p = p * (c * Q_TILE + tq >= CMP_LEN - 1).astype(F32)
    o_cmp = _dot(vct_ref[0, 0], p.astype(BF16))

    p_heads = p[:, 0:Q_TILE]
    for r in range(1, C_HPG):
        p_heads = p_heads + p[:, r * Q_TILE:(r + 1) * Q_TILE]
    p_hi = p_heads.astype(BF16)
    p_lo = (p_heads - p_hi.astype(F32)).astype(BF16)
    imp = _dot(ovt_ref[...], p_hi) + _dot(ovt_ref[...], p_lo)
    j = lax.broadcasted_iota(jnp.int32, (nsel, Q_TILE), 0)
    blk = c * (Q_TILE // SEL_LEN) + lax.shift_right_logical(
        lax.broadcasted_iota(jnp.int32, (nsel, Q_TILE), 1), SEL_SHIFT)
    forced = (j == 0) | (j == blk) | (j == blk - 1)
    score = jnp.where(j <= blk, jnp.where(forced, FORCE_SCORE, imp), -1.0)
    groups = range(nsel // SUBLANES)
    rows = [score[k * SUBLANES:(k + 1) * SUBLANES] for k in groups]
    j_rows = j[:SUBLANES]
    ranks = [jnp.zeros((SUBLANES, Q_TILE), jnp.int32) for _ in groups]
    for jp in range(nsel):
        other = score[jp:jp + 1, :]
        for k in groups:
            if k * SUBLANES > jp:
                beats = jnp.where(other >= rows[k], 1, 0)
            elif (k + 1) * SUBLANES <= jp:
                beats = jnp.where(other > rows[k], 1, 0)
            else:
                beats = jnp.where(j_rows > jp - k * SUBLANES,
                                  jnp.where(other >= rows[k], 1, 0), jnp.where(other > rows[k], 1, 0))
            ranks[k] = ranks[k] + beats
    rank = jnp.concatenate(ranks, axis=0)
    mask_rows = jnp.where((rank < SEL_TOP) & (score >= 0.0), 0.0, NEG).astype(BF16)
    q_aug = jnp.concatenate([qt, jnp.concatenate([mask_rows] * C_HPG, axis=1)], axis=0)

    x0 = (n_tiles - 1 - c) * Q_TILE

    def sel_body(i, carry):
        off = pl.multiple_of(i * SEL_KEYS, SEL_KEYS)
        s = _dot(ks_ref[0, 0, pl.ds(off, SEL_KEYS), :], q_aug)
        s = s + gsel_ref[0, pl.ds(pl.multiple_of(x0 + off, Q_TILE), SEL_KEYS), :]
        return _softmax_step(carry, s, vst_ref[0, 0, i])

    n_sel_steps = (c * Q_TILE + Q_TILE + SEL_KEYS - 1) // SEL_KEYS
    _, l_sel, acc_sel = lax.fori_loop(0, n_sel_steps, sel_body, init)
    o_sel = acc_sel / l_sel

    back = WINDOW // WIN_KEYS
    n_win = WINDOW + Q_TILE
    first = jnp.maximum(c - back, 0)
    start = pl.multiple_of(first * WIN_KEYS, WIN_KEYS)
    rel = pl.multiple_of(jnp.maximum(back - c, 0) * WIN_KEYS, WIN_KEYS)
    s = _dot(kw_ref[0, 0, pl.ds(start, n_win), :], qt) + gwin_ref[0, pl.ds(rel, n_win), :]
    e = jnp.exp(s - jnp.max(s, axis=0, keepdims=True))
    l_win = jnp.sum(e, axis=0, keepdims=True)
    e = e.astype(BF16)
    acc_win = jnp.zeros((HEAD_DIM, QL), F32)
    for i in range(n_win // WIN_KEYS):
        acc_win = acc_win + _dot(vwt_ref[0, 0, first + i], e[i * WIN_KEYS:(i + 1) * WIN_KEYS])
    o_win = acc_win / l_win

    cols = []
    for r in range(C_HPG):
        ln = slice(r * Q_TILE, (r + 1) * Q_TILE)
        cols.append(gate_ref[0, 0, 0, r:r + 1, :] * o_cmp[:, ln]
                    + gate_ref[0, 1, 0, r:r + 1, :] * o_sel[:, ln]
                    + gate_ref[0, 2, 0, r:r + 1, :] * o_win[:, ln])
    pairs = [jnp.concatenate(cols[2 * k:2 * k + 2], axis=0).T for k in range(C_HPG // 2)]
    o_ref[0] = jnp.concatenate(pairs, axis=1).astype(o_ref.dtype)


def _nsa(qt, kc, vct, ks, vst, kw, vwt, gates, gsel, gwin, gcmp, ovt):
    b, G, n_tiles = qt.shape[:3]
    s = n_tiles * Q_TILE
    per_bg = lambda *shape: pl.BlockSpec((1, 1) + shape, lambda g, bi, c: (bi, g) + (0,) * len(shape))
    table = lambda a: pl.BlockSpec((1,) + a.shape[1:], lambda g, bi, c: (g, 0, 0),
                                   pipeline_mode=pl.Buffered(1))
    return pl.pallas_call(
        functools.partial(_nsa_kernel, n_tiles=n_tiles),
        grid=(G, b, n_tiles),
        in_specs=[pl.BlockSpec((1, 1, 1, HEAD_DIM, QL), lambda g, bi, c: (bi, g, c, 0, 0)),
                  per_bg(*kc.shape[2:]), per_bg(*vct.shape[2:]),
                  per_bg(*ks.shape[2:]), per_bg(*vst.shape[2:]),
                  per_bg(*kw.shape[2:]), per_bg(*vwt.shape[2:]),
                  pl.BlockSpec((1, 3, 1, C_HPG, Q_TILE), lambda g, bi, c: (bi, 0, g, 0, c)),
                  table(gsel), table(gwin), table(gcmp), _resident(ovt.shape)],
        out_specs=pl.BlockSpec((1, Q_TILE, C_HPG * HEAD_DIM), lambda g, bi, c: (bi, c, g)),
        out_shape=jax.ShapeDtypeStruct((b, s, C_WIDTH), BF16),
        compiler_params=_cparams(3),
        name="nsa",
    )(qt, kc, vct, ks, vst, kw, vwt, gates, gsel, gwin, gcmp, ovt)


def _outproj_kernel(h_ref, ya_ref, yb_ref, yc_ref, wa_ref, wb_ref, wc_ref, g_ref, o_ref):
    mix = _dot(ya_ref[...], wa_ref[...]) + _dot(yb_ref[...], wb_ref[...]) + _dot(yc_ref[...], wc_ref[...])
    o_ref[...] = h_ref[...] + _rms(mix, g_ref[...])


def _outproj(h, ya, yb, yc, w_out, g_norm):
    n, d = h.shape
    w = w_out.astype(BF16)
    wa, wb, wc = w[:A_WIDTH], w[A_WIDTH:A_WIDTH + B_WIDTH], w[A_WIDTH + B_WIDTH:]
    tile = lambda width: pl.BlockSpec((TOKEN_TILE, width), lambda i: (i, 0))
    return pl.pallas_call(
        _outproj_kernel,
        grid=(n // TOKEN_TILE,),
        in_specs=[tile(d), tile(A_WIDTH), tile(B_WIDTH), tile(C_WIDTH),
                  _resident(wa.shape), _resident(wb.shape), _resident(wc.shape), _resident((1, d))],
        out_specs=tile(d),
        out_shape=jax.ShapeDtypeStruct((n, d), F32),
        compiler_params=_cparams(1),
        name="outproj",
    )(h, ya, yb, yc, wa, wb, wc, g_norm.reshape(1, d))


def _ple_kernel(h_ref, p_ref, gpre_ref, gpost_ref, wg_ref, wp_ref, o_ref):
    x = h_ref[...]
    gate = _sigmoid(_dot(_rms(x, gpre_ref[...]).astype(BF16), wg_ref[...]))
    emb = _dot(p_ref[...].astype(BF16), wp_ref[...])
    o_ref[...] = x + _rms(gate * emb, gpost_ref[...])


def _ple(h, p, g_pre, g_post, w_gate, w_proj):
    n, d = h.shape
    dp = p.shape[-1]
    tile = lambda width: pl.BlockSpec((TOKEN_TILE, width), lambda i: (i, 0))
    return pl.pallas_call(
        _ple_kernel,
        grid=(n // TOKEN_TILE,),
        in_specs=[tile(d), tile(dp), _resident((1, d)), _resident((1, d)),
                  _resident((d, d)), _resident((dp, d))],
        out_specs=tile(d),
        out_shape=jax.ShapeDtypeStruct((n, d), F32),
        compiler_params=_cparams(1),
        name="ple",
    )(h, p, g_pre.reshape(1, d), g_post.reshape(1, d), w_gate.astype(BF16), w_proj.astype(BF16))


def _overlap_t(s):
    ncp = s // CMP_STRIDE
    n_cmp = (s - CMP_LEN) // CMP_STRIDE + 1
    cs = jnp.arange(ncp) * CMP_STRIDE
    ss = jnp.arange(s // SEL_LEN) * SEL_LEN
    ov = jnp.clip(jnp.minimum(cs[None] + CMP_LEN, ss[:, None] + SEL_LEN)
                  - jnp.maximum(cs[None], ss[:, None]), 0, None).astype(F32) / CMP_LEN
    ov = jnp.where(jnp.arange(ncp)[None] < n_cmp, ov, 0.0).astype(BF16)
    return jnp.pad(ov, ((0, SEL_SLOTS - s // SEL_LEN), (0, 0)))


def kernel(x, p, rel_bias, norm_g, ffn_w_gate, ffn_w_up, ffn_w_down, w_in, w_out, sgu_norm_g, sgu_w, sgu_b,
           conv_w, conv_b, lru_wa, lru_ba, lru_wx, lru_bx, lru_lambda, cmp_pos, cmp_w1, cmp_b1, cmp_w2,
           cmp_b2, ple_w_gate, ple_w_proj):
    b, s, d = x.shape
    depth = norm_g.shape[0]
    assert s % TOKEN_TILE == 0 and s % SCAN_TILE == 0 and s % SEL_KEYS == 0
    assert s >= WINDOW + Q_TILE and SEL_TOP <= s // SEL_LEN <= SEL_SLOTS
    n_tiles = s // Q_TILE

    rbx = jnp.repeat(rel_bias.reshape(N_BUCKETS, C_KV_GROUPS, C_HPG).transpose(1, 0, 2), Q_TILE, axis=2)
    no_limit = 1 << 30
    gsel = _bias_table(rbx, s + SEL_KEYS - Q_TILE, 1, s - Q_TILE, no_limit)
    gwin = _bias_table(rbx, 2 * WINDOW + Q_TILE, 1, WINDOW, WINDOW)
    per_tile = Q_TILE // CMP_STRIDE
    gcmp = _bias_table(rbx, per_tile * (n_tiles - 1) + s // CMP_STRIDE, CMP_STRIDE,
                       CMP_STRIDE * per_tile * (n_tiles - 1) - (CMP_LEN - 1), no_limit)
    ovt = _overlap_t(s)

    h = x.reshape(b * s, d)
    for i in range(depth):
        g = norm_g[i]
        h = _ffn(h, g[0], g[1], ffn_w_gate[i, 0], ffn_w_up[i, 0], ffn_w_down[i, 0])
        ya, bx, qt, raw, ks, kw, vst, vwt, gates = _inproj(
            h.reshape(b, s, d), g[2], w_in[i], sgu_norm_g[i], sgu_w[i], sgu_b[i])
        yb = _rglru(bx, conv_w[i], conv_b[i], lru_wa[i], lru_ba[i], lru_wx[i], lru_bx[i], lru_lambda[i])
        kc, kct = _compress(raw, cmp_pos[i], cmp_w1[i], cmp_b1[i], cmp_w2[i], cmp_b2[i])
        yc = _nsa(qt, kc[0], kct[1], ks, vst, kw, vwt, gates, gsel, gwin, gcmp, ovt)
        h = _outproj(h, ya.reshape(b * s, -1), yb.reshape(b * s, -1), yc.reshape(b * s, -1), w_out[i], g[3])
        h = _ffn(h, g[4], g[5], ffn_w_gate[i, 1], ffn_w_up[i, 1], ffn_w_down[i, 1])
        h = _ple(h, p[i].reshape(b * s, -1), g[6], g[7], ple_w_gate[i], ple_w_proj[i])
    return h.reshape(b, s, d)
```

```python
import functools
import math

import jax
import jax.numpy as jnp
from jax import lax
from jax.experimental import pallas as pl
from jax.experimental.pallas import tpu as pltpu

F32 = jnp.float32
BF16 = jnp.bfloat16

RMS_EPS = 1e-6
A_GROUPS = 4
A_WIDTH = 256
A_CHUNK = 128
B_GROUPS = 4
B_WIDTH = 256
CONV_W = 4
LRU_C = 8.0
C_HEADS = 8
C_KV_GROUPS = 2
C_HPG = C_HEADS // C_KV_GROUPS
HEAD_DIM = 64
C_WIDTH = C_HEADS * HEAD_DIM
KV_W = C_KV_GROUPS * HEAD_DIM
CMP_LEN = 32
CMP_STRIDE = 16
SEL_LEN = 64
SEL_SHIFT = 6
SEL_SLOTS = 64
SEL_TOP = 16
WINDOW = 512
FORCE_SCORE = 1e4
NEG = -1e30
N_BUCKETS = 32
MAX_DISTANCE = 1024

LANES = 128
TOKEN_TILE = 512
FFN_CHUNK = 256
SCAN_TILE = 512
SUBLANES = 8
Q_TILE = 128
SEL_KEYS = 512
SEL_HALF = 256
WIN_KEYS = 128
QL = C_HPG * Q_TILE
TABLE_ROWS = 128
VMEM_LIMIT = 56 * 1024 * 1024


def _cparams(n_axes):
    return pltpu.CompilerParams(dimension_semantics=("arbitrary",) * n_axes,
                                vmem_limit_bytes=VMEM_LIMIT)


def _resident(shape):
    nd = len(shape)
    return pl.BlockSpec(shape, lambda *_: (0,) * nd, pipeline_mode=pl.Buffered(1))


def _rms(x, g):
    return x * lax.rsqrt(jnp.mean(x * x, axis=-1, keepdims=True) + RMS_EPS) * g


def _sigmoid(x):
    return 1.0 / (1.0 + jnp.exp(-x))


def _dot(a, b):
    return jnp.dot(a, b, preferred_element_type=F32)


def _bias_table_kernel(rbx_ref, o_ref, *, stride, offset, dmax):
    i = pl.program_id(1)
    shape = (TABLE_ROWS, QL)
    x = lax.broadcasted_iota(jnp.int32, shape, 0) + i * TABLE_ROWS
    t = lax.broadcasted_iota(jnp.int32, shape, 1) & (Q_TILE - 1)
    d = t - stride * x + offset
    n = jnp.maximum(d, 0)
    max_exact = N_BUCKETS // 2
    nf = jnp.maximum(n, max_exact).astype(F32)
    large = max_exact + (jnp.log(nf / max_exact) / math.log(MAX_DISTANCE / max_exact)
                         * (N_BUCKETS - max_exact)).astype(jnp.int32)
    large = jnp.minimum(large, N_BUCKETS - 1)
    bucket = jnp.where(n < max_exact, n, large)
    acc = jnp.zeros(shape, F32)
    for k in range(N_BUCKETS):
        acc = jnp.where(bucket == k, rbx_ref[0, k:k + 1, :], acc)
    o_ref[0] = jnp.where((d >= 0) & (d < dmax), acc, NEG)


def _bias_table(rbx, rows, stride, offset, dmax):
    rows_p = -(-rows // TABLE_ROWS) * TABLE_ROWS
    return pl.pallas_call(
        functools.partial(_bias_table_kernel, stride=stride, offset=offset, dmax=dmax),
        grid=(C_KV_GROUPS, rows_p // TABLE_ROWS),
        in_specs=[pl.BlockSpec((1, N_BUCKETS, QL), lambda g, i: (g, 0, 0))],
        out_specs=pl.BlockSpec((1, TABLE_ROWS, QL), lambda g, i: (g, i, 0)),
        out_shape=jax.ShapeDtypeStruct((C_KV_GROUPS, rows_p, QL), F32),
        compiler_params=_cparams(2),
        name="bias_table",
    )(rbx)


def _ffn_kernel(h_ref, gpre_ref, gpost_ref, wg_ref, wu_ref, wd_ref, o_ref, acc_ref):
    x = h_ref[...]
    xn = _rms(x, gpre_ref[...]).astype(BF16)
    acc_ref[...] = jnp.zeros_like(acc_ref)

    def body(j, carry):
        gate = _dot(xn, wg_ref[j])
        up = _dot(xn, wu_ref[j])
        hid = (gate * _sigmoid(gate) * up).astype(BF16)
        acc_ref[...] += _dot(hid, wd_ref[j])
        return carry

    lax.fori_loop(0, wg_ref.shape[0], body, 0)
    o_ref[...] = x + 0.5 * _rms(acc_ref[...], gpost_ref[...])


def _ffn(h, g_pre, g_post, wg, wu, wd):
    n, d = h.shape
    d_ff = wg.shape[1]
    nch = d_ff // FFN_CHUNK
    wg3 = wg.astype(BF16).reshape(d, nch, FFN_CHUNK).transpose(1, 0, 2)
    wu3 = wu.astype(BF16).reshape(d, nch, FFN_CHUNK).transpose(1, 0, 2)
    wd3 = wd.astype(BF16).reshape(nch, FFN_CHUNK, d)
    tile = pl.BlockSpec((TOKEN_TILE, d), lambda i: (i, 0))
    return pl.pallas_call(
        _ffn_kernel,
        grid=(n // TOKEN_TILE,),
        in_specs=[tile, _resident((1, d)), _resident((1, d)),
                  _resident(wg3.shape), _resident(wu3.shape), _resident(wd3.shape)],
        out_specs=tile,
        out_shape=jax.ShapeDtypeStruct((n, d), F32),
        scratch_shapes=[pltpu.VMEM((TOKEN_TILE, d), F32)],
        compiler_params=_cparams(1),
        name="ffn",
    )(h, g_pre.reshape(1, d), g_post.reshape(1, d), wg3, wu3, wd3)


def _inproj_kernel(h_ref, g_ref, wa_ref, wb_ref, wq_ref, wkv_ref, wgt_ref, sgn_ref, sgw_ref, sgb_ref,
                   ya_ref, bx_ref, qt_ref, raw_ref, ks_ref, kw_ref, vst_ref, vwt_ref, gt_ref):
    xn = _rms(h_ref[0], g_ref[...]).astype(BF16)

    za = _dot(xn, wa_ref[...])
    u = jax.nn.gelu(za[:, :A_WIDTH])
    v = _rms(jax.nn.gelu(za[:, A_WIDTH:]), sgn_ref[...]).astype(BF16)
    row = lax.broadcasted_iota(jnp.int32, (A_CHUNK, A_CHUNK), 0)
    col = lax.broadcasted_iota(jnp.int32, (A_CHUNK, A_CHUNK), 1)
    lane_group = lax.shift_right_logical(lax.broadcasted_iota(jnp.int32, (A_CHUNK, A_WIDTH), 1),
                                          (A_WIDTH // A_GROUPS).bit_length() - 1)
    w_tril = [jnp.where(row >= col, sgw_ref[g], 0.0).astype(BF16) for g in range(A_GROUPS)]
    for c in range(TOKEN_TILE // A_CHUNK):
        rows = slice(c * A_CHUNK, (c + 1) * A_CHUNK)
        mixed = jnp.zeros((A_CHUNK, A_WIDTH), F32)
        for g in range(A_GROUPS):
            mixed = jnp.where(lane_group == g, _dot(w_tril[g], v[rows]), mixed)
        ya_ref[0, rows, :] = (u[rows] * (mixed + sgb_ref[...])).astype(ya_ref.dtype)

    bx_ref[0] = _dot(xn, wb_ref[...])

    zq_t = (_dot(xn, wq_ref[...]) * (HEAD_DIM ** -0.5)).T
    for g in range(C_KV_GROUPS):
        for c in range(TOKEN_TILE // Q_TILE):
            parts = []
            for r in range(C_HPG):
                base = (g * C_HPG + r) * HEAD_DIM
                parts.append(zq_t[base:base + HEAD_DIM, c * Q_TILE:(c + 1) * Q_TILE])
            qt_ref[0, g, c] = jnp.concatenate(parts, axis=1).astype(qt_ref.dtype)

    zkv = _dot(xn, wkv_ref[...])
    vs_t = zkv[:, 3 * KV_W:4 * KV_W].T
    vw_t = zkv[:, 5 * KV_W:6 * KV_W].T
    key_blk = lax.shift_right_logical(
        lax.broadcasted_iota(jnp.int32, (TOKEN_TILE, SEL_SLOTS), 0) + pl.program_id(1) * TOKEN_TILE, SEL_SHIFT)
    blk_onehot = jnp.where(key_blk == lax.broadcasted_iota(jnp.int32, (TOKEN_TILE, SEL_SLOTS), 1), 1.0, 0.0)
    for g in range(C_KV_GROUPS):
        lo, hi = g * HEAD_DIM, (g + 1) * HEAD_DIM
        raw_ref[0, 0, g] = zkv[:, lo:hi]
        raw_ref[1, 0, g] = zkv[:, KV_W + lo:KV_W + hi]
        ks_ref[0, g] = jnp.concatenate([zkv[:, 2 * KV_W + lo:2 * KV_W + hi], blk_onehot],
                                       axis=1).astype(ks_ref.dtype)
        kw_ref[0, g] = zkv[:, 4 * KV_W + lo:4 * KV_W + hi].astype(kw_ref.dtype)
        for c in range(TOKEN_TILE // SEL_KEYS):
            vst_ref[0, g, c] = vs_t[lo:hi, c * SEL_KEYS:(c + 1) * SEL_KEYS].astype(vst_ref.dtype)
        for c in range(TOKEN_TILE // WIN_KEYS):
            vwt_ref[0, g, c] = vw_t[lo:hi, c * WIN_KEYS:(c + 1) * WIN_KEYS].astype(vwt_ref.dtype)

    sg_t = _sigmoid(_dot(xn, wgt_ref[...])).T
    for br in range(3):
        for g in range(C_KV_GROUPS):
            base = br * C_HEADS + g * C_HPG
            gt_ref[0, br, g] = sg_t[base:base + C_HPG, :]


def _inproj(h, g_norm, w_in, sgu_norm_g, sgu_w, sgu_b):
    b, s, d = h.shape
    wb16 = w_in.astype(BF16)
    o = 0
    wa = wb16[:, o:o + 2 * A_WIDTH]; o += 2 * A_WIDTH
    wb = wb16[:, o:o + 2 * B_WIDTH]; o += 2 * B_WIDTH
    wq = wb16[:, o:o + C_WIDTH]; o += C_WIDTH
    wkv = wb16[:, o:o + 6 * KV_W]; o += 6 * KV_W
    wgt = jnp.pad(wb16[:, o:o + 3 * C_HEADS], ((0, 0), (0, LANES - 3 * C_HEADS)))
    sgb = jnp.repeat(sgu_b.T, A_WIDTH // A_GROUPS, axis=1)
    nt = s // TOKEN_TILE
    grid = (b, nt)
    G = C_KV_GROUPS
    out_shape = [
        jax.ShapeDtypeStruct((b, s, A_WIDTH), BF16),
        jax.ShapeDtypeStruct((b, s, 2 * B_WIDTH), F32),
        jax.ShapeDtypeStruct((b, G, s // Q_TILE, HEAD_DIM, QL), BF16),
        jax.ShapeDtypeStruct((2, b, G, s, HEAD_DIM), F32),
        jax.ShapeDtypeStruct((b, G, s, HEAD_DIM + SEL_SLOTS), BF16),
        jax.ShapeDtypeStruct((b, G, s, HEAD_DIM), BF16),
        jax.ShapeDtypeStruct((b, G, s // SEL_KEYS, HEAD_DIM, SEL_KEYS), BF16),
        jax.ShapeDtypeStruct((b, G, s // WIN_KEYS, HEAD_DIM, WIN_KEYS), BF16),
        jax.ShapeDtypeStruct((b, 3, G, C_HPG, s), F32),
    ]
    out_specs = [
        pl.BlockSpec((1, TOKEN_TILE, A_WIDTH), lambda bi, i: (bi, i, 0)),
        pl.BlockSpec((1, TOKEN_TILE, 2 * B_WIDTH), lambda bi, i: (bi, i, 0)),
        pl.BlockSpec((1, G, TOKEN_TILE // Q_TILE, HEAD_DIM, QL), lambda bi, i: (bi, 0, i, 0, 0)),
        pl.BlockSpec((2, 1, G, TOKEN_TILE, HEAD_DIM), lambda bi, i: (0, bi, 0, i, 0)),
        pl.BlockSpec((1, G, TOKEN_TILE, HEAD_DIM + SEL_SLOTS), lambda bi, i: (bi, 0, i, 0)),
        pl.BlockSpec((1, G, TOKEN_TILE, HEAD_DIM), lambda bi, i: (bi, 0, i, 0)),
        pl.BlockSpec((1, G, TOKEN_TILE // SEL_KEYS, HEAD_DIM, SEL_KEYS), lambda bi, i: (bi, 0, i, 0, 0)),
        pl.BlockSpec((1, G, TOKEN_TILE // WIN_KEYS, HEAD_DIM, WIN_KEYS), lambda bi, i: (bi, 0, i, 0, 0)),
        pl.BlockSpec((1, 3, G, C_HPG, TOKEN_TILE), lambda bi, i: (bi, 0, 0, 0, i)),
    ]
    in_specs = [
        pl.BlockSpec((1, TOKEN_TILE, d), lambda bi, i: (bi, i, 0)),
        _resident((1, d)), _resident(wa.shape), _resident(wb.shape), _resident(wq.shape),
        _resident(wkv.shape), _resident(wgt.shape), _resident((1, A_WIDTH)),
        _resident(sgu_w.shape), _resident(sgb.shape),
    ]
    return pl.pallas_call(
        _inproj_kernel, grid=grid, in_specs=in_specs, out_specs=out_specs, out_shape=out_shape,
        compiler_params=_cparams(2), name="inproj",
    )(h, g_norm.reshape(1, d), wa, wb, wq, wkv, wgt, sgu_norm_g.reshape(1, A_WIDTH), sgu_w, sgb)


def _rglru_kernel(bx_ref, cw_ref, cb_ref, wa_ref, ba_ref, wx_ref, bxb_ref, lam_ref, o_ref,
                  tail_ref, h_ref, a_ref, b_ref):
    @pl.when(pl.program_id(1) == 0)
    def _():
        tail_ref[...] = jnp.zeros_like(tail_ref)
        h_ref[...] = jnp.zeros_like(h_ref)

    xb = bx_ref[0, :, :B_WIDTH]
    gate = bx_ref[0, :, B_WIDTH:]
    ext = jnp.concatenate([tail_ref[...], xb], axis=0)
    xc = cb_ref[...] + xb * cw_ref[CONV_W - 1:CONV_W, :]
    for k in range(CONV_W - 1):
        shift = CONV_W - 1 - k
        xc = xc + ext[SUBLANES - shift:SUBLANES - shift + SCAN_TILE] * cw_ref[k:k + 1, :]
    tail_ref[...] = xb[SCAN_TILE - SUBLANES:]

    xcb = xc.astype(BF16)
    r = _sigmoid(_dot(xcb, wa_ref[...]) + ba_ref[...])
    i = _sigmoid(_dot(xcb, wx_ref[...]) + bxb_ref[...])
    z = -lam_ref[...]
    e = jnp.exp(-jnp.abs(z))
    softplus = jnp.maximum(z, 0.0) + jnp.log1p(e)
    log_a = -LRU_C * r * softplus
    a = jnp.exp(log_a)
    b = jnp.sqrt(jnp.tanh(-log_a) * (a * a + 1.0)) * (i * xc)

    row = lax.broadcasted_iota(jnp.int32, a.shape, 0) & (SUBLANES - 1)
    for dist in (1, 2, 4):
        a_prev = jnp.where(row >= dist, pltpu.roll(a, dist, 0), 1.0)
        b_prev = jnp.where(row >= dist, pltpu.roll(b, dist, 0), 0.0)
        b = a * b_prev + b
        a = a * a_prev
    a_ref[...] = a
    b_ref[...] = b

    def body(k, h):
        off = pl.multiple_of(k * SUBLANES, SUBLANES)
        rows = pl.ds(off, SUBLANES)
        hs = b_ref[rows, :] + a_ref[rows, :] * h
        b_ref[rows, :] = hs
        return jnp.broadcast_to(hs[SUBLANES - 1:SUBLANES, :], hs.shape)

    h_ref[...] = lax.fori_loop(0, SCAN_TILE // SUBLANES, body, h_ref[...])
    o_ref[0] = (b_ref[...] * jax.nn.gelu(gate)).astype(o_ref.dtype)


def _block_diag(w):
    g, n, _ = w.shape
    out = jnp.zeros((g * n, g * n), w.dtype)
    for k in range(g):
        out = out.at[k * n:(k + 1) * n, k * n:(k + 1) * n].set(w[k])
    return out


def _rglru(bx, conv_w, conv_b, wa, ba, wx, bxb, lam):
    b, s, _ = bx.shape
    w = B_WIDTH
    row = lambda v: v.reshape(1, w)
    return pl.pallas_call(
        _rglru_kernel,
        grid=(b, s // SCAN_TILE),
        in_specs=[pl.BlockSpec((1, SCAN_TILE, 2 * w), lambda bi, i: (bi, i, 0)),
                  _resident((CONV_W, w)), _resident((1, w)), _resident((w, w)), _resident((1, w)),
                  _resident((w, w)), _resident((1, w)), _resident((1, w))],
        out_specs=pl.BlockSpec((1, SCAN_TILE, w), lambda bi, i: (bi, i, 0)),
        out_shape=jax.ShapeDtypeStruct((b, s, w), BF16),
        scratch_shapes=[pltpu.VMEM((SUBLANES, w), F32), pltpu.VMEM((SUBLANES, w), F32),
                        pltpu.VMEM((SCAN_TILE, w), F32), pltpu.VMEM((SCAN_TILE, w), F32)],
        compiler_params=_cparams(2),
        name="rglru",
    )(bx, conv_w, row(conv_b), _block_diag(wa).astype(BF16), row(ba),
      _block_diag(wx).astype(BF16), row(bxb), row(lam))


def _compress_kernel(raw_ref, pos_ref, w1_ref, b1_ref, w2_ref, b2_ref, w2t_ref, b2c_ref, kc_ref, kct_ref):
    g, ncp, half = raw_ref.shape[2:]
    x = raw_ref[0, 0].reshape(g * ncp, half)
    top = (x + pos_ref[0, :, :half]).astype(BF16)
    bot = (x + pos_ref[0, :, half:]).astype(BF16)
    u = _dot(top, w1_ref[0, :half, :])
    v = _dot(bot, w1_ref[0, half:, :])
    hid = jax.nn.gelu(u + pltpu.roll(v, g * ncp - 1, 0) + b1_ref[0]).astype(BF16)
    kc_ref[0, 0] = (_dot(hid, w2_ref[0]) + b2_ref[0]).reshape(g, ncp, HEAD_DIM).astype(kc_ref.dtype)
    for gi in range(g):
        t = lax.dot_general(w2t_ref[0], hid[gi * ncp:(gi + 1) * ncp], (((1,), (1,)), ((), ())),
                            preferred_element_type=F32)
        kct_ref[0, 0, gi] = (t + b2c_ref[0]).astype(kct_ref.dtype)


def _compress(raw, cmp_pos, cmp_w1, cmp_b1, cmp_w2, cmp_b2):
    _, b, g, s, hd = raw.shape
    ncp = s // CMP_STRIDE
    half = CMP_STRIDE * hd
    hid = cmp_w1.shape[-1]
    raw16 = raw.reshape(2, b, g, ncp, half)
    sel = lambda *shape: pl.BlockSpec((1,) + shape, lambda kv, bi: (kv,) + (0,) * len(shape))
    return pl.pallas_call(
        _compress_kernel,
        grid=(2, b),
        in_specs=[pl.BlockSpec((1, 1, g, ncp, half), lambda kv, bi: (kv, bi, 0, 0, 0)),
                  sel(1, 2 * half), sel(2 * half, hid), sel(1, hid), sel(hid, hd), sel(1, hd),
                  sel(hd, hid), sel(hd, 1)],
        out_specs=[pl.BlockSpec((1, 1, g, ncp, hd), lambda kv, bi: (kv, bi, 0, 0, 0)),
                   pl.BlockSpec((1, 1, g, hd, ncp), lambda kv, bi: (kv, bi, 0, 0, 0))],
        out_shape=[jax.ShapeDtypeStruct((2, b, g, ncp, hd), BF16),
                   jax.ShapeDtypeStruct((2, b, g, hd, ncp), BF16)],
        compiler_params=_cparams(2),
        name="compress",
    )(raw16, cmp_pos.reshape(2, 1, 2 * half), cmp_w1.astype(BF16), cmp_b1.reshape(2, 1, hid),
      cmp_w2.astype(BF16), cmp_b2.reshape(2, 1, hd),
      cmp_w2.astype(BF16).transpose(0, 2, 1), cmp_b2.reshape(2, hd, 1))


def _softmax_step(carry, s, v_t):
    m, l, acc = carry
    m_new = jnp.maximum(m, jnp.max(s, axis=0, keepdims=True))
    alpha = jnp.exp(m - m_new)
    p = jnp.exp(s - m_new)
    l = l * alpha + jnp.sum(p, axis=0, keepdims=True)
    acc = acc * alpha + _dot(v_t, p.astype(BF16))
    return m_new, l, acc


def _nsa_kernel(qt_ref, kc_ref, vct_ref, ks_ref, vst_ref, kw_ref, vwt_ref, gate_ref,
                gsel_ref, gwin_ref, gcmp_ref, ovt_ref, o_ref, *, n_tiles):
    c = pl.program_id(1)
    groups = range(C_KV_GROUPS)
    ncp = kc_ref.shape[2]
    qts = [qt_ref[0, g, 0] for g in groups]

    y0 = pl.multiple_of((n_tiles - 1 - c) * (Q_TILE // CMP_STRIDE), SUBLANES)
    back = WINDOW // WIN_KEYS
    n_win = WINDOW + Q_TILE
    first = jnp.maximum(c - back, 0)
    start = pl.multiple_of(first * WIN_KEYS, WIN_KEYS)
    rel = pl.multiple_of(jnp.maximum(back - c, 0) * WIN_KEYS, WIN_KEYS)
    s_cmp = [_dot(kc_ref[0, g], qts[g]) + gcmp_ref[g, pl.ds(y0, ncp), :] for g in groups]
    s_win = [_dot(kw_ref[0, g, pl.ds(start, n_win), :], qts[g]) + gwin_ref[g, pl.ds(rel, n_win), :]
             for g in groups]

    tq = lax.broadcasted_iota(jnp.int32, (1, QL), 1) & (Q_TILE - 1)
    has_cmp = (c * Q_TILE + tq >= CMP_LEN - 1).astype(F32)
    o_cmp, imp = [], []
    for g in groups:
        e = jnp.exp(s_cmp[g] - jnp.max(s_cmp[g], axis=0, keepdims=True))
        p = e / jnp.sum(e, axis=0, keepdims=True) * has_cmp
        o_cmp.append(_dot(vct_ref[0, g], p.astype(BF16)))
        p_heads = p[:, 0:Q_TILE]
        for r in range(1, C_HPG):
            p_heads = p_heads + p[:, r * Q_TILE:(r + 1) * Q_TILE]
        p_hi = p_heads.astype(BF16)
        p_lo = (p_heads - p_hi.astype(F32)).astype(BF16)
        imp.append(_dot(ovt_ref[...], p_hi) + _dot(ovt_ref[...], p_lo))

    o_win = []
    for g in groups:
        e = jnp.exp(s_win[g] - jnp.max(s_win[g], axis=0, keepdims=True))
        l_win = jnp.sum(e, axis=0, keepdims=True)
        e = e.astype(BF16)
        acc = jnp.zeros((HEAD_DIM, QL), F32)
        for i in range(n_win // WIN_KEYS):
            acc = acc + _dot(vwt_ref[0, g, first + i], e[i * WIN_KEYS:(i + 1) * WIN_KEYS])
        o_win.append(acc / l_win)

    q_aug = [_with_mask_rows(qts[g], c, imp[g]) for g in groups]
    x0 = (n_tiles - 1 - c) * Q_TILE
    halves = range(SEL_KEYS // SEL_HALF)
    init = (jnp.full((1, QL), NEG, F32), jnp.zeros((1, QL), F32), jnp.zeros((HEAD_DIM, QL), F32))

    def sel_body(i, carry):
        off = pl.multiple_of(i * SEL_KEYS, SEL_KEYS)
        s = [[_dot(ks_ref[0, g, pl.ds(off + h * SEL_HALF, SEL_HALF), :], q_aug[g])
              + gsel_ref[g, pl.ds(pl.multiple_of(x0 + off + h * SEL_HALF, Q_TILE), SEL_HALF), :]
              for h in halves] for g in groups]
        carry = list(carry)
        for h in halves:
            for g in groups:
                v_t = vst_ref[0, g, i][:, h * SEL_HALF:(h + 1) * SEL_HALF]
                carry[g] = _softmax_step(carry[g], s[g][h], v_t)
        return tuple(carry)

    n_sel_steps = (c * Q_TILE + Q_TILE + SEL_KEYS - 1) // SEL_KEYS
    sel = lax.fori_loop(0, n_sel_steps, sel_body, (init,) * C_KV_GROUPS)

    outs = []
    for g in groups:
        o_sel = sel[g][2] / sel[g][1]
        cols = []
        for r in range(C_HPG):
            ln = slice(r * Q_TILE, (r + 1) * Q_TILE)
            cols.append(gate_ref[0, 0, g, r:r + 1, :] * o_cmp[g][:, ln]
                        + gate_ref[0, 1, g, r:r + 1, :] * o_sel[:, ln]
                        + gate_ref[0, 2, g, r:r + 1, :] * o_win[g][:, ln])
        outs += [jnp.concatenate(cols[2 * k:2 * k + 2], axis=0).T for k in range(C_HPG // 2)]
    o_ref[0] = jnp.concatenate(outs, axis=1).astype(o_ref.dtype)


def _with_mask_rows(qt, c, imp):
    nsel = imp.shape[0]
    j = lax.broadcasted_iota(jnp.int32, (nsel, Q_TILE), 0)
    blk = c * (Q_TILE // SEL_LEN) + lax.shift_right_logical(
        lax.broadcasted_iota(jnp.int32, (nsel, Q_TILE), 1), SEL_SHIFT)
    forced = (j == 0) | (j == blk) | (j == blk - 1)
    score = jnp.where(j <= blk, jnp.where(forced, FORCE_SCORE, imp), -1.0)
    groups = range(nsel // SUBLANES)
    rows = [score[k * SUBLANES:(k + 1) * SUBLANES] for k in groups]
    j_rows = j[:SUBLANES]
    ranks = [jnp.zeros((SUBLANES, Q_TILE), jnp.int32) for _ in groups]
    for jp in range(nsel):
        other = score[jp:jp + 1, :]
        for k in groups:
            if k * SUBLANES > jp:
                beats = jnp.where(other >= rows[k], 1, 0)
            elif (k + 1) * SUBLANES <= jp:
                beats = jnp.where(other > rows[k], 1, 0)
            else:
                beats = jnp.where(j_rows > jp - k * SUBLANES,
                                  jnp.where(other >= rows[k], 1, 0), jnp.where(other > rows[k], 1, 0))
            ranks[k] = ranks[k] + beats
    rank = jnp.concatenate(ranks, axis=0)
    mask_rows = jnp.where((rank < SEL_TOP) & (score >= 0.0), 0.0, NEG).astype(BF16)
    return jnp.concatenate([qt, jnp.concatenate([mask_rows] * C_HPG, axis=1)], axis=0)


def _nsa(qt, kc, vct, ks, vst, kw, vwt, gates, gsel, gwin, gcmp, ovt):
    b, G, n_tiles = qt.shape[:3]
    s = n_tiles * Q_TILE
    per_b = lambda a: pl.BlockSpec((1,) + a.shape[1:], lambda bi, c: (bi,) + (0,) * (a.ndim - 1))
    return pl.pallas_call(
        functools.partial(_nsa_kernel, n_tiles=n_tiles),
        grid=(b, n_tiles),
        in_specs=[pl.BlockSpec((1, G, 1, HEAD_DIM, QL), lambda bi, c: (bi, 0, c, 0, 0)),
                  per_b(kc), per_b(vct), per_b(ks), per_b(vst), per_b(kw), per_b(vwt),
                  pl.BlockSpec((1, 3, G, C_HPG, Q_TILE), lambda bi, c: (bi, 0, 0, 0, c)),
                  _resident(gsel.shape), _resident(gwin.shape), _resident(gcmp.shape),
                  _resident(ovt.shape)],
        out_specs=pl.BlockSpec((1, Q_TILE, C_WIDTH), lambda bi, c: (bi, c, 0)),
        out_shape=jax.ShapeDtypeStruct((b, s, C_WIDTH), BF16),
        compiler_params=_cparams(2),
        name="nsa",
    )(qt, kc, vct, ks, vst, kw, vwt, gates, gsel, gwin, gcmp, ovt)


def _outproj_kernel(h_ref, ya_ref, yb_ref, yc_ref, wa_ref, wb_ref, wc_ref, g_ref, o_ref):
    mix = _dot(ya_ref[...], wa_ref[...]) + _dot(yb_ref[...], wb_ref[...]) + _dot(yc_ref[...], wc_ref[...])
    o_ref[...] = h_ref[...] + _rms(mix, g_ref[...])


def _outproj(h, ya, yb, yc, w_out, g_norm):
    n, d = h.shape
    w = w_out.astype(BF16)
    wa, wb, wc = w[:A_WIDTH], w[A_WIDTH:A_WIDTH + B_WIDTH], w[A_WIDTH + B_WIDTH:]
    tile = lambda width: pl.BlockSpec((TOKEN_TILE, width), lambda i: (i, 0))
    return pl.pallas_call(
        _outproj_kernel,
        grid=(n // TOKEN_TILE,),
        in_specs=[tile(d), tile(A_WIDTH), tile(B_WIDTH), tile(C_WIDTH),
                  _resident(wa.shape), _resident(wb.shape), _resident(wc.shape), _resident((1, d))],
        out_specs=tile(d),
        out_shape=jax.ShapeDtypeStruct((n, d), F32),
        compiler_params=_cparams(1),
        name="outproj",
    )(h, ya, yb, yc, wa, wb, wc, g_norm.reshape(1, d))


def _ple_kernel(h_ref, p_ref, gpre_ref, gpost_ref, wg_ref, wp_ref, o_ref):
    x = h_ref[...]
    gate = _sigmoid(_dot(_rms(x, gpre_ref[...]).astype(BF16), wg_ref[...]))
    emb = _dot(p_ref[...].astype(BF16), wp_ref[...])
    o_ref[...] = x + _rms(gate * emb, gpost_ref[...])


def _ple(h, p, g_pre, g_post, w_gate, w_proj):
    n, d = h.shape
    dp = p.shape[-1]
    tile = lambda width: pl.BlockSpec((TOKEN_TILE, width), lambda i: (i, 0))
    return pl.pallas_call(
        _ple_kernel,
        grid=(n // TOKEN_TILE,),
        in_specs=[tile(d), tile(dp), _resident((1, d)), _resident((1, d)),
                  _resident((d, d)), _resident((dp, d))],
        out_specs=tile(d),
        out_shape=jax.ShapeDtypeStruct((n, d), F32),
        compiler_params=_cparams(1),
        name="ple",
    )(h, p, g_pre.reshape(1, d), g_post.reshape(1, d), w_gate.astype(BF16), w_proj.astype(BF16))


def _overlap_t(s):
    ncp = s // CMP_STRIDE
    n_cmp = (s - CMP_LEN) // CMP_STRIDE + 1
    cs = jnp.arange(ncp) * CMP_STRIDE
    ss = jnp.arange(s // SEL_LEN) * SEL_LEN
    ov = jnp.clip(jnp.minimum(cs[None] + CMP_LEN, ss[:, None] + SEL_LEN)
                  - jnp.maximum(cs[None], ss[:, None]), 0, None).astype(F32) / CMP_LEN
    ov = jnp.where(jnp.arange(ncp)[None] < n_cmp, ov, 0.0).astype(BF16)
    return jnp.pad(ov, ((0, SEL_SLOTS - s // SEL_LEN), (0, 0)))


def kernel(x, p, rel_bias, norm_g, ffn_w_gate, ffn_w_up, ffn_w_down, w_in, w_out, sgu_norm_g, sgu_w, sgu_b,
           conv_w, conv_b, lru_wa, lru_ba, lru_wx, lru_bx, lru_lambda, cmp_pos, cmp_w1, cmp_b1, cmp_w2,
           cmp_b2, ple_w_gate, ple_w_proj):
    b, s, d = x.shape
    depth = norm_g.shape[0]
    assert s % TOKEN_TILE == 0 and s % SCAN_TILE == 0 and s % SEL_KEYS == 0
    assert s >= WINDOW + Q_TILE and SEL_TOP <= s // SEL_LEN <= SEL_SLOTS
    n_tiles = s // Q_TILE

    rbx = jnp.repeat(rel_bias.reshape(N_BUCKETS, C_KV_GROUPS, C_HPG).transpose(1, 0, 2), Q_TILE, axis=2)
    no_limit = 1 << 30
    gsel = _bias_table(rbx, s + SEL_KEYS - Q_TILE, 1, s - Q_TILE, no_limit)
    gwin = _bias_table(rbx, 2 * WINDOW + Q_TILE, 1, WINDOW, WINDOW)
    per_tile = Q_TILE // CMP_STRIDE
    gcmp = _bias_table(rbx, per_tile * (n_tiles - 1) + s // CMP_STRIDE, CMP_STRIDE,
                       CMP_STRIDE * per_tile * (n_tiles - 1) - (CMP_LEN - 1), no_limit)
    ovt = _overlap_t(s)

    h = x.reshape(b * s, d)
    for i in range(depth):
        g = norm_g[i]
        h = _ffn(h, g[0], g[1], ffn_w_gate[i, 0], ffn_w_up[i, 0], ffn_w_down[i, 0])
        ya, bx, qt, raw, ks, kw, vst, vwt, gates = _inproj(
            h.reshape(b, s, d), g[2], w_in[i], sgu_norm_g[i], sgu_w[i], sgu_b[i])
        yb = _rglru(bx, conv_w[i], conv_b[i], lru_wa[i], lru_ba[i], lru_wx[i], lru_bx[i], lru_lambda[i])
        kc, kct = _compress(raw, cmp_pos[i], cmp_w1[i], cmp_b1[i], cmp_w2[i], cmp_b2[i])
        yc = _nsa(qt, kc[0], kct[1], ks, vst, kw, vwt, gates, gsel, gwin, gcmp, ovt)
        h = _outproj(h, ya.reshape(b * s, -1), yb.reshape(b * s, -1), yc.reshape(b * s, -1), w_out[i], g[3])
        h = _ffn(h, g[4], g[5], ffn_w_gate[i, 1], ffn_w_up[i, 1], ffn_w_down[i, 1])
        h = _ple(h, p[i].reshape(b * s, -1), g[6], g[7], ple_w_gate[i], ple_w_proj[i])
    return h.reshape(b, s, d)
```

```python
import functools
import math

import jax
import jax.numpy as jnp
from jax import lax
from jax.experimental import pallas as pl
from jax.experimental.pallas import tpu as pltpu

F32 = jnp.float32
BF16 = jnp.bfloat16

RMS_EPS = 1e-6
A_GROUPS = 4
A_WIDTH = 256
A_CHUNK = 128
B_GROUPS = 4
B_WIDTH = 256
CONV_W = 4
LRU_C = 8.0
C_HEADS = 8
C_KV_GROUPS = 2
C_HPG = C_HEADS // C_KV_GROUPS
HEAD_DIM = 64
C_WIDTH = C_HEADS * HEAD_DIM
KV_W = C_KV_GROUPS * HEAD_DIM
CMP_LEN = 32
CMP_STRIDE = 16
SEL_LEN = 64
SEL_SHIFT = 6
SEL_SLOTS = 64
SEL_TOP = 16
WINDOW = 512
FORCE_SCORE = 1e4
NEG = -1e30
N_BUCKETS = 32
MAX_DISTANCE = 1024

LANES = 128
TOKEN_TILE = 512
FFN_CHUNK = 256
SCAN_TILE = 512
SUBLANES = 8
Q_TILE = 128
SEL_KEYS = 512
SEL_HALF = 256
WIN_KEYS = 128
QL = C_HPG * Q_TILE
TABLE_ROWS = 128
VMEM_LIMIT = 56 * 1024 * 1024


def _cparams(n_axes):
    return pltpu.CompilerParams(dimension_semantics=("arbitrary",) * n_axes,
                                vmem_limit_bytes=VMEM_LIMIT)


def _resident(shape):
    nd = len(shape)
    return pl.BlockSpec(shape, lambda *_: (0,) * nd, pipeline_mode=pl.Buffered(1))


def _rms(x, g):
    return x * lax.rsqrt(jnp.mean(x * x, axis=-1, keepdims=True) + RMS_EPS) * g


def _sigmoid(x):
    return 1.0 / (1.0 + jnp.exp(-x))


def _dot(a, b):
    return jnp.dot(a, b, preferred_element_type=F32)


def _bias_table_kernel(rbx_ref, o_ref, *, stride, offset, dmax):
    i = pl.program_id(1)
    shape = (TABLE_ROWS, QL)
    x = lax.broadcasted_iota(jnp.int32, shape, 0) + i * TABLE_ROWS
    t = lax.broadcasted_iota(jnp.int32, shape, 1) & (Q_TILE - 1)
    d = t - stride * x + offset
    n = jnp.maximum(d, 0)
    max_exact = N_BUCKETS // 2
    nf = jnp.maximum(n, max_exact).astype(F32)
    large = max_exact + (jnp.log(nf / max_exact) / math.log(MAX_DISTANCE / max_exact)
                         * (N_BUCKETS - max_exact)).astype(jnp.int32)
    large = jnp.minimum(large, N_BUCKETS - 1)
    bucket = jnp.where(n < max_exact, n, large)
    acc = jnp.zeros(shape, F32)
    for k in range(N_BUCKETS):
        acc = jnp.where(bucket == k, rbx_ref[0, k:k + 1, :], acc)
    o_ref[0] = jnp.where((d >= 0) & (d < dmax), acc, NEG)


def _bias_table(rbx, rows, stride, offset, dmax):
    rows_p = -(-rows // TABLE_ROWS) * TABLE_ROWS
    return pl.pallas_call(
        functools.partial(_bias_table_kernel, stride=stride, offset=offset, dmax=dmax),
        grid=(C_KV_GROUPS, rows_p // TABLE_ROWS),
        in_specs=[pl.BlockSpec((1, N_BUCKETS, QL), lambda g, i: (g, 0, 0))],
        out_specs=pl.BlockSpec((1, TABLE_ROWS, QL), lambda g, i: (g, i, 0)),
        out_shape=jax.ShapeDtypeStruct((C_KV_GROUPS, rows_p, QL), F32),
        compiler_params=_cparams(2),
        name="bias_table",
    )(rbx)


def _ffn_kernel(h_ref, gpre_ref, gpost_ref, wg_ref, wu_ref, wd_ref, o_ref, acc_ref):
    x = h_ref[...]
    xn = _rms(x, gpre_ref[...]).astype(BF16)
    nch = wg_ref.shape[0]
    gate, up = _dot(xn, wg_ref[0]), _dot(xn, wu_ref[0])
    for j in range(nch):
        if j + 1 < nch:
            gate_next, up_next = _dot(xn, wg_ref[j + 1]), _dot(xn, wu_ref[j + 1])
        hid = (gate * _sigmoid(gate) * up).astype(BF16)
        down = _dot(hid, wd_ref[j])
        if j == 0:
            acc_ref[...] = down
        else:
            acc_ref[...] += down
        gate, up = gate_next, up_next
    o_ref[...] = x + 0.5 * _rms(acc_ref[...], gpost_ref[...])


def _ffn(h, g_pre, g_post, wg, wu, wd):
    n, d = h.shape
    d_ff = wg.shape[1]
    nch = d_ff // FFN_CHUNK
    wg3 = wg.astype(BF16).reshape(d, nch, FFN_CHUNK).transpose(1, 0, 2)
    wu3 = wu.astype(BF16).reshape(d, nch, FFN_CHUNK).transpose(1, 0, 2)
    wd3 = wd.astype(BF16).reshape(nch, FFN_CHUNK, d)
    tile = pl.BlockSpec((TOKEN_TILE, d), lambda i: (i, 0))
    return pl.pallas_call(
        _ffn_kernel,
        grid=(n // TOKEN_TILE,),
        in_specs=[tile, _resident((1, d)), _resident((1, d)),
                  _resident(wg3.shape), _resident(wu3.shape), _resident(wd3.shape)],
        out_specs=tile,
        out_shape=jax.ShapeDtypeStruct((n, d), F32),
        scratch_shapes=[pltpu.VMEM((TOKEN_TILE, d), F32)],
        compiler_params=_cparams(1),
        name="ffn",
    )(h, g_pre.reshape(1, d), g_post.reshape(1, d), wg3, wu3, wd3)


def _inproj_kernel(h_ref, g_ref, wa_ref, wb_ref, wq_ref, wkv_ref, wgt_ref, sgn_ref, sgw_ref, sgb_ref,
                   ya_ref, bx_ref, qt_ref, raw_ref, ks_ref, kw_ref, vst_ref, vwt_ref, gt_ref):
    xn = _rms(h_ref[0], g_ref[...]).astype(BF16)

    za = _dot(xn, wa_ref[...])
    u = jax.nn.gelu(za[:, :A_WIDTH])
    v = _rms(jax.nn.gelu(za[:, A_WIDTH:]), sgn_ref[...]).astype(BF16)
    row = lax.broadcasted_iota(jnp.int32, (A_CHUNK, A_CHUNK), 0)
    col = lax.broadcasted_iota(jnp.int32, (A_CHUNK, A_CHUNK), 1)
    lane_group = lax.shift_right_logical(lax.broadcasted_iota(jnp.int32, (A_CHUNK, A_WIDTH), 1),
                                          (A_WIDTH // A_GROUPS).bit_length() - 1)
    w_tril = [jnp.where(row >= col, sgw_ref[g], 0.0).astype(BF16) for g in range(A_GROUPS)]
    for c in range(TOKEN_TILE // A_CHUNK):
        rows = slice(c * A_CHUNK, (c + 1) * A_CHUNK)
        mixed = jnp.zeros((A_CHUNK, A_WIDTH), F32)
        for g in range(A_GROUPS):
            mixed = jnp.where(lane_group == g, _dot(w_tril[g], v[rows]), mixed)
        ya_ref[0, rows, :] = (u[rows] * (mixed + sgb_ref[...])).astype(ya_ref.dtype)

    bx_ref[0] = _dot(xn, wb_ref[...])

    zq_t = (_dot(xn, wq_ref[...]) * (HEAD_DIM ** -0.5)).T
    for g in range(C_KV_GROUPS):
        for c in range(TOKEN_TILE // Q_TILE):
            parts = []
            for r in range(C_HPG):
                base = (g * C_HPG + r) * HEAD_DIM
                parts.append(zq_t[base:base + HEAD_DIM, c * Q_TILE:(c + 1) * Q_TILE])
            qt_ref[0, g, c] = jnp.concatenate(parts, axis=1).astype(qt_ref.dtype)

    zkv = _dot(xn, wkv_ref[...])
    vs_t = zkv[:, 3 * KV_W:4 * KV_W].T
    vw_t = zkv[:, 5 * KV_W:6 * KV_W].T
    key_blk = lax.shift_right_logical(
        lax.broadcasted_iota(jnp.int32, (TOKEN_TILE, SEL_SLOTS), 0) + pl.program_id(1) * TOKEN_TILE, SEL_SHIFT)
    blk_onehot = jnp.where(key_blk == lax.broadcasted_iota(jnp.int32, (TOKEN_TILE, SEL_SLOTS), 1), 1.0, 0.0)
    for g in range(C_KV_GROUPS):
        lo, hi = g * HEAD_DIM, (g + 1) * HEAD_DIM
        raw_ref[0, 0, g] = zkv[:, lo:hi]
        raw_ref[1, 0, g] = zkv[:, KV_W + lo:KV_W + hi]
        ks_ref[0, g] = jnp.concatenate([zkv[:, 2 * KV_W + lo:2 * KV_W + hi], blk_onehot],
                                       axis=1).astype(ks_ref.dtype)
        kw_ref[0, g] = zkv[:, 4 * KV_W + lo:4 * KV_W + hi].astype(kw_ref.dtype)
        for c in range(TOKEN_TILE // SEL_KEYS):
            vst_ref[0, g, c] = vs_t[lo:hi, c * SEL_KEYS:(c + 1) * SEL_KEYS].astype(vst_ref.dtype)
        for c in range(TOKEN_TILE // WIN_KEYS):
            vwt_ref[0, g, c] = vw_t[lo:hi, c * WIN_KEYS:(c + 1) * WIN_KEYS].astype(vwt_ref.dtype)

    sg_t = _sigmoid(_dot(xn, wgt_ref[...])).T
    for br in range(3):
        for g in range(C_KV_GROUPS):
            base = br * C_HEADS + g * C_HPG
            gt_ref[0, br, g] = sg_t[base:base + C_HPG, :]


def _inproj(h, g_norm, w_in, sgu_norm_g, sgu_w, sgu_b):
    b, s, d = h.shape
    wb16 = w_in.astype(BF16)
    o = 0
    wa = wb16[:, o:o + 2 * A_WIDTH]; o += 2 * A_WIDTH
    wb = wb16[:, o:o + 2 * B_WIDTH]; o += 2 * B_WIDTH
    wq = wb16[:, o:o + C_WIDTH]; o += C_WIDTH
    wkv = wb16[:, o:o + 6 * KV_W]; o += 6 * KV_W
    wgt = jnp.pad(wb16[:, o:o + 3 * C_HEADS], ((0, 0), (0, LANES - 3 * C_HEADS)))
    sgb = jnp.repeat(sgu_b.T, A_WIDTH // A_GROUPS, axis=1)
    nt = s // TOKEN_TILE
    grid = (b, nt)
    G = C_KV_GROUPS
    out_shape = [
        jax.ShapeDtypeStruct((b, s, A_WIDTH), BF16),
        jax.ShapeDtypeStruct((b, s, 2 * B_WIDTH), F32),
        jax.ShapeDtypeStruct((b, G, s // Q_TILE, HEAD_DIM, QL), BF16),
        jax.ShapeDtypeStruct((2, b, G, s, HEAD_DIM), F32),
        jax.ShapeDtypeStruct((b, G, s, HEAD_DIM + SEL_SLOTS), BF16),
        jax.ShapeDtypeStruct((b, G, s, HEAD_DIM), BF16),
        jax.ShapeDtypeStruct((b, G, s // SEL_KEYS, HEAD_DIM, SEL_KEYS), BF16),
        jax.ShapeDtypeStruct((b, G, s // WIN_KEYS, HEAD_DIM, WIN_KEYS), BF16),
        jax.ShapeDtypeStruct((b, 3, G, C_HPG, s), F32),
    ]
    out_specs = [
        pl.BlockSpec((1, TOKEN_TILE, A_WIDTH), lambda bi, i: (bi, i, 0)),
        pl.BlockSpec((1, TOKEN_TILE, 2 * B_WIDTH), lambda bi, i: (bi, i, 0)),
        pl.BlockSpec((1, G, TOKEN_TILE // Q_TILE, HEAD_DIM, QL), lambda bi, i: (bi, 0, i, 0, 0)),
        pl.BlockSpec((2, 1, G, TOKEN_TILE, HEAD_DIM), lambda bi, i: (0, bi, 0, i, 0)),
        pl.BlockSpec((1, G, TOKEN_TILE, HEAD_DIM + SEL_SLOTS), lambda bi, i: (bi, 0, i, 0)),
        pl.BlockSpec((1, G, TOKEN_TILE, HEAD_DIM), lambda bi, i: (bi, 0, i, 0)),
        pl.BlockSpec((1, G, TOKEN_TILE // SEL_KEYS, HEAD_DIM, SEL_KEYS), lambda bi, i: (bi, 0, i, 0, 0)),
        pl.BlockSpec((1, G, TOKEN_TILE // WIN_KEYS, HEAD_DIM, WIN_KEYS), lambda bi, i: (bi, 0, i, 0, 0)),
        pl.BlockSpec((1, 3, G, C_HPG, TOKEN_TILE), lambda bi, i: (bi, 0, 0, 0, i)),
    ]
    in_specs = [
        pl.BlockSpec((1, TOKEN_TILE, d), lambda bi, i: (bi, i, 0)),
        _resident((1, d)), _resident(wa.shape), _resident(wb.shape), _resident(wq.shape),
        _resident(wkv.shape), _resident(wgt.shape), _resident((1, A_WIDTH)),
        _resident(sgu_w.shape), _resident(sgb.shape),
    ]
    return pl.pallas_call(
        _inproj_kernel, grid=grid, in_specs=in_specs, out_specs=out_specs, out_shape=out_shape,
        compiler_params=_cparams(2), name="inproj",
    )(h, g_norm.reshape(1, d), wa, wb, wq, wkv, wgt, sgu_norm_g.reshape(1, A_WIDTH), sgu_w, sgb)


def _rglru_kernel(bx_ref, cw_ref, cb_ref, wa_ref, ba_ref, wx_ref, bxb_ref, lam_ref, o_ref,
                  tail_ref, h_ref, a_ref, b_ref):
    @pl.when(pl.program_id(1) == 0)
    def _():
        tail_ref[...] = jnp.zeros_like(tail_ref)
        h_ref[...] = jnp.zeros_like(h_ref)

    xb = bx_ref[0, :, :B_WIDTH]
    gate = bx_ref[0, :, B_WIDTH:]
    ext = jnp.concatenate([tail_ref[...], xb], axis=0)
    xc = cb_ref[...] + xb * cw_ref[CONV_W - 1:CONV_W, :]
    for k in range(CONV_W - 1):
        shift = CONV_W - 1 - k
        xc = xc + ext[SUBLANES - shift:SUBLANES - shift + SCAN_TILE] * cw_ref[k:k + 1, :]
    tail_ref[...] = xb[SCAN_TILE - SUBLANES:]

    xcb = xc.astype(BF16)
    r = _sigmoid(_dot(xcb, wa_ref[...]) + ba_ref[...])
    i = _sigmoid(_dot(xcb, wx_ref[...]) + bxb_ref[...])
    z = -lam_ref[...]
    e = jnp.exp(-jnp.abs(z))
    softplus = jnp.maximum(z, 0.0) + jnp.log1p(e)
    log_a = -LRU_C * r * softplus
    a = jnp.exp(log_a)
    b = jnp.sqrt(jnp.tanh(-log_a) * (a * a + 1.0)) * (i * xc)

    row = lax.broadcasted_iota(jnp.int32, a.shape, 0) & (SUBLANES - 1)
    for dist in (1, 2, 4):
        a_prev = jnp.where(row >= dist, pltpu.roll(a, dist, 0), 1.0)
        b_prev = jnp.where(row >= dist, pltpu.roll(b, dist, 0), 0.0)
        b = a * b_prev + b
        a = a * a_prev
    a_ref[...] = a
    b_ref[...] = b

    def body(k, h):
        off = pl.multiple_of(k * SUBLANES, SUBLANES)
        rows = pl.ds(off, SUBLANES)
        hs = b_ref[rows, :] + a_ref[rows, :] * h
        b_ref[rows, :] = hs
        return jnp.broadcast_to(hs[SUBLANES - 1:SUBLANES, :], hs.shape)

    h_ref[...] = lax.fori_loop(0, SCAN_TILE // SUBLANES, body, h_ref[...])
    o_ref[0] = (b_ref[...] * jax.nn.gelu(gate)).astype(o_ref.dtype)


def _block_diag(w):
    g, n, _ = w.shape
    out = jnp.zeros((g * n, g * n), w.dtype)
    for k in range(g):
        out = out.at[k * n:(k + 1) * n, k * n:(k + 1) * n].set(w[k])
    return out


def _rglru(bx, conv_w, conv_b, wa, ba, wx, bxb, lam):
    b, s, _ = bx.shape
    w = B_WIDTH
    row = lambda v: v.reshape(1, w)
    return pl.pallas_call(
        _rglru_kernel,
        grid=(b, s // SCAN_TILE),
        in_specs=[pl.BlockSpec((1, SCAN_TILE, 2 * w), lambda bi, i: (bi, i, 0)),
                  _resident((CONV_W, w)), _resident((1, w)), _resident((w, w)), _resident((1, w)),
                  _resident((w, w)), _resident((1, w)), _resident((1, w))],
        out_specs=pl.BlockSpec((1, SCAN_TILE, w), lambda bi, i: (bi, i, 0)),
        out_shape=jax.ShapeDtypeStruct((b, s, w), BF16),
        scratch_shapes=[pltpu.VMEM((SUBLANES, w), F32), pltpu.VMEM((SUBLANES, w), F32),
                        pltpu.VMEM((SCAN_TILE, w), F32), pltpu.VMEM((SCAN_TILE, w), F32)],
        compiler_params=_cparams(2),
        name="rglru",
    )(bx, conv_w, row(conv_b), _block_diag(wa).astype(BF16), row(ba),
      _block_diag(wx).astype(BF16), row(bxb), row(lam))


def _compress_kernel(raw_ref, pos_ref, w1_ref, b1_ref, w2_ref, b2_ref, w2t_ref, b2c_ref, kc_ref, kct_ref):
    g, ncp, half = raw_ref.shape[2:]
    x = raw_ref[0, 0].reshape(g * ncp, half)
    top = (x + pos_ref[0, :, :half]).astype(BF16)
    bot = (x + pos_ref[0, :, half:]).astype(BF16)
    u = _dot(top, w1_ref[0, :half, :])
    v = _dot(bot, w1_ref[0, half:, :])
    hid = jax.nn.gelu(u + pltpu.roll(v, g * ncp - 1, 0) + b1_ref[0]).astype(BF16)
    kc_ref[0, 0] = (_dot(hid, w2_ref[0]) + b2_ref[0]).reshape(g, ncp, HEAD_DIM).astype(kc_ref.dtype)
    for gi in range(g):
        t = lax.dot_general(w2t_ref[0], hid[gi * ncp:(gi + 1) * ncp], (((1,), (1,)), ((), ())),
                            preferred_element_type=F32)
        kct_ref[0, 0, gi] = (t + b2c_ref[0]).astype(kct_ref.dtype)


def _compress(raw, cmp_pos, cmp_w1, cmp_b1, cmp_w2, cmp_b2):
    _, b, g, s, hd = raw.shape
    ncp = s // CMP_STRIDE
    half = CMP_STRIDE * hd
    hid = cmp_w1.shape[-1]
    raw16 = raw.reshape(2, b, g, ncp, half)
    sel = lambda *shape: pl.BlockSpec((1,) + shape, lambda kv, bi: (kv,) + (0,) * len(shape))
    return pl.pallas_call(
        _compress_kernel,
        grid=(2, b),
        in_specs=[pl.BlockSpec((1, 1, g, ncp, half), lambda kv, bi: (kv, bi, 0, 0, 0)),
                  sel(1, 2 * half), sel(2 * half, hid), sel(1, hid), sel(hid, hd), sel(1, hd),
                  sel(hd, hid), sel(hd, 1)],
        out_specs=[pl.BlockSpec((1, 1, g, ncp, hd), lambda kv, bi: (kv, bi, 0, 0, 0)),
                   pl.BlockSpec((1, 1, g, hd, ncp), lambda kv, bi: (kv, bi, 0, 0, 0))],
        out_shape=[jax.ShapeDtypeStruct((2, b, g, ncp, hd), BF16),
                   jax.ShapeDtypeStruct((2, b, g, hd, ncp), BF16)],
        compiler_params=_cparams(2),
        name="compress",
    )(raw16, cmp_pos.reshape(2, 1, 2 * half), cmp_w1.astype(BF16), cmp_b1.reshape(2, 1, hid),
      cmp_w2.astype(BF16), cmp_b2.reshape(2, 1, hd),
      cmp_w2.astype(BF16).transpose(0, 2, 1), cmp_b2.reshape(2, hd, 1))


def _softmax_step(carry, s, v_t):
    m, l, acc = carry
    m_new = jnp.maximum(m, jnp.max(s, axis=0, keepdims=True))
    alpha = jnp.exp(m - m_new)
    p = jnp.exp(s - m_new)
    l = l * alpha + jnp.sum(p, axis=0, keepdims=True)
    acc = acc * alpha + _dot(v_t, p.astype(BF16))
    return m_new, l, acc


def _nsa_kernel(qt_ref, kc_ref, vct_ref, ks_ref, vst_ref, kw_ref, vwt_ref, gate_ref,
                gsel_ref, gwin_ref, gcmp_ref, ovt_ref, o_ref, *, n_tiles):
    c = pl.program_id(1)
    groups = range(C_KV_GROUPS)
    ncp = kc_ref.shape[2]
    qts = [qt_ref[0, g, 0] for g in groups]

    y0 = pl.multiple_of((n_tiles - 1 - c) * (Q_TILE // CMP_STRIDE), SUBLANES)
    back = WINDOW // WIN_KEYS
    n_win = WINDOW + Q_TILE
    first = jnp.maximum(c - back, 0)
    start = pl.multiple_of(first * WIN_KEYS, WIN_KEYS)
    rel = pl.multiple_of(jnp.maximum(back - c, 0) * WIN_KEYS, WIN_KEYS)
    s_cmp = [_dot(kc_ref[0, g], qts[g]) + gcmp_ref[g, pl.ds(y0, ncp), :] for g in groups]
    s_win = [_dot(kw_ref[0, g, pl.ds(start, n_win), :], qts[g]) + gwin_ref[g, pl.ds(rel, n_win), :]
             for g in groups]

    tq = lax.broadcasted_iota(jnp.int32, (1, QL), 1) & (Q_TILE - 1)
    has_cmp = (c * Q_TILE + tq >= CMP_LEN - 1).astype(F32)
    o_cmp, imp = [], []
    for g in groups:
        e = jnp.exp(s_cmp[g] - jnp.max(s_cmp[g], axis=0, keepdims=True))
        p = e / jnp.sum(e, axis=0, keepdims=True) * has_cmp
        o_cmp.append(_dot(vct_ref[0, g], p.astype(BF16)))
        p_heads = p[:, 0:Q_TILE]
        for r in range(1, C_HPG):
            p_heads = p_heads + p[:, r * Q_TILE:(r + 1) * Q_TILE]
        p_hi = p_heads.astype(BF16)
        p_lo = (p_heads - p_hi.astype(F32)).astype(BF16)
        imp.append(_dot(ovt_ref[...], p_hi) + _dot(ovt_ref[...], p_lo))

    o_win = []
    for g in groups:
        e = jnp.exp(s_win[g] - jnp.max(s_win[g], axis=0, keepdims=True))
        l_win = jnp.sum(e, axis=0, keepdims=True)
        e = e.astype(BF16)
        acc = jnp.zeros((HEAD_DIM, QL), F32)
        for i in range(n_win // WIN_KEYS):
            acc = acc + _dot(vwt_ref[0, g, first + i], e[i * WIN_KEYS:(i + 1) * WIN_KEYS])
        o_win.append(acc / l_win)

    q_aug = [_with_mask_rows(qts[g], c, imp[g]) for g in groups]
    x0 = (n_tiles - 1 - c) * Q_TILE
    halves = range(SEL_KEYS // SEL_HALF)
    init = (jnp.full((1, QL), NEG, F32), jnp.zeros((1, QL), F32), jnp.zeros((HEAD_DIM, QL), F32))

    def sel_body(i, carry):
        off = pl.multiple_of(i * SEL_KEYS, SEL_KEYS)
        s = [[_dot(ks_ref[0, g, pl.ds(off + h * SEL_HALF, SEL_HALF), :], q_aug[g])
              + gsel_ref[g, pl.ds(pl.multiple_of(x0 + off + h * SEL_HALF, Q_TILE), SEL_HALF), :]
              for h in halves] for g in groups]
        carry = list(carry)
        for h in halves:
            for g in groups:
                v_t = vst_ref[0, g, i][:, h * SEL_HALF:(h + 1) * SEL_HALF]
                carry[g] = _softmax_step(carry[g], s[g][h], v_t)
        return tuple(carry)

    n_sel_steps = (c * Q_TILE + Q_TILE + SEL_KEYS - 1) // SEL_KEYS
    sel = lax.fori_loop(0, n_sel_steps, sel_body, (init,) * C_KV_GROUPS)

    outs = []
    for g in groups:
        o_sel = sel[g][2] / sel[g][1]
        cols = []
        for r in range(C_HPG):
            ln = slice(r * Q_TILE, (r + 1) * Q_TILE)
            cols.append(gate_ref[0, 0, g, r:r + 1, :] * o_cmp[g][:, ln]
                        + gate_ref[0, 1, g, r:r + 1, :] * o_sel[:, ln]
                        + gate_ref[0, 2, g, r:r + 1, :] * o_win[g][:, ln])
        outs += [jnp.concatenate(cols[2 * k:2 * k + 2], axis=0).T for k in range(C_HPG // 2)]
    o_ref[0] = jnp.concatenate(outs, axis=1).astype(o_ref.dtype)


def _with_mask_rows(qt, c, imp):
    nsel = imp.shape[0]
    j = lax.broadcasted_iota(jnp.int32, (nsel, Q_TILE), 0)
    blk = c * (Q_TILE // SEL_LEN) + lax.shift_right_logical(
        lax.broadcasted_iota(jnp.int32, (nsel, Q_TILE), 1), SEL_SHIFT)
    forced = (j == 0) | (j == blk) | (j == blk - 1)
    score = jnp.where(j <= blk, jnp.where(forced, FORCE_SCORE, imp), -1.0)
    groups = range(nsel // SUBLANES)
    rows = [score[k * SUBLANES:(k + 1) * SUBLANES] for k in groups]
    j_rows = j[:SUBLANES]
    ranks = [jnp.zeros((SUBLANES, Q_TILE), jnp.int32) for _ in groups]
    for jp in range(nsel):
        other = score[jp:jp + 1, :]
        for k in groups:
            if k * SUBLANES > jp:
                beats = jnp.where(other >= rows[k], 1, 0)
            elif (k + 1) * SUBLANES <= jp:
                beats = jnp.where(other > rows[k], 1, 0)
            else:
                beats = jnp.where(j_rows > jp - k * SUBLANES,
                                  jnp.where(other >= rows[k], 1, 0), jnp.where(other > rows[k], 1, 0))
            ranks[k] = ranks[k] + beats
    rank = jnp.concatenate(ranks, axis=0)
    mask_rows = jnp.where((rank < SEL_TOP) & (score >= 0.0), 0.0, NEG).astype(BF16)
    return jnp.concatenate([qt, jnp.concatenate([mask_rows] * C_HPG, axis=1)], axis=0)


def _nsa(qt, kc, vct, ks, vst, kw, vwt, gates, gsel, gwin, gcmp, ovt):
    b, G, n_tiles = qt.shape[:3]
    s = n_tiles * Q_TILE
    per_b = lambda a: pl.BlockSpec((1,) + a.shape[1:], lambda bi, c: (bi,) + (0,) * (a.ndim - 1))
    return pl.pallas_call(
        functools.partial(_nsa_kernel, n_tiles=n_tiles),
        grid=(b, n_tiles),
        in_specs=[pl.BlockSpec((1, G, 1, HEAD_DIM, QL), lambda bi, c: (bi, 0, c, 0, 0)),
                  per_b(kc), per_b(vct), per_b(ks), per_b(vst), per_b(kw), per_b(vwt),
                  pl.BlockSpec((1, 3, G, C_HPG, Q_TILE), lambda bi, c: (bi, 0, 0, 0, c)),
                  _resident(gsel.shape), _resident(gwin.shape), _resident(gcmp.shape),
                  _resident(ovt.shape)],
        out_specs=pl.BlockSpec((1, Q_TILE, C_WIDTH), lambda bi, c: (bi, c, 0)),
        out_shape=jax.ShapeDtypeStruct((b, s, C_WIDTH), BF16),
        compiler_params=_cparams(2),
        name="nsa",
    )(qt, kc, vct, ks, vst, kw, vwt, gates, gsel, gwin, gcmp, ovt)


def _outproj_kernel(h_ref, ya_ref, yb_ref, yc_ref, wa_ref, wb_ref, wc_ref, g_ref, o_ref):
    mix = _dot(ya_ref[...], wa_ref[...]) + _dot(yb_ref[...], wb_ref[...]) + _dot(yc_ref[...], wc_ref[...])
    o_ref[...] = h_ref[...] + _rms(mix, g_ref[...])


def _outproj(h, ya, yb, yc, w_out, g_norm):
    n, d = h.shape
    w = w_out.astype(BF16)
    wa, wb, wc = w[:A_WIDTH], w[A_WIDTH:A_WIDTH + B_WIDTH], w[A_WIDTH + B_WIDTH:]
    tile = lambda width: pl.BlockSpec((TOKEN_TILE, width), lambda i: (i, 0))
    return pl.pallas_call(
        _outproj_kernel,
        grid=(n // TOKEN_TILE,),
        in_specs=[tile(d), tile(A_WIDTH), tile(B_WIDTH), tile(C_WIDTH),
                  _resident(wa.shape), _resident(wb.shape), _resident(wc.shape), _resident((1, d))],
        out_specs=tile(d),
        out_shape=jax.ShapeDtypeStruct((n, d), F32),
        compiler_params=_cparams(1),
        name="outproj",
    )(h, ya, yb, yc, wa, wb, wc, g_norm.reshape(1, d))


def _ple_kernel(h_ref, p_ref, gpre_ref, gpost_ref, wg_ref, wp_ref, o_ref):
    x = h_ref[...]
    gate = _sigmoid(_dot(_rms(x, gpre_ref[...]).astype(BF16), wg_ref[...]))
    emb = _dot(p_ref[...].astype(BF16), wp_ref[...])
    o_ref[...] = x + _rms(gate * emb, gpost_ref[...])


def _ple(h, p, g_pre, g_post, w_gate, w_proj):
    n, d = h.shape
    dp = p.shape[-1]
    tile = lambda width: pl.BlockSpec((TOKEN_TILE, width), lambda i: (i, 0))
    return pl.pallas_call(
        _ple_kernel,
        grid=(n // TOKEN_TILE,),
        in_specs=[tile(d), tile(dp), _resident((1, d)), _resident((1, d)),
                  _resident((d, d)), _resident((dp, d))],
        out_specs=tile(d),
        out_shape=jax.ShapeDtypeStruct((n, d), F32),
        compiler_params=_cparams(1),
        name="ple",
    )(h, p, g_pre.reshape(1, d), g_post.reshape(1, d), w_gate.astype(BF16), w_proj.astype(BF16))


def _overlap_t(s):
    ncp = s // CMP_STRIDE
    n_cmp = (s - CMP_LEN) // CMP_STRIDE + 1
    cs = jnp.arange(ncp) * CMP_STRIDE
    ss = jnp.arange(s // SEL_LEN) * SEL_LEN
    ov = jnp.clip(jnp.minimum(cs[None] + CMP_LEN, ss[:, None] + SEL_LEN)
                  - jnp.maximum(cs[None], ss[:, None]), 0, None).astype(F32) / CMP_LEN
    ov = jnp.where(jnp.arange(ncp)[None] < n_cmp, ov, 0.0).astype(BF16)
    return jnp.pad(ov, ((0, SEL_SLOTS - s // SEL_LEN), (0, 0)))


def kernel(x, p, rel_bias, norm_g, ffn_w_gate, ffn_w_up, ffn_w_down, w_in, w_out, sgu_norm_g, sgu_w, sgu_b,
           conv_w, conv_b, lru_wa, lru_ba, lru_wx, lru_bx, lru_lambda, cmp_pos, cmp_w1, cmp_b1, cmp_w2,
           cmp_b2, ple_w_gate, ple_w_proj):
    b, s, d = x.shape
    depth = norm_g.shape[0]
    assert s % TOKEN_TILE == 0 and s % SCAN_TILE == 0 and s % SEL_KEYS == 0
    assert s >= WINDOW + Q_TILE and SEL_TOP <= s // SEL_LEN <= SEL_SLOTS
    n_tiles = s // Q_TILE

    rbx = jnp.repeat(rel_bias.reshape(N_BUCKETS, C_KV_GROUPS, C_HPG).transpose(1, 0, 2), Q_TILE, axis=2)
    no_limit = 1 << 30
    gsel = _bias_table(rbx, s + SEL_KEYS - Q_TILE, 1, s - Q_TILE, no_limit)
    gwin = _bias_table(rbx, 2 * WINDOW + Q_TILE, 1, WINDOW, WINDOW)
    per_tile = Q_TILE // CMP_STRIDE
    gcmp = _bias_table(rbx, per_tile * (n_tiles - 1) + s // CMP_STRIDE, CMP_STRIDE,
                       CMP_STRIDE * per_tile * (n_tiles - 1) - (CMP_LEN - 1), no_limit)
    ovt = _overlap_t(s)

    h = x.reshape(b * s, d)
    for i in range(depth):
        g = norm_g[i]
        h = _ffn(h, g[0], g[1], ffn_w_gate[i, 0], ffn_w_up[i, 0], ffn_w_down[i, 0])
        ya, bx, qt, raw, ks, kw, vst, vwt, gates = _inproj(
            h.reshape(b, s, d), g[2], w_in[i], sgu_norm_g[i], sgu_w[i], sgu_b[i])
        yb = _rglru(bx, conv_w[i], conv_b[i], lru_wa[i], lru_ba[i], lru_wx[i], lru_bx[i], lru_lambda[i])
        kc, kct = _compress(raw, cmp_pos[i], cmp_w1[i], cmp_b1[i], cmp_w2[i], cmp_b2[i])
        yc = _nsa(qt, kc[0], kct[1], ks, vst, kw, vwt, gates, gsel, gwin, gcmp, ovt)
        h = _outproj(h, ya.reshape(b * s, -1), yb.reshape(b * s, -1), yc.reshape(b * s, -1), w_out[i], g[3])
        h = _ffn(h, g[4], g[5], ffn_w_gate[i, 1], ffn_w_up[i, 1], ffn_w_down[i, 1])
        h = _ple(h, p[i].reshape(b * s, -1), g[6], g[7], ple_w_gate[i], ple_w_proj[i])
    return h.reshape(b, s, d)
```

```python
import functools
import math

import jax
import jax.numpy as jnp
from jax import lax
from jax.experimental import pallas as pl
from jax.experimental.pallas import tpu as pltpu

F32 = jnp.float32
BF16 = jnp.bfloat16

RMS_EPS = 1e-6
A_GROUPS = 4
A_WIDTH = 256
A_CHUNK = 128
B_GROUPS = 4
B_WIDTH = 256
CONV_W = 4
LRU_C = 8.0
C_HEADS = 8
C_KV_GROUPS = 2
C_HPG = C_HEADS // C_KV_GROUPS
HEAD_DIM = 64
C_WIDTH = C_HEADS * HEAD_DIM
KV_W = C_KV_GROUPS * HEAD_DIM
CMP_LEN = 32
CMP_STRIDE = 16
SEL_LEN = 64
SEL_SHIFT = 6
SEL_SLOTS = 64
SEL_TOP = 16
WINDOW = 512
FORCE_SCORE = 1e4
NEG = -1e30
N_BUCKETS = 32
MAX_DISTANCE = 1024

LANES = 128
TOKEN_TILE = 512
FFN_CHUNK = 256
SCAN_TILE = 512
SUBLANES = 8
Q_TILE = 128
TILES_PER_STEP = 2
SEL_KEYS = 512
SEL_HALF = 256
WIN_KEYS = 128
QL = C_HPG * Q_TILE
TABLE_ROWS = 128
VMEM_LIMIT = 56 * 1024 * 1024


def _cparams(n_axes):
    return pltpu.CompilerParams(dimension_semantics=("arbitrary",) * n_axes,
                                vmem_limit_bytes=VMEM_LIMIT)


def _resident(shape):
    nd = len(shape)
    return pl.BlockSpec(shape, lambda *_: (0,) * nd, pipeline_mode=pl.Buffered(1))


def _rms(x, g):
    return x * lax.rsqrt(jnp.mean(x * x, axis=-1, keepdims=True) + RMS_EPS) * g


def _sigmoid(x):
    return 1.0 / (1.0 + jnp.exp(-x))


def _dot(a, b):
    return jnp.dot(a, b, preferred_element_type=F32)


def _bias_table_kernel(rbx_ref, o_ref, *, stride, offset, dmax):
    i = pl.program_id(1)
    shape = (TABLE_ROWS, QL)
    x = lax.broadcasted_iota(jnp.int32, shape, 0) + i * TABLE_ROWS
    t = lax.broadcasted_iota(jnp.int32, shape, 1) & (Q_TILE - 1)
    d = t - stride * x + offset
    n = jnp.maximum(d, 0)
    max_exact = N_BUCKETS // 2
    nf = jnp.maximum(n, max_exact).astype(F32)
    large = max_exact + (jnp.log(nf / max_exact) / math.log(MAX_DISTANCE / max_exact)
                         * (N_BUCKETS - max_exact)).astype(jnp.int32)
    large = jnp.minimum(large, N_BUCKETS - 1)
    bucket = jnp.where(n < max_exact, n, large)
    acc = jnp.zeros(shape, F32)
    for k in range(N_BUCKETS):
        acc = jnp.where(bucket == k, rbx_ref[0, k:k + 1, :], acc)
    o_ref[0] = jnp.where((d >= 0) & (d < dmax), acc, NEG)


def _bias_table(rbx, rows, stride, offset, dmax):
    rows_p = -(-rows // TABLE_ROWS) * TABLE_ROWS
    return pl.pallas_call(
        functools.partial(_bias_table_kernel, stride=stride, offset=offset, dmax=dmax),
        grid=(C_KV_GROUPS, rows_p // TABLE_ROWS),
        in_specs=[pl.BlockSpec((1, N_BUCKETS, QL), lambda g, i: (g, 0, 0))],
        out_specs=pl.BlockSpec((1, TABLE_ROWS, QL), lambda g, i: (g, i, 0)),
        out_shape=jax.ShapeDtypeStruct((C_KV_GROUPS, rows_p, QL), F32),
        compiler_params=_cparams(2),
        name="bias_table",
    )(rbx)


def _ffn_kernel(h_ref, gpre_ref, gpost_ref, wg_ref, wu_ref, wd_ref, o_ref, acc_ref):
    x = h_ref[...]
    xn = _rms(x, gpre_ref[...]).astype(BF16)
    nch = wg_ref.shape[0]
    gate, up = _dot(xn, wg_ref[0]), _dot(xn, wu_ref[0])
    for j in range(nch):
        if j + 1 < nch:
            gate_next, up_next = _dot(xn, wg_ref[j + 1]), _dot(xn, wu_ref[j + 1])
        hid = (gate * _sigmoid(gate) * up).astype(BF16)
        down = _dot(hid, wd_ref[j])
        if j == 0:
            acc_ref[...] = down
        else:
            acc_ref[...] += down
        gate, up = gate_next, up_next
    o_ref[...] = x + 0.5 * _rms(acc_ref[...], gpost_ref[...])


def _ffn(h, g_pre, g_post, wg, wu, wd):
    n, d = h.shape
    d_ff = wg.shape[1]
    nch = d_ff // FFN_CHUNK
    wg3 = wg.astype(BF16).reshape(d, nch, FFN_CHUNK).transpose(1, 0, 2)
    wu3 = wu.astype(BF16).reshape(d, nch, FFN_CHUNK).transpose(1, 0, 2)
    wd3 = wd.astype(BF16).reshape(nch, FFN_CHUNK, d)
    tile = pl.BlockSpec((TOKEN_TILE, d), lambda i: (i, 0))
    return pl.pallas_call(
        _ffn_kernel,
        grid=(n // TOKEN_TILE,),
        in_specs=[tile, _resident((1, d)), _resident((1, d)),
                  _resident(wg3.shape), _resident(wu3.shape), _resident(wd3.shape)],
        out_specs=tile,
        out_shape=jax.ShapeDtypeStruct((n, d), F32),
        scratch_shapes=[pltpu.VMEM((TOKEN_TILE, d), F32)],
        compiler_params=_cparams(1),
        name="ffn",
    )(h, g_pre.reshape(1, d), g_post.reshape(1, d), wg3, wu3, wd3)


def _inproj_kernel(h_ref, g_ref, wa_ref, wb_ref, wq_ref, wkv_ref, wgt_ref, sgn_ref, sgw_ref, sgb_ref,
                   ya_ref, bx_ref, qt_ref, raw_ref, ks_ref, kw_ref, vst_ref, vwt_ref, gt_ref):
    xn = _rms(h_ref[0], g_ref[...]).astype(BF16)

    za = _dot(xn, wa_ref[...])
    u = jax.nn.gelu(za[:, :A_WIDTH])
    v = _rms(jax.nn.gelu(za[:, A_WIDTH:]), sgn_ref[...]).astype(BF16)
    row = lax.broadcasted_iota(jnp.int32, (A_CHUNK, A_CHUNK), 0)
    col = lax.broadcasted_iota(jnp.int32, (A_CHUNK, A_CHUNK), 1)
    lane_group = lax.shift_right_logical(lax.broadcasted_iota(jnp.int32, (A_CHUNK, A_WIDTH), 1),
                                          (A_WIDTH // A_GROUPS).bit_length() - 1)
    w_tril = [jnp.where(row >= col, sgw_ref[g], 0.0).astype(BF16) for g in range(A_GROUPS)]
    for c in range(TOKEN_TILE // A_CHUNK):
        rows = slice(c * A_CHUNK, (c + 1) * A_CHUNK)
        mixed = jnp.zeros((A_CHUNK, A_WIDTH), F32)
        for g in range(A_GROUPS):
            mixed = jnp.where(lane_group == g, _dot(w_tril[g], v[rows]), mixed)
        ya_ref[0, rows, :] = (u[rows] * (mixed + sgb_ref[...])).astype(ya_ref.dtype)

    bx_ref[0] = _dot(xn, wb_ref[...])

    zq_t = (_dot(xn, wq_ref[...]) * (HEAD_DIM ** -0.5)).T
    for g in range(C_KV_GROUPS):
        for c in range(TOKEN_TILE // Q_TILE):
            parts = []
            for r in range(C_HPG):
                base = (g * C_HPG + r) * HEAD_DIM
                parts.append(zq_t[base:base + HEAD_DIM, c * Q_TILE:(c + 1) * Q_TILE])
            qt_ref[0, g, c] = jnp.concatenate(parts, axis=1).astype(qt_ref.dtype)

    zkv = _dot(xn, wkv_ref[...])
    vs_t = zkv[:, 3 * KV_W:4 * KV_W].T
    vw_t = zkv[:, 5 * KV_W:6 * KV_W].T
    key_blk = lax.shift_right_logical(
        lax.broadcasted_iota(jnp.int32, (TOKEN_TILE, SEL_SLOTS), 0) + pl.program_id(1) * TOKEN_TILE, SEL_SHIFT)
    blk_onehot = jnp.where(key_blk == lax.broadcasted_iota(jnp.int32, (TOKEN_TILE, SEL_SLOTS), 1), 1.0, 0.0)
    for g in range(C_KV_GROUPS):
        lo, hi = g * HEAD_DIM, (g + 1) * HEAD_DIM
        raw_ref[0, 0, g] = zkv[:, lo:hi]
        raw_ref[1, 0, g] = zkv[:, KV_W + lo:KV_W + hi]
        ks_ref[0, g] = jnp.concatenate([zkv[:, 2 * KV_W + lo:2 * KV_W + hi], blk_onehot],
                                       axis=1).astype(ks_ref.dtype)
        kw_ref[0, g] = zkv[:, 4 * KV_W + lo:4 * KV_W + hi].astype(kw_ref.dtype)
        for c in range(TOKEN_TILE // SEL_KEYS):
            vst_ref[0, g, c] = vs_t[lo:hi, c * SEL_KEYS:(c + 1) * SEL_KEYS].astype(vst_ref.dtype)
        for c in range(TOKEN_TILE // WIN_KEYS):
            vwt_ref[0, g, c] = vw_t[lo:hi, c * WIN_KEYS:(c + 1) * WIN_KEYS].astype(vwt_ref.dtype)

    sg_t = _sigmoid(_dot(xn, wgt_ref[...])).T
    for br in range(3):
        for g in range(C_KV_GROUPS):
            base = br * C_HEADS + g * C_HPG
            gt_ref[0, br, g] = sg_t[base:base + C_HPG, :]


def _inproj(h, g_norm, w_in, sgu_norm_g, sgu_w, sgu_b):
    b, s, d = h.shape
    wb16 = w_in.astype(BF16)
    o = 0
    wa = wb16[:, o:o + 2 * A_WIDTH]; o += 2 * A_WIDTH
    wb = wb16[:, o:o + 2 * B_WIDTH]; o += 2 * B_WIDTH
    wq = wb16[:, o:o + C_WIDTH]; o += C_WIDTH
    wkv = wb16[:, o:o + 6 * KV_W]; o += 6 * KV_W
    wgt = jnp.pad(wb16[:, o:o + 3 * C_HEADS], ((0, 0), (0, LANES - 3 * C_HEADS)))
    sgb = jnp.repeat(sgu_b.T, A_WIDTH // A_GROUPS, axis=1)
    nt = s // TOKEN_TILE
    grid = (b, nt)
    G = C_KV_GROUPS
    out_shape = [
        jax.ShapeDtypeStruct((b, s, A_WIDTH), BF16),
        jax.ShapeDtypeStruct((b, s, 2 * B_WIDTH), F32),
        jax.ShapeDtypeStruct((b, G, s // Q_TILE, HEAD_DIM, QL), BF16),
        jax.ShapeDtypeStruct((2, b, G, s, HEAD_DIM), F32),
        jax.ShapeDtypeStruct((b, G, s, HEAD_DIM + SEL_SLOTS), BF16),
        jax.ShapeDtypeStruct((b, G, s, HEAD_DIM), BF16),
        jax.ShapeDtypeStruct((b, G, s // SEL_KEYS, HEAD_DIM, SEL_KEYS), BF16),
        jax.ShapeDtypeStruct((b, G, s // WIN_KEYS, HEAD_DIM, WIN_KEYS), BF16),
        jax.ShapeDtypeStruct((b, 3, G, C_HPG, s), F32),
    ]
    out_specs = [
        pl.BlockSpec((1, TOKEN_TILE, A_WIDTH), lambda bi, i: (bi, i, 0)),
        pl.BlockSpec((1, TOKEN_TILE, 2 * B_WIDTH), lambda bi, i: (bi, i, 0)),
        pl.BlockSpec((1, G, TOKEN_TILE // Q_TILE, HEAD_DIM, QL), lambda bi, i: (bi, 0, i, 0, 0)),
        pl.BlockSpec((2, 1, G, TOKEN_TILE, HEAD_DIM), lambda bi, i: (0, bi, 0, i, 0)),
        pl.BlockSpec((1, G, TOKEN_TILE, HEAD_DIM + SEL_SLOTS), lambda bi, i: (bi, 0, i, 0)),
        pl.BlockSpec((1, G, TOKEN_TILE, HEAD_DIM), lambda bi, i: (bi, 0, i, 0)),
        pl.BlockSpec((1, G, TOKEN_TILE // SEL_KEYS, HEAD_DIM, SEL_KEYS), lambda bi, i: (bi, 0, i, 0, 0)),
        pl.BlockSpec((1, G, TOKEN_TILE // WIN_KEYS, HEAD_DIM, WIN_KEYS), lambda bi, i: (bi, 0, i, 0, 0)),
        pl.BlockSpec((1, 3, G, C_HPG, TOKEN_TILE), lambda bi, i: (bi, 0, 0, 0, i)),
    ]
    in_specs = [
        pl.BlockSpec((1, TOKEN_TILE, d), lambda bi, i: (bi, i, 0)),
        _resident((1, d)), _resident(wa.shape), _resident(wb.shape), _resident(wq.shape),
        _resident(wkv.shape), _resident(wgt.shape), _resident((1, A_WIDTH)),
        _resident(sgu_w.shape), _resident(sgb.shape),
    ]
    return pl.pallas_call(
        _inproj_kernel, grid=grid, in_specs=in_specs, out_specs=out_specs, out_shape=out_shape,
        compiler_params=_cparams(2), name="inproj",
    )(h, g_norm.reshape(1, d), wa, wb, wq, wkv, wgt, sgu_norm_g.reshape(1, A_WIDTH), sgu_w, sgb)


def _rglru_kernel(bx_ref, cw_ref, cb_ref, wa_ref, ba_ref, wx_ref, bxb_ref, lam_ref, o_ref,
                  tail_ref, h_ref, a_ref, b_ref):
    @pl.when(pl.program_id(1) == 0)
    def _():
        tail_ref[...] = jnp.zeros_like(tail_ref)
        h_ref[...] = jnp.zeros_like(h_ref)

    xb = bx_ref[0, :, :B_WIDTH]
    gate = bx_ref[0, :, B_WIDTH:]
    ext = jnp.concatenate([tail_ref[...], xb], axis=0)
    xc = cb_ref[...] + xb * cw_ref[CONV_W - 1:CONV_W, :]
    for k in range(CONV_W - 1):
        shift = CONV_W - 1 - k
        xc = xc + ext[SUBLANES - shift:SUBLANES - shift + SCAN_TILE] * cw_ref[k:k + 1, :]
    tail_ref[...] = xb[SCAN_TILE - SUBLANES:]

    xcb = xc.astype(BF16)
    r = _sigmoid(_dot(xcb, wa_ref[...]) + ba_ref[...])
    i = _sigmoid(_dot(xcb, wx_ref[...]) + bxb_ref[...])
    z = -lam_ref[...]
    e = jnp.exp(-jnp.abs(z))
    softplus = jnp.maximum(z, 0.0) + jnp.log1p(e)
    log_a = -LRU_C * r * softplus
    a = jnp.exp(log_a)
    b = jnp.sqrt(jnp.tanh(-log_a) * (a * a + 1.0)) * (i * xc)

    row = lax.broadcasted_iota(jnp.int32, a.shape, 0) & (SUBLANES - 1)
    for dist in (1, 2, 4):
        a_prev = jnp.where(row >= dist, pltpu.roll(a, dist, 0), 1.0)
        b_prev = jnp.where(row >= dist, pltpu.roll(b, dist, 0), 0.0)
        b = a * b_prev + b
        a = a * a_prev
    a_ref[...] = a
    b_ref[...] = b

    def body(k, h):
        off = pl.multiple_of(k * SUBLANES, SUBLANES)
        rows = pl.ds(off, SUBLANES)
        hs = b_ref[rows, :] + a_ref[rows, :] * h
        b_ref[rows, :] = hs
        return jnp.broadcast_to(hs[SUBLANES - 1:SUBLANES, :], hs.shape)

    h_ref[...] = lax.fori_loop(0, SCAN_TILE // SUBLANES, body, h_ref[...])
    o_ref[0] = (b_ref[...] * jax.nn.gelu(gate)).astype(o_ref.dtype)


def _block_diag(w):
    g, n, _ = w.shape
    out = jnp.zeros((g * n, g * n), w.dtype)
    for k in range(g):
        out = out.at[k * n:(k + 1) * n, k * n:(k + 1) * n].set(w[k])
    return out


def _rglru(bx, conv_w, conv_b, wa, ba, wx, bxb, lam):
    b, s, _ = bx.shape
    w = B_WIDTH
    row = lambda v: v.reshape(1, w)
    return pl.pallas_call(
        _rglru_kernel,
        grid=(b, s // SCAN_TILE),
        in_specs=[pl.BlockSpec((1, SCAN_TILE, 2 * w), lambda bi, i: (bi, i, 0)),
                  _resident((CONV_W, w)), _resident((1, w)), _resident((w, w)), _resident((1, w)),
                  _resident((w, w)), _resident((1, w)), _resident((1, w))],
        out_specs=pl.BlockSpec((1, SCAN_TILE, w), lambda bi, i: (bi, i, 0)),
        out_shape=jax.ShapeDtypeStruct((b, s, w), BF16),
        scratch_shapes=[pltpu.VMEM((SUBLANES, w), F32), pltpu.VMEM((SUBLANES, w), F32),
                        pltpu.VMEM((SCAN_TILE, w), F32), pltpu.VMEM((SCAN_TILE, w), F32)],
        compiler_params=_cparams(2),
        name="rglru",
    )(bx, conv_w, row(conv_b), _block_diag(wa).astype(BF16), row(ba),
      _block_diag(wx).astype(BF16), row(bxb), row(lam))


def _compress_kernel(raw_ref, pos_ref, w1_ref, b1_ref, w2_ref, b2_ref, w2t_ref, b2c_ref, kc_ref, kct_ref):
    g, ncp, half = raw_ref.shape[2:]
    x = raw_ref[0, 0].reshape(g * ncp, half)
    top = (x + pos_ref[0, :, :half]).astype(BF16)
    bot = (x + pos_ref[0, :, half:]).astype(BF16)
    u = _dot(top, w1_ref[0, :half, :])
    v = _dot(bot, w1_ref[0, half:, :])
    hid = jax.nn.gelu(u + pltpu.roll(v, g * ncp - 1, 0) + b1_ref[0]).astype(BF16)
    kc_ref[0, 0] = (_dot(hid, w2_ref[0]) + b2_ref[0]).reshape(g, ncp, HEAD_DIM).astype(kc_ref.dtype)
    for gi in range(g):
        t = lax.dot_general(w2t_ref[0], hid[gi * ncp:(gi + 1) * ncp], (((1,), (1,)), ((), ())),
                            preferred_element_type=F32)
        kct_ref[0, 0, gi] = (t + b2c_ref[0]).astype(kct_ref.dtype)


def _compress(raw, cmp_pos, cmp_w1, cmp_b1, cmp_w2, cmp_b2):
    _, b, g, s, hd = raw.shape
    ncp = s // CMP_STRIDE
    half = CMP_STRIDE * hd
    hid = cmp_w1.shape[-1]
    raw16 = raw.reshape(2, b, g, ncp, half)
    sel = lambda *shape: pl.BlockSpec((1,) + shape, lambda kv, bi: (kv,) + (0,) * len(shape))
    return pl.pallas_call(
        _compress_kernel,
        grid=(2, b),
        in_specs=[pl.BlockSpec((1, 1, g, ncp, half), lambda kv, bi: (kv, bi, 0, 0, 0)),
                  sel(1, 2 * half), sel(2 * half, hid), sel(1, hid), sel(hid, hd), sel(1, hd),
                  sel(hd, hid), sel(hd, 1)],
        out_specs=[pl.BlockSpec((1, 1, g, ncp, hd), lambda kv, bi: (kv, bi, 0, 0, 0)),
                   pl.BlockSpec((1, 1, g, hd, ncp), lambda kv, bi: (kv, bi, 0, 0, 0))],
        out_shape=[jax.ShapeDtypeStruct((2, b, g, ncp, hd), BF16),
                   jax.ShapeDtypeStruct((2, b, g, hd, ncp), BF16)],
        compiler_params=_cparams(2),
        name="compress",
    )(raw16, cmp_pos.reshape(2, 1, 2 * half), cmp_w1.astype(BF16), cmp_b1.reshape(2, 1, hid),
      cmp_w2.astype(BF16), cmp_b2.reshape(2, 1, hd),
      cmp_w2.astype(BF16).transpose(0, 2, 1), cmp_b2.reshape(2, hd, 1))


def _softmax_step(carry, s, v_t):
    m, l, acc = carry
    m_new = jnp.maximum(m, jnp.max(s, axis=0, keepdims=True))
    alpha = jnp.exp(m - m_new)
    p = jnp.exp(s - m_new)
    l = l * alpha + jnp.sum(p, axis=0, keepdims=True)
    acc = acc * alpha + _dot(v_t, p.astype(BF16))
    return m_new, l, acc


def _nsa_kernel(qt_ref, kc_ref, vct_ref, ks_ref, vst_ref, kw_ref, vwt_ref, gate_ref,
                gsel_ref, gwin_ref, gcmp_ref, ovt_ref, o_ref, *, n_tiles):
    step = pl.program_id(1)
    ncp = kc_ref.shape[2]
    chains = [(t, g) for t in range(TILES_PER_STEP) for g in range(C_KV_GROUPS)]
    tile = [step * TILES_PER_STEP + t for t in range(TILES_PER_STEP)]
    qts = [qt_ref[0, g, t] for t, g in chains]

    back = WINDOW // WIN_KEYS
    n_win = WINDOW + Q_TILE
    first = [jnp.maximum(c - back, 0) for c in tile]
    s_cmp, s_win = [], []
    for k, (t, g) in enumerate(chains):
        y0 = pl.multiple_of((n_tiles - 1 - tile[t]) * (Q_TILE // CMP_STRIDE), SUBLANES)
        s_cmp.append(_dot(kc_ref[0, g], qts[k]) + gcmp_ref[g, pl.ds(y0, ncp), :])
    for k, (t, g) in enumerate(chains):
        start = pl.multiple_of(first[t] * WIN_KEYS, WIN_KEYS)
        rel = pl.multiple_of(jnp.maximum(back - tile[t], 0) * WIN_KEYS, WIN_KEYS)
        s_win.append(_dot(kw_ref[0, g, pl.ds(start, n_win), :], qts[k]) + gwin_ref[g, pl.ds(rel, n_win), :])

    tq = lax.broadcasted_iota(jnp.int32, (1, QL), 1) & (Q_TILE - 1)
    o_cmp, imp = [], []
    for k, (t, g) in enumerate(chains):
        e = jnp.exp(s_cmp[k] - jnp.max(s_cmp[k], axis=0, keepdims=True))
        has_cmp = (tile[t] * Q_TILE + tq >= CMP_LEN - 1).astype(F32)
        p = e / jnp.sum(e, axis=0, keepdims=True) * has_cmp
        o_cmp.append(_dot(vct_ref[0, g], p.astype(BF16)))
        p_heads = p[:, 0:Q_TILE]
        for r in range(1, C_HPG):
            p_heads = p_heads + p[:, r * Q_TILE:(r + 1) * Q_TILE]
        p_hi = p_heads.astype(BF16)
        p_lo = (p_heads - p_hi.astype(F32)).astype(BF16)
        imp.append(_dot(ovt_ref[...], p_hi) + _dot(ovt_ref[...], p_lo))

    o_win = []
    for k, (t, g) in enumerate(chains):
        e = jnp.exp(s_win[k] - jnp.max(s_win[k], axis=0, keepdims=True))
        l_win = jnp.sum(e, axis=0, keepdims=True)
        e = e.astype(BF16)
        acc = jnp.zeros((HEAD_DIM, QL), F32)
        for i in range(n_win // WIN_KEYS):
            acc = acc + _dot(vwt_ref[0, g, first[t] + i], e[i * WIN_KEYS:(i + 1) * WIN_KEYS])
        o_win.append(acc / l_win)

    q_aug = [_with_mask_rows(qts[k], tile[t], imp[k]) for k, (t, g) in enumerate(chains)]
    halves = range(SEL_KEYS // SEL_HALF)
    init = (jnp.full((1, QL), NEG, F32), jnp.zeros((1, QL), F32), jnp.zeros((HEAD_DIM, QL), F32))

    def sel_body(i, carry):
        off = pl.multiple_of(i * SEL_KEYS, SEL_KEYS)
        s = []
        for h in halves:
            for k, (t, g) in enumerate(chains):
                x0 = (n_tiles - 1 - tile[t]) * Q_TILE
                s.append(_dot(ks_ref[0, g, pl.ds(off + h * SEL_HALF, SEL_HALF), :], q_aug[k])
                         + gsel_ref[g, pl.ds(pl.multiple_of(x0 + off + h * SEL_HALF, Q_TILE), SEL_HALF), :])
        carry = list(carry)
        for h in halves:
            for k, (t, g) in enumerate(chains):
                v_t = vst_ref[0, g, i][:, h * SEL_HALF:(h + 1) * SEL_HALF]
                carry[k] = _softmax_step(carry[k], s[h * len(chains) + k], v_t)
        return tuple(carry)

    n_sel_steps = (tile[-1] * Q_TILE + Q_TILE + SEL_KEYS - 1) // SEL_KEYS
    sel = lax.fori_loop(0, n_sel_steps, sel_body, (init,) * len(chains))

    for t in range(TILES_PER_STEP):
        outs = []
        for k, (tk, g) in enumerate(chains):
            if tk != t:
                continue
            o_sel = sel[k][2] / sel[k][1]
            tok = slice(t * Q_TILE, (t + 1) * Q_TILE)
            cols = []
            for r in range(C_HPG):
                ln = slice(r * Q_TILE, (r + 1) * Q_TILE)
                cols.append(gate_ref[0, 0, g, r:r + 1, tok] * o_cmp[k][:, ln]
                            + gate_ref[0, 1, g, r:r + 1, tok] * o_sel[:, ln]
                            + gate_ref[0, 2, g, r:r + 1, tok] * o_win[k][:, ln])
            outs += [jnp.concatenate(cols[2 * j:2 * j + 2], axis=0).T for j in range(C_HPG // 2)]
        o_ref[0, t * Q_TILE:(t + 1) * Q_TILE, :] = jnp.concatenate(outs, axis=1).astype(o_ref.dtype)


def _with_mask_rows(qt, c, imp):
    nsel = imp.shape[0]
    j = lax.broadcasted_iota(jnp.int32, (nsel, Q_TILE), 0)
    blk = c * (Q_TILE // SEL_LEN) + lax.shift_right_logical(
        lax.broadcasted_iota(jnp.int32, (nsel, Q_TILE), 1), SEL_SHIFT)
    forced = (j == 0) | (j == blk) | (j == blk - 1)
    score = jnp.where(j <= blk, jnp.where(forced, FORCE_SCORE, imp), -1.0)
    groups = range(nsel // SUBLANES)
    rows = [score[k * SUBLANES:(k + 1) * SUBLANES] for k in groups]
    j_rows = j[:SUBLANES]
    ranks = [jnp.zeros((SUBLANES, Q_TILE), jnp.int32) for _ in groups]
    for jp in range(nsel):
        other = score[jp:jp + 1, :]
        for k in groups:
            if k * SUBLANES > jp:
                beats = jnp.where(other >= rows[k], 1, 0)
            elif (k + 1) * SUBLANES <= jp:
                beats = jnp.where(other > rows[k], 1, 0)
            else:
                beats = jnp.where(j_rows > jp - k * SUBLANES,
                                  jnp.where(other >= rows[k], 1, 0), jnp.where(other > rows[k], 1, 0))
            ranks[k] = ranks[k] + beats
    rank = jnp.concatenate(ranks, axis=0)
    mask_rows = jnp.where((rank < SEL_TOP) & (score >= 0.0), 0.0, NEG).astype(BF16)
    return jnp.concatenate([qt, jnp.concatenate([mask_rows] * C_HPG, axis=1)], axis=0)


def _nsa(qt, kc, vct, ks, vst, kw, vwt, gates, gsel, gwin, gcmp, ovt):
    b, G, n_tiles = qt.shape[:3]
    s = n_tiles * Q_TILE
    per_b = lambda a: pl.BlockSpec((1,) + a.shape[1:], lambda bi, c: (bi,) + (0,) * (a.ndim - 1))
    return pl.pallas_call(
        functools.partial(_nsa_kernel, n_tiles=n_tiles),
        grid=(b, n_tiles // TILES_PER_STEP),
        in_specs=[pl.BlockSpec((1, G, TILES_PER_STEP, HEAD_DIM, QL), lambda bi, c: (bi, 0, c, 0, 0)),
                  per_b(kc), per_b(vct), per_b(ks), per_b(vst), per_b(kw), per_b(vwt),
                  pl.BlockSpec((1, 3, G, C_HPG, TILES_PER_STEP * Q_TILE), lambda bi, c: (bi, 0, 0, 0, c)),
                  _resident(gsel.shape), _resident(gwin.shape), _resident(gcmp.shape),
                  _resident(ovt.shape)],
        out_specs=pl.BlockSpec((1, TILES_PER_STEP * Q_TILE, C_WIDTH), lambda bi, c: (bi, c, 0)),
        out_shape=jax.ShapeDtypeStruct((b, s, C_WIDTH), BF16),
        compiler_params=_cparams(2),
        name="nsa",
    )(qt, kc, vct, ks, vst, kw, vwt, gates, gsel, gwin, gcmp, ovt)


def _outproj_kernel(h_ref, ya_ref, yb_ref, yc_ref, wa_ref, wb_ref, wc_ref, g_ref, o_ref):
    mix = _dot(ya_ref[...], wa_ref[...]) + _dot(yb_ref[...], wb_ref[...]) + _dot(yc_ref[...], wc_ref[...])
    o_ref[...] = h_ref[...] + _rms(mix, g_ref[...])


def _outproj(h, ya, yb, yc, w_out, g_norm):
    n, d = h.shape
    w = w_out.astype(BF16)
    wa, wb, wc = w[:A_WIDTH], w[A_WIDTH:A_WIDTH + B_WIDTH], w[A_WIDTH + B_WIDTH:]
    tile = lambda width: pl.BlockSpec((TOKEN_TILE, width), lambda i: (i, 0))
    return pl.pallas_call(
        _outproj_kernel,
        grid=(n // TOKEN_TILE,),
        in_specs=[tile(d), tile(A_WIDTH), tile(B_WIDTH), tile(C_WIDTH),
                  _resident(wa.shape), _resident(wb.shape), _resident(wc.shape), _resident((1, d))],
        out_specs=tile(d),
        out_shape=jax.ShapeDtypeStruct((n, d), F32),
        compiler_params=_cparams(1),
        name="outproj",
    )(h, ya, yb, yc, wa, wb, wc, g_norm.reshape(1, d))


def _ple_kernel(h_ref, p_ref, gpre_ref, gpost_ref, wg_ref, wp_ref, o_ref):
    x = h_ref[...]
    gate = _sigmoid(_dot(_rms(x, gpre_ref[...]).astype(BF16), wg_ref[...]))
    emb = _dot(p_ref[...].astype(BF16), wp_ref[...])
    o_ref[...] = x + _rms(gate * emb, gpost_ref[...])


def _ple(h, p, g_pre, g_post, w_gate, w_proj):
    n, d = h.shape
    dp = p.shape[-1]
    tile = lambda width: pl.BlockSpec((TOKEN_TILE, width), lambda i: (i, 0))
    return pl.pallas_call(
        _ple_kernel,
        grid=(n // TOKEN_TILE,),
        in_specs=[tile(d), tile(dp), _resident((1, d)), _resident((1, d)),
                  _resident((d, d)), _resident((dp, d))],
        out_specs=tile(d),
        out_shape=jax.ShapeDtypeStruct((n, d), F32),
        compiler_params=_cparams(1),
        name="ple",
    )(h, p, g_pre.reshape(1, d), g_post.reshape(1, d), w_gate.astype(BF16), w_proj.astype(BF16))


def _overlap_t(s):
    ncp = s // CMP_STRIDE
    n_cmp = (s - CMP_LEN) // CMP_STRIDE + 1
    cs = jnp.arange(ncp) * CMP_STRIDE
    ss = jnp.arange(s // SEL_LEN) * SEL_LEN
    ov = jnp.clip(jnp.minimum(cs[None] + CMP_LEN, ss[:, None] + SEL_LEN)
                  - jnp.maximum(cs[None], ss[:, None]), 0, None).astype(F32) / CMP_LEN
    ov = jnp.where(jnp.arange(ncp)[None] < n_cmp, ov, 0.0).astype(BF16)
    return jnp.pad(ov, ((0, SEL_SLOTS - s // SEL_LEN), (0, 0)))


def kernel(x, p, rel_bias, norm_g, ffn_w_gate, ffn_w_up, ffn_w_down, w_in, w_out, sgu_norm_g, sgu_w, sgu_b,
           conv_w, conv_b, lru_wa, lru_ba, lru_wx, lru_bx, lru_lambda, cmp_pos, cmp_w1, cmp_b1, cmp_w2,
           cmp_b2, ple_w_gate, ple_w_proj):
    b, s, d = x.shape
    depth = norm_g.shape[0]
    assert s % TOKEN_TILE == 0 and s % SCAN_TILE == 0 and s % SEL_KEYS == 0
    assert s % (TILES_PER_STEP * Q_TILE) == 0
    assert s >= WINDOW + Q_TILE and SEL_TOP <= s // SEL_LEN <= SEL_SLOTS
    n_tiles = s // Q_TILE

    rbx = jnp.repeat(rel_bias.reshape(N_BUCKETS, C_KV_GROUPS, C_HPG).transpose(1, 0, 2), Q_TILE, axis=2)
    no_limit = 1 << 30
    gsel = _bias_table(rbx, s + SEL_KEYS, 1, s - Q_TILE, no_limit)
    gwin = _bias_table(rbx, 2 * WINDOW + Q_TILE, 1, WINDOW, WINDOW)
    per_tile = Q_TILE // CMP_STRIDE
    gcmp = _bias_table(rbx, per_tile * (n_tiles - 1) + s // CMP_STRIDE, CMP_STRIDE,
                       CMP_STRIDE * per_tile * (n_tiles - 1) - (CMP_LEN - 1), no_limit)
    ovt = _overlap_t(s)

    h = x.reshape(b * s, d)
    for i in range(depth):
        g = norm_g[i]
        h = _ffn(h, g[0], g[1], ffn_w_gate[i, 0], ffn_w_up[i, 0], ffn_w_down[i, 0])
        ya, bx, qt, raw, ks, kw, vst, vwt, gates = _inproj(
            h.reshape(b, s, d), g[2], w_in[i], sgu_norm_g[i], sgu_w[i], sgu_b[i])
        yb = _rglru(bx, conv_w[i], conv_b[i], lru_wa[i], lru_ba[i], lru_wx[i], lru_bx[i], lru_lambda[i])
        kc, kct = _compress(raw, cmp_pos[i], cmp_w1[i], cmp_b1[i], cmp_w2[i], cmp_b2[i])
        yc = _nsa(qt, kc[0], kct[1], ks, vst, kw, vwt, gates, gsel, gwin, gcmp, ovt)
        h = _outproj(h, ya.reshape(b * s, -1), yb.reshape(b * s, -1), yc.reshape(b * s, -1), w_out[i], g[3])
        h = _ffn(h, g[4], g[5], ffn_w_gate[i, 1], ffn_w_up[i, 1], ffn_w_down[i, 1])
        h = _ple(h, p[i].reshape(b * s, -1), g[6], g[7], ple_w_gate[i], ple_w_proj[i])
    return h.reshape(b, s, d)
```

```python
import functools
import math

import jax
import jax.numpy as jnp
from jax import lax
from jax.experimental import pallas as pl
from jax.experimental.pallas import tpu as pltpu

F32 = jnp.float32
BF16 = jnp.bfloat16

RMS_EPS = 1e-6
A_GROUPS = 4
A_WIDTH = 256
A_CHUNK = 128
B_GROUPS = 4
B_WIDTH = 256
CONV_W = 4
LRU_C = 8.0
C_HEADS = 8
C_KV_GROUPS = 2
C_HPG = C_HEADS // C_KV_GROUPS
HEAD_DIM = 64
C_WIDTH = C_HEADS * HEAD_DIM
KV_W = C_KV_GROUPS * HEAD_DIM
CMP_LEN = 32
CMP_STRIDE = 16
SEL_LEN = 64
SEL_SHIFT = 6
SEL_SLOTS = 64
SEL_TOP = 16
WINDOW = 512
FORCE_SCORE = 1e4
NEG = -1e30
N_BUCKETS = 32
MAX_DISTANCE = 1024

LANES = 128
TOKEN_TILE = 512
FFN_CHUNK = 256
SCAN_TILE = 512
SUBLANES = 8
Q_TILE = 128
TILES_PER_STEP = 2
SEL_KEYS = 512
SEL_HALF = 256
WIN_KEYS = 128
QL = C_HPG * Q_TILE
V_PAD_ROWS = 16
V_ROWS = HEAD_DIM + V_PAD_ROWS
LOG2E = math.log2(math.e)
TABLE_ROWS = 128
VMEM_LIMIT = 56 * 1024 * 1024


def _cparams(n_axes):
    return pltpu.CompilerParams(dimension_semantics=("arbitrary",) * n_axes,
                                vmem_limit_bytes=VMEM_LIMIT)


def _resident(shape):
    nd = len(shape)
    return pl.BlockSpec(shape, lambda *_: (0,) * nd, pipeline_mode=pl.Buffered(1))


def _rms(x, g):
    return x * lax.rsqrt(jnp.mean(x * x, axis=-1, keepdims=True) + RMS_EPS) * g


def _sigmoid(x):
    return 1.0 / (1.0 + jnp.exp(-x))


def _dot(a, b):
    return jnp.dot(a, b, preferred_element_type=F32)


def _bias_table_kernel(rbx_ref, o_ref, *, stride, offset, dmax):
    i = pl.program_id(1)
    shape = (TABLE_ROWS, QL)
    x = lax.broadcasted_iota(jnp.int32, shape, 0) + i * TABLE_ROWS
    t = lax.broadcasted_iota(jnp.int32, shape, 1) & (Q_TILE - 1)
    d = t - stride * x + offset
    n = jnp.maximum(d, 0)
    max_exact = N_BUCKETS // 2
    nf = jnp.maximum(n, max_exact).astype(F32)
    large = max_exact + (jnp.log(nf / max_exact) / math.log(MAX_DISTANCE / max_exact)
                         * (N_BUCKETS - max_exact)).astype(jnp.int32)
    large = jnp.minimum(large, N_BUCKETS - 1)
    bucket = jnp.where(n < max_exact, n, large)
    acc = jnp.zeros(shape, F32)
    for k in range(N_BUCKETS):
        acc = jnp.where(bucket == k, rbx_ref[0, k:k + 1, :], acc)
    o_ref[0] = jnp.where((d >= 0) & (d < dmax), acc * LOG2E, NEG)


def _bias_table(rbx, rows, stride, offset, dmax):
    rows_p = -(-rows // TABLE_ROWS) * TABLE_ROWS
    return pl.pallas_call(
        functools.partial(_bias_table_kernel, stride=stride, offset=offset, dmax=dmax),
        grid=(C_KV_GROUPS, rows_p // TABLE_ROWS),
        in_specs=[pl.BlockSpec((1, N_BUCKETS, QL), lambda g, i: (g, 0, 0))],
        out_specs=pl.BlockSpec((1, TABLE_ROWS, QL), lambda g, i: (g, i, 0)),
        out_shape=jax.ShapeDtypeStruct((C_KV_GROUPS, rows_p, QL), F32),
        compiler_params=_cparams(2),
        name="bias_table",
    )(rbx)


def _ffn_kernel(h_ref, gpre_ref, gpost_ref, wg_ref, wu_ref, wd_ref, o_ref, acc_ref):
    x = h_ref[...]
    xn = _rms(x, gpre_ref[...]).astype(BF16)
    nch = wg_ref.shape[1] // FFN_CHUNK
    cols = lambda j: slice(j * FFN_CHUNK, (j + 1) * FFN_CHUNK)
    gate, up = _dot(xn, wg_ref[:, cols(0)]), _dot(xn, wu_ref[:, cols(0)])
    for j in range(nch):
        if j + 1 < nch:
            gate_next, up_next = _dot(xn, wg_ref[:, cols(j + 1)]), _dot(xn, wu_ref[:, cols(j + 1)])
        hid = (gate * _sigmoid(gate) * up).astype(BF16)
        down = _dot(hid, wd_ref[cols(j), :])
        if j == 0:
            acc_ref[...] = down
        else:
            acc_ref[...] += down
        gate, up = gate_next, up_next
    o_ref[...] = x + 0.5 * _rms(acc_ref[...], gpost_ref[...])


def _ffn(h, g_pre, g_post, wg, wu, wd):
    n, d = h.shape
    assert wg.shape[1] % FFN_CHUNK == 0
    wg, wu, wd = wg.astype(BF16), wu.astype(BF16), wd.astype(BF16)
    tile = pl.BlockSpec((TOKEN_TILE, d), lambda i: (i, 0))
    return pl.pallas_call(
        _ffn_kernel,
        grid=(n // TOKEN_TILE,),
        in_specs=[tile, _resident((1, d)), _resident((1, d)),
                  _resident(wg.shape), _resident(wu.shape), _resident(wd.shape)],
        out_specs=tile,
        out_shape=jax.ShapeDtypeStruct((n, d), F32),
        scratch_shapes=[pltpu.VMEM((TOKEN_TILE, d), F32)],
        compiler_params=_cparams(1),
        name="ffn",
    )(h, g_pre.reshape(1, d), g_post.reshape(1, d), wg, wu, wd)


def _inproj_kernel(h_ref, g_ref, wa_ref, wb_ref, wq_ref, wkv_ref, wgt_ref, sgn_ref, sgw_ref, sgb_ref,
                   ya_ref, bx_ref, qt_ref, raw_ref, ks_ref, kw_ref, vst_ref, vwt_ref, gt_ref):
    xn = _rms(h_ref[0], g_ref[...]).astype(BF16)

    za = _dot(xn, wa_ref[...])
    u = jax.nn.gelu(za[:, :A_WIDTH])
    v = _rms(jax.nn.gelu(za[:, A_WIDTH:]), sgn_ref[...]).astype(BF16)
    row = lax.broadcasted_iota(jnp.int32, (A_CHUNK, A_CHUNK), 0)
    col = lax.broadcasted_iota(jnp.int32, (A_CHUNK, A_CHUNK), 1)
    lane_group = lax.shift_right_logical(lax.broadcasted_iota(jnp.int32, (A_CHUNK, A_WIDTH), 1),
                                          (A_WIDTH // A_GROUPS).bit_length() - 1)
    w_tril = [jnp.where(row >= col, sgw_ref[g], 0.0).astype(BF16) for g in range(A_GROUPS)]
    for c in range(TOKEN_TILE // A_CHUNK):
        rows = slice(c * A_CHUNK, (c + 1) * A_CHUNK)
        mixed = jnp.zeros((A_CHUNK, A_WIDTH), F32)
        for g in range(A_GROUPS):
            mixed = jnp.where(lane_group == g, _dot(w_tril[g], v[rows]), mixed)
        ya_ref[0, rows, :] = (u[rows] * (mixed + sgb_ref[...])).astype(ya_ref.dtype)

    bx_ref[0] = _dot(xn, wb_ref[...])

    zq_t = (_dot(xn, wq_ref[...]) * (HEAD_DIM ** -0.5 * LOG2E)).T
    for g in range(C_KV_GROUPS):
        for c in range(TOKEN_TILE // Q_TILE):
            parts = []
            for r in range(C_HPG):
                base = (g * C_HPG + r) * HEAD_DIM
                parts.append(zq_t[base:base + HEAD_DIM, c * Q_TILE:(c + 1) * Q_TILE])
            qt_ref[0, g, c] = jnp.concatenate(parts, axis=1).astype(qt_ref.dtype)

    zkv = _dot(xn, wkv_ref[...])
    vs_t = zkv[:, 3 * KV_W:4 * KV_W].T
    vw_t = zkv[:, 5 * KV_W:6 * KV_W].T
    key_blk = lax.shift_right_logical(
        lax.broadcasted_iota(jnp.int32, (TOKEN_TILE, SEL_SLOTS), 0) + pl.program_id(1) * TOKEN_TILE, SEL_SHIFT)
    blk_onehot = jnp.where(key_blk == lax.broadcasted_iota(jnp.int32, (TOKEN_TILE, SEL_SLOTS), 1), 1.0, 0.0)
    ones_rows = jnp.where(lax.broadcasted_iota(jnp.int32, (V_PAD_ROWS, SEL_KEYS), 0) == 0, 1.0, 0.0)
    for g in range(C_KV_GROUPS):
        lo, hi = g * HEAD_DIM, (g + 1) * HEAD_DIM
        raw_ref[0, 0, g] = zkv[:, lo:hi]
        raw_ref[1, 0, g] = zkv[:, KV_W + lo:KV_W + hi]
        ks_ref[0, g] = jnp.concatenate([zkv[:, 2 * KV_W + lo:2 * KV_W + hi], blk_onehot],
                                       axis=1).astype(ks_ref.dtype)
        kw_ref[0, g] = zkv[:, 4 * KV_W + lo:4 * KV_W + hi].astype(kw_ref.dtype)
        for c in range(TOKEN_TILE // SEL_KEYS):
            vst_ref[0, g, c] = jnp.concatenate(
                [vs_t[lo:hi, c * SEL_KEYS:(c + 1) * SEL_KEYS], ones_rows[:, :SEL_KEYS]], axis=0).astype(vst_ref.dtype)
        for c in range(TOKEN_TILE // WIN_KEYS):
            vwt_ref[0, g, c] = jnp.concatenate(
                [vw_t[lo:hi, c * WIN_KEYS:(c + 1) * WIN_KEYS], ones_rows[:, :WIN_KEYS]], axis=0).astype(vwt_ref.dtype)

    sg_t = _sigmoid(_dot(xn, wgt_ref[...])).T
    for br in range(3):
        for g in range(C_KV_GROUPS):
            base = br * C_HEADS + g * C_HPG
            gt_ref[0, br, g] = sg_t[base:base + C_HPG, :]


def _inproj(h, g_norm, w_in, sgu_norm_g, sgu_w, sgu_b):
    b, s, d = h.shape
    wb16 = w_in.astype(BF16)
    o = 0
    wa = wb16[:, o:o + 2 * A_WIDTH]; o += 2 * A_WIDTH
    wb = wb16[:, o:o + 2 * B_WIDTH]; o += 2 * B_WIDTH
    wq = wb16[:, o:o + C_WIDTH]; o += C_WIDTH
    wkv = wb16[:, o:o + 6 * KV_W]; o += 6 * KV_W
    wgt = jnp.pad(wb16[:, o:o + 3 * C_HEADS], ((0, 0), (0, LANES - 3 * C_HEADS)))
    sgb = jnp.repeat(sgu_b.T, A_WIDTH // A_GROUPS, axis=1)
    nt = s // TOKEN_TILE
    grid = (b, nt)
    G = C_KV_GROUPS
    out_shape = [
        jax.ShapeDtypeStruct((b, s, A_WIDTH), BF16),
        jax.ShapeDtypeStruct((b, s, 2 * B_WIDTH), F32),
        jax.ShapeDtypeStruct((b, G, s // Q_TILE, HEAD_DIM, QL), BF16),
        jax.ShapeDtypeStruct((2, b, G, s, HEAD_DIM), F32),
        jax.ShapeDtypeStruct((b, G, s, HEAD_DIM + SEL_SLOTS), BF16),
        jax.ShapeDtypeStruct((b, G, s, HEAD_DIM), BF16),
        jax.ShapeDtypeStruct((b, G, s // SEL_KEYS, V_ROWS, SEL_KEYS), BF16),
        jax.ShapeDtypeStruct((b, G, s // WIN_KEYS, V_ROWS, WIN_KEYS), BF16),
        jax.ShapeDtypeStruct((b, 3, G, C_HPG, s), F32),
    ]
    out_specs = [
        pl.BlockSpec((1, TOKEN_TILE, A_WIDTH), lambda bi, i: (bi, i, 0)),
        pl.BlockSpec((1, TOKEN_TILE, 2 * B_WIDTH), lambda bi, i: (bi, i, 0)),
        pl.BlockSpec((1, G, TOKEN_TILE // Q_TILE, HEAD_DIM, QL), lambda bi, i: (bi, 0, i, 0, 0)),
        pl.BlockSpec((2, 1, G, TOKEN_TILE, HEAD_DIM), lambda bi, i: (0, bi, 0, i, 0)),
        pl.BlockSpec((1, G, TOKEN_TILE, HEAD_DIM + SEL_SLOTS), lambda bi, i: (bi, 0, i, 0)),
        pl.BlockSpec((1, G, TOKEN_TILE, HEAD_DIM), lambda bi, i: (bi, 0, i, 0)),
        pl.BlockSpec((1, G, TOKEN_TILE // SEL_KEYS, V_ROWS, SEL_KEYS), lambda bi, i: (bi, 0, i, 0, 0)),
        pl.BlockSpec((1, G, TOKEN_TILE // WIN_KEYS, V_ROWS, WIN_KEYS), lambda bi, i: (bi, 0, i, 0, 0)),
        pl.BlockSpec((1, 3, G, C_HPG, TOKEN_TILE), lambda bi, i: (bi, 0, 0, 0, i)),
    ]
    in_specs = [
        pl.BlockSpec((1, TOKEN_TILE, d), lambda bi, i: (bi, i, 0)),
        _resident((1, d)), _resident(wa.shape), _resident(wb.shape), _resident(wq.shape),
        _resident(wkv.shape), _resident(wgt.shape), _resident((1, A_WIDTH)),
        _resident(sgu_w.shape), _resident(sgb.shape),
    ]
    return pl.pallas_call(
        _inproj_kernel, grid=grid, in_specs=in_specs, out_specs=out_specs, out_shape=out_shape,
        compiler_params=_cparams(2), name="inproj",
    )(h, g_norm.reshape(1, d), wa, wb, wq, wkv, wgt, sgu_norm_g.reshape(1, A_WIDTH), sgu_w, sgb)


def _rglru_kernel(bx_ref, cw_ref, cb_ref, wa_ref, ba_ref, wx_ref, bxb_ref, lam_ref, o_ref,
                  tail_ref, h_ref, a_ref, b_ref):
    @pl.when(pl.program_id(1) == 0)
    def _():
        tail_ref[...] = jnp.zeros_like(tail_ref)
        h_ref[...] = jnp.zeros_like(h_ref)

    xb = bx_ref[0, :, :B_WIDTH]
    gate = bx_ref[0, :, B_WIDTH:]
    ext = jnp.concatenate([tail_ref[...], xb], axis=0)
    xc = cb_ref[...] + xb * cw_ref[CONV_W - 1:CONV_W, :]
    for k in range(CONV_W - 1):
        shift = CONV_W - 1 - k
        xc = xc + ext[SUBLANES - shift:SUBLANES - shift + SCAN_TILE] * cw_ref[k:k + 1, :]
    tail_ref[...] = xb[SCAN_TILE - SUBLANES:]

    xcb = xc.astype(BF16)
    r = _sigmoid(_dot(xcb, wa_ref[...]) + ba_ref[...])
    i = _sigmoid(_dot(xcb, wx_ref[...]) + bxb_ref[...])
    z = -lam_ref[...]
    e = jnp.exp(-jnp.abs(z))
    softplus = jnp.maximum(z, 0.0) + jnp.log1p(e)
    log_a = -LRU_C * r * softplus
    a = jnp.exp(log_a)
    b = jnp.sqrt(jnp.tanh(-log_a) * (a * a + 1.0)) * (i * xc)

    row = lax.broadcasted_iota(jnp.int32, a.shape, 0) & (SUBLANES - 1)
    for dist in (1, 2, 4):
        a_prev = jnp.where(row >= dist, pltpu.roll(a, dist, 0), 1.0)
        b_prev = jnp.where(row >= dist, pltpu.roll(b, dist, 0), 0.0)
        b = a * b_prev + b
        a = a * a_prev
    a_ref[...] = a
    b_ref[...] = b

    def body(k, h):
        off = pl.multiple_of(k * SUBLANES, SUBLANES)
        rows = pl.ds(off, SUBLANES)
        hs = b_ref[rows, :] + a_ref[rows, :] * h
        b_ref[rows, :] = hs
        return jnp.broadcast_to(hs[SUBLANES - 1:SUBLANES, :], hs.shape)

    h_ref[...] = lax.fori_loop(0, SCAN_TILE // SUBLANES, body, h_ref[...])
    o_ref[0] = (b_ref[...] * jax.nn.gelu(gate)).astype(o_ref.dtype)


def _block_diag(w):
    g, n, _ = w.shape
    out = jnp.zeros((g * n, g * n), w.dtype)
    for k in range(g):
        out = out.at[k * n:(k + 1) * n, k * n:(k + 1) * n].set(w[k])
    return out


def _rglru(bx, conv_w, conv_b, wa, ba, wx, bxb, lam):
    b, s, _ = bx.shape
    w = B_WIDTH
    row = lambda v: v.reshape(1, w)
    return pl.pallas_call(
        _rglru_kernel,
        grid=(b, s // SCAN_TILE),
        in_specs=[pl.BlockSpec((1, SCAN_TILE, 2 * w), lambda bi, i: (bi, i, 0)),
                  _resident((CONV_W, w)), _resident((1, w)), _resident((w, w)), _resident((1, w)),
                  _resident((w, w)), _resident((1, w)), _resident((1, w))],
        out_specs=pl.BlockSpec((1, SCAN_TILE, w), lambda bi, i: (bi, i, 0)),
        out_shape=jax.ShapeDtypeStruct((b, s, w), BF16),
        scratch_shapes=[pltpu.VMEM((SUBLANES, w), F32), pltpu.VMEM((SUBLANES, w), F32),
                        pltpu.VMEM((SCAN_TILE, w), F32), pltpu.VMEM((SCAN_TILE, w), F32)],
        compiler_params=_cparams(2),
        name="rglru",
    )(bx, conv_w, row(conv_b), _block_diag(wa).astype(BF16), row(ba),
      _block_diag(wx).astype(BF16), row(bxb), row(lam))


def _compress_kernel(raw_ref, pos_ref, w1_ref, b1_ref, w2_ref, b2_ref, w2t_ref, b2c_ref, kc_ref, kct_ref):
    g, ncp, half = raw_ref.shape[2:]
    x = raw_ref[0, 0].reshape(g * ncp, half)
    top = (x + pos_ref[0, :, :half]).astype(BF16)
    bot = (x + pos_ref[0, :, half:]).astype(BF16)
    u = _dot(top, w1_ref[0, :half, :])
    v = _dot(bot, w1_ref[0, half:, :])
    hid = jax.nn.gelu(u + pltpu.roll(v, g * ncp - 1, 0) + b1_ref[0]).astype(BF16)
    kc_ref[0, 0] = (_dot(hid, w2_ref[0]) + b2_ref[0]).reshape(g, ncp, HEAD_DIM).astype(kc_ref.dtype)
    for gi in range(g):
        t = lax.dot_general(w2t_ref[0], hid[gi * ncp:(gi + 1) * ncp], (((1,), (1,)), ((), ())),
                            preferred_element_type=F32)
        kct_ref[0, 0, gi] = (t + b2c_ref[0]).astype(kct_ref.dtype)


def _compress(raw, cmp_pos, cmp_w1, cmp_b1, cmp_w2, cmp_b2):
    _, b, g, s, hd = raw.shape
    ncp = s // CMP_STRIDE
    half = CMP_STRIDE * hd
    hid = cmp_w1.shape[-1]
    raw16 = raw.reshape(2, b, g, ncp, half)
    sel = lambda *shape: pl.BlockSpec((1,) + shape, lambda kv, bi: (kv,) + (0,) * len(shape))
    return pl.pallas_call(
        _compress_kernel,
        grid=(2, b),
        in_specs=[pl.BlockSpec((1, 1, g, ncp, half), lambda kv, bi: (kv, bi, 0, 0, 0)),
                  sel(1, 2 * half), sel(2 * half, hid), sel(1, hid), sel(hid, hd), sel(1, hd),
                  sel(hd, hid), sel(hd, 1)],
        out_specs=[pl.BlockSpec((1, 1, g, ncp, hd), lambda kv, bi: (kv, bi, 0, 0, 0)),
                   pl.BlockSpec((1, 1, g, hd, ncp), lambda kv, bi: (kv, bi, 0, 0, 0))],
        out_shape=[jax.ShapeDtypeStruct((2, b, g, ncp, hd), BF16),
                   jax.ShapeDtypeStruct((2, b, g, hd, ncp), BF16)],
        compiler_params=_cparams(2),
        name="compress",
    )(raw16, cmp_pos.reshape(2, 1, 2 * half), cmp_w1.astype(BF16), cmp_b1.reshape(2, 1, hid),
      cmp_w2.astype(BF16), cmp_b2.reshape(2, 1, hd),
      cmp_w2.astype(BF16).transpose(0, 2, 1), cmp_b2.reshape(2, hd, 1))


def _softmax_step(carry, s, v_t):
    m, acc = carry
    m_new = jnp.maximum(m, jnp.max(s, axis=0, keepdims=True))
    p = jnp.exp2(s - m_new).astype(BF16)
    acc = acc * jnp.exp2(m - m_new) + _dot(v_t, p)
    return m_new, acc


def _normalized(acc):
    return acc[:HEAD_DIM] / acc[HEAD_DIM:HEAD_DIM + 1]


def _nsa_kernel(qt_ref, kc_ref, vct_ref, ks_ref, vst_ref, kw_ref, vwt_ref, gate_ref,
                gsel_ref, gwin_ref, gcmp_ref, ovt_ref, o_ref, *, n_tiles):
    step = pl.program_id(1)
    ncp = kc_ref.shape[2]
    chains = [(t, g) for t in range(TILES_PER_STEP) for g in range(C_KV_GROUPS)]
    tile = [step * TILES_PER_STEP + t for t in range(TILES_PER_STEP)]
    qts = [qt_ref[0, g, t] for t, g in chains]

    back = WINDOW // WIN_KEYS
    n_win = WINDOW + Q_TILE
    first = [jnp.maximum(c - back, 0) for c in tile]
    s_cmp, s_win = [], []
    for k, (t, g) in enumerate(chains):
        y0 = pl.multiple_of((n_tiles - 1 - tile[t]) * (Q_TILE // CMP_STRIDE), SUBLANES)
        s_cmp.append(_dot(kc_ref[0, g], qts[k]) + gcmp_ref[g, pl.ds(y0, ncp), :])
    for k, (t, g) in enumerate(chains):
        start = pl.multiple_of(first[t] * WIN_KEYS, WIN_KEYS)
        rel = pl.multiple_of(jnp.maximum(back - tile[t], 0) * WIN_KEYS, WIN_KEYS)
        s_win.append(_dot(kw_ref[0, g, pl.ds(start, n_win), :], qts[k]) + gwin_ref[g, pl.ds(rel, n_win), :])

    tq = lax.broadcasted_iota(jnp.int32, (1, QL), 1) & (Q_TILE - 1)
    o_cmp, imp = [], []
    for k, (t, g) in enumerate(chains):
        e = jnp.exp2(s_cmp[k] - jnp.max(s_cmp[k], axis=0, keepdims=True))
        has_cmp = (tile[t] * Q_TILE + tq >= CMP_LEN - 1).astype(F32)
        p = e / jnp.sum(e, axis=0, keepdims=True) * has_cmp
        o_cmp.append(_dot(vct_ref[0, g], p.astype(BF16)))
        p_heads = p[:, 0:Q_TILE]
        for r in range(1, C_HPG):
            p_heads = p_heads + p[:, r * Q_TILE:(r + 1) * Q_TILE]
        p_hi = p_heads.astype(BF16)
        p_lo = (p_heads - p_hi.astype(F32)).astype(BF16)
        imp.append(_dot(ovt_ref[...], p_hi) + _dot(ovt_ref[...], p_lo))

    o_win = []
    for k, (t, g) in enumerate(chains):
        e = jnp.exp2(s_win[k] - jnp.max(s_win[k], axis=0, keepdims=True)).astype(BF16)
        acc = jnp.zeros((V_ROWS, QL), F32)
        for i in range(n_win // WIN_KEYS):
            acc = acc + _dot(vwt_ref[0, g, first[t] + i], e[i * WIN_KEYS:(i + 1) * WIN_KEYS])
        o_win.append(_normalized(acc))

    q_aug = [_with_mask_rows(qts[k], tile[t], imp[k]) for k, (t, g) in enumerate(chains)]
    halves = range(SEL_KEYS // SEL_HALF)
    init = (jnp.full((1, QL), NEG, F32), jnp.zeros((V_ROWS, QL), F32))

    def sel_body(i, carry):
        off = pl.multiple_of(i * SEL_KEYS, SEL_KEYS)
        s = []
        for h in halves:
            for k, (t, g) in enumerate(chains):
                x0 = (n_tiles - 1 - tile[t]) * Q_TILE
                s.append(_dot(ks_ref[0, g, pl.ds(off + h * SEL_HALF, SEL_HALF), :], q_aug[k])
                         + gsel_ref[g, pl.ds(pl.multiple_of(x0 + off + h * SEL_HALF, Q_TILE), SEL_HALF), :])
        carry = list(carry)
        for h in halves:
            for k, (t, g) in enumerate(chains):
                v_t = vst_ref[0, g, i][:, h * SEL_HALF:(h + 1) * SEL_HALF]
                carry[k] = _softmax_step(carry[k], s[h * len(chains) + k], v_t)
        return tuple(carry)

    n_sel_steps = (tile[-1] * Q_TILE + Q_TILE + SEL_KEYS - 1) // SEL_KEYS
    sel = lax.fori_loop(0, n_sel_steps, sel_body, (init,) * len(chains))

    for t in range(TILES_PER_STEP):
        outs = []
        for k, (tk, g) in enumerate(chains):
            if tk != t:
                continue
            o_sel = _normalized(sel[k][1])
            tok = slice(t * Q_TILE, (t + 1) * Q_TILE)
            cols = []
            for r in range(C_HPG):
                ln = slice(r * Q_TILE, (r + 1) * Q_TILE)
                cols.append(gate_ref[0, 0, g, r:r + 1, tok] * o_cmp[k][:, ln]
                            + gate_ref[0, 1, g, r:r + 1, tok] * o_sel[:, ln]
                            + gate_ref[0, 2, g, r:r + 1, tok] * o_win[k][:, ln])
            outs += [jnp.concatenate(cols[2 * j:2 * j + 2], axis=0).T for j in range(C_HPG // 2)]
        o_ref[0, t * Q_TILE:(t + 1) * Q_TILE, :] = jnp.concatenate(outs, axis=1).astype(o_ref.dtype)


def _with_mask_rows(qt, c, imp):
    nsel = imp.shape[0]
    j = lax.broadcasted_iota(jnp.int32, (nsel, Q_TILE), 0)
    blk = c * (Q_TILE // SEL_LEN) + lax.shift_right_logical(
        lax.broadcasted_iota(jnp.int32, (nsel, Q_TILE), 1), SEL_SHIFT)
    forced = (j == 0) | (j == blk) | (j == blk - 1)
    score = jnp.where(j <= blk, jnp.where(forced, FORCE_SCORE, imp), -1.0)
    groups = range(nsel // SUBLANES)
    rows = [score[k * SUBLANES:(k + 1) * SUBLANES] for k in groups]
    j_rows = j[:SUBLANES]
    ranks = [jnp.zeros((SUBLANES, Q_TILE), jnp.int32) for _ in groups]
    for jp in range(nsel):
        other = score[jp:jp + 1, :]
        for k in groups:
            if k * SUBLANES > jp:
                beats = jnp.where(other >= rows[k], 1, 0)
            elif (k + 1) * SUBLANES <= jp:
                beats = jnp.where(other > rows[k], 1, 0)
            else:
                beats = jnp.where(j_rows > jp - k * SUBLANES,
                                  jnp.where(other >= rows[k], 1, 0), jnp.where(other > rows[k], 1, 0))
            ranks[k] = ranks[k] + beats
    rank = jnp.concatenate(ranks, axis=0)
    mask_rows = jnp.where((rank < SEL_TOP) & (score >= 0.0), 0.0, NEG).astype(BF16)
    return jnp.concatenate([qt, jnp.concatenate([mask_rows] * C_HPG, axis=1)], axis=0)


def _nsa(qt, kc, vct, ks, vst, kw, vwt, gates, gsel, gwin, gcmp, ovt):
    b, G, n_tiles = qt.shape[:3]
    s = n_tiles * Q_TILE
    per_b = lambda a: pl.BlockSpec((1,) + a.shape[1:], lambda bi, c: (bi,) + (0,) * (a.ndim - 1))
    return pl.pallas_call(
        functools.partial(_nsa_kernel, n_tiles=n_tiles),
        grid=(b, n_tiles // TILES_PER_STEP),
        in_specs=[pl.BlockSpec((1, G, TILES_PER_STEP, HEAD_DIM, QL), lambda bi, c: (bi, 0, c, 0, 0)),
                  per_b(kc), per_b(vct), per_b(ks), per_b(vst), per_b(kw), per_b(vwt),
                  pl.BlockSpec((1, 3, G, C_HPG, TILES_PER_STEP * Q_TILE), lambda bi, c: (bi, 0, 0, 0, c)),
                  _resident(gsel.shape), _resident(gwin.shape), _resident(gcmp.shape),
                  _resident(ovt.shape)],
        out_specs=pl.BlockSpec((1, TILES_PER_STEP * Q_TILE, C_WIDTH), lambda bi, c: (bi, c, 0)),
        out_shape=jax.ShapeDtypeStruct((b, s, C_WIDTH), BF16),
        compiler_params=_cparams(2),
        name="nsa",
    )(qt, kc, vct, ks, vst, kw, vwt, gates, gsel, gwin, gcmp, ovt)


def _outproj_kernel(h_ref, ya_ref, yb_ref, yc_ref, wa_ref, wb_ref, wc_ref, g_ref, o_ref):
    mix = _dot(ya_ref[...], wa_ref[...]) + _dot(yb_ref[...], wb_ref[...]) + _dot(yc_ref[...], wc_ref[...])
    o_ref[...] = h_ref[...] + _rms(mix, g_ref[...])


def _outproj(h, ya, yb, yc, w_out, g_norm):
    n, d = h.shape
    w = w_out.astype(BF16)
    wa, wb, wc = w[:A_WIDTH], w[A_WIDTH:A_WIDTH + B_WIDTH], w[A_WIDTH + B_WIDTH:]
    tile = lambda width: pl.BlockSpec((TOKEN_TILE, width), lambda i: (i, 0))
    return pl.pallas_call(
        _outproj_kernel,
        grid=(n // TOKEN_TILE,),
        in_specs=[tile(d), tile(A_WIDTH), tile(B_WIDTH), tile(C_WIDTH),
                  _resident(wa.shape), _resident(wb.shape), _resident(wc.shape), _resident((1, d))],
        out_specs=tile(d),
        out_shape=jax.ShapeDtypeStruct((n, d), F32),
        compiler_params=_cparams(1),
        name="outproj",
    )(h, ya, yb, yc, wa, wb, wc, g_norm.reshape(1, d))


def _ple_kernel(h_ref, p_ref, gpre_ref, gpost_ref, wg_ref, wp_ref, o_ref):
    x = h_ref[...]
    gate = _sigmoid(_dot(_rms(x, gpre_ref[...]).astype(BF16), wg_ref[...]))
    emb = _dot(p_ref[...].astype(BF16), wp_ref[...])
    o_ref[...] = x + _rms(gate * emb, gpost_ref[...])


def _ple(h, p, g_pre, g_post, w_gate, w_proj):
    n, d = h.shape
    dp = p.shape[-1]
    tile = lambda width: pl.BlockSpec((TOKEN_TILE, width), lambda i: (i, 0))
    return pl.pallas_call(
        _ple_kernel,
        grid=(n // TOKEN_TILE,),
        in_specs=[tile(d), tile(dp), _resident((1, d)), _resident((1, d)),
                  _resident((d, d)), _resident((dp, d))],
        out_specs=tile(d),
        out_shape=jax.ShapeDtypeStruct((n, d), F32),
        compiler_params=_cparams(1),
        name="ple",
    )(h, p, g_pre.reshape(1, d), g_post.reshape(1, d), w_gate.astype(BF16), w_proj.astype(BF16))


def _overlap_t(s):
    ncp = s // CMP_STRIDE
    n_cmp = (s - CMP_LEN) // CMP_STRIDE + 1
    cs = jnp.arange(ncp) * CMP_STRIDE
    ss = jnp.arange(s // SEL_LEN) * SEL_LEN
    ov = jnp.clip(jnp.minimum(cs[None] + CMP_LEN, ss[:, None] + SEL_LEN)
                  - jnp.maximum(cs[None], ss[:, None]), 0, None).astype(F32) / CMP_LEN
    ov = jnp.where(jnp.arange(ncp)[None] < n_cmp, ov, 0.0).astype(BF16)
    return jnp.pad(ov, ((0, SEL_SLOTS - s // SEL_LEN), (0, 0)))


def kernel(x, p, rel_bias, norm_g, ffn_w_gate, ffn_w_up, ffn_w_down, w_in, w_out, sgu_norm_g, sgu_w, sgu_b,
           conv_w, conv_b, lru_wa, lru_ba, lru_wx, lru_bx, lru_lambda, cmp_pos, cmp_w1, cmp_b1, cmp_w2,
           cmp_b2, ple_w_gate, ple_w_proj):
    b, s, d = x.shape
    depth = norm_g.shape[0]
    assert s % TOKEN_TILE == 0 and s % SCAN_TILE == 0 and s % SEL_KEYS == 0
    assert s % (TILES_PER_STEP * Q_TILE) == 0
    assert s >= WINDOW + Q_TILE and SEL_TOP <= s // SEL_LEN <= SEL_SLOTS
    n_tiles = s // Q_TILE

    rbx = jnp.repeat(rel_bias.reshape(N_BUCKETS, C_KV_GROUPS, C_HPG).transpose(1, 0, 2), Q_TILE, axis=2)
    no_limit = 1 << 30
    gsel = _bias_table(rbx, s + SEL_KEYS, 1, s - Q_TILE, no_limit)
    gwin = _bias_table(rbx, 2 * WINDOW + Q_TILE, 1, WINDOW, WINDOW)
    per_tile = Q_TILE // CMP_STRIDE
    gcmp = _bias_table(rbx, per_tile * (n_tiles - 1) + s // CMP_STRIDE, CMP_STRIDE,
                       CMP_STRIDE * per_tile * (n_tiles - 1) - (CMP_LEN - 1), no_limit)
    ovt = _overlap_t(s)

    h = x.reshape(b * s, d)
    for i in range(depth):
        g = norm_g[i]
        h = _ffn(h, g[0], g[1], ffn_w_gate[i, 0], ffn_w_up[i, 0], ffn_w_down[i, 0])
        ya, bx, qt, raw, ks, kw, vst, vwt, gates = _inproj(
            h.reshape(b, s, d), g[2], w_in[i], sgu_norm_g[i], sgu_w[i], sgu_b[i])
        yb = _rglru(bx, conv_w[i], conv_b[i], lru_wa[i], lru_ba[i], lru_wx[i], lru_bx[i], lru_lambda[i])
        kc, kct = _compress(raw, cmp_pos[i], cmp_w1[i], cmp_b1[i], cmp_w2[i], cmp_b2[i])
        yc = _nsa(qt, kc[0], kct[1], ks, vst, kw, vwt, gates, gsel, gwin, gcmp, ovt)
        h = _outproj(h, ya.reshape(b * s, -1), yb.reshape(b * s, -1), yc.reshape(b * s, -1), w_out[i], g[3])
        h = _ffn(h, g[4], g[5], ffn_w_gate[i, 1], ffn_w_up[i, 1], ffn_w_down[i, 1])
        h = _ple(h, p[i].reshape(b * s, -1), g[6], g[7], ple_w_gate[i], ple_w_proj[i])
    return h.reshape(b, s, d)
```

```python
import functools
import math

import jax
import jax.numpy as jnp
from jax import lax
from jax.experimental import pallas as pl
from jax.experimental.pallas import tpu as pltpu

F32 = jnp.float32
BF16 = jnp.bfloat16

RMS_EPS = 1e-6
A_GROUPS = 4
A_WIDTH = 256
A_CHUNK = 128
B_GROUPS = 4
B_WIDTH = 256
CONV_W = 4
LRU_C = 8.0
C_HEADS = 8
C_KV_GROUPS = 2
C_HPG = C_HEADS // C_KV_GROUPS
HEAD_DIM = 64
C_WIDTH = C_HEADS * HEAD_DIM
KV_W = C_KV_GROUPS * HEAD_DIM
CMP_LEN = 32
CMP_STRIDE = 16
SEL_LEN = 64
SEL_SHIFT = 6
SEL_SLOTS = 64
SEL_TOP = 16
WINDOW = 512
FORCE_SCORE = 1e4
NEG = -1e30
N_BUCKETS = 32
MAX_DISTANCE = 1024

LANES = 128
TOKEN_TILE = 512
FFN_CHUNK = 256
SCAN_TILE = 512
SUBLANES = 8
Q_TILE = 128
TILES_PER_STEP = 2
SEL_KEYS = 512
SEL_HALF = 256
WIN_KEYS = 128
QL = C_HPG * Q_TILE
V_PAD_ROWS = 16
V_ROWS = HEAD_DIM + V_PAD_ROWS
LOG2E = math.log2(math.e)
TABLE_ROWS = 128
VMEM_LIMIT = 56 * 1024 * 1024


def _cparams(n_axes):
    return pltpu.CompilerParams(dimension_semantics=("arbitrary",) * n_axes,
                                vmem_limit_bytes=VMEM_LIMIT)


def _resident(shape):
    nd = len(shape)
    return pl.BlockSpec(shape, lambda *_: (0,) * nd, pipeline_mode=pl.Buffered(1))


def _rms(x, g):
    return x * lax.rsqrt(jnp.mean(x * x, axis=-1, keepdims=True) + RMS_EPS) * g


def _sigmoid(x):
    return 1.0 / (1.0 + jnp.exp(-x))


def _dot(a, b):
    return jnp.dot(a, b, preferred_element_type=F32)


def _bias_table_kernel(rbx_ref, o_ref, *, stride, offset, dmax):
    i = pl.program_id(1)
    shape = (TABLE_ROWS, QL)
    x = lax.broadcasted_iota(jnp.int32, shape, 0) + i * TABLE_ROWS
    t = lax.broadcasted_iota(jnp.int32, shape, 1) & (Q_TILE - 1)
    d = t - stride * x + offset
    n = jnp.maximum(d, 0)
    max_exact = N_BUCKETS // 2
    nf = jnp.maximum(n, max_exact).astype(F32)
    large = max_exact + (jnp.log(nf / max_exact) / math.log(MAX_DISTANCE / max_exact)
                         * (N_BUCKETS - max_exact)).astype(jnp.int32)
    large = jnp.minimum(large, N_BUCKETS - 1)
    bucket = jnp.where(n < max_exact, n, large)
    acc = jnp.zeros(shape, F32)
    for k in range(N_BUCKETS):
        acc = jnp.where(bucket == k, rbx_ref[0, k:k + 1, :], acc)
    o_ref[0] = jnp.where((d >= 0) & (d < dmax), acc * LOG2E, NEG)


def _bias_table(rbx, rows, stride, offset, dmax):
    rows_p = -(-rows // TABLE_ROWS) * TABLE_ROWS
    return pl.pallas_call(
        functools.partial(_bias_table_kernel, stride=stride, offset=offset, dmax=dmax),
        grid=(C_KV_GROUPS, rows_p // TABLE_ROWS),
        in_specs=[pl.BlockSpec((1, N_BUCKETS, QL), lambda g, i: (g, 0, 0))],
        out_specs=pl.BlockSpec((1, TABLE_ROWS, QL), lambda g, i: (g, i, 0)),
        out_shape=jax.ShapeDtypeStruct((C_KV_GROUPS, rows_p, QL), F32),
        compiler_params=_cparams(2),
        name="bias_table",
    )(rbx)


def _ffn_kernel(h_ref, gpre_ref, gpost_ref, wg_ref, wu_ref, wd_ref, o_ref, acc_ref):
    x = h_ref[...]
    xn = _rms(x, gpre_ref[...]).astype(BF16)
    nch = wg_ref.shape[1] // FFN_CHUNK
    cols = lambda j: slice(j * FFN_CHUNK, (j + 1) * FFN_CHUNK)
    gate, up = _dot(xn, wg_ref[:, cols(0)]), _dot(xn, wu_ref[:, cols(0)])
    for j in range(nch):
        if j + 1 < nch:
            gate_next, up_next = _dot(xn, wg_ref[:, cols(j + 1)]), _dot(xn, wu_ref[:, cols(j + 1)])
        hid = (gate * _sigmoid(gate) * up).astype(BF16)
        down = _dot(hid, wd_ref[cols(j), :])
        if j == 0:
            acc_ref[...] = down
        else:
            acc_ref[...] += down
        gate, up = gate_next, up_next
    o_ref[...] = x + 0.5 * _rms(acc_ref[...], gpost_ref[...])


def _ffn(h, g_pre, g_post, wg, wu, wd):
    n, d = h.shape
    assert wg.shape[1] % FFN_CHUNK == 0
    wg, wu, wd = wg.astype(BF16), wu.astype(BF16), wd.astype(BF16)
    tile = pl.BlockSpec((TOKEN_TILE, d), lambda i: (i, 0))
    return pl.pallas_call(
        _ffn_kernel,
        grid=(n // TOKEN_TILE,),
        in_specs=[tile, _resident((1, d)), _resident((1, d)),
                  _resident(wg.shape), _resident(wu.shape), _resident(wd.shape)],
        out_specs=tile,
        out_shape=jax.ShapeDtypeStruct((n, d), F32),
        scratch_shapes=[pltpu.VMEM((TOKEN_TILE, d), F32)],
        compiler_params=_cparams(1),
        name="ffn",
    )(h, g_pre.reshape(1, d), g_post.reshape(1, d), wg, wu, wd)


def _inproj_kernel(h_ref, g_ref, wa_ref, wb_ref, wq_ref, wkv_ref, wgt_ref, sgn_ref, sgw_ref, sgb_ref,
                   ya_ref, bx_ref, qt_ref, raw_ref, ks_ref, kw_ref, vst_ref, vwt_ref, gt_ref):
    xn = _rms(h_ref[0], g_ref[...]).astype(BF16)

    za = _dot(xn, wa_ref[...])
    u = jax.nn.gelu(za[:, :A_WIDTH])
    v = _rms(jax.nn.gelu(za[:, A_WIDTH:]), sgn_ref[...]).astype(BF16)
    row = lax.broadcasted_iota(jnp.int32, (A_CHUNK, A_CHUNK), 0)
    col = lax.broadcasted_iota(jnp.int32, (A_CHUNK, A_CHUNK), 1)
    lane_group = lax.shift_right_logical(lax.broadcasted_iota(jnp.int32, (A_CHUNK, A_WIDTH), 1),
                                          (A_WIDTH // A_GROUPS).bit_length() - 1)
    w_tril = [jnp.where(row >= col, sgw_ref[g], 0.0).astype(BF16) for g in range(A_GROUPS)]
    for c in range(TOKEN_TILE // A_CHUNK):
        rows = slice(c * A_CHUNK, (c + 1) * A_CHUNK)
        mixed = jnp.zeros((A_CHUNK, A_WIDTH), F32)
        for g in range(A_GROUPS):
            mixed = jnp.where(lane_group == g, _dot(w_tril[g], v[rows]), mixed)
        ya_ref[0, rows, :] = (u[rows] * (mixed + sgb_ref[...])).astype(ya_ref.dtype)

    bx_ref[0] = _dot(xn, wb_ref[...])

    zq_t = (_dot(xn, wq_ref[...]) * (HEAD_DIM ** -0.5 * LOG2E)).T
    for g in range(C_KV_GROUPS):
        for c in range(TOKEN_TILE // Q_TILE):
            parts = []
            for r in range(C_HPG):
                base = (g * C_HPG + r) * HEAD_DIM
                parts.append(zq_t[base:base + HEAD_DIM, c * Q_TILE:(c + 1) * Q_TILE])
            qt_ref[0, g, c] = jnp.concatenate(parts, axis=1).astype(qt_ref.dtype)

    zkv = _dot(xn, wkv_ref[...])
    vs_t = zkv[:, 3 * KV_W:4 * KV_W].T
    vw_t = zkv[:, 5 * KV_W:6 * KV_W].T
    key_blk = lax.shift_right_logical(
        lax.broadcasted_iota(jnp.int32, (TOKEN_TILE, SEL_SLOTS), 0) + pl.program_id(1) * TOKEN_TILE, SEL_SHIFT)
    blk_onehot = jnp.where(key_blk == lax.broadcasted_iota(jnp.int32, (TOKEN_TILE, SEL_SLOTS), 1), 1.0, 0.0)
    ones_rows = jnp.where(lax.broadcasted_iota(jnp.int32, (V_PAD_ROWS, SEL_KEYS), 0) == 0, 1.0, 0.0)
    for g in range(C_KV_GROUPS):
        lo, hi = g * HEAD_DIM, (g + 1) * HEAD_DIM
        raw_ref[0, 0, g] = zkv[:, lo:hi]
        raw_ref[1, 0, g] = zkv[:, KV_W + lo:KV_W + hi]
        ks_ref[0, g] = jnp.concatenate([zkv[:, 2 * KV_W + lo:2 * KV_W + hi], blk_onehot],
                                       axis=1).astype(ks_ref.dtype)
        kw_ref[0, g] = zkv[:, 4 * KV_W + lo:4 * KV_W + hi].astype(kw_ref.dtype)
        for c in range(TOKEN_TILE // SEL_KEYS):
            vst_ref[0, g, c] = jnp.concatenate(
                [vs_t[lo:hi, c * SEL_KEYS:(c + 1) * SEL_KEYS], ones_rows[:, :SEL_KEYS]], axis=0).astype(vst_ref.dtype)
        for c in range(TOKEN_TILE // WIN_KEYS):
            vwt_ref[0, g, c] = jnp.concatenate(
                [vw_t[lo:hi, c * WIN_KEYS:(c + 1) * WIN_KEYS], ones_rows[:, :WIN_KEYS]], axis=0).astype(vwt_ref.dtype)

    sg_t = _sigmoid(_dot(xn, wgt_ref[...])).T
    for br in range(3):
        for g in range(C_KV_GROUPS):
            base = br * C_HEADS + g * C_HPG
            gt_ref[0, br, g] = sg_t[base:base + C_HPG, :]


def _inproj(h, g_norm, w_in, sgu_norm_g, sgu_w, sgu_b):
    b, s, d = h.shape
    wb16 = w_in.astype(BF16)
    o = 0
    wa = wb16[:, o:o + 2 * A_WIDTH]; o += 2 * A_WIDTH
    wb = wb16[:, o:o + 2 * B_WIDTH]; o += 2 * B_WIDTH
    wq = wb16[:, o:o + C_WIDTH]; o += C_WIDTH
    wkv = wb16[:, o:o + 6 * KV_W]; o += 6 * KV_W
    wgt = jnp.pad(wb16[:, o:o + 3 * C_HEADS], ((0, 0), (0, LANES - 3 * C_HEADS)))
    sgb = jnp.repeat(sgu_b.T, A_WIDTH // A_GROUPS, axis=1)
    nt = s // TOKEN_TILE
    grid = (b, nt)
    G = C_KV_GROUPS
    out_shape = [
        jax.ShapeDtypeStruct((b, s, A_WIDTH), BF16),
        jax.ShapeDtypeStruct((b, s, 2 * B_WIDTH), F32),
        jax.ShapeDtypeStruct((b, G, s // Q_TILE, HEAD_DIM, QL), BF16),
        jax.ShapeDtypeStruct((2, b, G, s, HEAD_DIM), F32),
        jax.ShapeDtypeStruct((b, G, s, HEAD_DIM + SEL_SLOTS), BF16),
        jax.ShapeDtypeStruct((b, G, s, HEAD_DIM), BF16),
        jax.ShapeDtypeStruct((b, G, s // SEL_KEYS, V_ROWS, SEL_KEYS), BF16),
        jax.ShapeDtypeStruct((b, G, s // WIN_KEYS, V_ROWS, WIN_KEYS), BF16),
        jax.ShapeDtypeStruct((b, 3, G, C_HPG, s), F32),
    ]
    out_specs = [
        pl.BlockSpec((1, TOKEN_TILE, A_WIDTH), lambda bi, i: (bi, i, 0)),
        pl.BlockSpec((1, TOKEN_TILE, 2 * B_WIDTH), lambda bi, i: (bi, i, 0)),
        pl.BlockSpec((1, G, TOKEN_TILE // Q_TILE, HEAD_DIM, QL), lambda bi, i: (bi, 0, i, 0, 0)),
        pl.BlockSpec((2, 1, G, TOKEN_TILE, HEAD_DIM), lambda bi, i: (0, bi, 0, i, 0)),
        pl.BlockSpec((1, G, TOKEN_TILE, HEAD_DIM + SEL_SLOTS), lambda bi, i: (bi, 0, i, 0)),
        pl.BlockSpec((1, G, TOKEN_TILE, HEAD_DIM), lambda bi, i: (bi, 0, i, 0)),
        pl.BlockSpec((1, G, TOKEN_TILE // SEL_KEYS, V_ROWS, SEL_KEYS), lambda bi, i: (bi, 0, i, 0, 0)),
        pl.BlockSpec((1, G, TOKEN_TILE // WIN_KEYS, V_ROWS, WIN_KEYS), lambda bi, i: (bi, 0, i, 0, 0)),
        pl.BlockSpec((1, 3, G, C_HPG, TOKEN_TILE), lambda bi, i: (bi, 0, 0, 0, i)),
    ]
    in_specs = [
        pl.BlockSpec((1, TOKEN_TILE, d), lambda bi, i: (bi, i, 0)),
        _resident((1, d)), _resident(wa.shape), _resident(wb.shape), _resident(wq.shape),
        _resident(wkv.shape), _resident(wgt.shape), _resident((1, A_WIDTH)),
        _resident(sgu_w.shape), _resident(sgb.shape),
    ]
    return pl.pallas_call(
        _inproj_kernel, grid=grid, in_specs=in_specs, out_specs=out_specs, out_shape=out_shape,
        compiler_params=_cparams(2), name="inproj",
    )(h, g_norm.reshape(1, d), wa, wb, wq, wkv, wgt, sgu_norm_g.reshape(1, A_WIDTH), sgu_w, sgb)


def _rglru_kernel(bx_ref, cw_ref, cb_ref, wa_ref, ba_ref, wx_ref, bxb_ref, lam_ref, o_ref,
                  tail_ref, h_ref, a_ref, b_ref):
    @pl.when(pl.program_id(1) == 0)
    def _():
        tail_ref[...] = jnp.zeros_like(tail_ref)
        h_ref[...] = jnp.zeros_like(h_ref)

    xb = bx_ref[0, :, :B_WIDTH]
    gate = bx_ref[0, :, B_WIDTH:]
    ext = jnp.concatenate([tail_ref[...], xb], axis=0)
    xc = cb_ref[...] + xb * cw_ref[CONV_W - 1:CONV_W, :]
    for k in range(CONV_W - 1):
        shift = CONV_W - 1 - k
        xc = xc + ext[SUBLANES - shift:SUBLANES - shift + SCAN_TILE] * cw_ref[k:k + 1, :]
    tail_ref[...] = xb[SCAN_TILE - SUBLANES:]

    xcb = xc.astype(BF16)
    r = _sigmoid(_dot(xcb, wa_ref[...]) + ba_ref[...])
    i = _sigmoid(_dot(xcb, wx_ref[...]) + bxb_ref[...])
    z = -lam_ref[...]
    e = jnp.exp(-jnp.abs(z))
    softplus = jnp.maximum(z, 0.0) + jnp.log1p(e)
    log_a = -LRU_C * r * softplus
    a = jnp.exp(log_a)
    b = jnp.sqrt(jnp.tanh(-log_a) * (a * a + 1.0)) * (i * xc)

    row = lax.broadcasted_iota(jnp.int32, a.shape, 0) & (SUBLANES - 1)
    for dist in (1, 2, 4):
        a_prev = jnp.where(row >= dist, pltpu.roll(a, dist, 0), 1.0)
        b_prev = jnp.where(row >= dist, pltpu.roll(b, dist, 0), 0.0)
        b = a * b_prev + b
        a = a * a_prev
    a_ref[...] = a
    b_ref[...] = b

    def body(k, h):
        off = pl.multiple_of(k * SUBLANES, SUBLANES)
        rows = pl.ds(off, SUBLANES)
        hs = b_ref[rows, :] + a_ref[rows, :] * h
        b_ref[rows, :] = hs
        return jnp.broadcast_to(hs[SUBLANES - 1:SUBLANES, :], hs.shape)

    h_ref[...] = lax.fori_loop(0, SCAN_TILE // SUBLANES, body, h_ref[...])
    o_ref[0] = (b_ref[...] * jax.nn.gelu(gate)).astype(o_ref.dtype)


def _block_diag(w):
    g, n, _ = w.shape
    out = jnp.zeros((g * n, g * n), w.dtype)
    for k in range(g):
        out = out.at[k * n:(k + 1) * n, k * n:(k + 1) * n].set(w[k])
    return out


def _rglru(bx, conv_w, conv_b, wa, ba, wx, bxb, lam):
    b, s, _ = bx.shape
    w = B_WIDTH
    row = lambda v: v.reshape(1, w)
    return pl.pallas_call(
        _rglru_kernel,
        grid=(b, s // SCAN_TILE),
        in_specs=[pl.BlockSpec((1, SCAN_TILE, 2 * w), lambda bi, i: (bi, i, 0)),
                  _resident((CONV_W, w)), _resident((1, w)), _resident((w, w)), _resident((1, w)),
                  _resident((w, w)), _resident((1, w)), _resident((1, w))],
        out_specs=pl.BlockSpec((1, SCAN_TILE, w), lambda bi, i: (bi, i, 0)),
        out_shape=jax.ShapeDtypeStruct((b, s, w), BF16),
        scratch_shapes=[pltpu.VMEM((SUBLANES, w), F32), pltpu.VMEM((SUBLANES, w), F32),
                        pltpu.VMEM((SCAN_TILE, w), F32), pltpu.VMEM((SCAN_TILE, w), F32)],
        compiler_params=_cparams(2),
        name="rglru",
    )(bx, conv_w, row(conv_b), _block_diag(wa).astype(BF16), row(ba),
      _block_diag(wx).astype(BF16), row(bxb), row(lam))


def _compress_kernel(raw_ref, pos_ref, w1_ref, b1_ref, w2_ref, b2_ref, w2t_ref, b2c_ref, kc_ref, kct_ref):
    g, ncp, half = raw_ref.shape[2:]
    x = raw_ref[0, 0].reshape(g * ncp, half)
    top = (x + pos_ref[0, :, :half]).astype(BF16)
    bot = (x + pos_ref[0, :, half:]).astype(BF16)
    u = _dot(top, w1_ref[0, :half, :])
    v = _dot(bot, w1_ref[0, half:, :])
    hid = jax.nn.gelu(u + pltpu.roll(v, g * ncp - 1, 0) + b1_ref[0]).astype(BF16)
    kc_ref[0, 0] = (_dot(hid, w2_ref[0]) + b2_ref[0]).reshape(g, ncp, HEAD_DIM).astype(kc_ref.dtype)
    for gi in range(g):
        t = lax.dot_general(w2t_ref[0], hid[gi * ncp:(gi + 1) * ncp], (((1,), (1,)), ((), ())),
                            preferred_element_type=F32)
        kct_ref[0, 0, gi] = (t + b2c_ref[0]).astype(kct_ref.dtype)


def _compress(raw, cmp_pos, cmp_w1, cmp_b1, cmp_w2, cmp_b2):
    _, b, g, s, hd = raw.shape
    ncp = s // CMP_STRIDE
    half = CMP_STRIDE * hd
    hid = cmp_w1.shape[-1]
    raw16 = raw.reshape(2, b, g, ncp, half)
    sel = lambda *shape: pl.BlockSpec((1,) + shape, lambda kv, bi: (kv,) + (0,) * len(shape))
    return pl.pallas_call(
        _compress_kernel,
        grid=(2, b),
        in_specs=[pl.BlockSpec((1, 1, g, ncp, half), lambda kv, bi: (kv, bi, 0, 0, 0)),
                  sel(1, 2 * half), sel(2 * half, hid), sel(1, hid), sel(hid, hd), sel(1, hd),
                  sel(hd, hid), sel(hd, 1)],
        out_specs=[pl.BlockSpec((1, 1, g, ncp, hd), lambda kv, bi: (kv, bi, 0, 0, 0)),
                   pl.BlockSpec((1, 1, g, hd, ncp), lambda kv, bi: (kv, bi, 0, 0, 0))],
        out_shape=[jax.ShapeDtypeStruct((2, b, g, ncp, hd), BF16),
                   jax.ShapeDtypeStruct((2, b, g, hd, ncp), BF16)],
        compiler_params=_cparams(2),
        name="compress",
    )(raw16, cmp_pos.reshape(2, 1, 2 * half), cmp_w1.astype(BF16), cmp_b1.reshape(2, 1, hid),
      cmp_w2.astype(BF16), cmp_b2.reshape(2, 1, hd),
      cmp_w2.astype(BF16).transpose(0, 2, 1), cmp_b2.reshape(2, hd, 1))


def _softmax_step(carry, s, v_t):
    m, acc = carry
    m_new = jnp.maximum(m, jnp.max(s, axis=0, keepdims=True))
    p = jnp.exp2(s - m_new).astype(BF16)
    acc = acc * jnp.exp2(m - m_new) + _dot(v_t, p)
    return m_new, acc


def _normalized(acc):
    return acc[:HEAD_DIM] / acc[HEAD_DIM:HEAD_DIM + 1]


def _nsa_kernel(qt_ref, kc_ref, vct_ref, ks_ref, vst_ref, kw_ref, vwt_ref, gate_ref,
                gsel_ref, gwin_ref, gcmp_ref, ovt_ref, o_ref, *, n_tiles):
    step = pl.program_id(1)
    ncp = kc_ref.shape[2]
    chains = [(t, g) for t in range(TILES_PER_STEP) for g in range(C_KV_GROUPS)]
    tile = [step * TILES_PER_STEP + t for t in range(TILES_PER_STEP)]
    qts = [qt_ref[0, g, t] for t, g in chains]

    back = WINDOW // WIN_KEYS
    n_win = WINDOW + Q_TILE
    first = [jnp.maximum(c - back, 0) for c in tile]
    s_cmp, s_win = [], []
    for k, (t, g) in enumerate(chains):
        y0 = pl.multiple_of((n_tiles - 1 - tile[t]) * (Q_TILE // CMP_STRIDE), SUBLANES)
        s_cmp.append(_dot(kc_ref[0, g], qts[k]) + gcmp_ref[g, pl.ds(y0, ncp), :])
    for k, (t, g) in enumerate(chains):
        start = pl.multiple_of(first[t] * WIN_KEYS, WIN_KEYS)
        rel = pl.multiple_of(jnp.maximum(back - tile[t], 0) * WIN_KEYS, WIN_KEYS)
        s_win.append(_dot(kw_ref[0, g, pl.ds(start, n_win), :], qts[k]) + gwin_ref[g, pl.ds(rel, n_win), :])

    tq = lax.broadcasted_iota(jnp.int32, (1, QL), 1) & (Q_TILE - 1)
    o_cmp, imp = [], []
    for k, (t, g) in enumerate(chains):
        e = jnp.exp2(s_cmp[k] - jnp.max(s_cmp[k], axis=0, keepdims=True))
        has_cmp = (tile[t] * Q_TILE + tq >= CMP_LEN - 1).astype(F32)
        p = e / jnp.sum(e, axis=0, keepdims=True) * has_cmp
        o_cmp.append(_dot(vct_ref[0, g], p.astype(BF16)))
        p_heads = p[:, 0:Q_TILE]
        for r in range(1, C_HPG):
            p_heads = p_heads + p[:, r * Q_TILE:(r + 1) * Q_TILE]
        p_hi = p_heads.astype(BF16)
        p_lo = (p_heads - p_hi.astype(F32)).astype(BF16)
        imp.append(_dot(ovt_ref[...], p_hi) + _dot(ovt_ref[...], p_lo))

    o_win = []
    for k, (t, g) in enumerate(chains):
        e = jnp.exp2(s_win[k] - jnp.max(s_win[k], axis=0, keepdims=True)).astype(BF16)
        acc = jnp.zeros((V_ROWS, QL), F32)
        for i in range(n_win // WIN_KEYS):
            acc = acc + _dot(vwt_ref[0, g, first[t] + i], e[i * WIN_KEYS:(i + 1) * WIN_KEYS])
        o_win.append(_normalized(acc))

    q_aug = _with_mask_rows(qts, [tile[t] for t, g in chains], imp)
    halves = range(SEL_KEYS // SEL_HALF)
    init = (jnp.full((1, QL), NEG, F32), jnp.zeros((V_ROWS, QL), F32))

    def sel_body(i, carry):
        off = pl.multiple_of(i * SEL_KEYS, SEL_KEYS)
        s = []
        for h in halves:
            for k, (t, g) in enumerate(chains):
                x0 = (n_tiles - 1 - tile[t]) * Q_TILE
                s.append(_dot(ks_ref[0, g, pl.ds(off + h * SEL_HALF, SEL_HALF), :], q_aug[k])
                         + gsel_ref[g, pl.ds(pl.multiple_of(x0 + off + h * SEL_HALF, Q_TILE), SEL_HALF), :])
        carry = list(carry)
        for h in halves:
            for k, (t, g) in enumerate(chains):
                v_t = vst_ref[0, g, i][:, h * SEL_HALF:(h + 1) * SEL_HALF]
                carry[k] = _softmax_step(carry[k], s[h * len(chains) + k], v_t)
        return tuple(carry)

    n_sel_steps = (tile[-1] * Q_TILE + Q_TILE + SEL_KEYS - 1) // SEL_KEYS
    sel = lax.fori_loop(0, n_sel_steps, sel_body, (init,) * len(chains))

    for t in range(TILES_PER_STEP):
        outs = []
        for k, (tk, g) in enumerate(chains):
            if tk != t:
                continue
            o_sel = _normalized(sel[k][1])
            tok = slice(t * Q_TILE, (t + 1) * Q_TILE)
            cols = []
            for r in range(C_HPG):
                ln = slice(r * Q_TILE, (r + 1) * Q_TILE)
                cols.append(gate_ref[0, 0, g, r:r + 1, tok] * o_cmp[k][:, ln]
                            + gate_ref[0, 1, g, r:r + 1, tok] * o_sel[:, ln]
                            + gate_ref[0, 2, g, r:r + 1, tok] * o_win[k][:, ln])
            outs += [jnp.concatenate(cols[2 * j:2 * j + 2], axis=0).T for j in range(C_HPG // 2)]
        o_ref[0, t * Q_TILE:(t + 1) * Q_TILE, :] = jnp.concatenate(outs, axis=1).astype(o_ref.dtype)


def _with_mask_rows(qts, tiles, imps):
    n_groups = SEL_SLOTS // SUBLANES
    j = lax.broadcasted_iota(jnp.int32, (SEL_SLOTS, Q_TILE), 0)
    half = lax.shift_right_logical(lax.broadcasted_iota(jnp.int32, (SEL_SLOTS, Q_TILE), 1), SEL_SHIFT)
    j_rows = j[:SUBLANES]
    scores = []
    for c, imp in zip(tiles, imps):
        blk = c * (Q_TILE // SEL_LEN) + half
        forced = (j == 0) | (j == blk) | (j == blk - 1)
        scores.append(jnp.where(j <= blk, jnp.where(forced, FORCE_SCORE, imp), -1.0))
    rows = [[sc[k * SUBLANES:(k + 1) * SUBLANES] for k in range(n_groups)] for sc in scores]

    def add_pair(ranks, jg, k):
        for ci, sc in enumerate(scores):
            acc = ranks[ci][k]
            for jp in range(jg * SUBLANES, (jg + 1) * SUBLANES):
                other = sc[jp:jp + 1, :]
                if k > jg:
                    beats = jnp.where(other >= rows[ci][k], 1, 0)
                elif k < jg:
                    beats = jnp.where(other > rows[ci][k], 1, 0)
                else:
                    beats = jnp.where(j_rows > jp - k * SUBLANES, jnp.where(other >= rows[ci][k], 1, 0),
                                      jnp.where(other > rows[ci][k], 1, 0))
                acc = acc + beats
            ranks[ci][k] = acc

    last_blk = tiles[-1] * (Q_TILE // SEL_LEN) + Q_TILE // SEL_LEN - 1
    ranks = [[jnp.zeros((SUBLANES, Q_TILE), jnp.int32) for _ in range(n_groups)] for _ in scores]
    for m in range(n_groups):
        def shell(ranks, m=m):
            ranks = [list(r) for r in ranks]
            for k in range(m + 1):
                add_pair(ranks, m, k)
            for jg in range(m):
                add_pair(ranks, jg, m)
            return ranks
        ranks = shell(ranks) if m == 0 else lax.cond(m * SUBLANES <= last_blk, shell, lambda r: r, ranks)

    out = []
    for qt, sc, rk in zip(qts, scores, ranks):
        rank = jnp.concatenate(rk, axis=0)
        mask_rows = jnp.where((rank < SEL_TOP) & (sc >= 0.0), 0.0, NEG).astype(BF16)
        out.append(jnp.concatenate([qt, jnp.concatenate([mask_rows] * C_HPG, axis=1)], axis=0))
    return out


def _nsa(qt, kc, vct, ks, vst, kw, vwt, gates, gsel, gwin, gcmp, ovt):
    b, G, n_tiles = qt.shape[:3]
    s = n_tiles * Q_TILE
    per_b = lambda a: pl.BlockSpec((1,) + a.shape[1:], lambda bi, c: (bi,) + (0,) * (a.ndim - 1))
    return pl.pallas_call(
        functools.partial(_nsa_kernel, n_tiles=n_tiles),
        grid=(b, n_tiles // TILES_PER_STEP),
        in_specs=[pl.BlockSpec((1, G, TILES_PER_STEP, HEAD_DIM, QL), lambda bi, c: (bi, 0, c, 0, 0)),
                  per_b(kc), per_b(vct), per_b(ks), per_b(vst), per_b(kw), per_b(vwt),
                  pl.BlockSpec((1, 3, G, C_HPG, TILES_PER_STEP * Q_TILE), lambda bi, c: (bi, 0, 0, 0, c)),
                  _resident(gsel.shape), _resident(gwin.shape), _resident(gcmp.shape),
                  _resident(ovt.shape)],
        out_specs=pl.BlockSpec((1, TILES_PER_STEP * Q_TILE, C_WIDTH), lambda bi, c: (bi, c, 0)),
        out_shape=jax.ShapeDtypeStruct((b, s, C_WIDTH), BF16),
        compiler_params=_cparams(2),
        name="nsa",
    )(qt, kc, vct, ks, vst, kw, vwt, gates, gsel, gwin, gcmp, ovt)


def _outproj_kernel(h_ref, ya_ref, yb_ref, yc_ref, wa_ref, wb_ref, wc_ref, g_ref, o_ref):
    mix = _dot(ya_ref[...], wa_ref[...]) + _dot(yb_ref[...], wb_ref[...]) + _dot(yc_ref[...], wc_ref[...])
    o_ref[...] = h_ref[...] + _rms(mix, g_ref[...])


def _outproj(h, ya, yb, yc, w_out, g_norm):
    n, d = h.shape
    w = w_out.astype(BF16)
    wa, wb, wc = w[:A_WIDTH], w[A_WIDTH:A_WIDTH + B_WIDTH], w[A_WIDTH + B_WIDTH:]
    tile = lambda width: pl.BlockSpec((TOKEN_TILE, width), lambda i: (i, 0))
    return pl.pallas_call(
        _outproj_kernel,
        grid=(n // TOKEN_TILE,),
        in_specs=[tile(d), tile(A_WIDTH), tile(B_WIDTH), tile(C_WIDTH),
                  _resident(wa.shape), _resident(wb.shape), _resident(wc.shape), _resident((1, d))],
        out_specs=tile(d),
        out_shape=jax.ShapeDtypeStruct((n, d), F32),
        compiler_params=_cparams(1),
        name="outproj",
    )(h, ya, yb, yc, wa, wb, wc, g_norm.reshape(1, d))


def _ple_kernel(h_ref, p_ref, gpre_ref, gpost_ref, wg_ref, wp_ref, o_ref):
    x = h_ref[...]
    gate = _sigmoid(_dot(_rms(x, gpre_ref[...]).astype(BF16), wg_ref[...]))
    emb = _dot(p_ref[...].astype(BF16), wp_ref[...])
    o_ref[...] = x + _rms(gate * emb, gpost_ref[...])


def _ple(h, p, g_pre, g_post, w_gate, w_proj):
    n, d = h.shape
    dp = p.shape[-1]
    tile = lambda width: pl.BlockSpec((TOKEN_TILE, width), lambda i: (i, 0))
    return pl.pallas_call(
        _ple_kernel,
        grid=(n // TOKEN_TILE,),
        in_specs=[tile(d), tile(dp), _resident((1, d)), _resident((1, d)),
                  _resident((d, d)), _resident((dp, d))],
        out_specs=tile(d),
        out_shape=jax.ShapeDtypeStruct((n, d), F32),
        compiler_params=_cparams(1),
        name="ple",
    )(h, p, g_pre.reshape(1, d), g_post.reshape(1, d), w_gate.astype(BF16), w_proj.astype(BF16))


def _overlap_t(s):
    ncp = s // CMP_STRIDE
    n_cmp = (s - CMP_LEN) // CMP_STRIDE + 1
    cs = jnp.arange(ncp) * CMP_STRIDE
    ss = jnp.arange(s // SEL_LEN) * SEL_LEN
    ov = jnp.clip(jnp.minimum(cs[None] + CMP_LEN, ss[:, None] + SEL_LEN)
                  - jnp.maximum(cs[None], ss[:, None]), 0, None).astype(F32) / CMP_LEN
    ov = jnp.where(jnp.arange(ncp)[None] < n_cmp, ov, 0.0).astype(BF16)
    return jnp.pad(ov, ((0, SEL_SLOTS - s // SEL_LEN), (0, 0)))


def kernel(x, p, rel_bias, norm_g, ffn_w_gate, ffn_w_up, ffn_w_down, w_in, w_out, sgu_norm_g, sgu_w, sgu_b,
           conv_w, conv_b, lru_wa, lru_ba, lru_wx, lru_bx, lru_lambda, cmp_pos, cmp_w1, cmp_b1, cmp_w2,
           cmp_b2, ple_w_gate, ple_w_proj):
    b, s, d = x.shape
    depth = norm_g.shape[0]
    assert s % TOKEN_TILE == 0 and s % SCAN_TILE == 0 and s % SEL_KEYS == 0
    assert s % (TILES_PER_STEP * Q_TILE) == 0
    assert s >= WINDOW + Q_TILE and SEL_TOP <= s // SEL_LEN <= SEL_SLOTS
    n_tiles = s // Q_TILE

    rbx = jnp.repeat(rel_bias.reshape(N_BUCKETS, C_KV_GROUPS, C_HPG).transpose(1, 0, 2), Q_TILE, axis=2)
    no_limit = 1 << 30
    gsel = _bias_table(rbx, s + SEL_KEYS, 1, s - Q_TILE, no_limit)
    gwin = _bias_table(rbx, 2 * WINDOW + Q_TILE, 1, WINDOW, WINDOW)
    per_tile = Q_TILE // CMP_STRIDE
    gcmp = _bias_table(rbx, per_tile * (n_tiles - 1) + s // CMP_STRIDE, CMP_STRIDE,
                       CMP_STRIDE * per_tile * (n_tiles - 1) - (CMP_LEN - 1), no_limit)
    ovt = _overlap_t(s)

    h = x.reshape(b * s, d)
    for i in range(depth):
        g = norm_g[i]
        h = _ffn(h, g[0], g[1], ffn_w_gate[i, 0], ffn_w_up[i, 0], ffn_w_down[i, 0])
        ya, bx, qt, raw, ks, kw, vst, vwt, gates = _inproj(
            h.reshape(b, s, d), g[2], w_in[i], sgu_norm_g[i], sgu_w[i], sgu_b[i])
        yb = _rglru(bx, conv_w[i], conv_b[i], lru_wa[i], lru_ba[i], lru_wx[i], lru_bx[i], lru_lambda[i])
        kc, kct = _compress(raw, cmp_pos[i], cmp_w1[i], cmp_b1[i], cmp_w2[i], cmp_b2[i])
        yc = _nsa(qt, kc[0], kct[1], ks, vst, kw, vwt, gates, gsel, gwin, gcmp, ovt)
        h = _outproj(h, ya.reshape(b * s, -1), yb.reshape(b * s, -1), yc.reshape(b * s, -1), w_out[i], g[3])
        h = _ffn(h, g[4], g[5], ffn_w_gate[i, 1], ffn_w_up[i, 1], ffn_w_down[i, 1])
        h = _ple(h, p[i].reshape(b * s, -1), g[6], g[7], ple_w_gate[i], ple_w_proj[i])
    return h.reshape(b, s, d)
```

```python
import functools
import math

import jax
import jax.numpy as jnp
from jax import lax
from jax.experimental import pallas as pl
from jax.experimental.pallas import tpu as pltpu

F32 = jnp.float32
BF16 = jnp.bfloat16

RMS_EPS = 1e-6
A_GROUPS = 4
A_WIDTH = 256
A_CHUNK = 128
B_GROUPS = 4
B_WIDTH = 256
CONV_W = 4
LRU_C = 8.0
C_HEADS = 8
C_KV_GROUPS = 2
C_HPG = C_HEADS // C_KV_GROUPS
HEAD_DIM = 64
C_WIDTH = C_HEADS * HEAD_DIM
KV_W = C_KV_GROUPS * HEAD_DIM
CMP_LEN = 32
CMP_STRIDE = 16
SEL_LEN = 64
SEL_SHIFT = 6
SEL_SLOTS = 64
SEL_TOP = 16
WINDOW = 512
FORCE_SCORE = 1e4
NEG = -1e30
N_BUCKETS = 32
MAX_DISTANCE = 1024

LANES = 128
TOKEN_TILE = 512
FFN_CHUNK = 256
SCAN_TILE = 512
SUBLANES = 8
Q_TILE = 128
TILES_PER_STEP = 2
SEL_KEYS = 512
SEL_HALF = 256
WIN_KEYS = 128
QL = C_HPG * Q_TILE
V_PAD_ROWS = 16
V_ROWS = HEAD_DIM + V_PAD_ROWS
LOG2E = math.log2(math.e)
TABLE_ROWS = 128
VMEM_LIMIT = 56 * 1024 * 1024


def _cparams(n_axes):
    return pltpu.CompilerParams(dimension_semantics=("arbitrary",) * n_axes,
                                vmem_limit_bytes=VMEM_LIMIT)


def _resident(shape):
    nd = len(shape)
    return pl.BlockSpec(shape, lambda *_: (0,) * nd, pipeline_mode=pl.Buffered(1))


def _rms(x, g):
    return x * lax.rsqrt(jnp.mean(x * x, axis=-1, keepdims=True) + RMS_EPS) * g


def _sigmoid(x):
    return 1.0 / (1.0 + jnp.exp(-x))


def _dot(a, b):
    return jnp.dot(a, b, preferred_element_type=F32)


def _bias_table_kernel(rbx_ref, o_ref, *, stride, offset, dmax):
    i = pl.program_id(1)
    shape = (TABLE_ROWS, QL)
    x = lax.broadcasted_iota(jnp.int32, shape, 0) + i * TABLE_ROWS
    t = lax.broadcasted_iota(jnp.int32, shape, 1) & (Q_TILE - 1)
    d = t - stride * x + offset
    n = jnp.maximum(d, 0)
    max_exact = N_BUCKETS // 2
    nf = jnp.maximum(n, max_exact).astype(F32)
    large = max_exact + (jnp.log(nf / max_exact) / math.log(MAX_DISTANCE / max_exact)
                         * (N_BUCKETS - max_exact)).astype(jnp.int32)
    large = jnp.minimum(large, N_BUCKETS - 1)
    bucket = jnp.where(n < max_exact, n, large)
    acc = jnp.zeros(shape, F32)
    for k in range(N_BUCKETS):
        acc = jnp.where(bucket == k, rbx_ref[0, k:k + 1, :], acc)
    o_ref[0] = jnp.where((d >= 0) & (d < dmax), acc * LOG2E, NEG)


def _bias_table(rbx, rows, stride, offset, dmax):
    rows_p = -(-rows // TABLE_ROWS) * TABLE_ROWS
    return pl.pallas_call(
        functools.partial(_bias_table_kernel, stride=stride, offset=offset, dmax=dmax),
        grid=(C_KV_GROUPS, rows_p // TABLE_ROWS),
        in_specs=[pl.BlockSpec((1, N_BUCKETS, QL), lambda g, i: (g, 0, 0))],
        out_specs=pl.BlockSpec((1, TABLE_ROWS, QL), lambda g, i: (g, i, 0)),
        out_shape=jax.ShapeDtypeStruct((C_KV_GROUPS, rows_p, QL), F32),
        compiler_params=_cparams(2),
        name="bias_table",
    )(rbx)


def _ffn_kernel(h_ref, gpre_ref, gpost_ref, wg_ref, wu_ref, wd_ref, o_ref, acc_ref):
    x = h_ref[...]
    xn = _rms(x, gpre_ref[...]).astype(BF16)
    nch = wg_ref.shape[1] // FFN_CHUNK
    cols = lambda j: slice(j * FFN_CHUNK, (j + 1) * FFN_CHUNK)
    gate, up = _dot(xn, wg_ref[:, cols(0)]), _dot(xn, wu_ref[:, cols(0)])
    for j in range(nch):
        if j + 1 < nch:
            gate_next, up_next = _dot(xn, wg_ref[:, cols(j + 1)]), _dot(xn, wu_ref[:, cols(j + 1)])
        hid = (gate * _sigmoid(gate) * up).astype(BF16)
        down = _dot(hid, wd_ref[cols(j), :])
        if j == 0:
            acc_ref[...] = down
        else:
            acc_ref[...] += down
        gate, up = gate_next, up_next
    o_ref[...] = x + 0.5 * _rms(acc_ref[...], gpost_ref[...])


def _ffn(h, g_pre, g_post, wg, wu, wd, layer, which):
    n, d = h.shape
    assert wg.shape[-1] % FFN_CHUNK == 0
    tile = pl.BlockSpec((TOKEN_TILE, d), lambda i: (i, 0))
    weight = lambda w: pl.BlockSpec((None, None) + w.shape[2:], lambda i: (layer, which, 0, 0),
                                    pipeline_mode=pl.Buffered(1))
    return pl.pallas_call(
        _ffn_kernel,
        grid=(n // TOKEN_TILE,),
        in_specs=[tile, _resident((1, d)), _resident((1, d)), weight(wg), weight(wu), weight(wd)],
        out_specs=tile,
        out_shape=jax.ShapeDtypeStruct((n, d), F32),
        scratch_shapes=[pltpu.VMEM((TOKEN_TILE, d), F32)],
        compiler_params=_cparams(1),
        name="ffn",
    )(h, g_pre.reshape(1, d), g_post.reshape(1, d), wg, wu, wd)


def _inproj_kernel(h_ref, g_ref, wa_ref, wb_ref, wq_ref, wkv_ref, wgt_ref, sgn_ref, sgw_ref, sgb_ref,
                   ya_ref, bx_ref, qt_ref, raw_ref, ks_ref, kw_ref, vst_ref, vwt_ref, gt_ref):
    xn = _rms(h_ref[0], g_ref[...]).astype(BF16)

    za = _dot(xn, wa_ref[...])
    u = jax.nn.gelu(za[:, :A_WIDTH])
    v = _rms(jax.nn.gelu(za[:, A_WIDTH:]), sgn_ref[...]).astype(BF16)
    row = lax.broadcasted_iota(jnp.int32, (A_CHUNK, A_CHUNK), 0)
    col = lax.broadcasted_iota(jnp.int32, (A_CHUNK, A_CHUNK), 1)
    lane_group = lax.shift_right_logical(lax.broadcasted_iota(jnp.int32, (A_CHUNK, A_WIDTH), 1),
                                          (A_WIDTH // A_GROUPS).bit_length() - 1)
    w_tril = [jnp.where(row >= col, sgw_ref[g], 0.0).astype(BF16) for g in range(A_GROUPS)]
    for c in range(TOKEN_TILE // A_CHUNK):
        rows = slice(c * A_CHUNK, (c + 1) * A_CHUNK)
        mixed = jnp.zeros((A_CHUNK, A_WIDTH), F32)
        for g in range(A_GROUPS):
            mixed = jnp.where(lane_group == g, _dot(w_tril[g], v[rows]), mixed)
        ya_ref[0, rows, :] = (u[rows] * (mixed + sgb_ref[...])).astype(ya_ref.dtype)

    bx_ref[0] = _dot(xn, wb_ref[...])

    zq_t = (_dot(xn, wq_ref[...]) * (HEAD_DIM ** -0.5 * LOG2E)).T
    for g in range(C_KV_GROUPS):
        for c in range(TOKEN_TILE // Q_TILE):
            parts = []
            for r in range(C_HPG):
                base = (g * C_HPG + r) * HEAD_DIM
                parts.append(zq_t[base:base + HEAD_DIM, c * Q_TILE:(c + 1) * Q_TILE])
            qt_ref[0, g, c] = jnp.concatenate(parts, axis=1).astype(qt_ref.dtype)

    zkv = _dot(xn, wkv_ref[...])
    vs_t = zkv[:, 3 * KV_W:4 * KV_W].T
    vw_t = zkv[:, 5 * KV_W:6 * KV_W].T
    key_blk = lax.shift_right_logical(
        lax.broadcasted_iota(jnp.int32, (TOKEN_TILE, SEL_SLOTS), 0) + pl.program_id(1) * TOKEN_TILE, SEL_SHIFT)
    blk_onehot = jnp.where(key_blk == lax.broadcasted_iota(jnp.int32, (TOKEN_TILE, SEL_SLOTS), 1), 1.0, 0.0)
    ones_rows = jnp.where(lax.broadcasted_iota(jnp.int32, (V_PAD_ROWS, SEL_KEYS), 0) == 0, 1.0, 0.0)
    raw_ref[0, 0] = zkv[:, :KV_W]
    raw_ref[1, 0] = zkv[:, KV_W:2 * KV_W]
    for g in range(C_KV_GROUPS):
        lo, hi = g * HEAD_DIM, (g + 1) * HEAD_DIM
        ks_ref[0, g] = jnp.concatenate([zkv[:, 2 * KV_W + lo:2 * KV_W + hi], blk_onehot],
                                       axis=1).astype(ks_ref.dtype)
        kw_ref[0, g] = zkv[:, 4 * KV_W + lo:4 * KV_W + hi].astype(kw_ref.dtype)
        for c in range(TOKEN_TILE // SEL_KEYS):
            vst_ref[0, g, c] = jnp.concatenate(
                [vs_t[lo:hi, c * SEL_KEYS:(c + 1) * SEL_KEYS], ones_rows[:, :SEL_KEYS]], axis=0).astype(vst_ref.dtype)
        for c in range(TOKEN_TILE // WIN_KEYS):
            vwt_ref[0, g, c] = jnp.concatenate(
                [vw_t[lo:hi, c * WIN_KEYS:(c + 1) * WIN_KEYS], ones_rows[:, :WIN_KEYS]], axis=0).astype(vwt_ref.dtype)

    sg_t = _sigmoid(_dot(xn, wgt_ref[...])).T
    for br in range(3):
        for g in range(C_KV_GROUPS):
            base = br * C_HEADS + g * C_HPG
            gt_ref[0, br, g] = sg_t[base:base + C_HPG, :]


def _inproj(h, g_norm, w_in, sgu_norm_g, sgu_w, sgu_b):
    b, s, d = h.shape
    wb16 = w_in.astype(BF16)
    o = 0
    wa = wb16[:, o:o + 2 * A_WIDTH]; o += 2 * A_WIDTH
    wb = wb16[:, o:o + 2 * B_WIDTH]; o += 2 * B_WIDTH
    wq = wb16[:, o:o + C_WIDTH]; o += C_WIDTH
    wkv = wb16[:, o:o + 6 * KV_W]; o += 6 * KV_W
    wgt = jnp.pad(wb16[:, o:o + 3 * C_HEADS], ((0, 0), (0, LANES - 3 * C_HEADS)))
    sgb = jnp.repeat(sgu_b.T, A_WIDTH // A_GROUPS, axis=1)
    nt = s // TOKEN_TILE
    grid = (b, nt)
    G = C_KV_GROUPS
    out_shape = [
        jax.ShapeDtypeStruct((b, s, A_WIDTH), BF16),
        jax.ShapeDtypeStruct((b, s, 2 * B_WIDTH), F32),
        jax.ShapeDtypeStruct((b, G, s // Q_TILE, HEAD_DIM, QL), BF16),
        jax.ShapeDtypeStruct((2, b, s, KV_W), F32),
        jax.ShapeDtypeStruct((b, G, s, HEAD_DIM + SEL_SLOTS), BF16),
        jax.ShapeDtypeStruct((b, G, s, HEAD_DIM), BF16),
        jax.ShapeDtypeStruct((b, G, s // SEL_KEYS, V_ROWS, SEL_KEYS), BF16),
        jax.ShapeDtypeStruct((b, G, s // WIN_KEYS, V_ROWS, WIN_KEYS), BF16),
        jax.ShapeDtypeStruct((b, 3, G, C_HPG, s), F32),
    ]
    out_specs = [
        pl.BlockSpec((1, TOKEN_TILE, A_WIDTH), lambda bi, i: (bi, i, 0)),
        pl.BlockSpec((1, TOKEN_TILE, 2 * B_WIDTH), lambda bi, i: (bi, i, 0)),
        pl.BlockSpec((1, G, TOKEN_TILE // Q_TILE, HEAD_DIM, QL), lambda bi, i: (bi, 0, i, 0, 0)),
        pl.BlockSpec((2, 1, TOKEN_TILE, KV_W), lambda bi, i: (0, bi, i, 0)),
        pl.BlockSpec((1, G, TOKEN_TILE, HEAD_DIM + SEL_SLOTS), lambda bi, i: (bi, 0, i, 0)),
        pl.BlockSpec((1, G, TOKEN_TILE, HEAD_DIM), lambda bi, i: (bi, 0, i, 0)),
        pl.BlockSpec((1, G, TOKEN_TILE // SEL_KEYS, V_ROWS, SEL_KEYS), lambda bi, i: (bi, 0, i, 0, 0)),
        pl.BlockSpec((1, G, TOKEN_TILE // WIN_KEYS, V_ROWS, WIN_KEYS), lambda bi, i: (bi, 0, i, 0, 0)),
        pl.BlockSpec((1, 3, G, C_HPG, TOKEN_TILE), lambda bi, i: (bi, 0, 0, 0, i)),
    ]
    in_specs = [
        pl.BlockSpec((1, TOKEN_TILE, d), lambda bi, i: (bi, i, 0)),
        _resident((1, d)), _resident(wa.shape), _resident(wb.shape), _resident(wq.shape),
        _resident(wkv.shape), _resident(wgt.shape), _resident((1, A_WIDTH)),
        _resident(sgu_w.shape), _resident(sgb.shape),
    ]
    return pl.pallas_call(
        _inproj_kernel, grid=grid, in_specs=in_specs, out_specs=out_specs, out_shape=out_shape,
        compiler_params=_cparams(2), name="inproj",
    )(h, g_norm.reshape(1, d), wa, wb, wq, wkv, wgt, sgu_norm_g.reshape(1, A_WIDTH), sgu_w, sgb)


def _rglru_kernel(bx_ref, cw_ref, cb_ref, wa_ref, ba_ref, wx_ref, bxb_ref, lam_ref, o_ref,
                  tail_ref, h_ref, a_ref, b_ref):
    @pl.when(pl.program_id(1) == 0)
    def _():
        tail_ref[...] = jnp.zeros_like(tail_ref)
        h_ref[...] = jnp.zeros_like(h_ref)

    xb = bx_ref[0, :, :B_WIDTH]
    gate = bx_ref[0, :, B_WIDTH:]
    ext = jnp.concatenate([tail_ref[...], xb], axis=0)
    xc = cb_ref[...] + xb * cw_ref[CONV_W - 1:CONV_W, :]
    for k in range(CONV_W - 1):
        shift = CONV_W - 1 - k
        xc = xc + ext[SUBLANES - shift:SUBLANES - shift + SCAN_TILE] * cw_ref[k:k + 1, :]
    tail_ref[...] = xb[SCAN_TILE - SUBLANES:]

    xcb = xc.astype(BF16)
    r = _sigmoid(_dot(xcb, wa_ref[...]) + ba_ref[...])
    i = _sigmoid(_dot(xcb, wx_ref[...]) + bxb_ref[...])
    z = -lam_ref[...]
    e = jnp.exp(-jnp.abs(z))
    softplus = jnp.maximum(z, 0.0) + jnp.log1p(e)
    log_a = -LRU_C * r * softplus
    a = jnp.exp(log_a)
    b = jnp.sqrt(jnp.tanh(-log_a) * (a * a + 1.0)) * (i * xc)

    row = lax.broadcasted_iota(jnp.int32, a.shape, 0) & (SUBLANES - 1)
    for dist in (1, 2, 4):
        a_prev = jnp.where(row >= dist, pltpu.roll(a, dist, 0), 1.0)
        b_prev = jnp.where(row >= dist, pltpu.roll(b, dist, 0), 0.0)
        b = a * b_prev + b
        a = a * a_prev
    a_ref[...] = a
    b_ref[...] = b

    def body(k, h):
        off = pl.multiple_of(k * SUBLANES, SUBLANES)
        rows = pl.ds(off, SUBLANES)
        hs = b_ref[rows, :] + a_ref[rows, :] * h
        b_ref[rows, :] = hs
        return jnp.broadcast_to(hs[SUBLANES - 1:SUBLANES, :], hs.shape)

    h_ref[...] = lax.fori_loop(0, SCAN_TILE // SUBLANES, body, h_ref[...])
    o_ref[0] = (b_ref[...] * jax.nn.gelu(gate)).astype(o_ref.dtype)


def _block_diag(w):
    g, n, _ = w.shape
    out = jnp.zeros((g * n, g * n), w.dtype)
    for k in range(g):
        out = out.at[k * n:(k + 1) * n, k * n:(k + 1) * n].set(w[k])
    return out


def _rglru(bx, conv_w, conv_b, wa, ba, wx, bxb, lam):
    b, s, _ = bx.shape
    w = B_WIDTH
    row = lambda v: v.reshape(1, w)
    return pl.pallas_call(
        _rglru_kernel,
        grid=(b, s // SCAN_TILE),
        in_specs=[pl.BlockSpec((1, SCAN_TILE, 2 * w), lambda bi, i: (bi, i, 0)),
                  _resident((CONV_W, w)), _resident((1, w)), _resident((w, w)), _resident((1, w)),
                  _resident((w, w)), _resident((1, w)), _resident((1, w))],
        out_specs=pl.BlockSpec((1, SCAN_TILE, w), lambda bi, i: (bi, i, 0)),
        out_shape=jax.ShapeDtypeStruct((b, s, w), BF16),
        scratch_shapes=[pltpu.VMEM((SUBLANES, w), F32), pltpu.VMEM((SUBLANES, w), F32),
                        pltpu.VMEM((SCAN_TILE, w), F32), pltpu.VMEM((SCAN_TILE, w), F32)],
        compiler_params=_cparams(2),
        name="rglru",
    )(bx, conv_w, row(conv_b), _block_diag(wa).astype(BF16), row(ba),
      _block_diag(wx).astype(BF16), row(bxb), row(lam))


def _compress_kernel(raw_ref, pos_ref, w1_ref, b1_ref, w2_ref, b2_ref, w2t_ref, b2c_ref, kc_ref, kct_ref):
    ncp = raw_ref.shape[2] // CMP_STRIDE
    groups = range(C_KV_GROUPS)
    hid_dim = w1_ref.shape[-1]
    u = [jnp.zeros((ncp, hid_dim), F32) for _ in groups]
    v = [jnp.zeros((ncp, hid_dim), F32) for _ in groups]
    for l in range(CMP_STRIDE):
        x = raw_ref[0, 0, pl.ds(l, ncp, stride=CMP_STRIDE), :]
        top = (x + pos_ref[0, l:l + 1, :]).astype(BF16)
        bot = (x + pos_ref[0, CMP_STRIDE + l:CMP_STRIDE + l + 1, :]).astype(BF16)
        for g in groups:
            u[g] = u[g] + _dot(top, w1_ref[0, g, l])
            v[g] = v[g] + _dot(bot, w1_ref[0, g, CMP_STRIDE + l])
    for g in groups:
        hid = jax.nn.gelu(u[g] + pltpu.roll(v[g], ncp - 1, 0) + b1_ref[0]).astype(BF16)
        kc_ref[0, 0, g] = (_dot(hid, w2_ref[0]) + b2_ref[0]).astype(kc_ref.dtype)
        t = lax.dot_general(w2t_ref[0], hid, (((1,), (1,)), ((), ())), preferred_element_type=F32)
        kct_ref[0, 0, g] = (t + b2c_ref[0]).astype(kct_ref.dtype)


def _compress(raw, cmp_pos, cmp_w1, cmp_b1, cmp_w2, cmp_b2):
    _, b, s, _ = raw.shape
    g, hd = C_KV_GROUPS, HEAD_DIM
    ncp = s // CMP_STRIDE
    hid = cmp_w1.shape[-1]
    w1 = cmp_w1.astype(BF16).reshape(2, CMP_LEN, hd, hid)
    zero = jnp.zeros_like(w1)
    w1g = jnp.stack([jnp.concatenate([w1 if k == gi else zero for k in range(g)], axis=2)
                     for gi in range(g)], axis=1)
    pos = jnp.tile(cmp_pos, (1, 1, g))
    sel = lambda *shape: pl.BlockSpec((1,) + shape, lambda kv, bi: (kv,) + (0,) * len(shape))
    return pl.pallas_call(
        _compress_kernel,
        grid=(2, b),
        in_specs=[pl.BlockSpec((1, 1, s, g * hd), lambda kv, bi: (kv, bi, 0, 0)),
                  sel(CMP_LEN, g * hd), sel(g, CMP_LEN, g * hd, hid), sel(1, hid), sel(hid, hd), sel(1, hd),
                  sel(hd, hid), sel(hd, 1)],
        out_specs=[pl.BlockSpec((1, 1, g, ncp, hd), lambda kv, bi: (kv, bi, 0, 0, 0)),
                   pl.BlockSpec((1, 1, g, hd, ncp), lambda kv, bi: (kv, bi, 0, 0, 0))],
        out_shape=[jax.ShapeDtypeStruct((2, b, g, ncp, hd), BF16),
                   jax.ShapeDtypeStruct((2, b, g, hd, ncp), BF16)],
        compiler_params=_cparams(2),
        name="compress",
    )(raw, pos, w1g, cmp_b1.reshape(2, 1, hid), cmp_w2.astype(BF16), cmp_b2.reshape(2, 1, hd),
      cmp_w2.astype(BF16).transpose(0, 2, 1), cmp_b2.reshape(2, hd, 1))


def _softmax_step(carry, s, v_t):
    m, acc = carry
    m_new = jnp.maximum(m, jnp.max(s, axis=0, keepdims=True))
    p = jnp.exp2(s - m_new).astype(BF16)
    acc = acc * jnp.exp2(m - m_new) + _dot(v_t, p)
    return m_new, acc


def _normalized(acc):
    return acc[:HEAD_DIM] / acc[HEAD_DIM:HEAD_DIM + 1]


def _nsa_kernel(qt_ref, kc_ref, vct_ref, ks_ref, vst_ref, kw_ref, vwt_ref, gate_ref,
                gsel_ref, gwin_ref, gcmp_ref, ovt_ref, o_ref, *, n_tiles):
    step = pl.program_id(1)
    ncp = kc_ref.shape[2]
    chains = [(t, g) for t in range(TILES_PER_STEP) for g in range(C_KV_GROUPS)]
    tile = [step * TILES_PER_STEP + t for t in range(TILES_PER_STEP)]
    qts = [qt_ref[0, g, t] for t, g in chains]

    back = WINDOW // WIN_KEYS
    n_win = WINDOW + Q_TILE
    first = [jnp.maximum(c - back, 0) for c in tile]
    s_cmp, s_win = [], []
    for k, (t, g) in enumerate(chains):
        y0 = pl.multiple_of((n_tiles - 1 - tile[t]) * (Q_TILE // CMP_STRIDE), SUBLANES)
        s_cmp.append(_dot(kc_ref[0, g], qts[k]) + gcmp_ref[g, pl.ds(y0, ncp), :])
    for k, (t, g) in enumerate(chains):
        start = pl.multiple_of(first[t] * WIN_KEYS, WIN_KEYS)
        rel = pl.multiple_of(jnp.maximum(back - tile[t], 0) * WIN_KEYS, WIN_KEYS)
        s_win.append(_dot(kw_ref[0, g, pl.ds(start, n_win), :], qts[k]) + gwin_ref[g, pl.ds(rel, n_win), :])

    tq = lax.broadcasted_iota(jnp.int32, (1, QL), 1) & (Q_TILE - 1)
    o_cmp, imp = [], []
    for k, (t, g) in enumerate(chains):
        e = jnp.exp2(s_cmp[k] - jnp.max(s_cmp[k], axis=0, keepdims=True))
        has_cmp = (tile[t] * Q_TILE + tq >= CMP_LEN - 1).astype(F32)
        p = e / jnp.sum(e, axis=0, keepdims=True) * has_cmp
        o_cmp.append(_dot(vct_ref[0, g], p.astype(BF16)))
        p_heads = p[:, 0:Q_TILE]
        for r in range(1, C_HPG):
            p_heads = p_heads + p[:, r * Q_TILE:(r + 1) * Q_TILE]
        p_hi = p_heads.astype(BF16)
        p_lo = (p_heads - p_hi.astype(F32)).astype(BF16)
        imp.append(_dot(ovt_ref[...], p_hi) + _dot(ovt_ref[...], p_lo))

    o_win = []
    for k, (t, g) in enumerate(chains):
        e = jnp.exp2(s_win[k] - jnp.max(s_win[k], axis=0, keepdims=True)).astype(BF16)
        acc = jnp.zeros((V_ROWS, QL), F32)
        for i in range(n_win // WIN_KEYS):
            acc = acc + _dot(vwt_ref[0, g, first[t] + i], e[i * WIN_KEYS:(i + 1) * WIN_KEYS])
        o_win.append(_normalized(acc))

    q_aug = _with_mask_rows(qts, [tile[t] for t, g in chains], imp)
    halves = range(SEL_KEYS // SEL_HALF)
    init = (jnp.full((1, QL), NEG, F32), jnp.zeros((V_ROWS, QL), F32))

    def sel_body(i, carry, far):
        off = pl.multiple_of(i * SEL_KEYS, SEL_KEYS)
        s = []
        for h in halves:
            for k, (t, g) in enumerate(chains):
                logits = _dot(ks_ref[0, g, pl.ds(off + h * SEL_HALF, SEL_HALF), :], q_aug[k])
                if not far:
                    x0 = (n_tiles - 1 - tile[t]) * Q_TILE
                    logits = logits + gsel_ref[
                        g, pl.ds(pl.multiple_of(x0 + off + h * SEL_HALF, Q_TILE), SEL_HALF), :]
                s.append(logits)
        carry = list(carry)
        for h in halves:
            for k, (t, g) in enumerate(chains):
                v_t = vst_ref[0, g, i][:, h * SEL_HALF:(h + 1) * SEL_HALF]
                carry[k] = _softmax_step(carry[k], s[h * len(chains) + k], v_t)
        return tuple(carry)

    n_far = jnp.maximum(tile[0] * Q_TILE - MAX_DISTANCE + 1, 0) // SEL_KEYS
    n_sel_steps = (tile[-1] * Q_TILE + Q_TILE + SEL_KEYS - 1) // SEL_KEYS
    sel = lax.fori_loop(0, n_far, functools.partial(sel_body, far=True), (init,) * len(chains))
    sel = tuple((m + gsel_ref[g, 0:1, :], acc) for (m, acc), (t, g) in zip(sel, chains))
    sel = lax.fori_loop(n_far, n_sel_steps, functools.partial(sel_body, far=False), sel)

    for t in range(TILES_PER_STEP):
        outs = []
        for k, (tk, g) in enumerate(chains):
            if tk != t:
                continue
            o_sel = _normalized(sel[k][1])
            tok = slice(t * Q_TILE, (t + 1) * Q_TILE)
            cols = []
            for r in range(C_HPG):
                ln = slice(r * Q_TILE, (r + 1) * Q_TILE)
                cols.append(gate_ref[0, 0, g, r:r + 1, tok] * o_cmp[k][:, ln]
                            + gate_ref[0, 1, g, r:r + 1, tok] * o_sel[:, ln]
                            + gate_ref[0, 2, g, r:r + 1, tok] * o_win[k][:, ln])
            outs += [jnp.concatenate(cols[2 * j:2 * j + 2], axis=0).T for j in range(C_HPG // 2)]
        o_ref[0, t * Q_TILE:(t + 1) * Q_TILE, :] = jnp.concatenate(outs, axis=1).astype(o_ref.dtype)


def _with_mask_rows(qts, tiles, imps):
    n_groups = SEL_SLOTS // SUBLANES
    j = lax.broadcasted_iota(jnp.int32, (SEL_SLOTS, Q_TILE), 0)
    half = lax.shift_right_logical(lax.broadcasted_iota(jnp.int32, (SEL_SLOTS, Q_TILE), 1), SEL_SHIFT)
    j_rows = j[:SUBLANES]
    scores = []
    for c, imp in zip(tiles, imps):
        blk = c * (Q_TILE // SEL_LEN) + half
        forced = (j == 0) | (j == blk) | (j == blk - 1)
        scores.append(jnp.where(j <= blk, jnp.where(forced, FORCE_SCORE, imp), -1.0))
    rows = [[sc[k * SUBLANES:(k + 1) * SUBLANES] for k in range(n_groups)] for sc in scores]

    def add_pair(ranks, jg, k):
        for ci, sc in enumerate(scores):
            acc = ranks[ci][k]
            for jp in range(jg * SUBLANES, (jg + 1) * SUBLANES):
                other = sc[jp:jp + 1, :]
                if k > jg:
                    beats = jnp.where(other >= rows[ci][k], 1, 0)
                elif k < jg:
                    beats = jnp.where(other > rows[ci][k], 1, 0)
                else:
                    beats = jnp.where(j_rows > jp - k * SUBLANES, jnp.where(other >= rows[ci][k], 1, 0),
                                      jnp.where(other > rows[ci][k], 1, 0))
                acc = acc + beats
            ranks[ci][k] = acc

    last_blk = tiles[-1] * (Q_TILE // SEL_LEN) + Q_TILE // SEL_LEN - 1
    ranks = [[jnp.zeros((SUBLANES, Q_TILE), jnp.int32) for _ in range(n_groups)] for _ in scores]
    for m in range(n_groups):
        def shell(ranks, m=m):
            ranks = [list(r) for r in ranks]
            for k in range(m + 1):
                add_pair(ranks, m, k)
            for jg in range(m):
                add_pair(ranks, jg, m)
            return ranks
        ranks = shell(ranks) if m == 0 else lax.cond(m * SUBLANES <= last_blk, shell, lambda r: r, ranks)

    out = []
    for qt, sc, rk in zip(qts, scores, ranks):
        rank = jnp.concatenate(rk, axis=0)
        mask_rows = jnp.where((rank < SEL_TOP) & (sc >= 0.0), 0.0, NEG).astype(BF16)
        out.append(jnp.concatenate([qt, jnp.concatenate([mask_rows] * C_HPG, axis=1)], axis=0))
    return out


def _nsa(qt, kc, vct, ks, vst, kw, vwt, gates, gsel, gwin, gcmp, ovt):
    b, G, n_tiles = qt.shape[:3]
    s = n_tiles * Q_TILE
    per_b = lambda a: pl.BlockSpec((1,) + a.shape[1:], lambda bi, c: (bi,) + (0,) * (a.ndim - 1))
    return pl.pallas_call(
        functools.partial(_nsa_kernel, n_tiles=n_tiles),
        grid=(b, n_tiles // TILES_PER_STEP),
        in_specs=[pl.BlockSpec((1, G, TILES_PER_STEP, HEAD_DIM, QL), lambda bi, c: (bi, 0, c, 0, 0)),
                  per_b(kc), per_b(vct), per_b(ks), per_b(vst), per_b(kw), per_b(vwt),
                  pl.BlockSpec((1, 3, G, C_HPG, TILES_PER_STEP * Q_TILE), lambda bi, c: (bi, 0, 0, 0, c)),
                  _resident(gsel.shape), _resident(gwin.shape), _resident(gcmp.shape),
                  _resident(ovt.shape)],
        out_specs=pl.BlockSpec((1, TILES_PER_STEP * Q_TILE, C_WIDTH), lambda bi, c: (bi, c, 0)),
        out_shape=jax.ShapeDtypeStruct((b, s, C_WIDTH), BF16),
        compiler_params=_cparams(2),
        name="nsa",
    )(qt, kc, vct, ks, vst, kw, vwt, gates, gsel, gwin, gcmp, ovt)


def _outproj_kernel(h_ref, ya_ref, yb_ref, yc_ref, wa_ref, wb_ref, wc_ref, g_ref, o_ref):
    mix = _dot(ya_ref[...], wa_ref[...]) + _dot(yb_ref[...], wb_ref[...]) + _dot(yc_ref[...], wc_ref[...])
    o_ref[...] = h_ref[...] + _rms(mix, g_ref[...])


def _outproj(h, ya, yb, yc, w_out, g_norm):
    n, d = h.shape
    w = w_out.astype(BF16)
    wa, wb, wc = w[:A_WIDTH], w[A_WIDTH:A_WIDTH + B_WIDTH], w[A_WIDTH + B_WIDTH:]
    tile = lambda width: pl.BlockSpec((TOKEN_TILE, width), lambda i: (i, 0))
    return pl.pallas_call(
        _outproj_kernel,
        grid=(n // TOKEN_TILE,),
        in_specs=[tile(d), tile(A_WIDTH), tile(B_WIDTH), tile(C_WIDTH),
                  _resident(wa.shape), _resident(wb.shape), _resident(wc.shape), _resident((1, d))],
        out_specs=tile(d),
        out_shape=jax.ShapeDtypeStruct((n, d), F32),
        compiler_params=_cparams(1),
        name="outproj",
    )(h, ya, yb, yc, wa, wb, wc, g_norm.reshape(1, d))


def _ple_kernel(h_ref, p_ref, gpre_ref, gpost_ref, wg_ref, wp_ref, o_ref):
    x = h_ref[...]
    gate = _sigmoid(_dot(_rms(x, gpre_ref[...]).astype(BF16), wg_ref[...]))
    emb = _dot(p_ref[...].astype(BF16), wp_ref[...])
    o_ref[...] = x + _rms(gate * emb, gpost_ref[...])


def _ple(h, p, layer, g_pre, g_post, w_gate, w_proj):
    n, d = h.shape
    dp = p.shape[-1]
    tile = lambda width: pl.BlockSpec((TOKEN_TILE, width), lambda i: (i, 0))
    return pl.pallas_call(
        _ple_kernel,
        grid=(n // TOKEN_TILE,),
        in_specs=[tile(d), pl.BlockSpec((None, TOKEN_TILE, dp), lambda i: (layer, i, 0)),
                  _resident((1, d)), _resident((1, d)), _resident((d, d)), _resident((dp, d))],
        out_specs=tile(d),
        out_shape=jax.ShapeDtypeStruct((n, d), F32),
        compiler_params=_cparams(1),
        name="ple",
    )(h, p, g_pre.reshape(1, d), g_post.reshape(1, d), w_gate.astype(BF16), w_proj.astype(BF16))


def _overlap_t(s):
    ncp = s // CMP_STRIDE
    n_cmp = (s - CMP_LEN) // CMP_STRIDE + 1
    cs = jnp.arange(ncp) * CMP_STRIDE
    ss = jnp.arange(s // SEL_LEN) * SEL_LEN
    ov = jnp.clip(jnp.minimum(cs[None] + CMP_LEN, ss[:, None] + SEL_LEN)
                  - jnp.maximum(cs[None], ss[:, None]), 0, None).astype(F32) / CMP_LEN
    ov = jnp.where(jnp.arange(ncp)[None] < n_cmp, ov, 0.0).astype(BF16)
    return jnp.pad(ov, ((0, SEL_SLOTS - s // SEL_LEN), (0, 0)))


def kernel(x, p, rel_bias, norm_g, ffn_w_gate, ffn_w_up, ffn_w_down, w_in, w_out, sgu_norm_g, sgu_w, sgu_b,
           conv_w, conv_b, lru_wa, lru_ba, lru_wx, lru_bx, lru_lambda, cmp_pos, cmp_w1, cmp_b1, cmp_w2,
           cmp_b2, ple_w_gate, ple_w_proj):
    b, s, d = x.shape
    depth = norm_g.shape[0]
    assert s % TOKEN_TILE == 0 and s % SCAN_TILE == 0 and s % SEL_KEYS == 0
    assert s % (TILES_PER_STEP * Q_TILE) == 0
    assert s >= WINDOW + Q_TILE and SEL_TOP <= s // SEL_LEN <= SEL_SLOTS
    n_tiles = s // Q_TILE

    rbx = jnp.repeat(rel_bias.reshape(N_BUCKETS, C_KV_GROUPS, C_HPG).transpose(1, 0, 2), Q_TILE, axis=2)
    no_limit = 1 << 30
    gsel = _bias_table(rbx, s + SEL_KEYS, 1, s - Q_TILE, no_limit)
    gwin = _bias_table(rbx, 2 * WINDOW + Q_TILE, 1, WINDOW, WINDOW)
    per_tile = Q_TILE // CMP_STRIDE
    gcmp = _bias_table(rbx, per_tile * (n_tiles - 1) + s // CMP_STRIDE, CMP_STRIDE,
                       CMP_STRIDE * per_tile * (n_tiles - 1) - (CMP_LEN - 1), no_limit)
    ovt = _overlap_t(s)

    ffn_w = (ffn_w_gate.astype(BF16), ffn_w_up.astype(BF16), ffn_w_down.astype(BF16))
    h = x.reshape(b * s, d)
    for i in range(depth):
        g = norm_g[i]
        h = _ffn(h, g[0], g[1], *ffn_w, i, 0)
        ya, bx, qt, raw, ks, kw, vst, vwt, gates = _inproj(
            h.reshape(b, s, d), g[2], w_in[i], sgu_norm_g[i], sgu_w[i], sgu_b[i])
        yb = _rglru(bx, conv_w[i], conv_b[i], lru_wa[i], lru_ba[i], lru_wx[i], lru_bx[i], lru_lambda[i])
        kc, kct = _compress(raw, cmp_pos[i], cmp_w1[i], cmp_b1[i], cmp_w2[i], cmp_b2[i])
        yc = _nsa(qt, kc[0], kct[1], ks, vst, kw, vwt, gates, gsel, gwin, gcmp, ovt)
        h = _outproj(h, ya.reshape(b * s, -1), yb.reshape(b * s, -1), yc.reshape(b * s, -1), w_out[i], g[3])
        h = _ffn(h, g[4], g[5], *ffn_w, i, 1)
        h = _ple(h, p.reshape(depth, b * s, -1), i, g[6], g[7], ple_w_gate[i], ple_w_proj[i])
    return h.reshape(b, s, d)
```

```python
import functools
import math

import jax
import jax.numpy as jnp
from jax import lax
from jax.experimental import pallas as pl
from jax.experimental.pallas import tpu as pltpu

F32 = jnp.float32
BF16 = jnp.bfloat16

RMS_EPS = 1e-6
A_GROUPS = 4
A_WIDTH = 256
A_CHUNK = 128
B_GROUPS = 4
B_WIDTH = 256
CONV_W = 4
LRU_C = 8.0
C_HEADS = 8
C_KV_GROUPS = 2
C_HPG = C_HEADS // C_KV_GROUPS
HEAD_DIM = 64
C_WIDTH = C_HEADS * HEAD_DIM
KV_W = C_KV_GROUPS * HEAD_DIM
CMP_LEN = 32
CMP_STRIDE = 16
SEL_LEN = 64
SEL_SHIFT = 6
SEL_SLOTS = 64
SEL_TOP = 16
WINDOW = 512
FORCE_SCORE = 1e4
NEG = -1e30
N_BUCKETS = 32
MAX_DISTANCE = 1024

LANES = 128
TOKEN_TILE = 512
FFN_CHUNK = 256
SCAN_TILE = 512
SUBLANES = 8
Q_TILE = 128
TILES_PER_STEP = 2
SEL_KEYS = 512
SEL_HALF = 256
WIN_KEYS = 128
QL = C_HPG * Q_TILE
V_PAD_ROWS = 16
V_ROWS = HEAD_DIM + V_PAD_ROWS
LOG2E = math.log2(math.e)
TABLE_ROWS = 128
VMEM_LIMIT = 56 * 1024 * 1024


def _cparams(n_axes):
    return pltpu.CompilerParams(dimension_semantics=("arbitrary",) * n_axes,
                                vmem_limit_bytes=VMEM_LIMIT)


def _resident(shape):
    nd = len(shape)
    return pl.BlockSpec(shape, lambda *_: (0,) * nd, pipeline_mode=pl.Buffered(1))


def _rms(x, g):
    return x * lax.rsqrt(jnp.mean(x * x, axis=-1, keepdims=True) + RMS_EPS) * g


def _sigmoid(x):
    return 1.0 / (1.0 + jnp.exp(-x))


def _dot(a, b):
    return jnp.dot(a, b, preferred_element_type=F32)


def _bias_table_kernel(rbx_ref, o_ref, *, stride, offset, dmax):
    i = pl.program_id(1)
    shape = (TABLE_ROWS, QL)
    x = lax.broadcasted_iota(jnp.int32, shape, 0) + i * TABLE_ROWS
    t = lax.broadcasted_iota(jnp.int32, shape, 1) & (Q_TILE - 1)
    d = t - stride * x + offset
    n = jnp.maximum(d, 0)
    max_exact = N_BUCKETS // 2
    nf = jnp.maximum(n, max_exact).astype(F32)
    large = max_exact + (jnp.log(nf / max_exact) / math.log(MAX_DISTANCE / max_exact)
                         * (N_BUCKETS - max_exact)).astype(jnp.int32)
    large = jnp.minimum(large, N_BUCKETS - 1)
    bucket = jnp.where(n < max_exact, n, large)
    acc = jnp.zeros(shape, F32)
    for k in range(N_BUCKETS):
        acc = jnp.where(bucket == k, rbx_ref[0, k:k + 1, :], acc)
    o_ref[0] = jnp.where((d >= 0) & (d < dmax), acc * LOG2E, NEG)


def _bias_table(rbx, rows, stride, offset, dmax):
    rows_p = -(-rows // TABLE_ROWS) * TABLE_ROWS
    return pl.pallas_call(
        functools.partial(_bias_table_kernel, stride=stride, offset=offset, dmax=dmax),
        grid=(C_KV_GROUPS, rows_p // TABLE_ROWS),
        in_specs=[pl.BlockSpec((1, N_BUCKETS, QL), lambda g, i: (g, 0, 0))],
        out_specs=pl.BlockSpec((1, TABLE_ROWS, QL), lambda g, i: (g, i, 0)),
        out_shape=jax.ShapeDtypeStruct((C_KV_GROUPS, rows_p, QL), F32),
        compiler_params=_cparams(2),
        name="bias_table",
    )(rbx)


def _ffn_body(x, g_pre, g_post, wg_ref, wu_ref, wd_ref, acc_ref):
    xn = _rms(x, g_pre).astype(BF16)
    nch = wg_ref.shape[1] // FFN_CHUNK
    cols = lambda j: slice(j * FFN_CHUNK, (j + 1) * FFN_CHUNK)
    gate, up = _dot(xn, wg_ref[:, cols(0)]), _dot(xn, wu_ref[:, cols(0)])
    for j in range(nch):
        if j + 1 < nch:
            gate_next, up_next = _dot(xn, wg_ref[:, cols(j + 1)]), _dot(xn, wu_ref[:, cols(j + 1)])
        hid = (gate * _sigmoid(gate) * up).astype(BF16)
        down = _dot(hid, wd_ref[cols(j), :])
        if j == 0:
            acc_ref[...] = down
        else:
            acc_ref[...] += down
        gate, up = gate_next, up_next
    return x + 0.5 * _rms(acc_ref[...], g_post)


def _ffn_weight_spec(w, layer, which):
    return pl.BlockSpec((None, None) + w.shape[2:], lambda *_: (layer, which, 0, 0), pipeline_mode=pl.Buffered(1))


def _head_kernel(h_ref, norm_ref, wg_ref, wu_ref, wd_ref, wa_ref, wb_ref, wq_ref, wkv_ref, wgt_ref,
                 sgn_ref, sgw_ref, sgb_ref,
                 h_out_ref, ya_ref, bx_ref, qt_ref, raw_ref, ks_ref, kw_ref, vst_ref, vwt_ref, gt_ref, acc_ref):
    h = _ffn_body(h_ref[0], norm_ref[0:1, :], norm_ref[1:2, :], wg_ref, wu_ref, wd_ref, acc_ref)
    h_out_ref[0] = h
    xn = _rms(h, norm_ref[2:3, :]).astype(BF16)

    za = _dot(xn, wa_ref[...])
    u = jax.nn.gelu(za[:, :A_WIDTH])
    v = _rms(jax.nn.gelu(za[:, A_WIDTH:]), sgn_ref[...]).astype(BF16)
    row = lax.broadcasted_iota(jnp.int32, (A_CHUNK, A_CHUNK), 0)
    col = lax.broadcasted_iota(jnp.int32, (A_CHUNK, A_CHUNK), 1)
    lane_group = lax.shift_right_logical(lax.broadcasted_iota(jnp.int32, (A_CHUNK, A_WIDTH), 1),
                                          (A_WIDTH // A_GROUPS).bit_length() - 1)
    w_tril = [jnp.where(row >= col, sgw_ref[g], 0.0).astype(BF16) for g in range(A_GROUPS)]
    for c in range(TOKEN_TILE // A_CHUNK):
        rows = slice(c * A_CHUNK, (c + 1) * A_CHUNK)
        mixed = jnp.zeros((A_CHUNK, A_WIDTH), F32)
        for g in range(A_GROUPS):
            mixed = jnp.where(lane_group == g, _dot(w_tril[g], v[rows]), mixed)
        ya_ref[0, rows, :] = (u[rows] * (mixed + sgb_ref[...])).astype(ya_ref.dtype)

    bx_ref[0] = _dot(xn, wb_ref[...])

    zq_t = (_dot(xn, wq_ref[...]) * (HEAD_DIM ** -0.5 * LOG2E)).T
    for g in range(C_KV_GROUPS):
        for c in range(TOKEN_TILE // Q_TILE):
            parts = []
            for r in range(C_HPG):
                base = (g * C_HPG + r) * HEAD_DIM
                parts.append(zq_t[base:base + HEAD_DIM, c * Q_TILE:(c + 1) * Q_TILE])
            qt_ref[0, g, c] = jnp.concatenate(parts, axis=1).astype(qt_ref.dtype)

    zkv = _dot(xn, wkv_ref[...])
    vs_t = zkv[:, 3 * KV_W:4 * KV_W].T
    vw_t = zkv[:, 5 * KV_W:6 * KV_W].T
    key_blk = lax.shift_right_logical(
        lax.broadcasted_iota(jnp.int32, (TOKEN_TILE, SEL_SLOTS), 0) + pl.program_id(1) * TOKEN_TILE, SEL_SHIFT)
    blk_onehot = jnp.where(key_blk == lax.broadcasted_iota(jnp.int32, (TOKEN_TILE, SEL_SLOTS), 1), 1.0, 0.0)
    ones_rows = jnp.where(lax.broadcasted_iota(jnp.int32, (V_PAD_ROWS, SEL_KEYS), 0) == 0, 1.0, 0.0)
    raw_ref[0, 0] = zkv[:, :KV_W]
    raw_ref[1, 0] = zkv[:, KV_W:2 * KV_W]
    for g in range(C_KV_GROUPS):
        lo, hi = g * HEAD_DIM, (g + 1) * HEAD_DIM
        ks_ref[0, g] = jnp.concatenate([zkv[:, 2 * KV_W + lo:2 * KV_W + hi], blk_onehot],
                                       axis=1).astype(ks_ref.dtype)
        kw_ref[0, g] = zkv[:, 4 * KV_W + lo:4 * KV_W + hi].astype(kw_ref.dtype)
        for c in range(TOKEN_TILE // SEL_KEYS):
            vst_ref[0, g, c] = jnp.concatenate(
                [vs_t[lo:hi, c * SEL_KEYS:(c + 1) * SEL_KEYS], ones_rows[:, :SEL_KEYS]], axis=0).astype(vst_ref.dtype)
        for c in range(TOKEN_TILE // WIN_KEYS):
            vwt_ref[0, g, c] = jnp.concatenate(
                [vw_t[lo:hi, c * WIN_KEYS:(c + 1) * WIN_KEYS], ones_rows[:, :WIN_KEYS]], axis=0).astype(vwt_ref.dtype)

    sg_t = _sigmoid(_dot(xn, wgt_ref[...])).T
    for br in range(3):
        for g in range(C_KV_GROUPS):
            base = br * C_HEADS + g * C_HPG
            gt_ref[0, br, g] = sg_t[base:base + C_HPG, :]


def _head(h, norms, ffn_w, layer, w_in, sgu_norm_g, sgu_w, sgu_b):
    b, s, d = h.shape
    assert ffn_w[0].shape[-1] % FFN_CHUNK == 0
    wb16 = w_in.astype(BF16)
    o = 0
    wa = wb16[:, o:o + 2 * A_WIDTH]; o += 2 * A_WIDTH
    wb = wb16[:, o:o + 2 * B_WIDTH]; o += 2 * B_WIDTH
    wq = wb16[:, o:o + C_WIDTH]; o += C_WIDTH
    wkv = wb16[:, o:o + 6 * KV_W]; o += 6 * KV_W
    wgt = jnp.pad(wb16[:, o:o + 3 * C_HEADS], ((0, 0), (0, LANES - 3 * C_HEADS)))
    sgb = jnp.repeat(sgu_b.T, A_WIDTH // A_GROUPS, axis=1)
    nt = s // TOKEN_TILE
    grid = (b, nt)
    G = C_KV_GROUPS
    out_shape = [
        jax.ShapeDtypeStruct((b, s, d), F32),
        jax.ShapeDtypeStruct((b, s, A_WIDTH), BF16),
        jax.ShapeDtypeStruct((b, s, 2 * B_WIDTH), F32),
        jax.ShapeDtypeStruct((b, G, s // Q_TILE, HEAD_DIM, QL), BF16),
        jax.ShapeDtypeStruct((2, b, s, KV_W), F32),
        jax.ShapeDtypeStruct((b, G, s, HEAD_DIM + SEL_SLOTS), BF16),
        jax.ShapeDtypeStruct((b, G, s, HEAD_DIM), BF16),
        jax.ShapeDtypeStruct((b, G, s // SEL_KEYS, V_ROWS, SEL_KEYS), BF16),
        jax.ShapeDtypeStruct((b, G, s // WIN_KEYS, V_ROWS, WIN_KEYS), BF16),
        jax.ShapeDtypeStruct((b, 3, G, C_HPG, s), F32),
    ]
    out_specs = [
        pl.BlockSpec((1, TOKEN_TILE, d), lambda bi, i: (bi, i, 0)),
        pl.BlockSpec((1, TOKEN_TILE, A_WIDTH), lambda bi, i: (bi, i, 0)),
        pl.BlockSpec((1, TOKEN_TILE, 2 * B_WIDTH), lambda bi, i: (bi, i, 0)),
        pl.BlockSpec((1, G, TOKEN_TILE // Q_TILE, HEAD_DIM, QL), lambda bi, i: (bi, 0, i, 0, 0)),
        pl.BlockSpec((2, 1, TOKEN_TILE, KV_W), lambda bi, i: (0, bi, i, 0)),
        pl.BlockSpec((1, G, TOKEN_TILE, HEAD_DIM + SEL_SLOTS), lambda bi, i: (bi, 0, i, 0)),
        pl.BlockSpec((1, G, TOKEN_TILE, HEAD_DIM), lambda bi, i: (bi, 0, i, 0)),
        pl.BlockSpec((1, G, TOKEN_TILE // SEL_KEYS, V_ROWS, SEL_KEYS), lambda bi, i: (bi, 0, i, 0, 0)),
        pl.BlockSpec((1, G, TOKEN_TILE // WIN_KEYS, V_ROWS, WIN_KEYS), lambda bi, i: (bi, 0, i, 0, 0)),
        pl.BlockSpec((1, 3, G, C_HPG, TOKEN_TILE), lambda bi, i: (bi, 0, 0, 0, i)),
    ]
    in_specs = [
        pl.BlockSpec((1, TOKEN_TILE, d), lambda bi, i: (bi, i, 0)),
        _resident(norms.shape), *[_ffn_weight_spec(w, layer, 0) for w in ffn_w],
        _resident(wa.shape), _resident(wb.shape), _resident(wq.shape),
        _resident(wkv.shape), _resident(wgt.shape), _resident((1, A_WIDTH)),
        _resident(sgu_w.shape), _resident(sgb.shape),
    ]
    return pl.pallas_call(
        _head_kernel, grid=grid, in_specs=in_specs, out_specs=out_specs, out_shape=out_shape,
        scratch_shapes=[pltpu.VMEM((TOKEN_TILE, d), F32)],
        compiler_params=_cparams(2), name="head",
    )(h, norms, *ffn_w, wa, wb, wq, wkv, wgt, sgu_norm_g.reshape(1, A_WIDTH), sgu_w, sgb)


def _rglru_kernel(bx_ref, cw_ref, cb_ref, wa_ref, ba_ref, wx_ref, bxb_ref, lam_ref, o_ref,
                  tail_ref, h_ref, a_ref, b_ref):
    @pl.when(pl.program_id(1) == 0)
    def _():
        tail_ref[...] = jnp.zeros_like(tail_ref)
        h_ref[...] = jnp.zeros_like(h_ref)

    xb = bx_ref[0, :, :B_WIDTH]
    gate = bx_ref[0, :, B_WIDTH:]
    ext = jnp.concatenate([tail_ref[...], xb], axis=0)
    xc = cb_ref[...] + xb * cw_ref[CONV_W - 1:CONV_W, :]
    for k in range(CONV_W - 1):
        shift = CONV_W - 1 - k
        xc = xc + ext[SUBLANES - shift:SUBLANES - shift + SCAN_TILE] * cw_ref[k:k + 1, :]
    tail_ref[...] = xb[SCAN_TILE - SUBLANES:]

    xcb = xc.astype(BF16)
    r = _sigmoid(_dot(xcb, wa_ref[...]) + ba_ref[...])
    i = _sigmoid(_dot(xcb, wx_ref[...]) + bxb_ref[...])
    z = -lam_ref[...]
    e = jnp.exp(-jnp.abs(z))
    softplus = jnp.maximum(z, 0.0) + jnp.log1p(e)
    log_a = -LRU_C * r * softplus
    a = jnp.exp(log_a)
    b = jnp.sqrt(jnp.tanh(-log_a) * (a * a + 1.0)) * (i * xc)

    row = lax.broadcasted_iota(jnp.int32, a.shape, 0) & (SUBLANES - 1)
    for dist in (1, 2, 4):
        a_prev = jnp.where(row >= dist, pltpu.roll(a, dist, 0), 1.0)
        b_prev = jnp.where(row >= dist, pltpu.roll(b, dist, 0), 0.0)
        b = a * b_prev + b
        a = a * a_prev
    a_ref[...] = a
    b_ref[...] = b

    def body(k, h):
        off = pl.multiple_of(k * SUBLANES, SUBLANES)
        rows = pl.ds(off, SUBLANES)
        hs = b_ref[rows, :] + a_ref[rows, :] * h
        b_ref[rows, :] = hs
        return jnp.broadcast_to(hs[SUBLANES - 1:SUBLANES, :], hs.shape)

    h_ref[...] = lax.fori_loop(0, SCAN_TILE // SUBLANES, body, h_ref[...])
    o_ref[0] = (b_ref[...] * jax.nn.gelu(gate)).astype(o_ref.dtype)


def _block_diag(w):
    g, n, _ = w.shape
    out = jnp.zeros((g * n, g * n), w.dtype)
    for k in range(g):
        out = out.at[k * n:(k + 1) * n, k * n:(k + 1) * n].set(w[k])
    return out


def _rglru(bx, conv_w, conv_b, wa, ba, wx, bxb, lam):
    b, s, _ = bx.shape
    w = B_WIDTH
    row = lambda v: v.reshape(1, w)
    return pl.pallas_call(
        _rglru_kernel,
        grid=(b, s // SCAN_TILE),
        in_specs=[pl.BlockSpec((1, SCAN_TILE, 2 * w), lambda bi, i: (bi, i, 0)),
                  _resident((CONV_W, w)), _resident((1, w)), _resident((w, w)), _resident((1, w)),
                  _resident((w, w)), _resident((1, w)), _resident((1, w))],
        out_specs=pl.BlockSpec((1, SCAN_TILE, w), lambda bi, i: (bi, i, 0)),
        out_shape=jax.ShapeDtypeStruct((b, s, w), BF16),
        scratch_shapes=[pltpu.VMEM((SUBLANES, w), F32), pltpu.VMEM((SUBLANES, w), F32),
                        pltpu.VMEM((SCAN_TILE, w), F32), pltpu.VMEM((SCAN_TILE, w), F32)],
        compiler_params=_cparams(2),
        name="rglru",
    )(bx, conv_w, row(conv_b), _block_diag(wa).astype(BF16), row(ba),
      _block_diag(wx).astype(BF16), row(bxb), row(lam))


def _compress_kernel(raw_ref, pos_ref, w1_ref, b1_ref, w2_ref, b2_ref, w2t_ref, b2c_ref, kc_ref, kct_ref):
    ncp = raw_ref.shape[2] // CMP_STRIDE
    groups = range(C_KV_GROUPS)
    hid_dim = w1_ref.shape[-1]
    u = [jnp.zeros((ncp, hid_dim), F32) for _ in groups]
    v = [jnp.zeros((ncp, hid_dim), F32) for _ in groups]
    for l in range(CMP_STRIDE):
        x = raw_ref[0, 0, pl.ds(l, ncp, stride=CMP_STRIDE), :]
        top = (x + pos_ref[0, l:l + 1, :]).astype(BF16)
        bot = (x + pos_ref[0, CMP_STRIDE + l:CMP_STRIDE + l + 1, :]).astype(BF16)
        for g in groups:
            u[g] = u[g] + _dot(top, w1_ref[0, g, l])
            v[g] = v[g] + _dot(bot, w1_ref[0, g, CMP_STRIDE + l])
    for g in groups:
        hid = jax.nn.gelu(u[g] + pltpu.roll(v[g], ncp - 1, 0) + b1_ref[0]).astype(BF16)
        kc_ref[0, 0, g] = (_dot(hid, w2_ref[0]) + b2_ref[0]).astype(kc_ref.dtype)
        t = lax.dot_general(w2t_ref[0], hid, (((1,), (1,)), ((), ())), preferred_element_type=F32)
        kct_ref[0, 0, g] = (t + b2c_ref[0]).astype(kct_ref.dtype)


def _compress(raw, cmp_pos, cmp_w1, cmp_b1, cmp_w2, cmp_b2):
    _, b, s, _ = raw.shape
    g, hd = C_KV_GROUPS, HEAD_DIM
    ncp = s // CMP_STRIDE
    hid = cmp_w1.shape[-1]
    w1 = cmp_w1.astype(BF16).reshape(2, CMP_LEN, hd, hid)
    zero = jnp.zeros_like(w1)
    w1g = jnp.stack([jnp.concatenate([w1 if k == gi else zero for k in range(g)], axis=2)
                     for gi in range(g)], axis=1)
    pos = jnp.tile(cmp_pos, (1, 1, g))
    sel = lambda *shape: pl.BlockSpec((1,) + shape, lambda kv, bi: (kv,) + (0,) * len(shape))
    return pl.pallas_call(
        _compress_kernel,
        grid=(2, b),
        in_specs=[pl.BlockSpec((1, 1, s, g * hd), lambda kv, bi: (kv, bi, 0, 0)),
                  sel(CMP_LEN, g * hd), sel(g, CMP_LEN, g * hd, hid), sel(1, hid), sel(hid, hd), sel(1, hd),
                  sel(hd, hid), sel(hd, 1)],
        out_specs=[pl.BlockSpec((1, 1, g, ncp, hd), lambda kv, bi: (kv, bi, 0, 0, 0)),
                   pl.BlockSpec((1, 1, g, hd, ncp), lambda kv, bi: (kv, bi, 0, 0, 0))],
        out_shape=[jax.ShapeDtypeStruct((2, b, g, ncp, hd), BF16),
                   jax.ShapeDtypeStruct((2, b, g, hd, ncp), BF16)],
        compiler_params=_cparams(2),
        name="compress",
    )(raw, pos, w1g, cmp_b1.reshape(2, 1, hid), cmp_w2.astype(BF16), cmp_b2.reshape(2, 1, hd),
      cmp_w2.astype(BF16).transpose(0, 2, 1), cmp_b2.reshape(2, hd, 1))


def _softmax_step(carry, s, v_t):
    m, acc = carry
    m_new = jnp.maximum(m, jnp.max(s, axis=0, keepdims=True))
    p = jnp.exp2(s - m_new).astype(BF16)
    acc = acc * jnp.exp2(m - m_new) + _dot(v_t, p)
    return m_new, acc


def _normalized(acc):
    return acc[:HEAD_DIM] / acc[HEAD_DIM:HEAD_DIM + 1]


def _nsa_kernel(qt_ref, kc_ref, vct_ref, ks_ref, vst_ref, kw_ref, vwt_ref, gate_ref,
                gsel_ref, gwin_ref, gcmp_ref, ovt_ref, o_ref, *, n_tiles):
    step = pl.program_id(1)
    ncp = kc_ref.shape[2]
    chains = [(t, g) for t in range(TILES_PER_STEP) for g in range(C_KV_GROUPS)]
    tile = [step * TILES_PER_STEP + t for t in range(TILES_PER_STEP)]
    qts = [qt_ref[0, g, t] for t, g in chains]

    back = WINDOW // WIN_KEYS
    n_win = WINDOW + Q_TILE
    first = [jnp.maximum(c - back, 0) for c in tile]
    s_cmp, s_win = [], []
    for k, (t, g) in enumerate(chains):
        y0 = pl.multiple_of((n_tiles - 1 - tile[t]) * (Q_TILE // CMP_STRIDE), SUBLANES)
        s_cmp.append(_dot(kc_ref[0, g], qts[k]) + gcmp_ref[g, pl.ds(y0, ncp), :])
    for k, (t, g) in enumerate(chains):
        start = pl.multiple_of(first[t] * WIN_KEYS, WIN_KEYS)
        rel = pl.multiple_of(jnp.maximum(back - tile[t], 0) * WIN_KEYS, WIN_KEYS)
        s_win.append(_dot(kw_ref[0, g, pl.ds(start, n_win), :], qts[k]) + gwin_ref[g, pl.ds(rel, n_win), :])

    tq = lax.broadcasted_iota(jnp.int32, (1, QL), 1) & (Q_TILE - 1)
    o_cmp, imp = [], []
    for k, (t, g) in enumerate(chains):
        e = jnp.exp2(s_cmp[k] - jnp.max(s_cmp[k], axis=0, keepdims=True))
        has_cmp = (tile[t] * Q_TILE + tq >= CMP_LEN - 1).astype(F32)
        p = e / jnp.sum(e, axis=0, keepdims=True) * has_cmp
        o_cmp.append(_dot(vct_ref[0, g], p.astype(BF16)))
        p_heads = p[:, 0:Q_TILE]
        for r in range(1, C_HPG):
            p_heads = p_heads + p[:, r * Q_TILE:(r + 1) * Q_TILE]
        p_hi = p_heads.astype(BF16)
        p_lo = (p_heads - p_hi.astype(F32)).astype(BF16)
        imp.append(_dot(ovt_ref[...], p_hi) + _dot(ovt_ref[...], p_lo))

    o_win = []
    for k, (t, g) in enumerate(chains):
        e = jnp.exp2(s_win[k] - jnp.max(s_win[k], axis=0, keepdims=True)).astype(BF16)
        acc = jnp.zeros((V_ROWS, QL), F32)
        for i in range(n_win // WIN_KEYS):
            acc = acc + _dot(vwt_ref[0, g, first[t] + i], e[i * WIN_KEYS:(i + 1) * WIN_KEYS])
        o_win.append(_normalized(acc))

    q_aug = _with_mask_rows(qts, [tile[t] for t, g in chains], imp)
    halves = range(SEL_KEYS // SEL_HALF)
    init = (jnp.full((1, QL), NEG, F32), jnp.zeros((V_ROWS, QL), F32))

    def sel_body(i, carry, far):
        off = pl.multiple_of(i * SEL_KEYS, SEL_KEYS)
        s = []
        for h in halves:
            for k, (t, g) in enumerate(chains):
                logits = _dot(ks_ref[0, g, pl.ds(off + h * SEL_HALF, SEL_HALF), :], q_aug[k])
                if not far:
                    x0 = (n_tiles - 1 - tile[t]) * Q_TILE
                    logits = logits + gsel_ref[
                        g, pl.ds(pl.multiple_of(x0 + off + h * SEL_HALF, Q_TILE), SEL_HALF), :]
                s.append(logits)
        carry = list(carry)
        for h in halves:
            for k, (t, g) in enumerate(chains):
                v_t = vst_ref[0, g, i][:, h * SEL_HALF:(h + 1) * SEL_HALF]
                carry[k] = _softmax_step(carry[k], s[h * len(chains) + k], v_t)
        return tuple(carry)

    n_far = jnp.maximum(tile[0] * Q_TILE - MAX_DISTANCE + 1, 0) // SEL_KEYS
    n_sel_steps = (tile[-1] * Q_TILE + Q_TILE + SEL_KEYS - 1) // SEL_KEYS
    sel = lax.fori_loop(0, n_far, functools.partial(sel_body, far=True), (init,) * len(chains))
    sel = tuple((m + gsel_ref[g, 0:1, :], acc) for (m, acc), (t, g) in zip(sel, chains))
    sel = lax.fori_loop(n_far, n_sel_steps, functools.partial(sel_body, far=False), sel)

    for t in range(TILES_PER_STEP):
        outs = []
        for k, (tk, g) in enumerate(chains):
            if tk != t:
                continue
            o_sel = _normalized(sel[k][1])
            tok = slice(t * Q_TILE, (t + 1) * Q_TILE)
            cols = []
            for r in range(C_HPG):
                ln = slice(r * Q_TILE, (r + 1) * Q_TILE)
                cols.append(gate_ref[0, 0, g, r:r + 1, tok] * o_cmp[k][:, ln]
                            + gate_ref[0, 1, g, r:r + 1, tok] * o_sel[:, ln]
                            + gate_ref[0, 2, g, r:r + 1, tok] * o_win[k][:, ln])
            outs += [jnp.concatenate(cols[2 * j:2 * j + 2], axis=0).T for j in range(C_HPG // 2)]
        o_ref[0, t * Q_TILE:(t + 1) * Q_TILE, :] = jnp.concatenate(outs, axis=1).astype(o_ref.dtype)


def _with_mask_rows(qts, tiles, imps):
    n_groups = SEL_SLOTS // SUBLANES
    j = lax.broadcasted_iota(jnp.int32, (SEL_SLOTS, Q_TILE), 0)
    half = lax.shift_right_logical(lax.broadcasted_iota(jnp.int32, (SEL_SLOTS, Q_TILE), 1), SEL_SHIFT)
    j_rows = j[:SUBLANES]
    scores = []
    for c, imp in zip(tiles, imps):
        blk = c * (Q_TILE // SEL_LEN) + half
        forced = (j == 0) | (j == blk) | (j == blk - 1)
        scores.append(jnp.where(j <= blk, jnp.where(forced, FORCE_SCORE, imp), -1.0))
    rows = [[sc[k * SUBLANES:(k + 1) * SUBLANES] for k in range(n_groups)] for sc in scores]

    def add_pair(ranks, jg, k):
        for ci, sc in enumerate(scores):
            acc = ranks[ci][k]
            for jp in range(jg * SUBLANES, (jg + 1) * SUBLANES):
                other = sc[jp:jp + 1, :]
                if k > jg:
                    beats = jnp.where(other >= rows[ci][k], 1, 0)
                elif k < jg:
                    beats = jnp.where(other > rows[ci][k], 1, 0)
                else:
                    beats = jnp.where(j_rows > jp - k * SUBLANES, jnp.where(other >= rows[ci][k], 1, 0),
                                      jnp.where(other > rows[ci][k], 1, 0))
                acc = acc + beats
            ranks[ci][k] = acc

    last_blk = tiles[-1] * (Q_TILE // SEL_LEN) + Q_TILE // SEL_LEN - 1
    ranks = [[jnp.zeros((SUBLANES, Q_TILE), jnp.int32) for _ in range(n_groups)] for _ in scores]
    for m in range(n_groups):
        def shell(ranks, m=m):
            ranks = [list(r) for r in ranks]
            for k in range(m + 1):
                add_pair(ranks, m, k)
            for jg in range(m):
                add_pair(ranks, jg, m)
            return ranks
        ranks = shell(ranks) if m == 0 else lax.cond(m * SUBLANES <= last_blk, shell, lambda r: r, ranks)

    out = []
    for qt, sc, rk in zip(qts, scores, ranks):
        rank = jnp.concatenate(rk, axis=0)
        mask_rows = jnp.where((rank < SEL_TOP) & (sc >= 0.0), 0.0, NEG).astype(BF16)
        out.append(jnp.concatenate([qt, jnp.concatenate([mask_rows] * C_HPG, axis=1)], axis=0))
    return out


def _nsa(qt, kc, vct, ks, vst, kw, vwt, gates, gsel, gwin, gcmp, ovt):
    b, G, n_tiles = qt.shape[:3]
    s = n_tiles * Q_TILE
    per_b = lambda a: pl.BlockSpec((1,) + a.shape[1:], lambda bi, c: (bi,) + (0,) * (a.ndim - 1))
    return pl.pallas_call(
        functools.partial(_nsa_kernel, n_tiles=n_tiles),
        grid=(b, n_tiles // TILES_PER_STEP),
        in_specs=[pl.BlockSpec((1, G, TILES_PER_STEP, HEAD_DIM, QL), lambda bi, c: (bi, 0, c, 0, 0)),
                  per_b(kc), per_b(vct), per_b(ks), per_b(vst), per_b(kw), per_b(vwt),
                  pl.BlockSpec((1, 3, G, C_HPG, TILES_PER_STEP * Q_TILE), lambda bi, c: (bi, 0, 0, 0, c)),
                  _resident(gsel.shape), _resident(gwin.shape), _resident(gcmp.shape),
                  _resident(ovt.shape)],
        out_specs=pl.BlockSpec((1, TILES_PER_STEP * Q_TILE, C_WIDTH), lambda bi, c: (bi, c, 0)),
        out_shape=jax.ShapeDtypeStruct((b, s, C_WIDTH), BF16),
        compiler_params=_cparams(2),
        name="nsa",
    )(qt, kc, vct, ks, vst, kw, vwt, gates, gsel, gwin, gcmp, ovt)


def _tail_kernel(h_ref, ya_ref, yb_ref, yc_ref, p_ref, norm_ref, woa_ref, wob_ref, woc_ref,
                 wg_ref, wu_ref, wd_ref, wpg_ref, wpp_ref, o_ref, acc_ref):
    mix = _dot(ya_ref[...], woa_ref[...]) + _dot(yb_ref[...], wob_ref[...]) + _dot(yc_ref[...], woc_ref[...])
    emb = _dot(p_ref[...].astype(BF16), wpp_ref[...])
    h = h_ref[...] + _rms(mix, norm_ref[3:4, :])
    h = _ffn_body(h, norm_ref[4:5, :], norm_ref[5:6, :], wg_ref, wu_ref, wd_ref, acc_ref)
    gate = _sigmoid(_dot(_rms(h, norm_ref[6:7, :]).astype(BF16), wpg_ref[...]))
    o_ref[...] = h + _rms(gate * emb, norm_ref[7:8, :])


def _tail(h, ya, yb, yc, p, norms, ffn_w, layer, w_out, w_gate, w_proj):
    n, d = h.shape
    dp = p.shape[-1]
    w = w_out.astype(BF16)
    woa, wob, woc = w[:A_WIDTH], w[A_WIDTH:A_WIDTH + B_WIDTH], w[A_WIDTH + B_WIDTH:]
    tile = lambda width: pl.BlockSpec((TOKEN_TILE, width), lambda i: (i, 0))
    return pl.pallas_call(
        _tail_kernel,
        grid=(n // TOKEN_TILE,),
        in_specs=[tile(d), tile(A_WIDTH), tile(B_WIDTH), tile(C_WIDTH),
                  pl.BlockSpec((None, TOKEN_TILE, dp), lambda i: (layer, i, 0)),
                  _resident(norms.shape), _resident(woa.shape), _resident(wob.shape), _resident(woc.shape),
                  *[_ffn_weight_spec(wt, layer, 1) for wt in ffn_w],
                  _resident((d, d)), _resident((dp, d))],
        out_specs=tile(d),
        out_shape=jax.ShapeDtypeStruct((n, d), F32),
        scratch_shapes=[pltpu.VMEM((TOKEN_TILE, d), F32)],
        compiler_params=_cparams(1),
        name="tail",
    )(h, ya, yb, yc, p, norms, woa, wob, woc, *ffn_w, w_gate.astype(BF16), w_proj.astype(BF16))


def _overlap_t(s):
    ncp = s // CMP_STRIDE
    n_cmp = (s - CMP_LEN) // CMP_STRIDE + 1
    cs = jnp.arange(ncp) * CMP_STRIDE
    ss = jnp.arange(s // SEL_LEN) * SEL_LEN
    ov = jnp.clip(jnp.minimum(cs[None] + CMP_LEN, ss[:, None] + SEL_LEN)
                  - jnp.maximum(cs[None], ss[:, None]), 0, None).astype(F32) / CMP_LEN
    ov = jnp.where(jnp.arange(ncp)[None] < n_cmp, ov, 0.0).astype(BF16)
    return jnp.pad(ov, ((0, SEL_SLOTS - s // SEL_LEN), (0, 0)))


def kernel(x, p, rel_bias, norm_g, ffn_w_gate, ffn_w_up, ffn_w_down, w_in, w_out, sgu_norm_g, sgu_w, sgu_b,
           conv_w, conv_b, lru_wa, lru_ba, lru_wx, lru_bx, lru_lambda, cmp_pos, cmp_w1, cmp_b1, cmp_w2,
           cmp_b2, ple_w_gate, ple_w_proj):
    b, s, d = x.shape
    depth = norm_g.shape[0]
    assert s % TOKEN_TILE == 0 and s % SCAN_TILE == 0 and s % SEL_KEYS == 0
    assert s % (TILES_PER_STEP * Q_TILE) == 0
    assert s >= WINDOW + Q_TILE and SEL_TOP <= s // SEL_LEN <= SEL_SLOTS
    n_tiles = s // Q_TILE

    rbx = jnp.repeat(rel_bias.reshape(N_BUCKETS, C_KV_GROUPS, C_HPG).transpose(1, 0, 2), Q_TILE, axis=2)
    no_limit = 1 << 30
    gsel = _bias_table(rbx, s + SEL_KEYS, 1, s - Q_TILE, no_limit)
    gwin = _bias_table(rbx, 2 * WINDOW + Q_TILE, 1, WINDOW, WINDOW)
    per_tile = Q_TILE // CMP_STRIDE
    gcmp = _bias_table(rbx, per_tile * (n_tiles - 1) + s // CMP_STRIDE, CMP_STRIDE,
                       CMP_STRIDE * per_tile * (n_tiles - 1) - (CMP_LEN - 1), no_limit)
    ovt = _overlap_t(s)

    ffn_w = (ffn_w_gate.astype(BF16), ffn_w_up.astype(BF16), ffn_w_down.astype(BF16))
    h = x
    flat = lambda a: a.reshape(b * s, -1)
    for i in range(depth):
        h, ya, bx, qt, raw, ks, kw, vst, vwt, gates = _head(
            h, norm_g[i], ffn_w, i, w_in[i], sgu_norm_g[i], sgu_w[i], sgu_b[i])
        yb = _rglru(bx, conv_w[i], conv_b[i], lru_wa[i], lru_ba[i], lru_wx[i], lru_bx[i], lru_lambda[i])
        kc, kct = _compress(raw, cmp_pos[i], cmp_w1[i], cmp_b1[i], cmp_w2[i], cmp_b2[i])
        yc = _nsa(qt, kc[0], kct[1], ks, vst, kw, vwt, gates, gsel, gwin, gcmp, ovt)
        h = _tail(flat(h), flat(ya), flat(yb), flat(yc), p.reshape(depth, b * s, -1), norm_g[i], ffn_w, i,
                  w_out[i], ple_w_gate[i], ple_w_proj[i]).reshape(b, s, d)
    return h
```

```python
import functools
import math

import jax
import jax.numpy as jnp
from jax import lax
from jax.experimental import pallas as pl
from jax.experimental.pallas import tpu as pltpu

F32 = jnp.float32
BF16 = jnp.bfloat16

RMS_EPS = 1e-6
A_GROUPS = 4
A_WIDTH = 256
A_CHUNK = 128
B_GROUPS = 4
B_WIDTH = 256
CONV_W = 4
LRU_C = 8.0
C_HEADS = 8
C_KV_GROUPS = 2
C_HPG = C_HEADS // C_KV_GROUPS
HEAD_DIM = 64
C_WIDTH = C_HEADS * HEAD_DIM
KV_W = C_KV_GROUPS * HEAD_DIM
CMP_LEN = 32
CMP_STRIDE = 16
SEL_LEN = 64
SEL_SHIFT = 6
SEL_SLOTS = 64
SEL_TOP = 16
WINDOW = 512
FORCE_SCORE = 1e4
NEG = -1e30
N_BUCKETS = 32
MAX_DISTANCE = 1024

LANES = 128
TOKEN_TILE = 512
FFN_CHUNK = 256
SUBLANES = 8
Q_TILE = 128
TILES_PER_STEP = 2
SEL_KEYS = 512
SEL_HALF = 256
WIN_KEYS = 128
QL = C_HPG * Q_TILE
V_PAD_ROWS = 16
V_ROWS = HEAD_DIM + V_PAD_ROWS
LOG2E = math.log2(math.e)
TABLE_ROWS = 128
VMEM_LIMIT = 56 * 1024 * 1024


def _cparams(n_axes):
    return pltpu.CompilerParams(dimension_semantics=("arbitrary",) * n_axes,
                                vmem_limit_bytes=VMEM_LIMIT)


def _resident(shape):
    nd = len(shape)
    return pl.BlockSpec(shape, lambda *_: (0,) * nd, pipeline_mode=pl.Buffered(1))


def _rms(x, g):
    return x * lax.rsqrt(jnp.mean(x * x, axis=-1, keepdims=True) + RMS_EPS) * g


def _sigmoid(x):
    return 1.0 / (1.0 + jnp.exp(-x))


def _dot(a, b):
    return jnp.dot(a, b, preferred_element_type=F32)


def _bias_table_kernel(rbx_ref, o_ref, *, stride, offset, dmax):
    i = pl.program_id(1)
    shape = (TABLE_ROWS, QL)
    d_hi = (Q_TILE - 1) - stride * (i * TABLE_ROWS) + offset
    d_lo = -stride * (i * TABLE_ROWS + TABLE_ROWS - 1) + offset
    masked = (d_hi < 0) | (d_lo >= dmax)
    far = (d_lo >= MAX_DISTANCE) & (d_hi < dmax)

    @pl.when(masked)
    def _():
        o_ref[0] = jnp.full(shape, NEG, F32)

    @pl.when(far)
    def _():
        o_ref[0] = jnp.broadcast_to(rbx_ref[0, N_BUCKETS - 1:N_BUCKETS, :] * LOG2E, shape)

    @pl.when(jnp.logical_not(masked | far))
    def _():
        x = lax.broadcasted_iota(jnp.int32, shape, 0) + i * TABLE_ROWS
        t = lax.broadcasted_iota(jnp.int32, shape, 1) & (Q_TILE - 1)
        d = t - stride * x + offset
        n = jnp.maximum(d, 0)
        max_exact = N_BUCKETS // 2
        nf = jnp.maximum(n, max_exact).astype(F32)
        large = max_exact + (jnp.log(nf / max_exact) / math.log(MAX_DISTANCE / max_exact)
                             * (N_BUCKETS - max_exact)).astype(jnp.int32)
        large = jnp.minimum(large, N_BUCKETS - 1)
        bucket = jnp.where(n < max_exact, n, large)
        acc = jnp.zeros(shape, F32)
        for k in range(N_BUCKETS):
            acc = jnp.where(bucket == k, rbx_ref[0, k:k + 1, :], acc)
        o_ref[0] = jnp.where((d >= 0) & (d < dmax), acc * LOG2E, NEG)


def _bias_table(rbx, rows, stride, offset, dmax):
    rows_p = -(-rows // TABLE_ROWS) * TABLE_ROWS
    return pl.pallas_call(
        functools.partial(_bias_table_kernel, stride=stride, offset=offset, dmax=dmax),
        grid=(C_KV_GROUPS, rows_p // TABLE_ROWS),
        in_specs=[pl.BlockSpec((1, N_BUCKETS, QL), lambda g, i: (g, 0, 0))],
        out_specs=pl.BlockSpec((1, TABLE_ROWS, QL), lambda g, i: (g, i, 0)),
        out_shape=jax.ShapeDtypeStruct((C_KV_GROUPS, rows_p, QL), F32),
        compiler_params=_cparams(2),
        name="bias_table",
    )(rbx)


def _ffn_body(x, g_pre, g_post, wg_ref, wu_ref, wd_ref, acc_ref):
    xn = _rms(x, g_pre).astype(BF16)
    nch = wg_ref.shape[1] // FFN_CHUNK
    cols = lambda j: slice(j * FFN_CHUNK, (j + 1) * FFN_CHUNK)
    gate, up = _dot(xn, wg_ref[:, cols(0)]), _dot(xn, wu_ref[:, cols(0)])
    for j in range(nch):
        if j + 1 < nch:
            gate_next, up_next = _dot(xn, wg_ref[:, cols(j + 1)]), _dot(xn, wu_ref[:, cols(j + 1)])
        hid = (gate * _sigmoid(gate) * up).astype(BF16)
        down = _dot(hid, wd_ref[cols(j), :])
        if j == 0:
            acc_ref[...] = down
        else:
            acc_ref[...] += down
        gate, up = gate_next, up_next
    return x + 0.5 * _rms(acc_ref[...], g_post)


def _ffn_weight_spec(w, layer, which):
    return pl.BlockSpec((None, None) + w.shape[2:], lambda *_: (layer, which, 0, 0), pipeline_mode=pl.Buffered(1))


def _head_kernel(h_ref, norm_ref, wg_ref, wu_ref, wd_ref, wa_ref, wb_ref, wq_ref, wkv_ref, wgt_ref,
                 sgn_ref, sgw_ref, sgb_ref, cw_ref, cb_ref, lwa_ref, lba_ref, lwx_ref, lbx_ref, lam_ref,
                 h_out_ref, ya_ref, yb_ref, qt_ref, raw_ref, ks_ref, kw_ref, vst_ref, vwt_ref, gt_ref,
                 acc_ref, tail_ref, state_ref, a_ref, b_ref):
    @pl.when(pl.program_id(1) == 0)
    def _():
        tail_ref[...] = jnp.zeros_like(tail_ref)
        state_ref[...] = jnp.zeros_like(state_ref)

    h = _ffn_body(h_ref[0], norm_ref[0:1, :], norm_ref[1:2, :], wg_ref, wu_ref, wd_ref, acc_ref)
    h_out_ref[0] = h
    xn = _rms(h, norm_ref[2:3, :]).astype(BF16)

    zb = _dot(xn, wb_ref[...])
    za = _dot(xn, wa_ref[...])
    zq = _dot(xn, wq_ref[...])
    zkv = _dot(xn, wkv_ref[...])
    zg = _dot(xn, wgt_ref[...])

    gate_b = _rglru_coefficients(zb, cw_ref, cb_ref, lwa_ref, lba_ref, lwx_ref, lbx_ref, lam_ref,
                                 tail_ref, state_ref, a_ref, b_ref)

    u = jax.nn.gelu(za[:, :A_WIDTH])
    v = _rms(jax.nn.gelu(za[:, A_WIDTH:]), sgn_ref[...]).astype(BF16)
    row = lax.broadcasted_iota(jnp.int32, (A_CHUNK, A_CHUNK), 0)
    col = lax.broadcasted_iota(jnp.int32, (A_CHUNK, A_CHUNK), 1)
    lane_group = lax.shift_right_logical(lax.broadcasted_iota(jnp.int32, (A_CHUNK, A_WIDTH), 1),
                                          (A_WIDTH // A_GROUPS).bit_length() - 1)
    w_tril = [jnp.where(row >= col, sgw_ref[g], 0.0).astype(BF16) for g in range(A_GROUPS)]
    for c in range(TOKEN_TILE // A_CHUNK):
        rows = slice(c * A_CHUNK, (c + 1) * A_CHUNK)
        mixed = jnp.zeros((A_CHUNK, A_WIDTH), F32)
        for g in range(A_GROUPS):
            mixed = jnp.where(lane_group == g, _dot(w_tril[g], v[rows]), mixed)
        ya_ref[0, rows, :] = (u[rows] * (mixed + sgb_ref[...])).astype(ya_ref.dtype)

    zq_t = (zq * (HEAD_DIM ** -0.5 * LOG2E)).T
    for g in range(C_KV_GROUPS):
        for c in range(TOKEN_TILE // Q_TILE):
            parts = []
            for r in range(C_HPG):
                base = (g * C_HPG + r) * HEAD_DIM
                parts.append(zq_t[base:base + HEAD_DIM, c * Q_TILE:(c + 1) * Q_TILE])
            qt_ref[0, g, c] = jnp.concatenate(parts, axis=1).astype(qt_ref.dtype)

    vs_t = zkv[:, 3 * KV_W:4 * KV_W].T
    vw_t = zkv[:, 5 * KV_W:6 * KV_W].T
    key_blk = lax.shift_right_logical(
        lax.broadcasted_iota(jnp.int32, (TOKEN_TILE, SEL_SLOTS), 0) + pl.program_id(1) * TOKEN_TILE, SEL_SHIFT)
    blk_onehot = jnp.where(key_blk == lax.broadcasted_iota(jnp.int32, (TOKEN_TILE, SEL_SLOTS), 1), 1.0, 0.0)
    ones_rows = jnp.where(lax.broadcasted_iota(jnp.int32, (V_PAD_ROWS, SEL_KEYS), 0) == 0, 1.0, 0.0)
    raw_ref[0, 0] = zkv[:, :KV_W]
    raw_ref[1, 0] = zkv[:, KV_W:2 * KV_W]
    for g in range(C_KV_GROUPS):
        lo, hi = g * HEAD_DIM, (g + 1) * HEAD_DIM
        ks_ref[0, g] = jnp.concatenate([zkv[:, 2 * KV_W + lo:2 * KV_W + hi], blk_onehot],
                                       axis=1).astype(ks_ref.dtype)
        kw_ref[0, g] = zkv[:, 4 * KV_W + lo:4 * KV_W + hi].astype(kw_ref.dtype)
        for c in range(TOKEN_TILE // SEL_KEYS):
            vst_ref[0, g, c] = jnp.concatenate(
                [vs_t[lo:hi, c * SEL_KEYS:(c + 1) * SEL_KEYS], ones_rows[:, :SEL_KEYS]], axis=0).astype(vst_ref.dtype)
        for c in range(TOKEN_TILE // WIN_KEYS):
            vwt_ref[0, g, c] = jnp.concatenate(
                [vw_t[lo:hi, c * WIN_KEYS:(c + 1) * WIN_KEYS], ones_rows[:, :WIN_KEYS]], axis=0).astype(vwt_ref.dtype)

    sg_t = _sigmoid(zg).T
    for br in range(3):
        for g in range(C_KV_GROUPS):
            base = br * C_HEADS + g * C_HPG
            gt_ref[0, br, g] = sg_t[base:base + C_HPG, :]

    yb_ref[0] = (_rglru_scan(state_ref, a_ref, b_ref) * jax.nn.gelu(gate_b)).astype(yb_ref.dtype)


def _head(h, norms, ffn_w, layer, w_in, sgu_norm_g, sgu_w, sgu_b, lru):
    b, s, d = h.shape
    assert ffn_w[0].shape[-1] % FFN_CHUNK == 0
    wb16 = w_in.astype(BF16)
    o = 0
    wa = wb16[:, o:o + 2 * A_WIDTH]; o += 2 * A_WIDTH
    wb = wb16[:, o:o + 2 * B_WIDTH]; o += 2 * B_WIDTH
    wq = wb16[:, o:o + C_WIDTH]; o += C_WIDTH
    wkv = wb16[:, o:o + 6 * KV_W]; o += 6 * KV_W
    wgt = jnp.pad(wb16[:, o:o + 3 * C_HEADS], ((0, 0), (0, LANES - 3 * C_HEADS)))
    sgb = jnp.repeat(sgu_b.T, A_WIDTH // A_GROUPS, axis=1)
    nt = s // TOKEN_TILE
    grid = (b, nt)
    G = C_KV_GROUPS
    out_shape = [
        jax.ShapeDtypeStruct((b, s, d), F32),
        jax.ShapeDtypeStruct((b, s, A_WIDTH), BF16),
        jax.ShapeDtypeStruct((b, s, B_WIDTH), BF16),
        jax.ShapeDtypeStruct((b, G, s // Q_TILE, HEAD_DIM, QL), BF16),
        jax.ShapeDtypeStruct((2, b, s, KV_W), F32),
        jax.ShapeDtypeStruct((b, G, s, HEAD_DIM + SEL_SLOTS), BF16),
        jax.ShapeDtypeStruct((b, G, s, HEAD_DIM), BF16),
        jax.ShapeDtypeStruct((b, G, s // SEL_KEYS, V_ROWS, SEL_KEYS), BF16),
        jax.ShapeDtypeStruct((b, G, s // WIN_KEYS, V_ROWS, WIN_KEYS), BF16),
        jax.ShapeDtypeStruct((b, 3, G, C_HPG, s), F32),
    ]
    out_specs = [
        pl.BlockSpec((1, TOKEN_TILE, d), lambda bi, i: (bi, i, 0)),
        pl.BlockSpec((1, TOKEN_TILE, A_WIDTH), lambda bi, i: (bi, i, 0)),
        pl.BlockSpec((1, TOKEN_TILE, B_WIDTH), lambda bi, i: (bi, i, 0)),
        pl.BlockSpec((1, G, TOKEN_TILE // Q_TILE, HEAD_DIM, QL), lambda bi, i: (bi, 0, i, 0, 0)),
        pl.BlockSpec((2, 1, TOKEN_TILE, KV_W), lambda bi, i: (0, bi, i, 0)),
        pl.BlockSpec((1, G, TOKEN_TILE, HEAD_DIM + SEL_SLOTS), lambda bi, i: (bi, 0, i, 0)),
        pl.BlockSpec((1, G, TOKEN_TILE, HEAD_DIM), lambda bi, i: (bi, 0, i, 0)),
        pl.BlockSpec((1, G, TOKEN_TILE // SEL_KEYS, V_ROWS, SEL_KEYS), lambda bi, i: (bi, 0, i, 0, 0)),
        pl.BlockSpec((1, G, TOKEN_TILE // WIN_KEYS, V_ROWS, WIN_KEYS), lambda bi, i: (bi, 0, i, 0, 0)),
        pl.BlockSpec((1, 3, G, C_HPG, TOKEN_TILE), lambda bi, i: (bi, 0, 0, 0, i)),
    ]
    in_specs = [
        pl.BlockSpec((1, TOKEN_TILE, d), lambda bi, i: (bi, i, 0)),
        _resident(norms.shape), *[_ffn_weight_spec(w, layer, 0) for w in ffn_w],
        _resident(wa.shape), _resident(wb.shape), _resident(wq.shape),
        _resident(wkv.shape), _resident(wgt.shape), _resident((1, A_WIDTH)),
        _resident(sgu_w.shape), _resident(sgb.shape),
        _resident((CONV_W, B_WIDTH)), _resident((1, B_WIDTH)), _resident((B_WIDTH, B_WIDTH)),
        _resident((1, B_WIDTH)), _resident((B_WIDTH, B_WIDTH)), _resident((1, B_WIDTH)), _resident((1, B_WIDTH)),
    ]
    conv_w, conv_b, lru_wa, lru_ba, lru_wx, lru_bx, lru_lambda = lru
    row = lambda v: v.reshape(1, B_WIDTH)
    return pl.pallas_call(
        _head_kernel, grid=grid, in_specs=in_specs, out_specs=out_specs, out_shape=out_shape,
        scratch_shapes=[pltpu.VMEM((TOKEN_TILE, d), F32),
                        pltpu.VMEM((SUBLANES, B_WIDTH), F32), pltpu.VMEM((SUBLANES, B_WIDTH), F32),
                        pltpu.VMEM((TOKEN_TILE, B_WIDTH), F32), pltpu.VMEM((TOKEN_TILE, B_WIDTH), F32)],
        compiler_params=_cparams(2), name="head",
    )(h, norms, *ffn_w, wa, wb, wq, wkv, wgt, sgu_norm_g.reshape(1, A_WIDTH), sgu_w, sgb,
      conv_w, row(conv_b), _block_diag(lru_wa).astype(BF16), row(lru_ba),
      _block_diag(lru_wx).astype(BF16), row(lru_bx), row(lru_lambda))


def _rglru_coefficients(zb, cw_ref, cb_ref, wa_ref, ba_ref, wx_ref, bxb_ref, lam_ref,
                        tail_ref, state_ref, a_ref, b_ref):
    xb = zb[:, :B_WIDTH]
    ext = jnp.concatenate([tail_ref[...], xb], axis=0)
    xc = cb_ref[...] + xb * cw_ref[CONV_W - 1:CONV_W, :]
    for k in range(CONV_W - 1):
        shift = CONV_W - 1 - k
        xc = xc + ext[SUBLANES - shift:SUBLANES - shift + TOKEN_TILE] * cw_ref[k:k + 1, :]
    tail_ref[...] = xb[TOKEN_TILE - SUBLANES:]

    xcb = xc.astype(BF16)
    r = _sigmoid(_dot(xcb, wa_ref[...]) + ba_ref[...])
    i = _sigmoid(_dot(xcb, wx_ref[...]) + bxb_ref[...])
    z = -lam_ref[...]
    e = jnp.exp(-jnp.abs(z))
    softplus = jnp.maximum(z, 0.0) + jnp.log1p(e)
    log_a = -LRU_C * r * softplus
    a = jnp.exp(log_a)
    b = jnp.sqrt(jnp.tanh(-log_a) * (a * a + 1.0)) * (i * xc)

    row = lax.broadcasted_iota(jnp.int32, a.shape, 0) & (SUBLANES - 1)
    for dist in (1, 2, 4):
        a_prev = jnp.where(row >= dist, pltpu.roll(a, dist, 0), 1.0)
        b_prev = jnp.where(row >= dist, pltpu.roll(b, dist, 0), 0.0)
        b = a * b_prev + b
        a = a * a_prev
    a_ref[...] = a
    b_ref[...] = b
    return zb[:, B_WIDTH:]


def _rglru_scan(state_ref, a_ref, b_ref):
    def body(k, h):
        off = pl.multiple_of(k * SUBLANES, SUBLANES)
        rows = pl.ds(off, SUBLANES)
        hs = b_ref[rows, :] + a_ref[rows, :] * h
        b_ref[rows, :] = hs
        return jnp.broadcast_to(hs[SUBLANES - 1:SUBLANES, :], hs.shape)

    state_ref[...] = lax.fori_loop(0, TOKEN_TILE // SUBLANES, body, state_ref[...])
    return b_ref[...]


def _block_diag(w):
    g, n, _ = w.shape
    out = jnp.zeros((g * n, g * n), w.dtype)
    for k in range(g):
        out = out.at[k * n:(k + 1) * n, k * n:(k + 1) * n].set(w[k])
    return out


def _compress_kernel(raw_ref, pos_ref, w1_ref, b1_ref, w2_ref, b2_ref, w2t_ref, b2c_ref, kc_ref, kct_ref):
    ncp = raw_ref.shape[2] // CMP_STRIDE
    groups = range(C_KV_GROUPS)
    hid_dim = w1_ref.shape[-1]
    u = [jnp.zeros((ncp, hid_dim), F32) for _ in groups]
    v = [jnp.zeros((ncp, hid_dim), F32) for _ in groups]
    for l in range(CMP_STRIDE):
        x = raw_ref[0, 0, pl.ds(l, ncp, stride=CMP_STRIDE), :]
        top = (x + pos_ref[0, l:l + 1, :]).astype(BF16)
        bot = (x + pos_ref[0, CMP_STRIDE + l:CMP_STRIDE + l + 1, :]).astype(BF16)
        for g in groups:
            u[g] = u[g] + _dot(top, w1_ref[0, g, l])
            v[g] = v[g] + _dot(bot, w1_ref[0, g, CMP_STRIDE + l])
    for g in groups:
        hid = jax.nn.gelu(u[g] + pltpu.roll(v[g], ncp - 1, 0) + b1_ref[0]).astype(BF16)
        kc_ref[0, 0, g] = (_dot(hid, w2_ref[0]) + b2_ref[0]).astype(kc_ref.dtype)
        t = lax.dot_general(w2t_ref[0], hid, (((1,), (1,)), ((), ())), preferred_element_type=F32)
        kct_ref[0, 0, g] = (t + b2c_ref[0]).astype(kct_ref.dtype)


def _compress(raw, cmp_pos, cmp_w1, cmp_b1, cmp_w2, cmp_b2):
    _, b, s, _ = raw.shape
    g, hd = C_KV_GROUPS, HEAD_DIM
    ncp = s // CMP_STRIDE
    hid = cmp_w1.shape[-1]
    w1 = cmp_w1.astype(BF16).reshape(2, CMP_LEN, hd, hid)
    zero = jnp.zeros_like(w1)
    w1g = jnp.stack([jnp.concatenate([w1 if k == gi else zero for k in range(g)], axis=2)
                     for gi in range(g)], axis=1)
    pos = jnp.tile(cmp_pos, (1, 1, g))
    sel = lambda *shape: pl.BlockSpec((1,) + shape, lambda kv, bi: (kv,) + (0,) * len(shape))
    return pl.pallas_call(
        _compress_kernel,
        grid=(2, b),
        in_specs=[pl.BlockSpec((1, 1, s, g * hd), lambda kv, bi: (kv, bi, 0, 0)),
                  sel(CMP_LEN, g * hd), sel(g, CMP_LEN, g * hd, hid), sel(1, hid), sel(hid, hd), sel(1, hd),
                  sel(hd, hid), sel(hd, 1)],
        out_specs=[pl.BlockSpec((1, 1, g, ncp, hd), lambda kv, bi: (kv, bi, 0, 0, 0)),
                   pl.BlockSpec((1, 1, g, hd, ncp), lambda kv, bi: (kv, bi, 0, 0, 0))],
        out_shape=[jax.ShapeDtypeStruct((2, b, g, ncp, hd), BF16),
                   jax.ShapeDtypeStruct((2, b, g, hd, ncp), BF16)],
        compiler_params=_cparams(2),
        name="compress",
    )(raw, pos, w1g, cmp_b1.reshape(2, 1, hid), cmp_w2.astype(BF16), cmp_b2.reshape(2, 1, hd),
      cmp_w2.astype(BF16).transpose(0, 2, 1), cmp_b2.reshape(2, hd, 1))


def _softmax_step(carry, s, v_t):
    m, acc = carry
    m_new = jnp.maximum(m, jnp.max(s, axis=0, keepdims=True))
    p = jnp.exp2(s - m_new).astype(BF16)
    acc = acc * jnp.exp2(m - m_new) + _dot(v_t, p)
    return m_new, acc


def _normalized(acc):
    return acc[:HEAD_DIM] / acc[HEAD_DIM:HEAD_DIM + 1]


def _nsa_kernel(qt_ref, kc_ref, vct_ref, ks_ref, vst_ref, kw_ref, vwt_ref, gate_ref,
                gsel_ref, gwin_ref, gcmp_ref, ovt_ref, o_ref, *, n_tiles):
    step = pl.program_id(1)
    ncp = kc_ref.shape[2]
    chains = [(t, g) for t in range(TILES_PER_STEP) for g in range(C_KV_GROUPS)]
    tile = [step * TILES_PER_STEP + t for t in range(TILES_PER_STEP)]
    qts = [qt_ref[0, g, t] for t, g in chains]

    back = WINDOW // WIN_KEYS
    n_win = WINDOW + Q_TILE
    first = [jnp.maximum(c - back, 0) for c in tile]
    s_cmp, s_win = [], []
    for k, (t, g) in enumerate(chains):
        y0 = pl.multiple_of((n_tiles - 1 - tile[t]) * (Q_TILE // CMP_STRIDE), SUBLANES)
        s_cmp.append(_dot(kc_ref[0, g], qts[k]) + gcmp_ref[g, pl.ds(y0, ncp), :])
    for k, (t, g) in enumerate(chains):
        start = pl.multiple_of(first[t] * WIN_KEYS, WIN_KEYS)
        rel = pl.multiple_of(jnp.maximum(back - tile[t], 0) * WIN_KEYS, WIN_KEYS)
        s_win.append(_dot(kw_ref[0, g, pl.ds(start, n_win), :], qts[k]) + gwin_ref[g, pl.ds(rel, n_win), :])

    tq = lax.broadcasted_iota(jnp.int32, (1, QL), 1) & (Q_TILE - 1)
    o_cmp, imp = [], []
    for k, (t, g) in enumerate(chains):
        e = jnp.exp2(s_cmp[k] - jnp.max(s_cmp[k], axis=0, keepdims=True))
        has_cmp = (tile[t] * Q_TILE + tq >= CMP_LEN - 1).astype(F32)
        p = e / jnp.sum(e, axis=0, keepdims=True) * has_cmp
        o_cmp.append(_dot(vct_ref[0, g], p.astype(BF16)))
        p_heads = p[:, 0:Q_TILE]
        for r in range(1, C_HPG):
            p_heads = p_heads + p[:, r * Q_TILE:(r + 1) * Q_TILE]
        p_hi = p_heads.astype(BF16)
        p_lo = (p_heads - p_hi.astype(F32)).astype(BF16)
        imp.append(_dot(ovt_ref[...], p_hi) + _dot(ovt_ref[...], p_lo))

    o_win = []
    for k, (t, g) in enumerate(chains):
        e = jnp.exp2(s_win[k] - jnp.max(s_win[k], axis=0, keepdims=True)).astype(BF16)
        acc = jnp.zeros((V_ROWS, QL), F32)
        for i in range(n_win // WIN_KEYS):
            acc = acc + _dot(vwt_ref[0, g, first[t] + i], e[i * WIN_KEYS:(i + 1) * WIN_KEYS])
        o_win.append(_normalized(acc))

    q_aug = _with_mask_rows(qts, [tile[t] for t, g in chains], imp)
    halves = range(SEL_KEYS // SEL_HALF)
    init = (jnp.full((1, QL), NEG, F32), jnp.zeros((V_ROWS, QL), F32))

    def sel_body(i, carry, far):
        off = pl.multiple_of(i * SEL_KEYS, SEL_KEYS)
        s = []
        for h in halves:
            for k, (t, g) in enumerate(chains):
                logits = _dot(ks_ref[0, g, pl.ds(off + h * SEL_HALF, SEL_HALF), :], q_aug[k])
                if not far:
                    x0 = (n_tiles - 1 - tile[t]) * Q_TILE
                    logits = logits + gsel_ref[
                        g, pl.ds(pl.multiple_of(x0 + off + h * SEL_HALF, Q_TILE), SEL_HALF), :]
                s.append(logits)
        carry = list(carry)
        for h in halves:
            for k, (t, g) in enumerate(chains):
                v_t = vst_ref[0, g, i][:, h * SEL_HALF:(h + 1) * SEL_HALF]
                carry[k] = _softmax_step(carry[k], s[h * len(chains) + k], v_t)
        return tuple(carry)

    n_far = jnp.maximum(tile[0] * Q_TILE - MAX_DISTANCE + 1, 0) // SEL_KEYS
    n_sel_steps = (tile[-1] * Q_TILE + Q_TILE + SEL_KEYS - 1) // SEL_KEYS
    sel = lax.fori_loop(0, n_far, functools.partial(sel_body, far=True), (init,) * len(chains))
    sel = tuple((m + gsel_ref[g, 0:1, :], acc) for (m, acc), (t, g) in zip(sel, chains))
    sel = lax.fori_loop(n_far, n_sel_steps, functools.partial(sel_body, far=False), sel)

    for t in range(TILES_PER_STEP):
        outs = []
        for k, (tk, g) in enumerate(chains):
            if tk != t:
                continue
            o_sel = _normalized(sel[k][1])
            tok = slice(t * Q_TILE, (t + 1) * Q_TILE)
            cols = []
            for r in range(C_HPG):
                ln = slice(r * Q_TILE, (r + 1) * Q_TILE)
                cols.append(gate_ref[0, 0, g, r:r + 1, tok] * o_cmp[k][:, ln]
                            + gate_ref[0, 1, g, r:r + 1, tok] * o_sel[:, ln]
                            + gate_ref[0, 2, g, r:r + 1, tok] * o_win[k][:, ln])
            outs += [jnp.concatenate(cols[2 * j:2 * j + 2], axis=0).T for j in range(C_HPG // 2)]
        o_ref[0, t * Q_TILE:(t + 1) * Q_TILE, :] = jnp.concatenate(outs, axis=1).astype(o_ref.dtype)


def _with_mask_rows(qts, tiles, imps):
    n_groups = SEL_SLOTS // SUBLANES
    j = lax.broadcasted_iota(jnp.int32, (SEL_SLOTS, Q_TILE), 0)
    half = lax.shift_right_logical(lax.broadcasted_iota(jnp.int32, (SEL_SLOTS, Q_TILE), 1), SEL_SHIFT)
    j_rows = j[:SUBLANES]
    scores = []
    for c, imp in zip(tiles, imps):
        blk = c * (Q_TILE // SEL_LEN) + half
        forced = (j == 0) | (j == blk) | (j == blk - 1)
        scores.append(jnp.where(j <= blk, jnp.where(forced, FORCE_SCORE, imp), -1.0))
    rows = [[sc[k * SUBLANES:(k + 1) * SUBLANES] for k in range(n_groups)] for sc in scores]

    def add_pair(ranks, jg, k):
        for ci, sc in enumerate(scores):
            acc = ranks[ci][k]
            for jp in range(jg * SUBLANES, (jg + 1) * SUBLANES):
                other = sc[jp:jp + 1, :]
                if k > jg:
                    beats = jnp.where(other >= rows[ci][k], 1, 0)
                elif k < jg:
                    beats = jnp.where(other > rows[ci][k], 1, 0)
                else:
                    beats = jnp.where(j_rows > jp - k * SUBLANES, jnp.where(other >= rows[ci][k], 1, 0),
                                      jnp.where(other > rows[ci][k], 1, 0))
                acc = acc + beats
            ranks[ci][k] = acc

    last_blk = tiles[-1] * (Q_TILE // SEL_LEN) + Q_TILE // SEL_LEN - 1
    ranks = [[jnp.zeros((SUBLANES, Q_TILE), jnp.int32) for _ in range(n_groups)] for _ in scores]
    for m in range(n_groups):
        def shell(ranks, m=m):
            ranks = [list(r) for r in ranks]
            for k in range(m + 1):
                add_pair(ranks, m, k)
            for jg in range(m):
                add_pair(ranks, jg, m)
            return ranks
        ranks = shell(ranks) if m == 0 else lax.cond(m * SUBLANES <= last_blk, shell, lambda r: r, ranks)

    out = []
    for qt, sc, rk in zip(qts, scores, ranks):
        rank = jnp.concatenate(rk, axis=0)
        mask_rows = jnp.where((rank < SEL_TOP) & (sc >= 0.0), 0.0, NEG).astype(BF16)
        out.append(jnp.concatenate([qt, jnp.concatenate([mask_rows] * C_HPG, axis=1)], axis=0))
    return out


def _nsa(qt, kc, vct, ks, vst, kw, vwt, gates, gsel, gwin, gcmp, ovt):
    b, G, n_tiles = qt.shape[:3]
    s = n_tiles * Q_TILE
    per_b = lambda a: pl.BlockSpec((1,) + a.shape[1:], lambda bi, c: (bi,) + (0,) * (a.ndim - 1))
    return pl.pallas_call(
        functools.partial(_nsa_kernel, n_tiles=n_tiles),
        grid=(b, n_tiles // TILES_PER_STEP),
        in_specs=[pl.BlockSpec((1, G, TILES_PER_STEP, HEAD_DIM, QL), lambda bi, c: (bi, 0, c, 0, 0)),
                  per_b(kc), per_b(vct), per_b(ks), per_b(vst), per_b(kw), per_b(vwt),
                  pl.BlockSpec((1, 3, G, C_HPG, TILES_PER_STEP * Q_TILE), lambda bi, c: (bi, 0, 0, 0, c)),
                  _resident(gsel.shape), _resident(gwin.shape), _resident(gcmp.shape),
                  _resident(ovt.shape)],
        out_specs=pl.BlockSpec((1, TILES_PER_STEP * Q_TILE, C_WIDTH), lambda bi, c: (bi, c, 0)),
        out_shape=jax.ShapeDtypeStruct((b, s, C_WIDTH), BF16),
        compiler_params=_cparams(2),
        name="nsa",
    )(qt, kc, vct, ks, vst, kw, vwt, gates, gsel, gwin, gcmp, ovt)


def _tail_kernel(h_ref, ya_ref, yb_ref, yc_ref, p_ref, norm_ref, woa_ref, wob_ref, woc_ref,
                 wg_ref, wu_ref, wd_ref, wpg_ref, wpp_ref, o_ref, acc_ref):
    mix = _dot(ya_ref[...], woa_ref[...]) + _dot(yb_ref[...], wob_ref[...]) + _dot(yc_ref[...], woc_ref[...])
    emb = _dot(p_ref[...].astype(BF16), wpp_ref[...])
    h = h_ref[...] + _rms(mix, norm_ref[3:4, :])
    h = _ffn_body(h, norm_ref[4:5, :], norm_ref[5:6, :], wg_ref, wu_ref, wd_ref, acc_ref)
    gate = _sigmoid(_dot(_rms(h, norm_ref[6:7, :]).astype(BF16), wpg_ref[...]))
    o_ref[...] = h + _rms(gate * emb, norm_ref[7:8, :])


def _tail(h, ya, yb, yc, p, norms, ffn_w, layer, w_out, w_gate, w_proj):
    n, d = h.shape
    dp = p.shape[-1]
    w = w_out.astype(BF16)
    woa, wob, woc = w[:A_WIDTH], w[A_WIDTH:A_WIDTH + B_WIDTH], w[A_WIDTH + B_WIDTH:]
    tile = lambda width: pl.BlockSpec((TOKEN_TILE, width), lambda i: (i, 0))
    return pl.pallas_call(
        _tail_kernel,
        grid=(n // TOKEN_TILE,),
        in_specs=[tile(d), tile(A_WIDTH), tile(B_WIDTH), tile(C_WIDTH),
                  pl.BlockSpec((None, TOKEN_TILE, dp), lambda i: (layer, i, 0)),
                  _resident(norms.shape), _resident(woa.shape), _resident(wob.shape), _resident(woc.shape),
                  *[_ffn_weight_spec(wt, layer, 1) for wt in ffn_w],
                  _resident((d, d)), _resident((dp, d))],
        out_specs=tile(d),
        out_shape=jax.ShapeDtypeStruct((n, d), F32),
        scratch_shapes=[pltpu.VMEM((TOKEN_TILE, d), F32)],
        compiler_params=_cparams(1),
        name="tail",
    )(h, ya, yb, yc, p, norms, woa, wob, woc, *ffn_w, w_gate.astype(BF16), w_proj.astype(BF16))


def _overlap_t(s):
    ncp = s // CMP_STRIDE
    n_cmp = (s - CMP_LEN) // CMP_STRIDE + 1
    cs = jnp.arange(ncp) * CMP_STRIDE
    ss = jnp.arange(s // SEL_LEN) * SEL_LEN
    ov = jnp.clip(jnp.minimum(cs[None] + CMP_LEN, ss[:, None] + SEL_LEN)
                  - jnp.maximum(cs[None], ss[:, None]), 0, None).astype(F32) / CMP_LEN
    ov = jnp.where(jnp.arange(ncp)[None] < n_cmp, ov, 0.0).astype(BF16)
    return jnp.pad(ov, ((0, SEL_SLOTS - s // SEL_LEN), (0, 0)))


def kernel(x, p, rel_bias, norm_g, ffn_w_gate, ffn_w_up, ffn_w_down, w_in, w_out, sgu_norm_g, sgu_w, sgu_b,
           conv_w, conv_b, lru_wa, lru_ba, lru_wx, lru_bx, lru_lambda, cmp_pos, cmp_w1, cmp_b1, cmp_w2,
           cmp_b2, ple_w_gate, ple_w_proj):
    b, s, d = x.shape
    depth = norm_g.shape[0]
    assert s % TOKEN_TILE == 0 and s % SEL_KEYS == 0
    assert s % (TILES_PER_STEP * Q_TILE) == 0
    assert s >= WINDOW + Q_TILE and SEL_TOP <= s // SEL_LEN <= SEL_SLOTS
    n_tiles = s // Q_TILE

    rbx = jnp.repeat(rel_bias.reshape(N_BUCKETS, C_KV_GROUPS, C_HPG).transpose(1, 0, 2), Q_TILE, axis=2)
    no_limit = 1 << 30
    gsel = _bias_table(rbx, s + SEL_KEYS, 1, s - Q_TILE, no_limit)
    gwin = _bias_table(rbx, 2 * WINDOW + Q_TILE, 1, WINDOW, WINDOW)
    per_tile = Q_TILE // CMP_STRIDE
    gcmp = _bias_table(rbx, per_tile * (n_tiles - 1) + s // CMP_STRIDE, CMP_STRIDE,
                       CMP_STRIDE * per_tile * (n_tiles - 1) - (CMP_LEN - 1), no_limit)
    ovt = _overlap_t(s)

    ffn_w = (ffn_w_gate.astype(BF16), ffn_w_up.astype(BF16), ffn_w_down.astype(BF16))
    h = x
    flat = lambda a: a.reshape(b * s, -1)
    for i in range(depth):
        h, ya, yb, qt, raw, ks, kw, vst, vwt, gates = _head(
            h, norm_g[i], ffn_w, i, w_in[i], sgu_norm_g[i], sgu_w[i], sgu_b[i],
            (conv_w[i], conv_b[i], lru_wa[i], lru_ba[i], lru_wx[i], lru_bx[i], lru_lambda[i]))
        kc, kct = _compress(raw, cmp_pos[i], cmp_w1[i], cmp_b1[i], cmp_w2[i], cmp_b2[i])
        yc = _nsa(qt, kc[0], kct[1], ks, vst, kw, vwt, gates, gsel, gwin, gcmp, ovt)
        h = _tail(flat(h), flat(ya), flat(yb), flat(yc), p.reshape(depth, b * s, -1), norm_g[i], ffn_w, i,
                  w_out[i], ple_w_gate[i], ple_w_proj[i]).reshape(b, s, d)
    return h
```

```python
import functools
import math

import jax
import jax.numpy as jnp
from jax import lax
from jax.experimental import pallas as pl
from jax.experimental.pallas import tpu as pltpu

F32 = jnp.float32
BF16 = jnp.bfloat16

RMS_EPS = 1e-6
A_GROUPS = 4
A_WIDTH = 256
A_CHUNK = 128
B_GROUPS = 4
B_WIDTH = 256
CONV_W = 4
LRU_C = 8.0
C_HEADS = 8
C_KV_GROUPS = 2
C_HPG = C_HEADS // C_KV_GROUPS
HEAD_DIM = 64
C_WIDTH = C_HEADS * HEAD_DIM
KV_W = C_KV_GROUPS * HEAD_DIM
CMP_LEN = 32
CMP_STRIDE = 16
SEL_LEN = 64
SEL_SHIFT = 6
SEL_SLOTS = 64
SEL_TOP = 16
WINDOW = 512
FORCE_SCORE = 1e4
NEG = -1e30
N_BUCKETS = 32
MAX_DISTANCE = 1024

LANES = 128
TOKEN_TILE = 512
SUB_TILES = 2
FFN_CHUNK = 256
SUBLANES = 8
Q_TILE = 128
TILES_PER_STEP = 2
SEL_KEYS = 512
SEL_HALF = 256
SEL_AHEAD = 2
WIN_KEYS = 128
QL = C_HPG * Q_TILE
V_PAD_ROWS = 16
V_ROWS = HEAD_DIM + V_PAD_ROWS
LOG2E = math.log2(math.e)
TABLE_ROWS = 128
VMEM_LIMIT = 56 * 1024 * 1024


def _cparams(n_axes):
    return pltpu.CompilerParams(dimension_semantics=("arbitrary",) * n_axes,
                                vmem_limit_bytes=VMEM_LIMIT)


def _resident(shape):
    nd = len(shape)
    return pl.BlockSpec(shape, lambda *_: (0,) * nd, pipeline_mode=pl.Buffered(1))


def _rms(x, g):
    return x * lax.rsqrt(jnp.mean(x * x, axis=-1, keepdims=True) + RMS_EPS) * g


def _sigmoid(x):
    return 1.0 / (1.0 + jnp.exp(-x))


def _dot(a, b):
    return jnp.dot(a, b, preferred_element_type=F32)


def _bias_table_kernel(rbx_ref, o_ref, *, stride, offset, dmax):
    i = pl.program_id(1)
    shape = (TABLE_ROWS, QL)
    d_hi = (Q_TILE - 1) - stride * (i * TABLE_ROWS) + offset
    d_lo = -stride * (i * TABLE_ROWS + TABLE_ROWS - 1) + offset
    masked = (d_hi < 0) | (d_lo >= dmax)
    far = (d_lo >= MAX_DISTANCE) & (d_hi < dmax)

    @pl.when(masked)
    def _():
        o_ref[0] = jnp.full(shape, NEG, F32)

    @pl.when(far)
    def _():
        o_ref[0] = jnp.broadcast_to(rbx_ref[0, N_BUCKETS - 1:N_BUCKETS, :] * LOG2E, shape)

    @pl.when(jnp.logical_not(masked | far))
    def _():
        x = lax.broadcasted_iota(jnp.int32, shape, 0) + i * TABLE_ROWS
        t = lax.broadcasted_iota(jnp.int32, shape, 1) & (Q_TILE - 1)
        d = t - stride * x + offset
        n = jnp.maximum(d, 0)
        max_exact = N_BUCKETS // 2
        nf = jnp.maximum(n, max_exact).astype(F32)
        large = max_exact + (jnp.log(nf / max_exact) / math.log(MAX_DISTANCE / max_exact)
                             * (N_BUCKETS - max_exact)).astype(jnp.int32)
        large = jnp.minimum(large, N_BUCKETS - 1)
        bucket = jnp.where(n < max_exact, n, large)
        acc = jnp.zeros(shape, F32)
        for k in range(N_BUCKETS):
            acc = jnp.where(bucket == k, rbx_ref[0, k:k + 1, :], acc)
        o_ref[0] = jnp.where((d >= 0) & (d < dmax), acc * LOG2E, NEG)


def _bias_table(rbx, rows, stride, offset, dmax):
    rows_p = -(-rows // TABLE_ROWS) * TABLE_ROWS
    return pl.pallas_call(
        functools.partial(_bias_table_kernel, stride=stride, offset=offset, dmax=dmax),
        grid=(C_KV_GROUPS, rows_p // TABLE_ROWS),
        in_specs=[pl.BlockSpec((1, N_BUCKETS, QL), lambda g, i: (g, 0, 0))],
        out_specs=pl.BlockSpec((1, TABLE_ROWS, QL), lambda g, i: (g, i, 0)),
        out_shape=jax.ShapeDtypeStruct((C_KV_GROUPS, rows_p, QL), F32),
        compiler_params=_cparams(2),
        name="bias_table",
    )(rbx)


def _sub_tiles(n_rows):
    rows = n_rows // SUB_TILES
    return [slice(k * rows, (k + 1) * rows) for k in range(SUB_TILES)]


def _ffn_body(xs, g_pre, g_post, wg_ref, wu_ref, wd_ref, acc_ref):
    subs = _sub_tiles(acc_ref.shape[0])
    xn = [_rms(x, g_pre).astype(BF16) for x in xs]
    nch = wg_ref.shape[1] // FFN_CHUNK
    cols = lambda j: slice(j * FFN_CHUNK, (j + 1) * FFN_CHUNK)
    gate_up = [(_dot(x, wg_ref[:, cols(0)]), _dot(x, wu_ref[:, cols(0)])) for x in xn]
    for j in range(nch):
        if j + 1 < nch:
            nxt = [(_dot(x, wg_ref[:, cols(j + 1)]), _dot(x, wu_ref[:, cols(j + 1)])) for x in xn]
        for rows, (gate, up) in zip(subs, gate_up):
            hid = (gate * _sigmoid(gate) * up).astype(BF16)
            down = _dot(hid, wd_ref[cols(j), :])
            if j == 0:
                acc_ref[rows, :] = down
            else:
                acc_ref[rows, :] += down
        gate_up = nxt
    return [x + 0.5 * _rms(acc_ref[rows, :], g_post) for x, rows in zip(xs, subs)]


def _ffn_weight_spec(w, layer, which):
    return pl.BlockSpec((None, None) + w.shape[2:], lambda *_: (layer, which, 0, 0), pipeline_mode=pl.Buffered(1))


def _head_kernel(h_ref, norm_ref, wg_ref, wu_ref, wd_ref, wa_ref, wb_ref, wq_ref, wkv_ref, wgt_ref,
                 sgn_ref, sgw_ref, sgb_ref, cw_ref, cb_ref, lwa_ref, lba_ref, lwx_ref, lbx_ref, lam_ref,
                 h_out_ref, ya_ref, yb_ref, qt_ref, raw_ref, ks_ref, kw_ref, vst_ref, vwt_ref, gt_ref,
                 acc_ref, tail_ref, state_ref, a_ref, b_ref):
    @pl.when(pl.program_id(1) == 0)
    def _():
        tail_ref[...] = jnp.zeros_like(tail_ref)
        state_ref[...] = jnp.zeros_like(state_ref)

    h = _ffn_body([h_ref[0, r, :] for r in _sub_tiles(TOKEN_TILE)], norm_ref[0:1, :], norm_ref[1:2, :],
                  wg_ref, wu_ref, wd_ref, acc_ref)
    h = jnp.concatenate(h, axis=0)
    h_out_ref[0] = h
    xn = _rms(h, norm_ref[2:3, :]).astype(BF16)

    zb = _dot(xn, wb_ref[...])
    za = _dot(xn, wa_ref[...])
    zq = _dot(xn, wq_ref[...])
    zkv = _dot(xn, wkv_ref[...])
    zg = _dot(xn, wgt_ref[...])

    gate_b = _rglru_coefficients(zb, cw_ref, cb_ref, lwa_ref, lba_ref, lwx_ref, lbx_ref, lam_ref,
                                 tail_ref, state_ref, a_ref, b_ref)

    u = jax.nn.gelu(za[:, :A_WIDTH])
    v = _rms(jax.nn.gelu(za[:, A_WIDTH:]), sgn_ref[...]).astype(BF16)
    row = lax.broadcasted_iota(jnp.int32, (A_CHUNK, A_CHUNK), 0)
    col = lax.broadcasted_iota(jnp.int32, (A_CHUNK, A_CHUNK), 1)
    lane_group = lax.shift_right_logical(lax.broadcasted_iota(jnp.int32, (A_CHUNK, A_WIDTH), 1),
                                          (A_WIDTH // A_GROUPS).bit_length() - 1)
    w_tril = [jnp.where(row >= col, sgw_ref[g], 0.0).astype(BF16) for g in range(A_GROUPS)]
    for c in range(TOKEN_TILE // A_CHUNK):
        rows = slice(c * A_CHUNK, (c + 1) * A_CHUNK)
        mixed = jnp.zeros((A_CHUNK, A_WIDTH), F32)
        for g in range(A_GROUPS):
            mixed = jnp.where(lane_group == g, _dot(w_tril[g], v[rows]), mixed)
        ya_ref[0, rows, :] = (u[rows] * (mixed + sgb_ref[...])).astype(ya_ref.dtype)

    zq_t = (zq * (HEAD_DIM ** -0.5 * LOG2E)).T
    for g in range(C_KV_GROUPS):
        for c in range(TOKEN_TILE // Q_TILE):
            parts = []
            for r in range(C_HPG):
                base = (g * C_HPG + r) * HEAD_DIM
                parts.append(zq_t[base:base + HEAD_DIM, c * Q_TILE:(c + 1) * Q_TILE])
            qt_ref[0, g, c] = jnp.concatenate(parts, axis=1).astype(qt_ref.dtype)

    vs_t = zkv[:, 3 * KV_W:4 * KV_W].T
    vw_t = zkv[:, 5 * KV_W:6 * KV_W].T
    key_blk = lax.shift_right_logical(
        lax.broadcasted_iota(jnp.int32, (TOKEN_TILE, SEL_SLOTS), 0) + pl.program_id(1) * TOKEN_TILE, SEL_SHIFT)
    blk_onehot = jnp.where(key_blk == lax.broadcasted_iota(jnp.int32, (TOKEN_TILE, SEL_SLOTS), 1), 1.0, 0.0)
    ones_rows = jnp.where(lax.broadcasted_iota(jnp.int32, (V_PAD_ROWS, SEL_KEYS), 0) == 0, 1.0, 0.0)
    raw_ref[0, 0] = zkv[:, :KV_W]
    raw_ref[1, 0] = zkv[:, KV_W:2 * KV_W]
    for g in range(C_KV_GROUPS):
        lo, hi = g * HEAD_DIM, (g + 1) * HEAD_DIM
        ks_ref[0, g] = jnp.concatenate([zkv[:, 2 * KV_W + lo:2 * KV_W + hi], blk_onehot],
                                       axis=1).astype(ks_ref.dtype)
        kw_ref[0, g] = zkv[:, 4 * KV_W + lo:4 * KV_W + hi].astype(kw_ref.dtype)
        for c in range(TOKEN_TILE // SEL_KEYS):
            vst_ref[0, g, c] = jnp.concatenate(
                [vs_t[lo:hi, c * SEL_KEYS:(c + 1) * SEL_KEYS], ones_rows[:, :SEL_KEYS]], axis=0).astype(vst_ref.dtype)
        for c in range(TOKEN_TILE // WIN_KEYS):
            vwt_ref[0, g, c] = jnp.concatenate(
                [vw_t[lo:hi, c * WIN_KEYS:(c + 1) * WIN_KEYS], ones_rows[:, :WIN_KEYS]], axis=0).astype(vwt_ref.dtype)

    sg_t = _sigmoid(zg).T
    for br in range(3):
        for g in range(C_KV_GROUPS):
            base = br * C_HEADS + g * C_HPG
            gt_ref[0, br, g] = sg_t[base:base + C_HPG, :]

    yb_ref[0] = (_rglru_scan(state_ref, a_ref, b_ref) * jax.nn.gelu(gate_b)).astype(yb_ref.dtype)


def _head(h, norms, ffn_w, layer, w_in, sgu_norm_g, sgu_w, sgu_b, lru):
    b, s, d = h.shape
    assert ffn_w[0].shape[-1] % FFN_CHUNK == 0
    wb16 = w_in.astype(BF16)
    o = 0
    wa = wb16[:, o:o + 2 * A_WIDTH]; o += 2 * A_WIDTH
    wb = wb16[:, o:o + 2 * B_WIDTH]; o += 2 * B_WIDTH
    wq = wb16[:, o:o + C_WIDTH]; o += C_WIDTH
    wkv = wb16[:, o:o + 6 * KV_W]; o += 6 * KV_W
    wgt = jnp.pad(wb16[:, o:o + 3 * C_HEADS], ((0, 0), (0, LANES - 3 * C_HEADS)))
    sgb = jnp.repeat(sgu_b.T, A_WIDTH // A_GROUPS, axis=1)
    nt = s // TOKEN_TILE
    grid = (b, nt)
    G = C_KV_GROUPS
    out_shape = [
        jax.ShapeDtypeStruct((b, s, d), F32),
        jax.ShapeDtypeStruct((b, s, A_WIDTH), BF16),
        jax.ShapeDtypeStruct((b, s, B_WIDTH), BF16),
        jax.ShapeDtypeStruct((b, G, s // Q_TILE, HEAD_DIM, QL), BF16),
        jax.ShapeDtypeStruct((2, b, s, KV_W), F32),
        jax.ShapeDtypeStruct((b, G, s, HEAD_DIM + SEL_SLOTS), BF16),
        jax.ShapeDtypeStruct((b, G, s, HEAD_DIM), BF16),
        jax.ShapeDtypeStruct((b, G, s // SEL_KEYS, V_ROWS, SEL_KEYS), BF16),
        jax.ShapeDtypeStruct((b, G, s // WIN_KEYS, V_ROWS, WIN_KEYS), BF16),
        jax.ShapeDtypeStruct((b, 3, G, C_HPG, s), F32),
    ]
    out_specs = [
        pl.BlockSpec((1, TOKEN_TILE, d), lambda bi, i: (bi, i, 0)),
        pl.BlockSpec((1, TOKEN_TILE, A_WIDTH), lambda bi, i: (bi, i, 0)),
        pl.BlockSpec((1, TOKEN_TILE, B_WIDTH), lambda bi, i: (bi, i, 0)),
        pl.BlockSpec((1, G, TOKEN_TILE // Q_TILE, HEAD_DIM, QL), lambda bi, i: (bi, 0, i, 0, 0)),
        pl.BlockSpec((2, 1, TOKEN_TILE, KV_W), lambda bi, i: (0, bi, i, 0)),
        pl.BlockSpec((1, G, TOKEN_TILE, HEAD_DIM + SEL_SLOTS), lambda bi, i: (bi, 0, i, 0)),
        pl.BlockSpec((1, G, TOKEN_TILE, HEAD_DIM), lambda bi, i: (bi, 0, i, 0)),
        pl.BlockSpec((1, G, TOKEN_TILE // SEL_KEYS, V_ROWS, SEL_KEYS), lambda bi, i: (bi, 0, i, 0, 0)),
        pl.BlockSpec((1, G, TOKEN_TILE // WIN_KEYS, V_ROWS, WIN_KEYS), lambda bi, i: (bi, 0, i, 0, 0)),
        pl.BlockSpec((1, 3, G, C_HPG, TOKEN_TILE), lambda bi, i: (bi, 0, 0, 0, i)),
    ]
    in_specs = [
        pl.BlockSpec((1, TOKEN_TILE, d), lambda bi, i: (bi, i, 0)),
        _resident(norms.shape), *[_ffn_weight_spec(w, layer, 0) for w in ffn_w],
        _resident(wa.shape), _resident(wb.shape), _resident(wq.shape),
        _resident(wkv.shape), _resident(wgt.shape), _resident((1, A_WIDTH)),
        _resident(sgu_w.shape), _resident(sgb.shape),
        _resident((CONV_W, B_WIDTH)), _resident((1, B_WIDTH)), _resident((B_WIDTH, B_WIDTH)),
        _resident((1, B_WIDTH)), _resident((B_WIDTH, B_WIDTH)), _resident((1, B_WIDTH)), _resident((1, B_WIDTH)),
    ]
    conv_w, conv_b, lru_wa, lru_ba, lru_wx, lru_bx, lru_lambda = lru
    row = lambda v: v.reshape(1, B_WIDTH)
    return pl.pallas_call(
        _head_kernel, grid=grid, in_specs=in_specs, out_specs=out_specs, out_shape=out_shape,
        scratch_shapes=[pltpu.VMEM((TOKEN_TILE, d), F32),
                        pltpu.VMEM((SUBLANES, B_WIDTH), F32), pltpu.VMEM((SUBLANES, B_WIDTH), F32),
                        pltpu.VMEM((TOKEN_TILE, B_WIDTH), F32), pltpu.VMEM((TOKEN_TILE, B_WIDTH), F32)],
        compiler_params=_cparams(2), name="head",
    )(h, norms, *ffn_w, wa, wb, wq, wkv, wgt, sgu_norm_g.reshape(1, A_WIDTH), sgu_w, sgb,
      conv_w, row(conv_b), _block_diag(lru_wa).astype(BF16), row(lru_ba),
      _block_diag(lru_wx).astype(BF16), row(lru_bx), row(lru_lambda))


def _rglru_coefficients(zb, cw_ref, cb_ref, wa_ref, ba_ref, wx_ref, bxb_ref, lam_ref,
                        tail_ref, state_ref, a_ref, b_ref):
    xb = zb[:, :B_WIDTH]
    ext = jnp.concatenate([tail_ref[...], xb], axis=0)
    xc = cb_ref[...] + xb * cw_ref[CONV_W - 1:CONV_W, :]
    for k in range(CONV_W - 1):
        shift = CONV_W - 1 - k
        xc = xc + ext[SUBLANES - shift:SUBLANES - shift + TOKEN_TILE] * cw_ref[k:k + 1, :]
    tail_ref[...] = xb[TOKEN_TILE - SUBLANES:]

    xcb = xc.astype(BF16)
    r = _sigmoid(_dot(xcb, wa_ref[...]) + ba_ref[...])
    i = _sigmoid(_dot(xcb, wx_ref[...]) + bxb_ref[...])
    z = -lam_ref[...]
    e = jnp.exp(-jnp.abs(z))
    softplus = jnp.maximum(z, 0.0) + jnp.log1p(e)
    log_a = -LRU_C * r * softplus
    a = jnp.exp(log_a)
    b = jnp.sqrt(jnp.tanh(-log_a) * (a * a + 1.0)) * (i * xc)

    row = lax.broadcasted_iota(jnp.int32, a.shape, 0) & (SUBLANES - 1)
    for dist in (1, 2, 4):
        a_prev = jnp.where(row >= dist, pltpu.roll(a, dist, 0), 1.0)
        b_prev = jnp.where(row >= dist, pltpu.roll(b, dist, 0), 0.0)
        b = a * b_prev + b
        a = a * a_prev
    a_ref[...] = a
    b_ref[...] = b
    return zb[:, B_WIDTH:]


def _rglru_scan(state_ref, a_ref, b_ref):
    def body(k, h):
        off = pl.multiple_of(k * SUBLANES, SUBLANES)
        rows = pl.ds(off, SUBLANES)
        hs = b_ref[rows, :] + a_ref[rows, :] * h
        b_ref[rows, :] = hs
        return jnp.broadcast_to(hs[SUBLANES - 1:SUBLANES, :], hs.shape)

    state_ref[...] = lax.fori_loop(0, TOKEN_TILE // SUBLANES, body, state_ref[...])
    return b_ref[...]


def _block_diag(w):
    g, n, _ = w.shape
    out = jnp.zeros((g * n, g * n), w.dtype)
    for k in range(g):
        out = out.at[k * n:(k + 1) * n, k * n:(k + 1) * n].set(w[k])
    return out


def _compress_kernel(raw_ref, pos_ref, w1_ref, b1_ref, w2_ref, b2_ref, w2t_ref, b2c_ref, kc_ref, kct_ref):
    ncp = raw_ref.shape[2] // CMP_STRIDE
    groups = range(C_KV_GROUPS)
    hid_dim = w1_ref.shape[-1]
    u = [jnp.zeros((ncp, hid_dim), F32) for _ in groups]
    v = [jnp.zeros((ncp, hid_dim), F32) for _ in groups]
    for l in range(CMP_STRIDE):
        x = raw_ref[0, 0, pl.ds(l, ncp, stride=CMP_STRIDE), :]
        top = (x + pos_ref[0, l:l + 1, :]).astype(BF16)
        bot = (x + pos_ref[0, CMP_STRIDE + l:CMP_STRIDE + l + 1, :]).astype(BF16)
        for g in groups:
            u[g] = u[g] + _dot(top, w1_ref[0, g, l])
            v[g] = v[g] + _dot(bot, w1_ref[0, g, CMP_STRIDE + l])
    for g in groups:
        hid = jax.nn.gelu(u[g] + pltpu.roll(v[g], ncp - 1, 0) + b1_ref[0]).astype(BF16)
        kc_ref[0, 0, g] = (_dot(hid, w2_ref[0]) + b2_ref[0]).astype(kc_ref.dtype)
        t = lax.dot_general(w2t_ref[0], hid, (((1,), (1,)), ((), ())), preferred_element_type=F32)
        kct_ref[0, 0, g] = (t + b2c_ref[0]).astype(kct_ref.dtype)


def _compress(raw, cmp_pos, cmp_w1, cmp_b1, cmp_w2, cmp_b2):
    _, b, s, _ = raw.shape
    g, hd = C_KV_GROUPS, HEAD_DIM
    ncp = s // CMP_STRIDE
    hid = cmp_w1.shape[-1]
    w1 = cmp_w1.astype(BF16).reshape(2, CMP_LEN, hd, hid)
    zero = jnp.zeros_like(w1)
    w1g = jnp.stack([jnp.concatenate([w1 if k == gi else zero for k in range(g)], axis=2)
                     for gi in range(g)], axis=1)
    pos = jnp.tile(cmp_pos, (1, 1, g))
    sel = lambda *shape: pl.BlockSpec((1,) + shape, lambda kv, bi: (kv,) + (0,) * len(shape))
    return pl.pallas_call(
        _compress_kernel,
        grid=(2, b),
        in_specs=[pl.BlockSpec((1, 1, s, g * hd), lambda kv, bi: (kv, bi, 0, 0)),
                  sel(CMP_LEN, g * hd), sel(g, CMP_LEN, g * hd, hid), sel(1, hid), sel(hid, hd), sel(1, hd),
                  sel(hd, hid), sel(hd, 1)],
        out_specs=[pl.BlockSpec((1, 1, g, ncp, hd), lambda kv, bi: (kv, bi, 0, 0, 0)),
                   pl.BlockSpec((1, 1, g, hd, ncp), lambda kv, bi: (kv, bi, 0, 0, 0))],
        out_shape=[jax.ShapeDtypeStruct((2, b, g, ncp, hd), BF16),
                   jax.ShapeDtypeStruct((2, b, g, hd, ncp), BF16)],
        compiler_params=_cparams(2),
        name="compress",
    )(raw, pos, w1g, cmp_b1.reshape(2, 1, hid), cmp_w2.astype(BF16), cmp_b2.reshape(2, 1, hd),
      cmp_w2.astype(BF16).transpose(0, 2, 1), cmp_b2.reshape(2, hd, 1))


def _softmax_step(carry, s, v_t):
    m, acc = carry
    m_new = jnp.maximum(m, jnp.max(s, axis=0, keepdims=True))
    p = jnp.exp2(s - m_new).astype(BF16)
    acc = acc * jnp.exp2(m - m_new) + _dot(v_t, p)
    return m_new, acc


def _normalized(acc):
    return acc[:HEAD_DIM] / acc[HEAD_DIM:HEAD_DIM + 1]


def _nsa_kernel(qt_ref, kc_ref, vct_ref, ks_ref, vst_ref, kw_ref, vwt_ref, gate_ref,
                gsel_ref, gwin_ref, gcmp_ref, ovt_ref, o_ref, *, n_tiles):
    step = pl.program_id(1)
    ncp = kc_ref.shape[2]
    chains = [(t, g) for t in range(TILES_PER_STEP) for g in range(C_KV_GROUPS)]
    tile = [step * TILES_PER_STEP + t for t in range(TILES_PER_STEP)]
    qts = [qt_ref[0, g, t] for t, g in chains]

    back = WINDOW // WIN_KEYS
    n_win = WINDOW + Q_TILE
    first = [jnp.maximum(c - back, 0) for c in tile]
    s_cmp, s_win = [], []
    for k, (t, g) in enumerate(chains):
        y0 = pl.multiple_of((n_tiles - 1 - tile[t]) * (Q_TILE // CMP_STRIDE), SUBLANES)
        s_cmp.append(_dot(kc_ref[0, g], qts[k]) + gcmp_ref[g, pl.ds(y0, ncp), :])
    for k, (t, g) in enumerate(chains):
        start = pl.multiple_of(first[t] * WIN_KEYS, WIN_KEYS)
        rel = pl.multiple_of(jnp.maximum(back - tile[t], 0) * WIN_KEYS, WIN_KEYS)
        s_win.append(_dot(kw_ref[0, g, pl.ds(start, n_win), :], qts[k]) + gwin_ref[g, pl.ds(rel, n_win), :])

    tq = lax.broadcasted_iota(jnp.int32, (1, QL), 1) & (Q_TILE - 1)
    o_cmp, imp = [], []
    for k, (t, g) in enumerate(chains):
        e = jnp.exp2(s_cmp[k] - jnp.max(s_cmp[k], axis=0, keepdims=True))
        has_cmp = (tile[t] * Q_TILE + tq >= CMP_LEN - 1).astype(F32)
        p = e / jnp.sum(e, axis=0, keepdims=True) * has_cmp
        o_cmp.append(_dot(vct_ref[0, g], p.astype(BF16)))
        p_heads = p[:, 0:Q_TILE]
        for r in range(1, C_HPG):
            p_heads = p_heads + p[:, r * Q_TILE:(r + 1) * Q_TILE]
        p_hi = p_heads.astype(BF16)
        p_lo = (p_heads - p_hi.astype(F32)).astype(BF16)
        imp.append(_dot(ovt_ref[...], p_hi) + _dot(ovt_ref[...], p_lo))

    o_win = []
    for k, (t, g) in enumerate(chains):
        e = jnp.exp2(s_win[k] - jnp.max(s_win[k], axis=0, keepdims=True)).astype(BF16)
        acc = jnp.zeros((V_ROWS, QL), F32)
        for i in range(n_win // WIN_KEYS):
            acc = acc + _dot(vwt_ref[0, g, first[t] + i], e[i * WIN_KEYS:(i + 1) * WIN_KEYS])
        o_win.append(_normalized(acc))

    q_aug = _with_mask_rows(qts, [tile[t] for t, g in chains], imp)
    halves = range(SEL_KEYS // SEL_HALF)
    init = (jnp.full((1, QL), NEG, F32), jnp.zeros((V_ROWS, QL), F32))

    def sel_body(i, carry, far):
        off = pl.multiple_of(i * SEL_KEYS, SEL_KEYS)
        units = [(h, k) for h in halves for k in range(len(chains))]

        def logits_of(h, k):
            t, g = chains[k]
            logits = _dot(ks_ref[0, g, pl.ds(off + h * SEL_HALF, SEL_HALF), :], q_aug[k])
            if not far:
                x0 = (n_tiles - 1 - tile[t]) * Q_TILE
                logits = logits + gsel_ref[g, pl.ds(pl.multiple_of(x0 + off + h * SEL_HALF, Q_TILE), SEL_HALF), :]
            return logits

        carry = list(carry)
        s = [logits_of(*u) for u in units[:SEL_AHEAD]]
        for n, (h, k) in enumerate(units):
            if n + SEL_AHEAD < len(units):
                s.append(logits_of(*units[n + SEL_AHEAD]))
            t, g = chains[k]
            v_t = vst_ref[0, g, i][:, h * SEL_HALF:(h + 1) * SEL_HALF]
            carry[k] = _softmax_step(carry[k], s[n], v_t)
        return tuple(carry)

    n_far = jnp.maximum(tile[0] * Q_TILE - MAX_DISTANCE + 1, 0) // SEL_KEYS
    n_sel_steps = (tile[-1] * Q_TILE + Q_TILE + SEL_KEYS - 1) // SEL_KEYS
    sel = lax.fori_loop(0, n_far, functools.partial(sel_body, far=True), (init,) * len(chains))
    sel = tuple((m + gsel_ref[g, 0:1, :], acc) for (m, acc), (t, g) in zip(sel, chains))
    sel = lax.fori_loop(n_far, n_sel_steps, functools.partial(sel_body, far=False), sel)

    for t in range(TILES_PER_STEP):
        outs = []
        for k, (tk, g) in enumerate(chains):
            if tk != t:
                continue
            o_sel = _normalized(sel[k][1])
            tok = slice(t * Q_TILE, (t + 1) * Q_TILE)
            cols = []
            for r in range(C_HPG):
                ln = slice(r * Q_TILE, (r + 1) * Q_TILE)
                cols.append(gate_ref[0, 0, g, r:r + 1, tok] * o_cmp[k][:, ln]
                            + gate_ref[0, 1, g, r:r + 1, tok] * o_sel[:, ln]
                            + gate_ref[0, 2, g, r:r + 1, tok] * o_win[k][:, ln])
            outs += [jnp.concatenate(cols[2 * j:2 * j + 2], axis=0).T for j in range(C_HPG // 2)]
        o_ref[0, t * Q_TILE:(t + 1) * Q_TILE, :] = jnp.concatenate(outs, axis=1).astype(o_ref.dtype)


def _with_mask_rows(qts, tiles, imps):
    n_groups = SEL_SLOTS // SUBLANES
    j = lax.broadcasted_iota(jnp.int32, (SEL_SLOTS, Q_TILE), 0)
    half = lax.shift_right_logical(lax.broadcasted_iota(jnp.int32, (SEL_SLOTS, Q_TILE), 1), SEL_SHIFT)
    j_rows = j[:SUBLANES]
    scores = []
    for c, imp in zip(tiles, imps):
        blk = c * (Q_TILE // SEL_LEN) + half
        forced = (j == 0) | (j == blk) | (j == blk - 1)
        scores.append(jnp.where(j <= blk, jnp.where(forced, FORCE_SCORE, imp), -1.0))
    rows = [[sc[k * SUBLANES:(k + 1) * SUBLANES] for k in range(n_groups)] for sc in scores]

    def add_pair(ranks, jg, k):
        for ci, sc in enumerate(scores):
            acc = ranks[ci][k]
            for jp in range(jg * SUBLANES, (jg + 1) * SUBLANES):
                other = sc[jp:jp + 1, :]
                if k > jg:
                    beats = jnp.where(other >= rows[ci][k], 1, 0)
                elif k < jg:
                    beats = jnp.where(other > rows[ci][k], 1, 0)
                else:
                    beats = jnp.where(j_rows > jp - k * SUBLANES, jnp.where(other >= rows[ci][k], 1, 0),
                                      jnp.where(other > rows[ci][k], 1, 0))
                acc = acc + beats
            ranks[ci][k] = acc

    last_blk = tiles[-1] * (Q_TILE // SEL_LEN) + Q_TILE // SEL_LEN - 1
    ranks = [[jnp.zeros((SUBLANES, Q_TILE), jnp.int32) for _ in range(n_groups)] for _ in scores]
    for m in range(n_groups):
        def shell(ranks, m=m):
            ranks = [list(r) for r in ranks]
            for k in range(m + 1):
                add_pair(ranks, m, k)
            for jg in range(m):
                add_pair(ranks, jg, m)
            return ranks
        ranks = shell(ranks) if m == 0 else lax.cond(m * SUBLANES <= last_blk, shell, lambda r: r, ranks)

    out = []
    for qt, sc, rk in zip(qts, scores, ranks):
        rank = jnp.concatenate(rk, axis=0)
        mask_rows = jnp.where((rank < SEL_TOP) & (sc >= 0.0), 0.0, NEG).astype(BF16)
        out.append(jnp.concatenate([qt, jnp.concatenate([mask_rows] * C_HPG, axis=1)], axis=0))
    return out


def _nsa(qt, kc, vct, ks, vst, kw, vwt, gates, gsel, gwin, gcmp, ovt):
    b, G, n_tiles = qt.shape[:3]
    s = n_tiles * Q_TILE
    per_b = lambda a: pl.BlockSpec((1,) + a.shape[1:], lambda bi, c: (bi,) + (0,) * (a.ndim - 1))
    return pl.pallas_call(
        functools.partial(_nsa_kernel, n_tiles=n_tiles),
        grid=(b, n_tiles // TILES_PER_STEP),
        in_specs=[pl.BlockSpec((1, G, TILES_PER_STEP, HEAD_DIM, QL), lambda bi, c: (bi, 0, c, 0, 0)),
                  per_b(kc), per_b(vct), per_b(ks), per_b(vst), per_b(kw), per_b(vwt),
                  pl.BlockSpec((1, 3, G, C_HPG, TILES_PER_STEP * Q_TILE), lambda bi, c: (bi, 0, 0, 0, c)),
                  _resident(gsel.shape), _resident(gwin.shape), _resident(gcmp.shape),
                  _resident(ovt.shape)],
        out_specs=pl.BlockSpec((1, TILES_PER_STEP * Q_TILE, C_WIDTH), lambda bi, c: (bi, c, 0)),
        out_shape=jax.ShapeDtypeStruct((b, s, C_WIDTH), BF16),
        compiler_params=_cparams(2),
        name="nsa",
    )(qt, kc, vct, ks, vst, kw, vwt, gates, gsel, gwin, gcmp, ovt)


def _tail_kernel(h_ref, ya_ref, yb_ref, yc_ref, p_ref, norm_ref, woa_ref, wob_ref, woc_ref,
                 wg_ref, wu_ref, wd_ref, wpg_ref, wpp_ref, o_ref, acc_ref):
    subs = _sub_tiles(h_ref.shape[0])
    mix = [_dot(ya_ref[r, :], woa_ref[...]) + _dot(yb_ref[r, :], wob_ref[...]) + _dot(yc_ref[r, :], woc_ref[...])
           for r in subs]
    emb = [_dot(p_ref[r, :].astype(BF16), wpp_ref[...]) for r in subs]
    h = [h_ref[r, :] + _rms(m, norm_ref[3:4, :]) for r, m in zip(subs, mix)]
    h = _ffn_body(h, norm_ref[4:5, :], norm_ref[5:6, :], wg_ref, wu_ref, wd_ref, acc_ref)
    gate = [_dot(_rms(x, norm_ref[6:7, :]).astype(BF16), wpg_ref[...]) for x in h]
    for r, x, g, e in zip(subs, h, gate, emb):
        o_ref[r, :] = x + _rms(_sigmoid(g) * e, norm_ref[7:8, :])


def _tail(h, ya, yb, yc, p, norms, ffn_w, layer, w_out, w_gate, w_proj):
    n, d = h.shape
    dp = p.shape[-1]
    w = w_out.astype(BF16)
    woa, wob, woc = w[:A_WIDTH], w[A_WIDTH:A_WIDTH + B_WIDTH], w[A_WIDTH + B_WIDTH:]
    tile = lambda width: pl.BlockSpec((TOKEN_TILE, width), lambda i: (i, 0))
    return pl.pallas_call(
        _tail_kernel,
        grid=(n // TOKEN_TILE,),
        in_specs=[tile(d), tile(A_WIDTH), tile(B_WIDTH), tile(C_WIDTH),
                  pl.BlockSpec((None, TOKEN_TILE, dp), lambda i: (layer, i, 0)),
                  _resident(norms.shape), _resident(woa.shape), _resident(wob.shape), _resident(woc.shape),
                  *[_ffn_weight_spec(wt, layer, 1) for wt in ffn_w],
                  _resident((d, d)), _resident((dp, d))],
        out_specs=tile(d),
        out_shape=jax.ShapeDtypeStruct((n, d), F32),
        scratch_shapes=[pltpu.VMEM((TOKEN_TILE, d), F32)],
        compiler_params=_cparams(1),
        name="tail",
    )(h, ya, yb, yc, p, norms, woa, wob, woc, *ffn_w, w_gate.astype(BF16), w_proj.astype(BF16))


def _overlap_t(s):
    ncp = s // CMP_STRIDE
    n_cmp = (s - CMP_LEN) // CMP_STRIDE + 1
    cs = jnp.arange(ncp) * CMP_STRIDE
    ss = jnp.arange(s // SEL_LEN) * SEL_LEN
    ov = jnp.clip(jnp.minimum(cs[None] + CMP_LEN, ss[:, None] + SEL_LEN)
                  - jnp.maximum(cs[None], ss[:, None]), 0, None).astype(F32) / CMP_LEN
    ov = jnp.where(jnp.arange(ncp)[None] < n_cmp, ov, 0.0).astype(BF16)
    return jnp.pad(ov, ((0, SEL_SLOTS - s // SEL_LEN), (0, 0)))


def kernel(x, p, rel_bias, norm_g, ffn_w_gate, ffn_w_up, ffn_w_down, w_in, w_out, sgu_norm_g, sgu_w, sgu_b,
           conv_w, conv_b, lru_wa, lru_ba, lru_wx, lru_bx, lru_lambda, cmp_pos, cmp_w1, cmp_b1, cmp_w2,
           cmp_b2, ple_w_gate, ple_w_proj):
    b, s, d = x.shape
    depth = norm_g.shape[0]
    assert s % TOKEN_TILE == 0 and s % SEL_KEYS == 0
    assert s % (TILES_PER_STEP * Q_TILE) == 0
    assert s >= WINDOW + Q_TILE and SEL_TOP <= s // SEL_LEN <= SEL_SLOTS
    n_tiles = s // Q_TILE

    rbx = jnp.repeat(rel_bias.reshape(N_BUCKETS, C_KV_GROUPS, C_HPG).transpose(1, 0, 2), Q_TILE, axis=2)
    no_limit = 1 << 30
    gsel = _bias_table(rbx, s + SEL_KEYS, 1, s - Q_TILE, no_limit)
    gwin = _bias_table(rbx, 2 * WINDOW + Q_TILE, 1, WINDOW, WINDOW)
    per_tile = Q_TILE // CMP_STRIDE
    gcmp = _bias_table(rbx, per_tile * (n_tiles - 1) + s // CMP_STRIDE, CMP_STRIDE,
                       CMP_STRIDE * per_tile * (n_tiles - 1) - (CMP_LEN - 1), no_limit)
    ovt = _overlap_t(s)

    ffn_w = (ffn_w_gate.astype(BF16), ffn_w_up.astype(BF16), ffn_w_down.astype(BF16))
    h = x
    flat = lambda a: a.reshape(b * s, -1)
    for i in range(depth):
        h, ya, yb, qt, raw, ks, kw, vst, vwt, gates = _head(
            h, norm_g[i], ffn_w, i, w_in[i], sgu_norm_g[i], sgu_w[i], sgu_b[i],
            (conv_w[i], conv_b[i], lru_wa[i], lru_ba[i], lru_wx[i], lru_bx[i], lru_lambda[i]))
        kc, kct = _compress(raw, cmp_pos[i], cmp_w1[i], cmp_b1[i], cmp_w2[i], cmp_b2[i])
        yc = _nsa(qt, kc[0], kct[1], ks, vst, kw, vwt, gates, gsel, gwin, gcmp, ovt)
        h = _tail(flat(h), flat(ya), flat(yb), flat(yc), p.reshape(depth, b * s, -1), norm_g[i], ffn_w, i,
                  w_out[i], ple_w_gate[i], ple_w_proj[i]).reshape(b, s, d)
    return h
```

```python
import functools
import math

import jax
import jax.numpy as jnp
from jax import lax
from jax.experimental import pallas as pl
from jax.experimental.pallas import tpu as pltpu

F32 = jnp.float32
BF16 = jnp.bfloat16

RMS_EPS = 1e-6
A_GROUPS = 4
A_WIDTH = 256
A_CHUNK = 128
B_GROUPS = 4
B_WIDTH = 256
CONV_W = 4
LRU_C = 8.0
C_HEADS = 8
C_KV_GROUPS = 2
C_HPG = C_HEADS // C_KV_GROUPS
HEAD_DIM = 64
C_WIDTH = C_HEADS * HEAD_DIM
KV_W = C_KV_GROUPS * HEAD_DIM
CMP_LEN = 32
CMP_STRIDE = 16
SEL_LEN = 64
SEL_SHIFT = 6
SEL_SLOTS = 64
SEL_TOP = 16
WINDOW = 512
FORCE_SCORE = 1e4
NEG = -1e30
N_BUCKETS = 32
MAX_DISTANCE = 1024

LANES = 128
TOKEN_TILE = 512
SUB_TILES = 2
FFN_CHUNK = 256
SUBLANES = 8
Q_TILE = 128
TILES_PER_STEP = 4
SEL_KEYS = 512
SEL_HALF = 256
SEL_AHEAD = 2
WIN_KEYS = 128
QL = C_HPG * Q_TILE
V_PAD_ROWS = 16
V_ROWS = HEAD_DIM + V_PAD_ROWS
LOG2E = math.log2(math.e)
TABLE_ROWS = 128
VMEM_LIMIT = 56 * 1024 * 1024


def _cparams(n_axes):
    return pltpu.CompilerParams(dimension_semantics=("arbitrary",) * n_axes,
                                vmem_limit_bytes=VMEM_LIMIT)


def _resident(shape):
    nd = len(shape)
    return pl.BlockSpec(shape, lambda *_: (0,) * nd, pipeline_mode=pl.Buffered(1))


def _rms(x, g):
    return x * lax.rsqrt(jnp.mean(x * x, axis=-1, keepdims=True) + RMS_EPS) * g


def _sigmoid(x):
    return 1.0 / (1.0 + jnp.exp(-x))


def _dot(a, b):
    return jnp.dot(a, b, preferred_element_type=F32)


def _bias_table_kernel(rbx_ref, o_ref, *, stride, offset, dmax):
    i = pl.program_id(1)
    shape = (TABLE_ROWS, QL)
    d_hi = (Q_TILE - 1) - stride * (i * TABLE_ROWS) + offset
    d_lo = -stride * (i * TABLE_ROWS + TABLE_ROWS - 1) + offset
    masked = (d_hi < 0) | (d_lo >= dmax)
    far = (d_lo >= MAX_DISTANCE) & (d_hi < dmax)

    @pl.when(masked)
    def _():
        o_ref[0] = jnp.full(shape, NEG, F32)

    @pl.when(far)
    def _():
        o_ref[0] = jnp.broadcast_to(rbx_ref[0, N_BUCKETS - 1:N_BUCKETS, :] * LOG2E, shape)

    @pl.when(jnp.logical_not(masked | far))
    def _():
        x = lax.broadcasted_iota(jnp.int32, shape, 0) + i * TABLE_ROWS
        t = lax.broadcasted_iota(jnp.int32, shape, 1) & (Q_TILE - 1)
        d = t - stride * x + offset
        n = jnp.maximum(d, 0)
        max_exact = N_BUCKETS // 2
        nf = jnp.maximum(n, max_exact).astype(F32)
        large = max_exact + (jnp.log(nf / max_exact) / math.log(MAX_DISTANCE / max_exact)
                             * (N_BUCKETS - max_exact)).astype(jnp.int32)
        large = jnp.minimum(large, N_BUCKETS - 1)
        bucket = jnp.where(n < max_exact, n, large)
        acc = jnp.zeros(shape, F32)
        for k in range(N_BUCKETS):
            acc = jnp.where(bucket == k, rbx_ref[0, k:k + 1, :], acc)
        o_ref[0] = jnp.where((d >= 0) & (d < dmax), acc * LOG2E, NEG)


def _bias_table(rbx, rows, stride, offset, dmax):
    rows_p = -(-rows // TABLE_ROWS) * TABLE_ROWS
    return pl.pallas_call(
        functools.partial(_bias_table_kernel, stride=stride, offset=offset, dmax=dmax),
        grid=(C_KV_GROUPS, rows_p // TABLE_ROWS),
        in_specs=[pl.BlockSpec((1, N_BUCKETS, QL), lambda g, i: (g, 0, 0))],
        out_specs=pl.BlockSpec((1, TABLE_ROWS, QL), lambda g, i: (g, i, 0)),
        out_shape=jax.ShapeDtypeStruct((C_KV_GROUPS, rows_p, QL), F32),
        compiler_params=_cparams(2),
        name="bias_table",
    )(rbx)


def _sub_tiles(n_rows):
    rows = n_rows // SUB_TILES
    return [slice(k * rows, (k + 1) * rows) for k in range(SUB_TILES)]


def _ffn_body(xs, g_pre, g_post, wg_ref, wu_ref, wd_ref, acc_ref):
    subs = _sub_tiles(acc_ref.shape[0])
    xn = [_rms(x, g_pre).astype(BF16) for x in xs]
    nch = wg_ref.shape[1] // FFN_CHUNK
    cols = lambda j: slice(j * FFN_CHUNK, (j + 1) * FFN_CHUNK)
    gate_up = [(_dot(x, wg_ref[:, cols(0)]), _dot(x, wu_ref[:, cols(0)])) for x in xn]
    for j in range(nch):
        if j + 1 < nch:
            nxt = [(_dot(x, wg_ref[:, cols(j + 1)]), _dot(x, wu_ref[:, cols(j + 1)])) for x in xn]
        for rows, (gate, up) in zip(subs, gate_up):
            hid = (gate * _sigmoid(gate) * up).astype(BF16)
            down = _dot(hid, wd_ref[cols(j), :])
            if j == 0:
                acc_ref[rows, :] = down
            else:
                acc_ref[rows, :] += down
        gate_up = nxt
    return [x + 0.5 * _rms(acc_ref[rows, :], g_post) for x, rows in zip(xs, subs)]


def _ffn_weight_spec(w, layer, which):
    return pl.BlockSpec((None, None) + w.shape[2:], lambda *_: (layer, which, 0, 0), pipeline_mode=pl.Buffered(1))


def _head_kernel(h_ref, norm_ref, wg_ref, wu_ref, wd_ref, wa_ref, wb_ref, wq_ref, wkv_ref, wgt_ref,
                 sgn_ref, sgw_ref, sgb_ref, cw_ref, cb_ref, lwa_ref, lba_ref, lwx_ref, lbx_ref, lam_ref,
                 h_out_ref, ya_ref, yb_ref, qt_ref, raw_ref, ks_ref, kw_ref, vst_ref, vwt_ref, gt_ref,
                 acc_ref, tail_ref, state_ref, a_ref, b_ref):
    @pl.when(pl.program_id(1) == 0)
    def _():
        tail_ref[...] = jnp.zeros_like(tail_ref)
        state_ref[...] = jnp.zeros_like(state_ref)

    h = _ffn_body([h_ref[0, r, :] for r in _sub_tiles(TOKEN_TILE)], norm_ref[0:1, :], norm_ref[1:2, :],
                  wg_ref, wu_ref, wd_ref, acc_ref)
    h = jnp.concatenate(h, axis=0)
    h_out_ref[0] = h
    xn = _rms(h, norm_ref[2:3, :]).astype(BF16)

    zb = _dot(xn, wb_ref[...])
    za = _dot(xn, wa_ref[...])
    zq = _dot(xn, wq_ref[...])
    zkv = _dot(xn, wkv_ref[...])
    zg = _dot(xn, wgt_ref[...])

    gate_b = _rglru_coefficients(zb, cw_ref, cb_ref, lwa_ref, lba_ref, lwx_ref, lbx_ref, lam_ref,
                                 tail_ref, state_ref, a_ref, b_ref)

    u = jax.nn.gelu(za[:, :A_WIDTH])
    v = _rms(jax.nn.gelu(za[:, A_WIDTH:]), sgn_ref[...]).astype(BF16)
    row = lax.broadcasted_iota(jnp.int32, (A_CHUNK, A_CHUNK), 0)
    col = lax.broadcasted_iota(jnp.int32, (A_CHUNK, A_CHUNK), 1)
    lane_group = lax.shift_right_logical(lax.broadcasted_iota(jnp.int32, (A_CHUNK, A_WIDTH), 1),
                                          (A_WIDTH // A_GROUPS).bit_length() - 1)
    w_tril = [jnp.where(row >= col, sgw_ref[g], 0.0).astype(BF16) for g in range(A_GROUPS)]
    for c in range(TOKEN_TILE // A_CHUNK):
        rows = slice(c * A_CHUNK, (c + 1) * A_CHUNK)
        mixed = jnp.zeros((A_CHUNK, A_WIDTH), F32)
        for g in range(A_GROUPS):
            mixed = jnp.where(lane_group == g, _dot(w_tril[g], v[rows]), mixed)
        ya_ref[0, rows, :] = (u[rows] * (mixed + sgb_ref[...])).astype(ya_ref.dtype)

    zq_t = (zq * (HEAD_DIM ** -0.5 * LOG2E)).T
    for g in range(C_KV_GROUPS):
        for c in range(TOKEN_TILE // Q_TILE):
            parts = []
            for r in range(C_HPG):
                base = (g * C_HPG + r) * HEAD_DIM
                parts.append(zq_t[base:base + HEAD_DIM, c * Q_TILE:(c + 1) * Q_TILE])
            qt_ref[0, g, c] = jnp.concatenate(parts, axis=1).astype(qt_ref.dtype)

    vs_t = zkv[:, 3 * KV_W:4 * KV_W].T
    vw_t = zkv[:, 5 * KV_W:6 * KV_W].T
    key_blk = lax.shift_right_logical(
        lax.broadcasted_iota(jnp.int32, (TOKEN_TILE, SEL_SLOTS), 0) + pl.program_id(1) * TOKEN_TILE, SEL_SHIFT)
    blk_onehot = jnp.where(key_blk == lax.broadcasted_iota(jnp.int32, (TOKEN_TILE, SEL_SLOTS), 1), 1.0, 0.0)
    ones_rows = jnp.where(lax.broadcasted_iota(jnp.int32, (V_PAD_ROWS, SEL_KEYS), 0) == 0, 1.0, 0.0)
    raw_ref[0, 0] = zkv[:, :KV_W]
    raw_ref[1, 0] = zkv[:, KV_W:2 * KV_W]
    for g in range(C_KV_GROUPS):
        lo, hi = g * HEAD_DIM, (g + 1) * HEAD_DIM
        ks_ref[0, g] = jnp.concatenate([zkv[:, 2 * KV_W + lo:2 * KV_W + hi], blk_onehot],
                                       axis=1).astype(ks_ref.dtype)
        kw_ref[0, g] = zkv[:, 4 * KV_W + lo:4 * KV_W + hi].astype(kw_ref.dtype)
        for c in range(TOKEN_TILE // SEL_KEYS):
            vst_ref[0, g, c] = jnp.concatenate(
                [vs_t[lo:hi, c * SEL_KEYS:(c + 1) * SEL_KEYS], ones_rows[:, :SEL_KEYS]], axis=0).astype(vst_ref.dtype)
        for c in range(TOKEN_TILE // WIN_KEYS):
            vwt_ref[0, g, c] = jnp.concatenate(
                [vw_t[lo:hi, c * WIN_KEYS:(c + 1) * WIN_KEYS], ones_rows[:, :WIN_KEYS]], axis=0).astype(vwt_ref.dtype)

    sg_t = _sigmoid(zg).T
    for br in range(3):
        for g in range(C_KV_GROUPS):
            base = br * C_HEADS + g * C_HPG
            gt_ref[0, br, g] = sg_t[base:base + C_HPG, :]

    yb_ref[0] = (_rglru_scan(state_ref, a_ref, b_ref) * jax.nn.gelu(gate_b)).astype(yb_ref.dtype)


def _head(h, norms, ffn_w, layer, w_in, sgu_norm_g, sgu_w, sgu_b, lru):
    b, s, d = h.shape
    assert ffn_w[0].shape[-1] % FFN_CHUNK == 0
    wb16 = w_in.astype(BF16)
    o = 0
    wa = wb16[:, o:o + 2 * A_WIDTH]; o += 2 * A_WIDTH
    wb = wb16[:, o:o + 2 * B_WIDTH]; o += 2 * B_WIDTH
    wq = wb16[:, o:o + C_WIDTH]; o += C_WIDTH
    wkv = wb16[:, o:o + 6 * KV_W]; o += 6 * KV_W
    wgt = jnp.pad(wb16[:, o:o + 3 * C_HEADS], ((0, 0), (0, LANES - 3 * C_HEADS)))
    sgb = jnp.repeat(sgu_b.T, A_WIDTH // A_GROUPS, axis=1)
    nt = s // TOKEN_TILE
    grid = (b, nt)
    G = C_KV_GROUPS
    out_shape = [
        jax.ShapeDtypeStruct((b, s, d), F32),
        jax.ShapeDtypeStruct((b, s, A_WIDTH), BF16),
        jax.ShapeDtypeStruct((b, s, B_WIDTH), BF16),
        jax.ShapeDtypeStruct((b, G, s // Q_TILE, HEAD_DIM, QL), BF16),
        jax.ShapeDtypeStruct((2, b, s, KV_W), F32),
        jax.ShapeDtypeStruct((b, G, s, HEAD_DIM + SEL_SLOTS), BF16),
        jax.ShapeDtypeStruct((b, G, s, HEAD_DIM), BF16),
        jax.ShapeDtypeStruct((b, G, s // SEL_KEYS, V_ROWS, SEL_KEYS), BF16),
        jax.ShapeDtypeStruct((b, G, s // WIN_KEYS, V_ROWS, WIN_KEYS), BF16),
        jax.ShapeDtypeStruct((b, 3, G, C_HPG, s), F32),
    ]
    out_specs = [
        pl.BlockSpec((1, TOKEN_TILE, d), lambda bi, i: (bi, i, 0)),
        pl.BlockSpec((1, TOKEN_TILE, A_WIDTH), lambda bi, i: (bi, i, 0)),
        pl.BlockSpec((1, TOKEN_TILE, B_WIDTH), lambda bi, i: (bi, i, 0)),
        pl.BlockSpec((1, G, TOKEN_TILE // Q_TILE, HEAD_DIM, QL), lambda bi, i: (bi, 0, i, 0, 0)),
        pl.BlockSpec((2, 1, TOKEN_TILE, KV_W), lambda bi, i: (0, bi, i, 0)),
        pl.BlockSpec((1, G, TOKEN_TILE, HEAD_DIM + SEL_SLOTS), lambda bi, i: (bi, 0, i, 0)),
        pl.BlockSpec((1, G, TOKEN_TILE, HEAD_DIM), lambda bi, i: (bi, 0, i, 0)),
        pl.BlockSpec((1, G, TOKEN_TILE // SEL_KEYS, V_ROWS, SEL_KEYS), lambda bi, i: (bi, 0, i, 0, 0)),
        pl.BlockSpec((1, G, TOKEN_TILE // WIN_KEYS, V_ROWS, WIN_KEYS), lambda bi, i: (bi, 0, i, 0, 0)),
        pl.BlockSpec((1, 3, G, C_HPG, TOKEN_TILE), lambda bi, i: (bi, 0, 0, 0, i)),
    ]
    in_specs = [
        pl.BlockSpec((1, TOKEN_TILE, d), lambda bi, i: (bi, i, 0)),
        _resident(norms.shape), *[_ffn_weight_spec(w, layer, 0) for w in ffn_w],
        _resident(wa.shape), _resident(wb.shape), _resident(wq.shape),
        _resident(wkv.shape), _resident(wgt.shape), _resident((1, A_WIDTH)),
        _resident(sgu_w.shape), _resident(sgb.shape),
        _resident((CONV_W, B_WIDTH)), _resident((1, B_WIDTH)), _resident((B_WIDTH, B_WIDTH)),
        _resident((1, B_WIDTH)), _resident((B_WIDTH, B_WIDTH)), _resident((1, B_WIDTH)), _resident((1, B_WIDTH)),
    ]
    conv_w, conv_b, lru_wa, lru_ba, lru_wx, lru_bx, lru_lambda = lru
    row = lambda v: v.reshape(1, B_WIDTH)
    return pl.pallas_call(
        _head_kernel, grid=grid, in_specs=in_specs, out_specs=out_specs, out_shape=out_shape,
        scratch_shapes=[pltpu.VMEM((TOKEN_TILE, d), F32),
                        pltpu.VMEM((SUBLANES, B_WIDTH), F32), pltpu.VMEM((SUBLANES, B_WIDTH), F32),
                        pltpu.VMEM((TOKEN_TILE, B_WIDTH), F32), pltpu.VMEM((TOKEN_TILE, B_WIDTH), F32)],
        compiler_params=_cparams(2), name="head",
    )(h, norms, *ffn_w, wa, wb, wq, wkv, wgt, sgu_norm_g.reshape(1, A_WIDTH), sgu_w, sgb,
      conv_w, row(conv_b), _block_diag(lru_wa).astype(BF16), row(lru_ba),
      _block_diag(lru_wx).astype(BF16), row(lru_bx), row(lru_lambda))


def _rglru_coefficients(zb, cw_ref, cb_ref, wa_ref, ba_ref, wx_ref, bxb_ref, lam_ref,
                        tail_ref, state_ref, a_ref, b_ref):
    xb = zb[:, :B_WIDTH]
    ext = jnp.concatenate([tail_ref[...], xb], axis=0)
    xc = cb_ref[...] + xb * cw_ref[CONV_W - 1:CONV_W, :]
    for k in range(CONV_W - 1):
        shift = CONV_W - 1 - k
        xc = xc + ext[SUBLANES - shift:SUBLANES - shift + TOKEN_TILE] * cw_ref[k:k + 1, :]
    tail_ref[...] = xb[TOKEN_TILE - SUBLANES:]

    xcb = xc.astype(BF16)
    r = _sigmoid(_dot(xcb, wa_ref[...]) + ba_ref[...])
    i = _sigmoid(_dot(xcb, wx_ref[...]) + bxb_ref[...])
    z = -lam_ref[...]
    e = jnp.exp(-jnp.abs(z))
    softplus = jnp.maximum(z, 0.0) + jnp.log1p(e)
    log_a = -LRU_C * r * softplus
    a = jnp.exp(log_a)
    b = jnp.sqrt(jnp.tanh(-log_a) * (a * a + 1.0)) * (i * xc)

    row = lax.broadcasted_iota(jnp.int32, a.shape, 0) & (SUBLANES - 1)
    for dist in (1, 2, 4):
        a_prev = jnp.where(row >= dist, pltpu.roll(a, dist, 0), 1.0)
        b_prev = jnp.where(row >= dist, pltpu.roll(b, dist, 0), 0.0)
        b = a * b_prev + b
        a = a * a_prev
    a_ref[...] = a
    b_ref[...] = b
    return zb[:, B_WIDTH:]


def _rglru_scan(state_ref, a_ref, b_ref):
    def body(k, h):
        off = pl.multiple_of(k * SUBLANES, SUBLANES)
        rows = pl.ds(off, SUBLANES)
        hs = b_ref[rows, :] + a_ref[rows, :] * h
        b_ref[rows, :] = hs
        return jnp.broadcast_to(hs[SUBLANES - 1:SUBLANES, :], hs.shape)

    state_ref[...] = lax.fori_loop(0, TOKEN_TILE // SUBLANES, body, state_ref[...])
    return b_ref[...]


def _block_diag(w):
    g, n, _ = w.shape
    out = jnp.zeros((g * n, g * n), w.dtype)
    for k in range(g):
        out = out.at[k * n:(k + 1) * n, k * n:(k + 1) * n].set(w[k])
    return out


def _compress_kernel(raw_ref, pos_ref, w1_ref, b1_ref, w2_ref, b2_ref, w2t_ref, b2c_ref, kc_ref, kct_ref):
    ncp = raw_ref.shape[2] // CMP_STRIDE
    groups = range(C_KV_GROUPS)
    hid_dim = w1_ref.shape[-1]
    u = [jnp.zeros((ncp, hid_dim), F32) for _ in groups]
    v = [jnp.zeros((ncp, hid_dim), F32) for _ in groups]
    for l in range(CMP_STRIDE):
        x = raw_ref[0, 0, pl.ds(l, ncp, stride=CMP_STRIDE), :]
        top = (x + pos_ref[0, l:l + 1, :]).astype(BF16)
        bot = (x + pos_ref[0, CMP_STRIDE + l:CMP_STRIDE + l + 1, :]).astype(BF16)
        for g in groups:
            u[g] = u[g] + _dot(top, w1_ref[0, g, l])
            v[g] = v[g] + _dot(bot, w1_ref[0, g, CMP_STRIDE + l])
    for g in groups:
        hid = jax.nn.gelu(u[g] + pltpu.roll(v[g], ncp - 1, 0) + b1_ref[0]).astype(BF16)
        kc_ref[0, 0, g] = (_dot(hid, w2_ref[0]) + b2_ref[0]).astype(kc_ref.dtype)
        t = lax.dot_general(w2t_ref[0], hid, (((1,), (1,)), ((), ())), preferred_element_type=F32)
        kct_ref[0, 0, g] = (t + b2c_ref[0]).astype(kct_ref.dtype)


def _compress(raw, cmp_pos, cmp_w1, cmp_b1, cmp_w2, cmp_b2):
    _, b, s, _ = raw.shape
    g, hd = C_KV_GROUPS, HEAD_DIM
    ncp = s // CMP_STRIDE
    hid = cmp_w1.shape[-1]
    w1 = cmp_w1.astype(BF16).reshape(2, CMP_LEN, hd, hid)
    zero = jnp.zeros_like(w1)
    w1g = jnp.stack([jnp.concatenate([w1 if k == gi else zero for k in range(g)], axis=2)
                     for gi in range(g)], axis=1)
    pos = jnp.tile(cmp_pos, (1, 1, g))
    sel = lambda *shape: pl.BlockSpec((1,) + shape, lambda kv, bi: (kv,) + (0,) * len(shape))
    return pl.pallas_call(
        _compress_kernel,
        grid=(2, b),
        in_specs=[pl.BlockSpec((1, 1, s, g * hd), lambda kv, bi: (kv, bi, 0, 0)),
                  sel(CMP_LEN, g * hd), sel(g, CMP_LEN, g * hd, hid), sel(1, hid), sel(hid, hd), sel(1, hd),
                  sel(hd, hid), sel(hd, 1)],
        out_specs=[pl.BlockSpec((1, 1, g, ncp, hd), lambda kv, bi: (kv, bi, 0, 0, 0)),
                   pl.BlockSpec((1, 1, g, hd, ncp), lambda kv, bi: (kv, bi, 0, 0, 0))],
        out_shape=[jax.ShapeDtypeStruct((2, b, g, ncp, hd), BF16),
                   jax.ShapeDtypeStruct((2, b, g, hd, ncp), BF16)],
        compiler_params=_cparams(2),
        name="compress",
    )(raw, pos, w1g, cmp_b1.reshape(2, 1, hid), cmp_w2.astype(BF16), cmp_b2.reshape(2, 1, hd),
      cmp_w2.astype(BF16).transpose(0, 2, 1), cmp_b2.reshape(2, hd, 1))


def _softmax_step(carry, s, v_t):
    m, acc = carry
    m_new = jnp.maximum(m, jnp.max(s, axis=0, keepdims=True))
    p = jnp.exp2(s - m_new).astype(BF16)
    acc = acc * jnp.exp2(m - m_new) + _dot(v_t, p)
    return m_new, acc


def _normalized(acc):
    return acc[:HEAD_DIM] / acc[HEAD_DIM:HEAD_DIM + 1]


def _nsa_kernel(qt_ref, kc_ref, vct_ref, ks_ref, vst_ref, kw_ref, vwt_ref, gate_ref,
                gsel_ref, gwin_ref, gcmp_ref, ovt_ref, o_ref, *, n_tiles):
    step = pl.program_id(1)
    ncp = kc_ref.shape[2]
    chains = [(t, g) for t in range(TILES_PER_STEP) for g in range(C_KV_GROUPS)]
    tile = [step * TILES_PER_STEP + t for t in range(TILES_PER_STEP)]
    qts = [qt_ref[0, g, t] for t, g in chains]

    back = WINDOW // WIN_KEYS
    n_win = WINDOW + Q_TILE
    first = [jnp.maximum(c - back, 0) for c in tile]
    s_cmp, s_win = [], []
    for k, (t, g) in enumerate(chains):
        y0 = pl.multiple_of((n_tiles - 1 - tile[t]) * (Q_TILE // CMP_STRIDE), SUBLANES)
        s_cmp.append(_dot(kc_ref[0, g], qts[k]) + gcmp_ref[g, pl.ds(y0, ncp), :])
    for k, (t, g) in enumerate(chains):
        start = pl.multiple_of(first[t] * WIN_KEYS, WIN_KEYS)
        rel = pl.multiple_of(jnp.maximum(back - tile[t], 0) * WIN_KEYS, WIN_KEYS)
        s_win.append(_dot(kw_ref[0, g, pl.ds(start, n_win), :], qts[k]) + gwin_ref[g, pl.ds(rel, n_win), :])

    tq = lax.broadcasted_iota(jnp.int32, (1, QL), 1) & (Q_TILE - 1)
    o_cmp, imp = [], []
    for k, (t, g) in enumerate(chains):
        e = jnp.exp2(s_cmp[k] - jnp.max(s_cmp[k], axis=0, keepdims=True))
        has_cmp = (tile[t] * Q_TILE + tq >= CMP_LEN - 1).astype(F32)
        p = e / jnp.sum(e, axis=0, keepdims=True) * has_cmp
        o_cmp.append(_dot(vct_ref[0, g], p.astype(BF16)))
        p_heads = p[:, 0:Q_TILE]
        for r in range(1, C_HPG):
            p_heads = p_heads + p[:, r * Q_TILE:(r + 1) * Q_TILE]
        p_hi = p_heads.astype(BF16)
        p_lo = (p_heads - p_hi.astype(F32)).astype(BF16)
        imp.append(_dot(ovt_ref[...], p_hi) + _dot(ovt_ref[...], p_lo))

    o_win = []
    for k, (t, g) in enumerate(chains):
        e = jnp.exp2(s_win[k] - jnp.max(s_win[k], axis=0, keepdims=True)).astype(BF16)
        acc = jnp.zeros((V_ROWS, QL), F32)
        for i in range(n_win // WIN_KEYS):
            acc = acc + _dot(vwt_ref[0, g, first[t] + i], e[i * WIN_KEYS:(i + 1) * WIN_KEYS])
        o_win.append(_normalized(acc))

    q_aug = _with_mask_rows(qts, [tile[t] for t, g in chains], imp)
    halves = range(SEL_KEYS // SEL_HALF)
    init = (jnp.full((1, QL), NEG, F32), jnp.zeros((V_ROWS, QL), F32))

    def sel_body(i, carry, far):
        off = pl.multiple_of(i * SEL_KEYS, SEL_KEYS)
        units = [(h, k) for h in halves for k in range(len(chains))]

        def logits_of(h, k):
            t, g = chains[k]
            logits = _dot(ks_ref[0, g, pl.ds(off + h * SEL_HALF, SEL_HALF), :], q_aug[k])
            if not far:
                x0 = (n_tiles - 1 - tile[t]) * Q_TILE
                logits = logits + gsel_ref[g, pl.ds(pl.multiple_of(x0 + off + h * SEL_HALF, Q_TILE), SEL_HALF), :]
            return logits

        carry = list(carry)
        s = [logits_of(*u) for u in units[:SEL_AHEAD]]
        for n, (h, k) in enumerate(units):
            if n + SEL_AHEAD < len(units):
                s.append(logits_of(*units[n + SEL_AHEAD]))
            t, g = chains[k]
            v_t = vst_ref[0, g, i][:, h * SEL_HALF:(h + 1) * SEL_HALF]
            carry[k] = _softmax_step(carry[k], s[n], v_t)
        return tuple(carry)

    n_far = jnp.maximum(tile[0] * Q_TILE - MAX_DISTANCE + 1, 0) // SEL_KEYS
    n_sel_steps = (tile[-1] * Q_TILE + Q_TILE + SEL_KEYS - 1) // SEL_KEYS
    sel = lax.fori_loop(0, n_far, functools.partial(sel_body, far=True), (init,) * len(chains))
    sel = tuple((m + gsel_ref[g, 0:1, :], acc) for (m, acc), (t, g) in zip(sel, chains))
    sel = lax.fori_loop(n_far, n_sel_steps, functools.partial(sel_body, far=False), sel)

    for t in range(TILES_PER_STEP):
        outs = []
        for k, (tk, g) in enumerate(chains):
            if tk != t:
                continue
            o_sel = _normalized(sel[k][1])
            tok = slice(t * Q_TILE, (t + 1) * Q_TILE)
            cols = []
            for r in range(C_HPG):
                ln = slice(r * Q_TILE, (r + 1) * Q_TILE)
                cols.append(gate_ref[0, 0, g, r:r + 1, tok] * o_cmp[k][:, ln]
                            + gate_ref[0, 1, g, r:r + 1, tok] * o_sel[:, ln]
                            + gate_ref[0, 2, g, r:r + 1, tok] * o_win[k][:, ln])
            outs += [jnp.concatenate(cols[2 * j:2 * j + 2], axis=0).T for j in range(C_HPG // 2)]
        o_ref[0, t * Q_TILE:(t + 1) * Q_TILE, :] = jnp.concatenate(outs, axis=1).astype(o_ref.dtype)


def _with_mask_rows(qts, tiles, imps):
    n_groups = SEL_SLOTS // SUBLANES
    j = lax.broadcasted_iota(jnp.int32, (SEL_SLOTS, Q_TILE), 0)
    half = lax.shift_right_logical(lax.broadcasted_iota(jnp.int32, (SEL_SLOTS, Q_TILE), 1), SEL_SHIFT)
    j_rows = j[:SUBLANES]
    scores = []
    for c, imp in zip(tiles, imps):
        blk = c * (Q_TILE // SEL_LEN) + half
        forced = (j == 0) | (j == blk) | (j == blk - 1)
        scores.append(jnp.where(j <= blk, jnp.where(forced, FORCE_SCORE, imp), -1.0))
    rows = [[sc[k * SUBLANES:(k + 1) * SUBLANES] for k in range(n_groups)] for sc in scores]

    def add_pair(ranks, jg, k):
        for ci, sc in enumerate(scores):
            acc = ranks[ci][k]
            for jp in range(jg * SUBLANES, (jg + 1) * SUBLANES):
                other = sc[jp:jp + 1, :]
                if k > jg:
                    beats = jnp.where(other >= rows[ci][k], 1, 0)
                elif k < jg:
                    beats = jnp.where(other > rows[ci][k], 1, 0)
                else:
                    beats = jnp.where(j_rows > jp - k * SUBLANES, jnp.where(other >= rows[ci][k], 1, 0),
                                      jnp.where(other > rows[ci][k], 1, 0))
                acc = acc + beats
            ranks[ci][k] = acc

    last_blk = tiles[-1] * (Q_TILE // SEL_LEN) + Q_TILE // SEL_LEN - 1
    ranks = [[jnp.zeros((SUBLANES, Q_TILE), jnp.int32) for _ in range(n_groups)] for _ in scores]
    for m in range(n_groups):
        def shell(ranks, m=m):
            ranks = [list(r) for r in ranks]
            for k in range(m + 1):
                add_pair(ranks, m, k)
            for jg in range(m):
                add_pair(ranks, jg, m)
            return ranks
        ranks = shell(ranks) if m == 0 else lax.cond(m * SUBLANES <= last_blk, shell, lambda r: r, ranks)

    out = []
    for qt, sc, rk in zip(qts, scores, ranks):
        rank = jnp.concatenate(rk, axis=0)
        mask_rows = jnp.where((rank < SEL_TOP) & (sc >= 0.0), 0.0, NEG).astype(BF16)
        out.append(jnp.concatenate([qt, jnp.concatenate([mask_rows] * C_HPG, axis=1)], axis=0))
    return out


def _nsa(qt, kc, vct, ks, vst, kw, vwt, gates, gsel, gwin, gcmp, ovt):
    b, G, n_tiles = qt.shape[:3]
    s = n_tiles * Q_TILE
    per_b = lambda a: pl.BlockSpec((1,) + a.shape[1:], lambda bi, c: (bi,) + (0,) * (a.ndim - 1),
                                   pipeline_mode=pl.Buffered(1))
    return pl.pallas_call(
        functools.partial(_nsa_kernel, n_tiles=n_tiles),
        grid=(b, n_tiles // TILES_PER_STEP),
        in_specs=[pl.BlockSpec((1, G, TILES_PER_STEP, HEAD_DIM, QL), lambda bi, c: (bi, 0, c, 0, 0)),
                  per_b(kc), per_b(vct), per_b(ks), per_b(vst), per_b(kw), per_b(vwt),
                  pl.BlockSpec((1, 3, G, C_HPG, TILES_PER_STEP * Q_TILE), lambda bi, c: (bi, 0, 0, 0, c)),
                  _resident(gsel.shape), _resident(gwin.shape), _resident(gcmp.shape),
                  _resident(ovt.shape)],
        out_specs=pl.BlockSpec((1, TILES_PER_STEP * Q_TILE, C_WIDTH), lambda bi, c: (bi, c, 0)),
        out_shape=jax.ShapeDtypeStruct((b, s, C_WIDTH), BF16),
        compiler_params=_cparams(2),
        name="nsa",
    )(qt, kc, vct, ks, vst, kw, vwt, gates, gsel, gwin, gcmp, ovt)


def _tail_kernel(h_ref, ya_ref, yb_ref, yc_ref, p_ref, norm_ref, woa_ref, wob_ref, woc_ref,
                 wg_ref, wu_ref, wd_ref, wpg_ref, wpp_ref, o_ref, acc_ref):
    subs = _sub_tiles(h_ref.shape[0])
    mix = [_dot(ya_ref[r, :], woa_ref[...]) + _dot(yb_ref[r, :], wob_ref[...]) + _dot(yc_ref[r, :], woc_ref[...])
           for r in subs]
    emb = [_dot(p_ref[r, :].astype(BF16), wpp_ref[...]) for r in subs]
    h = [h_ref[r, :] + _rms(m, norm_ref[3:4, :]) for r, m in zip(subs, mix)]
    h = _ffn_body(h, norm_ref[4:5, :], norm_ref[5:6, :], wg_ref, wu_ref, wd_ref, acc_ref)
    gate = [_dot(_rms(x, norm_ref[6:7, :]).astype(BF16), wpg_ref[...]) for x in h]
    for r, x, g, e in zip(subs, h, gate, emb):
        o_ref[r, :] = x + _rms(_sigmoid(g) * e, norm_ref[7:8, :])


def _tail(h, ya, yb, yc, p, norms, ffn_w, layer, w_out, w_gate, w_proj):
    n, d = h.shape
    dp = p.shape[-1]
    w = w_out.astype(BF16)
    woa, wob, woc = w[:A_WIDTH], w[A_WIDTH:A_WIDTH + B_WIDTH], w[A_WIDTH + B_WIDTH:]
    tile = lambda width: pl.BlockSpec((TOKEN_TILE, width), lambda i: (i, 0))
    return pl.pallas_call(
        _tail_kernel,
        grid=(n // TOKEN_TILE,),
        in_specs=[tile(d), tile(A_WIDTH), tile(B_WIDTH), tile(C_WIDTH),
                  pl.BlockSpec((None, TOKEN_TILE, dp), lambda i: (layer, i, 0)),
                  _resident(norms.shape), _resident(woa.shape), _resident(wob.shape), _resident(woc.shape),
                  *[_ffn_weight_spec(wt, layer, 1) for wt in ffn_w],
                  _resident((d, d)), _resident((dp, d))],
        out_specs=tile(d),
        out_shape=jax.ShapeDtypeStruct((n, d), F32),
        scratch_shapes=[pltpu.VMEM((TOKEN_TILE, d), F32)],
        compiler_params=_cparams(1),
        name="tail",
    )(h, ya, yb, yc, p, norms, woa, wob, woc, *ffn_w, w_gate.astype(BF16), w_proj.astype(BF16))


def _overlap_t(s):
    ncp = s // CMP_STRIDE
    n_cmp = (s - CMP_LEN) // CMP_STRIDE + 1
    cs = jnp.arange(ncp) * CMP_STRIDE
    ss = jnp.arange(s // SEL_LEN) * SEL_LEN
    ov = jnp.clip(jnp.minimum(cs[None] + CMP_LEN, ss[:, None] + SEL_LEN)
                  - jnp.maximum(cs[None], ss[:, None]), 0, None).astype(F32) / CMP_LEN
    ov = jnp.where(jnp.arange(ncp)[None] < n_cmp, ov, 0.0).astype(BF16)
    return jnp.pad(ov, ((0, SEL_SLOTS - s // SEL_LEN), (0, 0)))


def kernel(x, p, rel_bias, norm_g, ffn_w_gate, ffn_w_up, ffn_w_down, w_in, w_out, sgu_norm_g, sgu_w, sgu_b,
           conv_w, conv_b, lru_wa, lru_ba, lru_wx, lru_bx, lru_lambda, cmp_pos, cmp_w1, cmp_b1, cmp_w2,
           cmp_b2, ple_w_gate, ple_w_proj):
    b, s, d = x.shape
    depth = norm_g.shape[0]
    assert s % TOKEN_TILE == 0 and s % SEL_KEYS == 0
    assert s % (TILES_PER_STEP * Q_TILE) == 0
    assert s >= WINDOW + Q_TILE and SEL_TOP <= s // SEL_LEN <= SEL_SLOTS
    n_tiles = s // Q_TILE

    rbx = jnp.repeat(rel_bias.reshape(N_BUCKETS, C_KV_GROUPS, C_HPG).transpose(1, 0, 2), Q_TILE, axis=2)
    no_limit = 1 << 30
    gsel = _bias_table(rbx, s + SEL_KEYS, 1, s - Q_TILE, no_limit)
    gwin = _bias_table(rbx, 2 * WINDOW + Q_TILE, 1, WINDOW, WINDOW)
    per_tile = Q_TILE // CMP_STRIDE
    gcmp = _bias_table(rbx, per_tile * (n_tiles - 1) + s // CMP_STRIDE, CMP_STRIDE,
                       CMP_STRIDE * per_tile * (n_tiles - 1) - (CMP_LEN - 1), no_limit)
    ovt = _overlap_t(s)

    ffn_w = (ffn_w_gate.astype(BF16), ffn_w_up.astype(BF16), ffn_w_down.astype(BF16))
    h = x
    flat = lambda a: a.reshape(b * s, -1)
    for i in range(depth):
        h, ya, yb, qt, raw, ks, kw, vst, vwt, gates = _head(
            h, norm_g[i], ffn_w, i, w_in[i], sgu_norm_g[i], sgu_w[i], sgu_b[i],
            (conv_w[i], conv_b[i], lru_wa[i], lru_ba[i], lru_wx[i], lru_bx[i], lru_lambda[i]))
        kc, kct = _compress(raw, cmp_pos[i], cmp_w1[i], cmp_b1[i], cmp_w2[i], cmp_b2[i])
        yc = _nsa(qt, kc[0], kct[1], ks, vst, kw, vwt, gates, gsel, gwin, gcmp, ovt)
        h = _tail(flat(h), flat(ya), flat(yb), flat(yc), p.reshape(depth, b * s, -1), norm_g[i], ffn_w, i,
                  w_out[i], ple_w_gate[i], ple_w_proj[i]).reshape(b, s, d)
    return h
```

```python
import functools
import math

import jax
import jax.numpy as jnp
from jax import lax
from jax.experimental import pallas as pl
from jax.experimental.pallas import tpu as pltpu

F32 = jnp.float32
BF16 = jnp.bfloat16

RMS_EPS = 1e-6
A_GROUPS = 4
A_WIDTH = 256
A_CHUNK = 128
B_WIDTH = 256
CONV_W = 4
LRU_C = 8.0
C_HEADS = 8
C_KV_GROUPS = 2
C_HPG = C_HEADS // C_KV_GROUPS
HEAD_DIM = 64
C_WIDTH = C_HEADS * HEAD_DIM
KV_W = C_KV_GROUPS * HEAD_DIM
CMP_LEN = 32
CMP_STRIDE = 16
SEL_LEN = 64
SEL_SHIFT = 6
SEL_SLOTS = 64
SEL_TOP = 16
WINDOW = 512
FORCE_SCORE = 1e4
NEG = -1e30
N_BUCKETS = 32
MAX_DISTANCE = 1024

LANES = 128
TOKEN_TILE = 512
SUB_TILES = 2
FFN_CHUNK = 256
SUBLANES = 8
Q_TILE = 128
TILES_PER_STEP = 4
SEL_KEYS = 512
SEL_HALF = 256
SEL_AHEAD = 2
WIN_KEYS = 128
QL = C_HPG * Q_TILE
V_PAD_ROWS = 16
V_ROWS = HEAD_DIM + V_PAD_ROWS
LOG2E = math.log2(math.e)
TABLE_ROWS = 128
VMEM_LIMIT = 56 * 1024 * 1024


def _cparams(n_axes):
    return pltpu.CompilerParams(dimension_semantics=("arbitrary",) * n_axes,
                                vmem_limit_bytes=VMEM_LIMIT)


def _resident(shape):
    nd = len(shape)
    return pl.BlockSpec(shape, lambda *_: (0,) * nd, pipeline_mode=pl.Buffered(1))


def _rms(x, g):
    return x * lax.rsqrt(jnp.mean(x * x, axis=-1, keepdims=True) + RMS_EPS) * g


def _sigmoid(x):
    return 1.0 / (1.0 + jnp.exp(-x))


def _dot(a, b):
    return jnp.dot(a, b, preferred_element_type=F32)


def _bias_table_kernel(rbx_ref, o_ref, *, stride, offset, dmax):
    i = pl.program_id(1)
    shape = (TABLE_ROWS, QL)
    d_hi = (Q_TILE - 1) - stride * (i * TABLE_ROWS) + offset
    d_lo = -stride * (i * TABLE_ROWS + TABLE_ROWS - 1) + offset
    masked = (d_hi < 0) | (d_lo >= dmax)
    far = (d_lo >= MAX_DISTANCE) & (d_hi < dmax)

    @pl.when(masked)
    def _():
        o_ref[0] = jnp.full(shape, NEG, F32)

    @pl.when(far)
    def _():
        o_ref[0] = jnp.broadcast_to(rbx_ref[0, N_BUCKETS - 1:N_BUCKETS, :] * LOG2E, shape)

    @pl.when(jnp.logical_not(masked | far))
    def _():
        x = lax.broadcasted_iota(jnp.int32, shape, 0) + i * TABLE_ROWS
        t = lax.broadcasted_iota(jnp.int32, shape, 1) & (Q_TILE - 1)
        d = t - stride * x + offset
        n = jnp.maximum(d, 0)
        max_exact = N_BUCKETS // 2
        nf = jnp.maximum(n, max_exact).astype(F32)
        large = max_exact + (jnp.log(nf / max_exact) / math.log(MAX_DISTANCE / max_exact)
                             * (N_BUCKETS - max_exact)).astype(jnp.int32)
        large = jnp.minimum(large, N_BUCKETS - 1)
        bucket = jnp.where(n < max_exact, n, large)
        acc = jnp.zeros(shape, F32)
        for k in range(N_BUCKETS):
            acc = jnp.where(bucket == k, rbx_ref[0, k:k + 1, :], acc)
        o_ref[0] = jnp.where((d >= 0) & (d < dmax), acc * LOG2E, NEG)


def _bias_table(rbx, rows, stride, offset, dmax):
    rows_p = -(-rows // TABLE_ROWS) * TABLE_ROWS
    return pl.pallas_call(
        functools.partial(_bias_table_kernel, stride=stride, offset=offset, dmax=dmax),
        grid=(C_KV_GROUPS, rows_p // TABLE_ROWS),
        in_specs=[pl.BlockSpec((1, N_BUCKETS, QL), lambda g, i: (g, 0, 0))],
        out_specs=pl.BlockSpec((1, TABLE_ROWS, QL), lambda g, i: (g, i, 0)),
        out_shape=jax.ShapeDtypeStruct((C_KV_GROUPS, rows_p, QL), F32),
        compiler_params=_cparams(2),
        name="bias_table",
    )(rbx)


def _sub_tiles(n_rows):
    rows = n_rows // SUB_TILES
    return [slice(k * rows, (k + 1) * rows) for k in range(SUB_TILES)]


def _ffn_body(xs, g_pre, g_post, wg_ref, wu_ref, wd_ref, acc_ref):
    subs = _sub_tiles(acc_ref.shape[0])
    xn = [_rms(x, g_pre).astype(BF16) for x in xs]
    nch = wg_ref.shape[1] // FFN_CHUNK
    cols = lambda j: slice(j * FFN_CHUNK, (j + 1) * FFN_CHUNK)
    gate_up = [(_dot(x, wg_ref[:, cols(0)]), _dot(x, wu_ref[:, cols(0)])) for x in xn]
    for j in range(nch):
        if j + 1 < nch:
            nxt = [(_dot(x, wg_ref[:, cols(j + 1)]), _dot(x, wu_ref[:, cols(j + 1)])) for x in xn]
        for rows, (gate, up) in zip(subs, gate_up):
            hid = (gate * _sigmoid(gate) * up).astype(BF16)
            down = _dot(hid, wd_ref[cols(j), :])
            if j == 0:
                acc_ref[rows, :] = down
            else:
                acc_ref[rows, :] += down
        gate_up = nxt
    return [x + 0.5 * _rms(acc_ref[rows, :], g_post) for x, rows in zip(xs, subs)]


def _ffn_weight_spec(w, layer, which):
    return pl.BlockSpec((None, None) + w.shape[2:], lambda *_: (layer, which, 0, 0), pipeline_mode=pl.Buffered(1))


def _head_kernel(h_ref, norm_ref, wg_ref, wu_ref, wd_ref, wa_ref, wb_ref, wq_ref, wkv_ref, wgt_ref,
                 sgn_ref, sgw_ref, sgb_ref, cw_ref, cb_ref, lwa_ref, lba_ref, lwx_ref, lbx_ref, lam_ref,
                 h_out_ref, ya_ref, yb_ref, qt_ref, raw_ref, ks_ref, kw_ref, vst_ref, vwt_ref, gt_ref,
                 acc_ref, tail_ref, state_ref, a_ref, b_ref):
    @pl.when(pl.program_id(1) == 0)
    def _():
        tail_ref[...] = jnp.zeros_like(tail_ref)
        state_ref[...] = jnp.zeros_like(state_ref)

    h = _ffn_body([h_ref[0, r, :] for r in _sub_tiles(TOKEN_TILE)], norm_ref[0:1, :], norm_ref[1:2, :],
                  wg_ref, wu_ref, wd_ref, acc_ref)
    h = jnp.concatenate(h, axis=0)
    h_out_ref[0] = h
    xn = _rms(h, norm_ref[2:3, :]).astype(BF16)

    zb = _dot(xn, wb_ref[...])
    za = _dot(xn, wa_ref[...])
    zq = _dot(xn, wq_ref[...])
    zkv = _dot(xn, wkv_ref[...])
    zg = _dot(xn, wgt_ref[...])

    gate_b = _rglru_coefficients(zb, cw_ref, cb_ref, lwa_ref, lba_ref, lwx_ref, lbx_ref, lam_ref,
                                 tail_ref, state_ref, a_ref, b_ref)

    u = jax.nn.gelu(za[:, :A_WIDTH])
    v = _rms(jax.nn.gelu(za[:, A_WIDTH:]), sgn_ref[...]).astype(BF16)
    row = lax.broadcasted_iota(jnp.int32, (A_CHUNK, A_CHUNK), 0)
    col = lax.broadcasted_iota(jnp.int32, (A_CHUNK, A_CHUNK), 1)
    lane_group = lax.shift_right_logical(lax.broadcasted_iota(jnp.int32, (A_CHUNK, A_WIDTH), 1),
                                          (A_WIDTH // A_GROUPS).bit_length() - 1)
    w_tril = [jnp.where(row >= col, sgw_ref[g], 0.0).astype(BF16) for g in range(A_GROUPS)]
    for c in range(TOKEN_TILE // A_CHUNK):
        rows = slice(c * A_CHUNK, (c + 1) * A_CHUNK)
        mixed = jnp.zeros((A_CHUNK, A_WIDTH), F32)
        for g in range(A_GROUPS):
            mixed = jnp.where(lane_group == g, _dot(w_tril[g], v[rows]), mixed)
        ya_ref[0, rows, :] = (u[rows] * (mixed + sgb_ref[...])).astype(ya_ref.dtype)

    zq_t = (zq * (HEAD_DIM ** -0.5 * LOG2E)).T
    for g in range(C_KV_GROUPS):
        for c in range(TOKEN_TILE // Q_TILE):
            parts = []
            for r in range(C_HPG):
                base = (g * C_HPG + r) * HEAD_DIM
                parts.append(zq_t[base:base + HEAD_DIM, c * Q_TILE:(c + 1) * Q_TILE])
            qt_ref[0, g, c] = jnp.concatenate(parts, axis=1).astype(qt_ref.dtype)

    vs_t = zkv[:, 3 * KV_W:4 * KV_W].T
    vw_t = zkv[:, 5 * KV_W:6 * KV_W].T
    key_blk = lax.shift_right_logical(
        lax.broadcasted_iota(jnp.int32, (TOKEN_TILE, SEL_SLOTS), 0) + pl.program_id(1) * TOKEN_TILE, SEL_SHIFT)
    blk_onehot = jnp.where(key_blk == lax.broadcasted_iota(jnp.int32, (TOKEN_TILE, SEL_SLOTS), 1), 1.0, 0.0)
    ones_rows = jnp.where(lax.broadcasted_iota(jnp.int32, (V_PAD_ROWS, SEL_KEYS), 0) == 0, 1.0, 0.0)
    raw_ref[0, 0] = zkv[:, :KV_W]
    raw_ref[1, 0] = zkv[:, KV_W:2 * KV_W]
    for g in range(C_KV_GROUPS):
        lo, hi = g * HEAD_DIM, (g + 1) * HEAD_DIM
        ks_ref[0, g] = jnp.concatenate([zkv[:, 2 * KV_W + lo:2 * KV_W + hi], blk_onehot],
                                       axis=1).astype(ks_ref.dtype)
        kw_ref[0, g] = zkv[:, 4 * KV_W + lo:4 * KV_W + hi].astype(kw_ref.dtype)
        for c in range(TOKEN_TILE // SEL_KEYS):
            vst_ref[0, g, c] = jnp.concatenate(
                [vs_t[lo:hi, c * SEL_KEYS:(c + 1) * SEL_KEYS], ones_rows[:, :SEL_KEYS]], axis=0).astype(vst_ref.dtype)
        for c in range(TOKEN_TILE // WIN_KEYS):
            vwt_ref[0, g, c] = jnp.concatenate(
                [vw_t[lo:hi, c * WIN_KEYS:(c + 1) * WIN_KEYS], ones_rows[:, :WIN_KEYS]], axis=0).astype(vwt_ref.dtype)

    sg_t = _sigmoid(zg).T
    for br in range(3):
        for g in range(C_KV_GROUPS):
            base = br * C_HEADS + g * C_HPG
            gt_ref[0, br, g] = sg_t[base:base + C_HPG, :]

    yb_ref[0] = (_rglru_scan(state_ref, a_ref, b_ref) * jax.nn.gelu(gate_b)).astype(yb_ref.dtype)


def _head(h, norms, ffn_w, layer, w_in, sgu_norm_g, sgu_w, sgu_b, lru):
    b, s, d = h.shape
    assert ffn_w[0].shape[-1] % FFN_CHUNK == 0
    wb16 = w_in.astype(BF16)
    o = 0
    wa = wb16[:, o:o + 2 * A_WIDTH]; o += 2 * A_WIDTH
    wb = wb16[:, o:o + 2 * B_WIDTH]; o += 2 * B_WIDTH
    wq = wb16[:, o:o + C_WIDTH]; o += C_WIDTH
    wkv = wb16[:, o:o + 6 * KV_W]; o += 6 * KV_W
    wgt = jnp.pad(wb16[:, o:o + 3 * C_HEADS], ((0, 0), (0, LANES - 3 * C_HEADS)))
    sgb = jnp.repeat(sgu_b.T, A_WIDTH // A_GROUPS, axis=1)
    nt = s // TOKEN_TILE
    grid = (b, nt)
    G = C_KV_GROUPS
    out_shape = [
        jax.ShapeDtypeStruct((b, s, d), F32),
        jax.ShapeDtypeStruct((b, s, A_WIDTH), BF16),
        jax.ShapeDtypeStruct((b, s, B_WIDTH), BF16),
        jax.ShapeDtypeStruct((b, G, s // Q_TILE, HEAD_DIM, QL), BF16),
        jax.ShapeDtypeStruct((2, b, s, KV_W), F32),
        jax.ShapeDtypeStruct((b, G, s, HEAD_DIM + SEL_SLOTS), BF16),
        jax.ShapeDtypeStruct((b, G, s, HEAD_DIM), BF16),
        jax.ShapeDtypeStruct((b, G, s // SEL_KEYS, V_ROWS, SEL_KEYS), BF16),
        jax.ShapeDtypeStruct((b, G, s // WIN_KEYS, V_ROWS, WIN_KEYS), BF16),
        jax.ShapeDtypeStruct((b, 3, G, C_HPG, s), F32),
    ]
    out_specs = [
        pl.BlockSpec((1, TOKEN_TILE, d), lambda bi, i: (bi, i, 0)),
        pl.BlockSpec((1, TOKEN_TILE, A_WIDTH), lambda bi, i: (bi, i, 0)),
        pl.BlockSpec((1, TOKEN_TILE, B_WIDTH), lambda bi, i: (bi, i, 0)),
        pl.BlockSpec((1, G, TOKEN_TILE // Q_TILE, HEAD_DIM, QL), lambda bi, i: (bi, 0, i, 0, 0)),
        pl.BlockSpec((2, 1, TOKEN_TILE, KV_W), lambda bi, i: (0, bi, i, 0)),
        pl.BlockSpec((1, G, TOKEN_TILE, HEAD_DIM + SEL_SLOTS), lambda bi, i: (bi, 0, i, 0)),
        pl.BlockSpec((1, G, TOKEN_TILE, HEAD_DIM), lambda bi, i: (bi, 0, i, 0)),
        pl.BlockSpec((1, G, TOKEN_TILE // SEL_KEYS, V_ROWS, SEL_KEYS), lambda bi, i: (bi, 0, i, 0, 0)),
        pl.BlockSpec((1, G, TOKEN_TILE // WIN_KEYS, V_ROWS, WIN_KEYS), lambda bi, i: (bi, 0, i, 0, 0)),
        pl.BlockSpec((1, 3, G, C_HPG, TOKEN_TILE), lambda bi, i: (bi, 0, 0, 0, i)),
    ]
    in_specs = [
        pl.BlockSpec((1, TOKEN_TILE, d), lambda bi, i: (bi, i, 0)),
        _resident(norms.shape), *[_ffn_weight_spec(w, layer, 0) for w in ffn_w],
        _resident(wa.shape), _resident(wb.shape), _resident(wq.shape),
        _resident(wkv.shape), _resident(wgt.shape), _resident((1, A_WIDTH)),
        _resident(sgu_w.shape), _resident(sgb.shape),
        _resident((CONV_W, B_WIDTH)), _resident((1, B_WIDTH)), _resident((B_WIDTH, B_WIDTH)),
        _resident((1, B_WIDTH)), _resident((B_WIDTH, B_WIDTH)), _resident((1, B_WIDTH)), _resident((1, B_WIDTH)),
    ]
    conv_w, conv_b, lru_wa, lru_ba, lru_wx, lru_bx, lru_lambda = lru
    row = lambda v: v.reshape(1, B_WIDTH)
    return pl.pallas_call(
        _head_kernel, grid=grid, in_specs=in_specs, out_specs=out_specs, out_shape=out_shape,
        scratch_shapes=[pltpu.VMEM((TOKEN_TILE, d), F32),
                        pltpu.VMEM((SUBLANES, B_WIDTH), F32), pltpu.VMEM((SUBLANES, B_WIDTH), F32),
                        pltpu.VMEM((TOKEN_TILE, B_WIDTH), F32), pltpu.VMEM((TOKEN_TILE, B_WIDTH), F32)],
        compiler_params=_cparams(2), name="head",
    )(h, norms, *ffn_w, wa, wb, wq, wkv, wgt, sgu_norm_g.reshape(1, A_WIDTH), sgu_w, sgb,
      conv_w, row(conv_b), _block_diag(lru_wa).astype(BF16), row(lru_ba),
      _block_diag(lru_wx).astype(BF16), row(lru_bx), row(lru_lambda))


def _rglru_coefficients(zb, cw_ref, cb_ref, wa_ref, ba_ref, wx_ref, bxb_ref, lam_ref,
                        tail_ref, state_ref, a_ref, b_ref):
    xb = zb[:, :B_WIDTH]
    ext = jnp.concatenate([tail_ref[...], xb], axis=0)
    xc = cb_ref[...] + xb * cw_ref[CONV_W - 1:CONV_W, :]
    for k in range(CONV_W - 1):
        shift = CONV_W - 1 - k
        xc = xc + ext[SUBLANES - shift:SUBLANES - shift + TOKEN_TILE] * cw_ref[k:k + 1, :]
    tail_ref[...] = xb[TOKEN_TILE - SUBLANES:]

    xcb = xc.astype(BF16)
    r = _sigmoid(_dot(xcb, wa_ref[...]) + ba_ref[...])
    i = _sigmoid(_dot(xcb, wx_ref[...]) + bxb_ref[...])
    z = -lam_ref[...]
    e = jnp.exp(-jnp.abs(z))
    softplus = jnp.maximum(z, 0.0) + jnp.log1p(e)
    log_a = -LRU_C * r * softplus
    a = jnp.exp(log_a)
    b = jnp.sqrt(jnp.tanh(-log_a) * (a * a + 1.0)) * (i * xc)

    row = lax.broadcasted_iota(jnp.int32, a.shape, 0) & (SUBLANES - 1)
    for dist in (1, 2, 4):
        a_prev = jnp.where(row >= dist, pltpu.roll(a, dist, 0), 1.0)
        b_prev = jnp.where(row >= dist, pltpu.roll(b, dist, 0), 0.0)
        b = a * b_prev + b
        a = a * a_prev
    a_ref[...] = a
    b_ref[...] = b
    return zb[:, B_WIDTH:]


def _rglru_scan(state_ref, a_ref, b_ref):
    def body(k, h):
        off = pl.multiple_of(k * SUBLANES, SUBLANES)
        rows = pl.ds(off, SUBLANES)
        hs = b_ref[rows, :] + a_ref[rows, :] * h
        b_ref[rows, :] = hs
        return jnp.broadcast_to(hs[SUBLANES - 1:SUBLANES, :], hs.shape)

    state_ref[...] = lax.fori_loop(0, TOKEN_TILE // SUBLANES, body, state_ref[...])
    return b_ref[...]


def _block_diag(w):
    g, n, _ = w.shape
    out = jnp.zeros((g * n, g * n), w.dtype)
    for k in range(g):
        out = out.at[k * n:(k + 1) * n, k * n:(k + 1) * n].set(w[k])
    return out


def _compress_kernel(raw_ref, pos_ref, w1_ref, b1_ref, w2_ref, b2_ref, w2t_ref, b2c_ref, kc_ref, kct_ref):
    ncp = raw_ref.shape[2] // CMP_STRIDE
    groups = range(C_KV_GROUPS)
    hid_dim = w1_ref.shape[-1]
    u = [jnp.zeros((ncp, hid_dim), F32) for _ in groups]
    v = [jnp.zeros((ncp, hid_dim), F32) for _ in groups]
    for l in range(CMP_STRIDE):
        x = raw_ref[0, 0, pl.ds(l, ncp, stride=CMP_STRIDE), :]
        top = (x + pos_ref[0, l:l + 1, :]).astype(BF16)
        bot = (x + pos_ref[0, CMP_STRIDE + l:CMP_STRIDE + l + 1, :]).astype(BF16)
        for g in groups:
            u[g] = u[g] + _dot(top, w1_ref[0, g, l])
            v[g] = v[g] + _dot(bot, w1_ref[0, g, CMP_STRIDE + l])
    for g in groups:
        hid = jax.nn.gelu(u[g] + pltpu.roll(v[g], ncp - 1, 0) + b1_ref[0]).astype(BF16)
        kc_ref[0, 0, g] = (_dot(hid, w2_ref[0]) + b2_ref[0]).astype(kc_ref.dtype)
        t = lax.dot_general(w2t_ref[0], hid, (((1,), (1,)), ((), ())), preferred_element_type=F32)
        kct_ref[0, 0, g] = (t + b2c_ref[0]).astype(kct_ref.dtype)


def _compress(raw, cmp_pos, cmp_w1, cmp_b1, cmp_w2, cmp_b2):
    _, b, s, _ = raw.shape
    g, hd = C_KV_GROUPS, HEAD_DIM
    ncp = s // CMP_STRIDE
    hid = cmp_w1.shape[-1]
    w1 = cmp_w1.astype(BF16).reshape(2, CMP_LEN, hd, hid)
    zero = jnp.zeros_like(w1)
    w1g = jnp.stack([jnp.concatenate([w1 if k == gi else zero for k in range(g)], axis=2)
                     for gi in range(g)], axis=1)
    pos = jnp.tile(cmp_pos, (1, 1, g))
    sel = lambda *shape: pl.BlockSpec((1,) + shape, lambda kv, bi: (kv,) + (0,) * len(shape))
    return pl.pallas_call(
        _compress_kernel,
        grid=(2, b),
        in_specs=[pl.BlockSpec((1, 1, s, g * hd), lambda kv, bi: (kv, bi, 0, 0)),
                  sel(CMP_LEN, g * hd), sel(g, CMP_LEN, g * hd, hid), sel(1, hid), sel(hid, hd), sel(1, hd),
                  sel(hd, hid), sel(hd, 1)],
        out_specs=[pl.BlockSpec((1, 1, g, ncp, hd), lambda kv, bi: (kv, bi, 0, 0, 0)),
                   pl.BlockSpec((1, 1, g, hd, ncp), lambda kv, bi: (kv, bi, 0, 0, 0))],
        out_shape=[jax.ShapeDtypeStruct((2, b, g, ncp, hd), BF16),
                   jax.ShapeDtypeStruct((2, b, g, hd, ncp), BF16)],
        compiler_params=_cparams(2),
        name="compress",
    )(raw, pos, w1g, cmp_b1.reshape(2, 1, hid), cmp_w2.astype(BF16), cmp_b2.reshape(2, 1, hd),
      cmp_w2.astype(BF16).transpose(0, 2, 1), cmp_b2.reshape(2, hd, 1))


def _softmax_step(carry, s, v_t):
    m, acc = carry
    m_new = jnp.maximum(m, jnp.max(s, axis=0, keepdims=True))
    p = jnp.exp2(s - m_new).astype(BF16)
    acc = acc * jnp.exp2(m - m_new) + _dot(v_t, p)
    return m_new, acc


def _normalized(acc):
    return acc[:HEAD_DIM] / acc[HEAD_DIM:HEAD_DIM + 1]


def _nsa_kernel(qt_ref, kc_ref, vct_ref, ks_ref, vst_ref, kw_ref, vwt_ref, gate_ref,
                gsel_ref, gwin_ref, gcmp_ref, ovt_ref, o_ref, *, n_tiles):
    step = pl.program_id(1)
    ncp = kc_ref.shape[2]
    chains = [(t, g) for t in range(TILES_PER_STEP) for g in range(C_KV_GROUPS)]
    tile = [step * TILES_PER_STEP + t for t in range(TILES_PER_STEP)]
    qts = [qt_ref[0, g, t] for t, g in chains]

    back = WINDOW // WIN_KEYS
    n_win = WINDOW + Q_TILE
    first = [jnp.maximum(c - back, 0) for c in tile]
    s_cmp, s_win = [], []
    for k, (t, g) in enumerate(chains):
        y0 = pl.multiple_of((n_tiles - 1 - tile[t]) * (Q_TILE // CMP_STRIDE), SUBLANES)
        s_cmp.append(_dot(kc_ref[0, g], qts[k]) + gcmp_ref[g, pl.ds(y0, ncp), :])
    for k, (t, g) in enumerate(chains):
        start = pl.multiple_of(first[t] * WIN_KEYS, WIN_KEYS)
        rel = pl.multiple_of(jnp.maximum(back - tile[t], 0) * WIN_KEYS, WIN_KEYS)
        s_win.append(_dot(kw_ref[0, g, pl.ds(start, n_win), :], qts[k]) + gwin_ref[g, pl.ds(rel, n_win), :])

    tq = lax.broadcasted_iota(jnp.int32, (1, QL), 1) & (Q_TILE - 1)
    o_cmp, imp = [], []
    for k, (t, g) in enumerate(chains):
        e = jnp.exp2(s_cmp[k] - jnp.max(s_cmp[k], axis=0, keepdims=True))
        has_cmp = (tile[t] * Q_TILE + tq >= CMP_LEN - 1).astype(F32)
        p = e / jnp.sum(e, axis=0, keepdims=True) * has_cmp
        o_cmp.append(_dot(vct_ref[0, g], p.astype(BF16)))
        p_heads = p[:, 0:Q_TILE]
        for r in range(1, C_HPG):
            p_heads = p_heads + p[:, r * Q_TILE:(r + 1) * Q_TILE]
        p_hi = p_heads.astype(BF16)
        p_lo = (p_heads - p_hi.astype(F32)).astype(BF16)
        imp.append(_dot(ovt_ref[...], p_hi) + _dot(ovt_ref[...], p_lo))

    o_win = []
    for k, (t, g) in enumerate(chains):
        e = jnp.exp2(s_win[k] - jnp.max(s_win[k], axis=0, keepdims=True)).astype(BF16)
        acc = jnp.zeros((V_ROWS, QL), F32)
        for i in range(n_win // WIN_KEYS):
            acc = acc + _dot(vwt_ref[0, g, first[t] + i], e[i * WIN_KEYS:(i + 1) * WIN_KEYS])
        o_win.append(_normalized(acc))

    q_aug = _with_mask_rows(qts, [tile[t] for t, g in chains], imp)
    halves = range(SEL_KEYS // SEL_HALF)
    init = (jnp.full((1, QL), NEG, F32), jnp.zeros((V_ROWS, QL), F32))

    def sel_body(i, carry, far):
        off = pl.multiple_of(i * SEL_KEYS, SEL_KEYS)
        units = [(h, k) for h in halves for k in range(len(chains))]

        def logits_of(h, k):
            t, g = chains[k]
            logits = _dot(ks_ref[0, g, pl.ds(off + h * SEL_HALF, SEL_HALF), :], q_aug[k])
            if not far:
                x0 = (n_tiles - 1 - tile[t]) * Q_TILE
                logits = logits + gsel_ref[g, pl.ds(pl.multiple_of(x0 + off + h * SEL_HALF, Q_TILE), SEL_HALF), :]
            return logits

        carry = list(carry)
        s = [logits_of(*u) for u in units[:SEL_AHEAD]]
        for n, (h, k) in enumerate(units):
            if n + SEL_AHEAD < len(units):
                s.append(logits_of(*units[n + SEL_AHEAD]))
            t, g = chains[k]
            v_t = vst_ref[0, g, i][:, h * SEL_HALF:(h + 1) * SEL_HALF]
            carry[k] = _softmax_step(carry[k], s[n], v_t)
        return tuple(carry)

    n_far = jnp.maximum(tile[0] * Q_TILE - MAX_DISTANCE + 1, 0) // SEL_KEYS
    n_sel_steps = (tile[-1] * Q_TILE + Q_TILE + SEL_KEYS - 1) // SEL_KEYS
    sel = lax.fori_loop(0, n_far, functools.partial(sel_body, far=True), (init,) * len(chains))
    sel = tuple((m + gsel_ref[g, 0:1, :], acc) for (m, acc), (t, g) in zip(sel, chains))
    sel = lax.fori_loop(n_far, n_sel_steps, functools.partial(sel_body, far=False), sel)

    for t in range(TILES_PER_STEP):
        outs = []
        for k, (tk, g) in enumerate(chains):
            if tk != t:
                continue
            o_sel = _normalized(sel[k][1])
            tok = slice(t * Q_TILE, (t + 1) * Q_TILE)
            cols = []
            for r in range(C_HPG):
                ln = slice(r * Q_TILE, (r + 1) * Q_TILE)
                cols.append(gate_ref[0, 0, g, r:r + 1, tok] * o_cmp[k][:, ln]
                            + gate_ref[0, 1, g, r:r + 1, tok] * o_sel[:, ln]
                            + gate_ref[0, 2, g, r:r + 1, tok] * o_win[k][:, ln])
            outs += [jnp.concatenate(cols[2 * j:2 * j + 2], axis=0).T for j in range(C_HPG // 2)]
        o_ref[0, t * Q_TILE:(t + 1) * Q_TILE, :] = jnp.concatenate(outs, axis=1).astype(o_ref.dtype)


def _with_mask_rows(qts, tiles, imps):
    n_groups = SEL_SLOTS // SUBLANES
    j = lax.broadcasted_iota(jnp.int32, (SEL_SLOTS, Q_TILE), 0)
    half = lax.shift_right_logical(lax.broadcasted_iota(jnp.int32, (SEL_SLOTS, Q_TILE), 1), SEL_SHIFT)
    j_rows = j[:SUBLANES]
    scores = []
    for c, imp in zip(tiles, imps):
        blk = c * (Q_TILE // SEL_LEN) + half
        forced = (j == 0) | (j == blk) | (j == blk - 1)
        scores.append(jnp.where(j <= blk, jnp.where(forced, FORCE_SCORE, imp), -1.0))
    rows = [[sc[k * SUBLANES:(k + 1) * SUBLANES] for k in range(n_groups)] for sc in scores]

    def add_pair(ranks, jg, k):
        for ci, sc in enumerate(scores):
            acc = ranks[ci][k]
            for jp in range(jg * SUBLANES, (jg + 1) * SUBLANES):
                other = sc[jp:jp + 1, :]
                if k > jg:
                    beats = jnp.where(other >= rows[ci][k], 1, 0)
                elif k < jg:
                    beats = jnp.where(other > rows[ci][k], 1, 0)
                else:
                    beats = jnp.where(j_rows > jp - k * SUBLANES, jnp.where(other >= rows[ci][k], 1, 0),
                                      jnp.where(other > rows[ci][k], 1, 0))
                acc = acc + beats
            ranks[ci][k] = acc

    last_blk = tiles[-1] * (Q_TILE // SEL_LEN) + Q_TILE // SEL_LEN - 1
    ranks = [[jnp.zeros((SUBLANES, Q_TILE), jnp.int32) for _ in range(n_groups)] for _ in scores]
    for m in range(n_groups):
        def shell(ranks, m=m):
            ranks = [list(r) for r in ranks]
            for k in range(m + 1):
                add_pair(ranks, m, k)
            for jg in range(m):
                add_pair(ranks, jg, m)
            return ranks
        ranks = shell(ranks) if m == 0 else lax.cond(m * SUBLANES <= last_blk, shell, lambda r: r, ranks)

    out = []
    for qt, sc, rk in zip(qts, scores, ranks):
        rank = jnp.concatenate(rk, axis=0)
        mask_rows = jnp.where((rank < SEL_TOP) & (sc >= 0.0), 0.0, NEG).astype(BF16)
        out.append(jnp.concatenate([qt, jnp.concatenate([mask_rows] * C_HPG, axis=1)], axis=0))
    return out


def _nsa(qt, kc, vct, ks, vst, kw, vwt, gates, gsel, gwin, gcmp, ovt):
    b, G, n_tiles = qt.shape[:3]
    s = n_tiles * Q_TILE
    per_b = lambda a: pl.BlockSpec((1,) + a.shape[1:], lambda bi, c: (bi,) + (0,) * (a.ndim - 1),
                                   pipeline_mode=pl.Buffered(1))
    return pl.pallas_call(
        functools.partial(_nsa_kernel, n_tiles=n_tiles),
        grid=(b, n_tiles // TILES_PER_STEP),
        in_specs=[pl.BlockSpec((1, G, TILES_PER_STEP, HEAD_DIM, QL), lambda bi, c: (bi, 0, c, 0, 0)),
                  per_b(kc), per_b(vct), per_b(ks), per_b(vst), per_b(kw), per_b(vwt),
                  pl.BlockSpec((1, 3, G, C_HPG, TILES_PER_STEP * Q_TILE), lambda bi, c: (bi, 0, 0, 0, c)),
                  _resident(gsel.shape), _resident(gwin.shape), _resident(gcmp.shape),
                  _resident(ovt.shape)],
        out_specs=pl.BlockSpec((1, TILES_PER_STEP * Q_TILE, C_WIDTH), lambda bi, c: (bi, c, 0)),
        out_shape=jax.ShapeDtypeStruct((b, s, C_WIDTH), BF16),
        compiler_params=_cparams(2),
        name="nsa",
    )(qt, kc, vct, ks, vst, kw, vwt, gates, gsel, gwin, gcmp, ovt)


def _tail_kernel(h_ref, ya_ref, yb_ref, yc_ref, p_ref, norm_ref, woa_ref, wob_ref, woc_ref,
                 wg_ref, wu_ref, wd_ref, wpg_ref, wpp_ref, o_ref, acc_ref):
    subs = _sub_tiles(h_ref.shape[0])
    mix = [_dot(ya_ref[r, :], woa_ref[...]) + _dot(yb_ref[r, :], wob_ref[...]) + _dot(yc_ref[r, :], woc_ref[...])
           for r in subs]
    emb = [_dot(p_ref[r, :].astype(BF16), wpp_ref[...]) for r in subs]
    h = [h_ref[r, :] + _rms(m, norm_ref[3:4, :]) for r, m in zip(subs, mix)]
    h = _ffn_body(h, norm_ref[4:5, :], norm_ref[5:6, :], wg_ref, wu_ref, wd_ref, acc_ref)
    gate = [_dot(_rms(x, norm_ref[6:7, :]).astype(BF16), wpg_ref[...]) for x in h]
    for r, x, g, e in zip(subs, h, gate, emb):
        o_ref[r, :] = x + _rms(_sigmoid(g) * e, norm_ref[7:8, :])


def _tail(h, ya, yb, yc, p, norms, ffn_w, layer, w_out, w_gate, w_proj):
    n, d = h.shape
    dp = p.shape[-1]
    w = w_out.astype(BF16)
    woa, wob, woc = w[:A_WIDTH], w[A_WIDTH:A_WIDTH + B_WIDTH], w[A_WIDTH + B_WIDTH:]
    tile = lambda width: pl.BlockSpec((TOKEN_TILE, width), lambda i: (i, 0))
    return pl.pallas_call(
        _tail_kernel,
        grid=(n // TOKEN_TILE,),
        in_specs=[tile(d), tile(A_WIDTH), tile(B_WIDTH), tile(C_WIDTH),
                  pl.BlockSpec((None, TOKEN_TILE, dp), lambda i: (layer, i, 0)),
                  _resident(norms.shape), _resident(woa.shape), _resident(wob.shape), _resident(woc.shape),
                  *[_ffn_weight_spec(wt, layer, 1) for wt in ffn_w],
                  _resident((d, d)), _resident((dp, d))],
        out_specs=tile(d),
        out_shape=jax.ShapeDtypeStruct((n, d), F32),
        scratch_shapes=[pltpu.VMEM((TOKEN_TILE, d), F32)],
        compiler_params=_cparams(1),
        name="tail",
    )(h, ya, yb, yc, p, norms, woa, wob, woc, *ffn_w, w_gate.astype(BF16), w_proj.astype(BF16))


def _overlap_t(s):
    ncp = s // CMP_STRIDE
    n_cmp = (s - CMP_LEN) // CMP_STRIDE + 1
    cs = jnp.arange(ncp) * CMP_STRIDE
    ss = jnp.arange(s // SEL_LEN) * SEL_LEN
    ov = jnp.clip(jnp.minimum(cs[None] + CMP_LEN, ss[:, None] + SEL_LEN)
                  - jnp.maximum(cs[None], ss[:, None]), 0, None).astype(F32) / CMP_LEN
    ov = jnp.where(jnp.arange(ncp)[None] < n_cmp, ov, 0.0).astype(BF16)
    return jnp.pad(ov, ((0, SEL_SLOTS - s // SEL_LEN), (0, 0)))


def kernel(x, p, rel_bias, norm_g, ffn_w_gate, ffn_w_up, ffn_w_down, w_in, w_out, sgu_norm_g, sgu_w, sgu_b,
           conv_w, conv_b, lru_wa, lru_ba, lru_wx, lru_bx, lru_lambda, cmp_pos, cmp_w1, cmp_b1, cmp_w2,
           cmp_b2, ple_w_gate, ple_w_proj):
    b, s, d = x.shape
    depth = norm_g.shape[0]
    assert s % TOKEN_TILE == 0 and s % SEL_KEYS == 0
    assert s % (TILES_PER_STEP * Q_TILE) == 0
    assert s >= WINDOW + Q_TILE and SEL_TOP <= s // SEL_LEN <= SEL_SLOTS
    n_tiles = s // Q_TILE

    rbx = jnp.repeat(rel_bias.reshape(N_BUCKETS, C_KV_GROUPS, C_HPG).transpose(1, 0, 2), Q_TILE, axis=2)
    no_limit = 1 << 30
    gsel = _bias_table(rbx, s + SEL_KEYS, 1, s - Q_TILE, no_limit)
    gwin = _bias_table(rbx, 2 * WINDOW + Q_TILE, 1, WINDOW, WINDOW)
    per_tile = Q_TILE // CMP_STRIDE
    gcmp = _bias_table(rbx, per_tile * (n_tiles - 1) + s // CMP_STRIDE, CMP_STRIDE,
                       CMP_STRIDE * per_tile * (n_tiles - 1) - (CMP_LEN - 1), no_limit)
    ovt = _overlap_t(s)

    ffn_w = (ffn_w_gate.astype(BF16), ffn_w_up.astype(BF16), ffn_w_down.astype(BF16))
    h = x
    flat = lambda a: a.reshape(b * s, -1)
    for i in range(depth):
        h, ya, yb, qt, raw, ks, kw, vst, vwt, gates = _head(
            h, norm_g[i], ffn_w, i, w_in[i], sgu_norm_g[i], sgu_w[i], sgu_b[i],
            (conv_w[i], conv_b[i], lru_wa[i], lru_ba[i], lru_wx[i], lru_bx[i], lru_lambda[i]))
        kc, kct = _compress(raw, cmp_pos[i], cmp_w1[i], cmp_b1[i], cmp_w2[i], cmp_b2[i])
        yc = _nsa(qt, kc[0], kct[1], ks, vst, kw, vwt, gates, gsel, gwin, gcmp, ovt)
        h = _tail(flat(h), flat(ya), flat(yb), flat(yc), p.reshape(depth, b * s, -1), norm_g[i], ffn_w, i,
                  w_out[i], ple_w_gate[i], ple_w_proj[i]).reshape(b, s, d)
    return h
```

```python
import functools
import math

import jax
import jax.numpy as jnp
from jax import lax
from jax.experimental import pallas as pl
from jax.experimental.pallas import tpu as pltpu

F32 = jnp.float32
BF16 = jnp.bfloat16

RMS_EPS = 1e-6
A_GROUPS = 4
A_WIDTH = 256
A_CHUNK = 128
B_WIDTH = 256
CONV_W = 4
LRU_C = 8.0
C_HEADS = 8
C_KV_GROUPS = 2
C_HPG = C_HEADS // C_KV_GROUPS
HEAD_DIM = 64
C_WIDTH = C_HEADS * HEAD_DIM
KV_W = C_KV_GROUPS * HEAD_DIM
CMP_LEN = 32
CMP_STRIDE = 16
SEL_LEN = 64
SEL_SHIFT = 6
SEL_SLOTS = 64
SEL_TOP = 16
WINDOW = 512
FORCE_SCORE = 1e4
NEG = -1e30
N_BUCKETS = 32
MAX_DISTANCE = 1024

LANES = 128
TOKEN_TILE = 512
SUB_TILES = 2
FFN_CHUNK = 256
SUBLANES = 8
Q_TILE = 128
TILES_PER_STEP = 4
SEL_KEYS = 512
SEL_HALF = 256
SEL_AHEAD = 2
WIN_KEYS = 128
QL = C_HPG * Q_TILE
V_PAD_ROWS = 16
V_ROWS = HEAD_DIM + V_PAD_ROWS
LOG2E = math.log2(math.e)
TABLE_ROWS = 128
VMEM_LIMIT = 56 * 1024 * 1024


def _cparams(n_axes):
    return pltpu.CompilerParams(dimension_semantics=("arbitrary",) * n_axes,
                                vmem_limit_bytes=VMEM_LIMIT)


def _resident(shape):
    nd = len(shape)
    return pl.BlockSpec(shape, lambda *_: (0,) * nd, pipeline_mode=pl.Buffered(1))


def _rms(x, g):
    return x * lax.rsqrt(jnp.mean(x * x, axis=-1, keepdims=True) + RMS_EPS) * g


def _sigmoid(x):
    return 1.0 / (1.0 + jnp.exp(-x))


def _dot(a, b):
    return jnp.dot(a, b, preferred_element_type=F32)


def _bias_table_kernel(rbx_ref, o_ref, *, stride, offset, dmax):
    i = pl.program_id(1)
    shape = (TABLE_ROWS, QL)
    d_hi = (Q_TILE - 1) - stride * (i * TABLE_ROWS) + offset
    d_lo = -stride * (i * TABLE_ROWS + TABLE_ROWS - 1) + offset
    masked = (d_hi < 0) | (d_lo >= dmax)
    far = (d_lo >= MAX_DISTANCE) & (d_hi < dmax)

    @pl.when(masked)
    def _():
        o_ref[0] = jnp.full(shape, NEG, F32)

    @pl.when(far)
    def _():
        o_ref[0] = jnp.broadcast_to(rbx_ref[0, N_BUCKETS - 1:N_BUCKETS, :] * LOG2E, shape)

    @pl.when(jnp.logical_not(masked | far))
    def _():
        x = lax.broadcasted_iota(jnp.int32, shape, 0) + i * TABLE_ROWS
        t = lax.broadcasted_iota(jnp.int32, shape, 1) & (Q_TILE - 1)
        d = t - stride * x + offset
        n = jnp.maximum(d, 0)
        max_exact = N_BUCKETS // 2
        nf = jnp.maximum(n, max_exact).astype(F32)
        large = max_exact + (jnp.log(nf / max_exact) / math.log(MAX_DISTANCE / max_exact)
                             * (N_BUCKETS - max_exact)).astype(jnp.int32)
        large = jnp.minimum(large, N_BUCKETS - 1)
        bucket = jnp.where(n < max_exact, n, large)
        acc = jnp.zeros(shape, F32)
        for k in range(N_BUCKETS):
            acc = jnp.where(bucket == k, rbx_ref[0, k:k + 1, :], acc)
        o_ref[0] = jnp.where((d >= 0) & (d < dmax), acc * LOG2E, NEG)


def _bias_table(rbx, rows, stride, offset, dmax):
    rows_p = -(-rows // TABLE_ROWS) * TABLE_ROWS
    return pl.pallas_call(
        functools.partial(_bias_table_kernel, stride=stride, offset=offset, dmax=dmax),
        grid=(C_KV_GROUPS, rows_p // TABLE_ROWS),
        in_specs=[pl.BlockSpec((1, N_BUCKETS, QL), lambda g, i: (g, 0, 0))],
        out_specs=pl.BlockSpec((1, TABLE_ROWS, QL), lambda g, i: (g, i, 0)),
        out_shape=jax.ShapeDtypeStruct((C_KV_GROUPS, rows_p, QL), F32),
        compiler_params=_cparams(2),
        name="bias_table",
    )(rbx)


def _sub_tiles(n_rows):
    rows = n_rows // SUB_TILES
    return [slice(k * rows, (k + 1) * rows) for k in range(SUB_TILES)]


def _ffn_body(xs, g_pre, g_post, wg_ref, wu_ref, wd_ref, acc_ref):
    subs = _sub_tiles(acc_ref.shape[0])
    xn = [_rms(x, g_pre).astype(BF16) for x in xs]
    nch = wg_ref.shape[1] // FFN_CHUNK
    cols = lambda j: slice(j * FFN_CHUNK, (j + 1) * FFN_CHUNK)
    gate_up = [(_dot(x, wg_ref[:, cols(0)]), _dot(x, wu_ref[:, cols(0)])) for x in xn]
    for j in range(nch):
        if j + 1 < nch:
            nxt = [(_dot(x, wg_ref[:, cols(j + 1)]), _dot(x, wu_ref[:, cols(j + 1)])) for x in xn]
        for rows, (gate, up) in zip(subs, gate_up):
            hid = (gate * _sigmoid(gate) * up).astype(BF16)
            down = _dot(hid, wd_ref[cols(j), :])
            if j == 0:
                acc_ref[rows, :] = down
            else:
                acc_ref[rows, :] += down
        gate_up = nxt
    return [x + 0.5 * _rms(acc_ref[rows, :], g_post) for x, rows in zip(xs, subs)]


def _ffn_weight_spec(w, layer, which):
    return pl.BlockSpec((None, None) + w.shape[2:], lambda *_: (layer, which, 0, 0), pipeline_mode=pl.Buffered(1))


def _head_kernel(h_ref, norm_ref, wg_ref, wu_ref, wd_ref, wa_ref, wb_ref, wq_ref, wkv_ref, wgt_ref,
                 sgn_ref, sgw_ref, sgb_ref, cw_ref, cb_ref, lwa_ref, lba_ref, lwx_ref, lbx_ref, lam_ref,
                 h_out_ref, ya_ref, yb_ref, qt_ref, raw_ref, ks_ref, kw_ref, vst_ref, vwt_ref, gt_ref,
                 acc_ref, tail_ref, state_ref, a_ref, b_ref):
    @pl.when(pl.program_id(1) == 0)
    def _():
        tail_ref[...] = jnp.zeros_like(tail_ref)
        state_ref[...] = jnp.zeros_like(state_ref)

    h = _ffn_body([h_ref[0, r, :] for r in _sub_tiles(TOKEN_TILE)], norm_ref[0:1, :], norm_ref[1:2, :],
                  wg_ref, wu_ref, wd_ref, acc_ref)
    h = jnp.concatenate(h, axis=0)
    h_out_ref[0] = h
    xn = _rms(h, norm_ref[2:3, :]).astype(BF16)

    zb = _dot(xn, wb_ref[...])
    gate_b = _rglru_coefficients(zb, cw_ref, cb_ref, lwa_ref, lba_ref, lwx_ref, lbx_ref, lam_ref,
                                 tail_ref, state_ref, a_ref, b_ref)
    za = _dot(xn, wa_ref[...])
    zq = _dot(xn, wq_ref[...])
    zkv = _dot(xn, wkv_ref[...])
    zg = _dot(xn, wgt_ref[...])

    u = jax.nn.gelu(za[:, :A_WIDTH])
    v = _rms(jax.nn.gelu(za[:, A_WIDTH:]), sgn_ref[...]).astype(BF16)
    row = lax.broadcasted_iota(jnp.int32, (A_CHUNK, A_CHUNK), 0)
    col = lax.broadcasted_iota(jnp.int32, (A_CHUNK, A_CHUNK), 1)
    lane_group = lax.shift_right_logical(lax.broadcasted_iota(jnp.int32, (A_CHUNK, A_WIDTH), 1),
                                          (A_WIDTH // A_GROUPS).bit_length() - 1)
    w_tril = [jnp.where(row >= col, sgw_ref[g], 0.0).astype(BF16) for g in range(A_GROUPS)]
    for c in range(TOKEN_TILE // A_CHUNK):
        rows = slice(c * A_CHUNK, (c + 1) * A_CHUNK)
        mixed = jnp.zeros((A_CHUNK, A_WIDTH), F32)
        for g in range(A_GROUPS):
            mixed = jnp.where(lane_group == g, _dot(w_tril[g], v[rows]), mixed)
        ya_ref[0, rows, :] = (u[rows] * (mixed + sgb_ref[...])).astype(ya_ref.dtype)

    zq_t = (zq * (HEAD_DIM ** -0.5 * LOG2E)).T
    for g in range(C_KV_GROUPS):
        for c in range(TOKEN_TILE // Q_TILE):
            parts = []
            for r in range(C_HPG):
                base = (g * C_HPG + r) * HEAD_DIM
                parts.append(zq_t[base:base + HEAD_DIM, c * Q_TILE:(c + 1) * Q_TILE])
            qt_ref[0, g, c] = jnp.concatenate(parts, axis=1).astype(qt_ref.dtype)

    vs_t = zkv[:, 3 * KV_W:4 * KV_W].T
    vw_t = zkv[:, 5 * KV_W:6 * KV_W].T
    key_blk = lax.shift_right_logical(
        lax.broadcasted_iota(jnp.int32, (TOKEN_TILE, SEL_SLOTS), 0) + pl.program_id(1) * TOKEN_TILE, SEL_SHIFT)
    blk_onehot = jnp.where(key_blk == lax.broadcasted_iota(jnp.int32, (TOKEN_TILE, SEL_SLOTS), 1), 1.0, 0.0)
    ones_rows = jnp.where(lax.broadcasted_iota(jnp.int32, (V_PAD_ROWS, SEL_KEYS), 0) == 0, 1.0, 0.0)
    raw_ref[0, 0] = zkv[:, :KV_W]
    raw_ref[1, 0] = zkv[:, KV_W:2 * KV_W]
    for g in range(C_KV_GROUPS):
        lo, hi = g * HEAD_DIM, (g + 1) * HEAD_DIM
        ks_ref[0, g] = jnp.concatenate([zkv[:, 2 * KV_W + lo:2 * KV_W + hi], blk_onehot],
                                       axis=1).astype(ks_ref.dtype)
        kw_ref[0, g] = zkv[:, 4 * KV_W + lo:4 * KV_W + hi].astype(kw_ref.dtype)
        for c in range(TOKEN_TILE // SEL_KEYS):
            vst_ref[0, g, c] = jnp.concatenate(
                [vs_t[lo:hi, c * SEL_KEYS:(c + 1) * SEL_KEYS], ones_rows[:, :SEL_KEYS]], axis=0).astype(vst_ref.dtype)
        for c in range(TOKEN_TILE // WIN_KEYS):
            vwt_ref[0, g, c] = jnp.concatenate(
                [vw_t[lo:hi, c * WIN_KEYS:(c + 1) * WIN_KEYS], ones_rows[:, :WIN_KEYS]], axis=0).astype(vwt_ref.dtype)

    sg_t = _sigmoid(zg).T
    for br in range(3):
        for g in range(C_KV_GROUPS):
            base = br * C_HEADS + g * C_HPG
            gt_ref[0, br, g] = sg_t[base:base + C_HPG, :]

    yb_ref[0] = (_rglru_scan(state_ref, a_ref, b_ref) * jax.nn.gelu(gate_b)).astype(yb_ref.dtype)


def _head(h, norms, ffn_w, layer, w_in, sgu_norm_g, sgu_w, sgu_b, lru):
    b, s, d = h.shape
    assert ffn_w[0].shape[-1] % FFN_CHUNK == 0
    wb16 = w_in.astype(BF16)
    o = 0
    wa = wb16[:, o:o + 2 * A_WIDTH]; o += 2 * A_WIDTH
    wb = wb16[:, o:o + 2 * B_WIDTH]; o += 2 * B_WIDTH
    wq = wb16[:, o:o + C_WIDTH]; o += C_WIDTH
    wkv = wb16[:, o:o + 6 * KV_W]; o += 6 * KV_W
    wgt = jnp.pad(wb16[:, o:o + 3 * C_HEADS], ((0, 0), (0, LANES - 3 * C_HEADS)))
    sgb = jnp.repeat(sgu_b.T, A_WIDTH // A_GROUPS, axis=1)
    nt = s // TOKEN_TILE
    grid = (b, nt)
    G = C_KV_GROUPS
    out_shape = [
        jax.ShapeDtypeStruct((b, s, d), F32),
        jax.ShapeDtypeStruct((b, s, A_WIDTH), BF16),
        jax.ShapeDtypeStruct((b, s, B_WIDTH), BF16),
        jax.ShapeDtypeStruct((b, G, s // Q_TILE, HEAD_DIM, QL), BF16),
        jax.ShapeDtypeStruct((2, b, s, KV_W), F32),
        jax.ShapeDtypeStruct((b, G, s, HEAD_DIM + SEL_SLOTS), BF16),
        jax.ShapeDtypeStruct((b, G, s, HEAD_DIM), BF16),
        jax.ShapeDtypeStruct((b, G, s // SEL_KEYS, V_ROWS, SEL_KEYS), BF16),
        jax.ShapeDtypeStruct((b, G, s // WIN_KEYS, V_ROWS, WIN_KEYS), BF16),
        jax.ShapeDtypeStruct((b, 3, G, C_HPG, s), F32),
    ]
    out_specs = [
        pl.BlockSpec((1, TOKEN_TILE, d), lambda bi, i: (bi, i, 0)),
        pl.BlockSpec((1, TOKEN_TILE, A_WIDTH), lambda bi, i: (bi, i, 0)),
        pl.BlockSpec((1, TOKEN_TILE, B_WIDTH), lambda bi, i: (bi, i, 0)),
        pl.BlockSpec((1, G, TOKEN_TILE // Q_TILE, HEAD_DIM, QL), lambda bi, i: (bi, 0, i, 0, 0)),
        pl.BlockSpec((2, 1, TOKEN_TILE, KV_W), lambda bi, i: (0, bi, i, 0)),
        pl.BlockSpec((1, G, TOKEN_TILE, HEAD_DIM + SEL_SLOTS), lambda bi, i: (bi, 0, i, 0)),
        pl.BlockSpec((1, G, TOKEN_TILE, HEAD_DIM), lambda bi, i: (bi, 0, i, 0)),
        pl.BlockSpec((1, G, TOKEN_TILE // SEL_KEYS, V_ROWS, SEL_KEYS), lambda bi, i: (bi, 0, i, 0, 0)),
        pl.BlockSpec((1, G, TOKEN_TILE // WIN_KEYS, V_ROWS, WIN_KEYS), lambda bi, i: (bi, 0, i, 0, 0)),
        pl.BlockSpec((1, 3, G, C_HPG, TOKEN_TILE), lambda bi, i: (bi, 0, 0, 0, i)),
    ]
    in_specs = [
        pl.BlockSpec((1, TOKEN_TILE, d), lambda bi, i: (bi, i, 0)),
        _resident(norms.shape), *[_ffn_weight_spec(w, layer, 0) for w in ffn_w],
        _resident(wa.shape), _resident(wb.shape), _resident(wq.shape),
        _resident(wkv.shape), _resident(wgt.shape), _resident((1, A_WIDTH)),
        _resident(sgu_w.shape), _resident(sgb.shape),
        _resident((CONV_W, B_WIDTH)), _resident((1, B_WIDTH)), _resident((B_WIDTH, B_WIDTH)),
        _resident((1, B_WIDTH)), _resident((B_WIDTH, B_WIDTH)), _resident((1, B_WIDTH)), _resident((1, B_WIDTH)),
    ]
    conv_w, conv_b, lru_wa, lru_ba, lru_wx, lru_bx, lru_lambda = lru
    row = lambda v: v.reshape(1, B_WIDTH)
    return pl.pallas_call(
        _head_kernel, grid=grid, in_specs=in_specs, out_specs=out_specs, out_shape=out_shape,
        scratch_shapes=[pltpu.VMEM((TOKEN_TILE, d), F32),
                        pltpu.VMEM((SUBLANES, B_WIDTH), F32), pltpu.VMEM((SUBLANES, B_WIDTH), F32),
                        pltpu.VMEM((TOKEN_TILE, B_WIDTH), F32), pltpu.VMEM((TOKEN_TILE, B_WIDTH), F32)],
        compiler_params=_cparams(2), name="head",
    )(h, norms, *ffn_w, wa, wb, wq, wkv, wgt, sgu_norm_g.reshape(1, A_WIDTH), sgu_w, sgb,
      conv_w, row(conv_b), _block_diag(lru_wa).astype(BF16), row(lru_ba),
      _block_diag(lru_wx).astype(BF16), row(lru_bx), row(lru_lambda))


def _rglru_coefficients(zb, cw_ref, cb_ref, wa_ref, ba_ref, wx_ref, bxb_ref, lam_ref,
                        tail_ref, state_ref, a_ref, b_ref):
    xb = zb[:, :B_WIDTH]
    ext = jnp.concatenate([tail_ref[...], xb], axis=0)
    xc = cb_ref[...] + xb * cw_ref[CONV_W - 1:CONV_W, :]
    for k in range(CONV_W - 1):
        shift = CONV_W - 1 - k
        xc = xc + ext[SUBLANES - shift:SUBLANES - shift + TOKEN_TILE] * cw_ref[k:k + 1, :]
    tail_ref[...] = xb[TOKEN_TILE - SUBLANES:]

    xcb = xc.astype(BF16)
    r = _sigmoid(_dot(xcb, wa_ref[...]) + ba_ref[...])
    i = _sigmoid(_dot(xcb, wx_ref[...]) + bxb_ref[...])
    z = -lam_ref[...]
    e = jnp.exp(-jnp.abs(z))
    softplus = jnp.maximum(z, 0.0) + jnp.log1p(e)
    log_a = -LRU_C * r * softplus
    a = jnp.exp(log_a)
    b = jnp.sqrt(jnp.tanh(-log_a) * (a * a + 1.0)) * (i * xc)

    row = lax.broadcasted_iota(jnp.int32, a.shape, 0) & (SUBLANES - 1)
    for dist in (1, 2, 4):
        a_prev = jnp.where(row >= dist, pltpu.roll(a, dist, 0), 1.0)
        b_prev = jnp.where(row >= dist, pltpu.roll(b, dist, 0), 0.0)
        b = a * b_prev + b
        a = a * a_prev
    a_ref[...] = a
    b_ref[...] = b
    return zb[:, B_WIDTH:]


def _rglru_scan(state_ref, a_ref, b_ref):
    def body(k, h):
        off = pl.multiple_of(k * SUBLANES, SUBLANES)
        rows = pl.ds(off, SUBLANES)
        hs = b_ref[rows, :] + a_ref[rows, :] * h
        b_ref[rows, :] = hs
        return jnp.broadcast_to(hs[SUBLANES - 1:SUBLANES, :], hs.shape)

    state_ref[...] = lax.fori_loop(0, TOKEN_TILE // SUBLANES, body, state_ref[...])
    return b_ref[...]


def _block_diag(w):
    g, n, _ = w.shape
    out = jnp.zeros((g * n, g * n), w.dtype)
    for k in range(g):
        out = out.at[k * n:(k + 1) * n, k * n:(k + 1) * n].set(w[k])
    return out


def _compress_kernel(raw_ref, pos_ref, w1_ref, b1_ref, w2_ref, b2_ref, w2t_ref, b2c_ref, kc_ref, kct_ref):
    ncp = raw_ref.shape[2] // CMP_STRIDE
    groups = range(C_KV_GROUPS)
    hid_dim = w1_ref.shape[-1]
    u = [jnp.zeros((ncp, hid_dim), F32) for _ in groups]
    v = [jnp.zeros((ncp, hid_dim), F32) for _ in groups]
    for l in range(CMP_STRIDE):
        x = raw_ref[0, 0, pl.ds(l, ncp, stride=CMP_STRIDE), :]
        top = (x + pos_ref[0, l:l + 1, :]).astype(BF16)
        bot = (x + pos_ref[0, CMP_STRIDE + l:CMP_STRIDE + l + 1, :]).astype(BF16)
        for g in groups:
            u[g] = u[g] + _dot(top, w1_ref[0, g, l])
            v[g] = v[g] + _dot(bot, w1_ref[0, g, CMP_STRIDE + l])
    for g in groups:
        hid = jax.nn.gelu(u[g] + pltpu.roll(v[g], ncp - 1, 0) + b1_ref[0]).astype(BF16)
        kc_ref[0, 0, g] = (_dot(hid, w2_ref[0]) + b2_ref[0]).astype(kc_ref.dtype)
        t = lax.dot_general(w2t_ref[0], hid, (((1,), (1,)), ((), ())), preferred_element_type=F32)
        kct_ref[0, 0, g] = (t + b2c_ref[0]).astype(kct_ref.dtype)


def _compress(raw, cmp_pos, cmp_w1, cmp_b1, cmp_w2, cmp_b2):
    _, b, s, _ = raw.shape
    g, hd = C_KV_GROUPS, HEAD_DIM
    ncp = s // CMP_STRIDE
    hid = cmp_w1.shape[-1]
    w1 = cmp_w1.astype(BF16).reshape(2, CMP_LEN, hd, hid)
    zero = jnp.zeros_like(w1)
    w1g = jnp.stack([jnp.concatenate([w1 if k == gi else zero for k in range(g)], axis=2)
                     for gi in range(g)], axis=1)
    pos = jnp.tile(cmp_pos, (1, 1, g))
    sel = lambda *shape: pl.BlockSpec((1,) + shape, lambda kv, bi: (kv,) + (0,) * len(shape))
    return pl.pallas_call(
        _compress_kernel,
        grid=(2, b),
        in_specs=[pl.BlockSpec((1, 1, s, g * hd), lambda kv, bi: (kv, bi, 0, 0)),
                  sel(CMP_LEN, g * hd), sel(g, CMP_LEN, g * hd, hid), sel(1, hid), sel(hid, hd), sel(1, hd),
                  sel(hd, hid), sel(hd, 1)],
        out_specs=[pl.BlockSpec((1, 1, g, ncp, hd), lambda kv, bi: (kv, bi, 0, 0, 0)),
                   pl.BlockSpec((1, 1, g, hd, ncp), lambda kv, bi: (kv, bi, 0, 0, 0))],
        out_shape=[jax.ShapeDtypeStruct((2, b, g, ncp, hd), BF16),
                   jax.ShapeDtypeStruct((2, b, g, hd, ncp), BF16)],
        compiler_params=_cparams(2),
        name="compress",
    )(raw, pos, w1g, cmp_b1.reshape(2, 1, hid), cmp_w2.astype(BF16), cmp_b2.reshape(2, 1, hd),
      cmp_w2.astype(BF16).transpose(0, 2, 1), cmp_b2.reshape(2, hd, 1))


def _softmax_step(carry, s, v_t):
    m, acc = carry
    m_new = jnp.maximum(m, jnp.max(s, axis=0, keepdims=True))
    p = jnp.exp2(s - m_new).astype(BF16)
    acc = acc * jnp.exp2(m - m_new) + _dot(v_t, p)
    return m_new, acc


def _normalized(acc):
    return acc[:HEAD_DIM] / acc[HEAD_DIM:HEAD_DIM + 1]


def _nsa_kernel(qt_ref, kc_ref, vct_ref, ks_ref, vst_ref, kw_ref, vwt_ref, gate_ref,
                gsel_ref, gwin_ref, gcmp_ref, ovt_ref, o_ref, *, n_tiles):
    step = pl.program_id(1)
    ncp = kc_ref.shape[2]
    chains = [(t, g) for t in range(TILES_PER_STEP) for g in range(C_KV_GROUPS)]
    tile = [step * TILES_PER_STEP + t for t in range(TILES_PER_STEP)]
    qts = [qt_ref[0, g, t] for t, g in chains]

    back = WINDOW // WIN_KEYS
    n_win = WINDOW + Q_TILE
    first = [jnp.maximum(c - back, 0) for c in tile]
    s_cmp, s_win = [], []
    for k, (t, g) in enumerate(chains):
        y0 = pl.multiple_of((n_tiles - 1 - tile[t]) * (Q_TILE // CMP_STRIDE), SUBLANES)
        s_cmp.append(_dot(kc_ref[0, g], qts[k]) + gcmp_ref[g, pl.ds(y0, ncp), :])
    for k, (t, g) in enumerate(chains):
        start = pl.multiple_of(first[t] * WIN_KEYS, WIN_KEYS)
        rel = pl.multiple_of(jnp.maximum(back - tile[t], 0) * WIN_KEYS, WIN_KEYS)
        s_win.append(_dot(kw_ref[0, g, pl.ds(start, n_win), :], qts[k]) + gwin_ref[g, pl.ds(rel, n_win), :])

    tq = lax.broadcasted_iota(jnp.int32, (1, QL), 1) & (Q_TILE - 1)
    o_cmp, imp = [], []
    for k, (t, g) in enumerate(chains):
        e = jnp.exp2(s_cmp[k] - jnp.max(s_cmp[k], axis=0, keepdims=True))
        has_cmp = (tile[t] * Q_TILE + tq >= CMP_LEN - 1).astype(F32)
        p = e / jnp.sum(e, axis=0, keepdims=True) * has_cmp
        o_cmp.append(_dot(vct_ref[0, g], p.astype(BF16)))
        p_heads = p[:, 0:Q_TILE]
        for r in range(1, C_HPG):
            p_heads = p_heads + p[:, r * Q_TILE:(r + 1) * Q_TILE]
        p_hi = p_heads.astype(BF16)
        p_lo = (p_heads - p_hi.astype(F32)).astype(BF16)
        imp.append(_dot(ovt_ref[...], p_hi) + _dot(ovt_ref[...], p_lo))

    o_win = []
    for k, (t, g) in enumerate(chains):
        e = jnp.exp2(s_win[k] - jnp.max(s_win[k], axis=0, keepdims=True)).astype(BF16)
        acc = jnp.zeros((V_ROWS, QL), F32)
        for i in range(n_win // WIN_KEYS):
            acc = acc + _dot(vwt_ref[0, g, first[t] + i], e[i * WIN_KEYS:(i + 1) * WIN_KEYS])
        o_win.append(_normalized(acc))

    q_aug = _with_mask_rows(qts, [tile[t] for t, g in chains], imp)
    halves = range(SEL_KEYS // SEL_HALF)
    init = (jnp.full((1, QL), NEG, F32), jnp.zeros((V_ROWS, QL), F32))

    def sel_body(i, carry, far):
        off = pl.multiple_of(i * SEL_KEYS, SEL_KEYS)
        units = [(h, k) for h in halves for k in range(len(chains))]

        def logits_of(h, k):
            t, g = chains[k]
            logits = _dot(ks_ref[0, g, pl.ds(off + h * SEL_HALF, SEL_HALF), :], q_aug[k])
            if not far:
                x0 = (n_tiles - 1 - tile[t]) * Q_TILE
                logits = logits + gsel_ref[g, pl.ds(pl.multiple_of(x0 + off + h * SEL_HALF, Q_TILE), SEL_HALF), :]
            return logits

        carry = list(carry)
        s = [logits_of(*u) for u in units[:SEL_AHEAD]]
        for n, (h, k) in enumerate(units):
            if n + SEL_AHEAD < len(units):
                s.append(logits_of(*units[n + SEL_AHEAD]))
            t, g = chains[k]
            v_t = vst_ref[0, g, i][:, h * SEL_HALF:(h + 1) * SEL_HALF]
            carry[k] = _softmax_step(carry[k], s[n], v_t)
        return tuple(carry)

    n_far = jnp.maximum(tile[0] * Q_TILE - MAX_DISTANCE + 1, 0) // SEL_KEYS
    n_sel_steps = (tile[-1] * Q_TILE + Q_TILE + SEL_KEYS - 1) // SEL_KEYS
    sel = lax.fori_loop(0, n_far, functools.partial(sel_body, far=True), (init,) * len(chains))
    sel = tuple((m + gsel_ref[g, 0:1, :], acc) for (m, acc), (t, g) in zip(sel, chains))
    sel = lax.fori_loop(n_far, n_sel_steps, functools.partial(sel_body, far=False), sel)

    for t in range(TILES_PER_STEP):
        outs = []
        for k, (tk, g) in enumerate(chains):
            if tk != t:
                continue
            o_sel = _normalized(sel[k][1])
            tok = slice(t * Q_TILE, (t + 1) * Q_TILE)
            cols = []
            for r in range(C_HPG):
                ln = slice(r * Q_TILE, (r + 1) * Q_TILE)
                cols.append(gate_ref[0, 0, g, r:r + 1, tok] * o_cmp[k][:, ln]
                            + gate_ref[0, 1, g, r:r + 1, tok] * o_sel[:, ln]
                            + gate_ref[0, 2, g, r:r + 1, tok] * o_win[k][:, ln])
            outs += [jnp.concatenate(cols[2 * j:2 * j + 2], axis=0).T for j in range(C_HPG // 2)]
        o_ref[0, t * Q_TILE:(t + 1) * Q_TILE, :] = jnp.concatenate(outs, axis=1).astype(o_ref.dtype)


def _with_mask_rows(qts, tiles, imps):
    n_groups = SEL_SLOTS // SUBLANES
    j = lax.broadcasted_iota(jnp.int32, (SEL_SLOTS, Q_TILE), 0)
    half = lax.shift_right_logical(lax.broadcasted_iota(jnp.int32, (SEL_SLOTS, Q_TILE), 1), SEL_SHIFT)
    j_rows = j[:SUBLANES]
    scores = []
    for c, imp in zip(tiles, imps):
        blk = c * (Q_TILE // SEL_LEN) + half
        forced = (j == 0) | (j == blk) | (j == blk - 1)
        scores.append(jnp.where(j <= blk, jnp.where(forced, FORCE_SCORE, imp), -1.0))
    rows = [[sc[k * SUBLANES:(k + 1) * SUBLANES] for k in range(n_groups)] for sc in scores]

    def add_pair(ranks, jg, k):
        for ci, sc in enumerate(scores):
            acc = ranks[ci][k]
            for jp in range(jg * SUBLANES, (jg + 1) * SUBLANES):
                other = sc[jp:jp + 1, :]
                if k > jg:
                    beats = jnp.where(other >= rows[ci][k], 1, 0)
                elif k < jg:
                    beats = jnp.where(other > rows[ci][k], 1, 0)
                else:
                    beats = jnp.where(j_rows > jp - k * SUBLANES, jnp.where(other >= rows[ci][k], 1, 0),
                                      jnp.where(other > rows[ci][k], 1, 0))
                acc = acc + beats
            ranks[ci][k] = acc

    last_blk = tiles[-1] * (Q_TILE // SEL_LEN) + Q_TILE // SEL_LEN - 1
    ranks = [[jnp.zeros((SUBLANES, Q_TILE), jnp.int32) for _ in range(n_groups)] for _ in scores]
    for m in range(n_groups):
        def shell(ranks, m=m):
            ranks = [list(r) for r in ranks]
            for k in range(m + 1):
                add_pair(ranks, m, k)
            for jg in range(m):
                add_pair(ranks, jg, m)
            return ranks
        ranks = shell(ranks) if m == 0 else lax.cond(m * SUBLANES <= last_blk, shell, lambda r: r, ranks)

    out = []
    for qt, sc, rk in zip(qts, scores, ranks):
        rank = jnp.concatenate(rk, axis=0)
        mask_rows = jnp.where((rank < SEL_TOP) & (sc >= 0.0), 0.0, NEG).astype(BF16)
        out.append(jnp.concatenate([qt, jnp.concatenate([mask_rows] * C_HPG, axis=1)], axis=0))
    return out


def _nsa(qt, kc, vct, ks, vst, kw, vwt, gates, gsel, gwin, gcmp, ovt):
    b, G, n_tiles = qt.shape[:3]
    s = n_tiles * Q_TILE
    per_b = lambda a: pl.BlockSpec((1,) + a.shape[1:], lambda bi, c: (bi,) + (0,) * (a.ndim - 1),
                                   pipeline_mode=pl.Buffered(1))
    return pl.pallas_call(
        functools.partial(_nsa_kernel, n_tiles=n_tiles),
        grid=(b, n_tiles // TILES_PER_STEP),
        in_specs=[pl.BlockSpec((1, G, TILES_PER_STEP, HEAD_DIM, QL), lambda bi, c: (bi, 0, c, 0, 0)),
                  per_b(kc), per_b(vct), per_b(ks), per_b(vst), per_b(kw), per_b(vwt),
                  pl.BlockSpec((1, 3, G, C_HPG, TILES_PER_STEP * Q_TILE), lambda bi, c: (bi, 0, 0, 0, c)),
                  _resident(gsel.shape), _resident(gwin.shape), _resident(gcmp.shape),
                  _resident(ovt.shape)],
        out_specs=pl.BlockSpec((1, TILES_PER_STEP * Q_TILE, C_WIDTH), lambda bi, c: (bi, c, 0)),
        out_shape=jax.ShapeDtypeStruct((b, s, C_WIDTH), BF16),
        compiler_params=_cparams(2),
        name="nsa",
    )(qt, kc, vct, ks, vst, kw, vwt, gates, gsel, gwin, gcmp, ovt)


def _tail_kernel(h_ref, ya_ref, yb_ref, yc_ref, p_ref, norm_ref, woa_ref, wob_ref, woc_ref,
                 wg_ref, wu_ref, wd_ref, wpg_ref, wpp_ref, o_ref, acc_ref):
    subs = _sub_tiles(h_ref.shape[0])
    mix = [_dot(ya_ref[r, :], woa_ref[...]) + _dot(yb_ref[r, :], wob_ref[...]) + _dot(yc_ref[r, :], woc_ref[...])
           for r in subs]
    emb = [_dot(p_ref[r, :].astype(BF16), wpp_ref[...]) for r in subs]
    h = [h_ref[r, :] + _rms(m, norm_ref[3:4, :]) for r, m in zip(subs, mix)]
    h = _ffn_body(h, norm_ref[4:5, :], norm_ref[5:6, :], wg_ref, wu_ref, wd_ref, acc_ref)
    gate = [_dot(_rms(x, norm_ref[6:7, :]).astype(BF16), wpg_ref[...]) for x in h]
    for r, x, g, e in zip(subs, h, gate, emb):
        o_ref[r, :] = x + _rms(_sigmoid(g) * e, norm_ref[7:8, :])


def _tail(h, ya, yb, yc, p, norms, ffn_w, layer, w_out, w_gate, w_proj):
    n, d = h.shape
    dp = p.shape[-1]
    w = w_out.astype(BF16)
    woa, wob, woc = w[:A_WIDTH], w[A_WIDTH:A_WIDTH + B_WIDTH], w[A_WIDTH + B_WIDTH:]
    tile = lambda width: pl.BlockSpec((TOKEN_TILE, width), lambda i: (i, 0))
    return pl.pallas_call(
        _tail_kernel,
        grid=(n // TOKEN_TILE,),
        in_specs=[tile(d), tile(A_WIDTH), tile(B_WIDTH), tile(C_WIDTH),
                  pl.BlockSpec((None, TOKEN_TILE, dp), lambda i: (layer, i, 0)),
                  _resident(norms.shape), _resident(woa.shape), _resident(wob.shape), _resident(woc.shape),
                  *[_ffn_weight_spec(wt, layer, 1) for wt in ffn_w],
                  _resident((d, d)), _resident((dp, d))],
        out_specs=tile(d),
        out_shape=jax.ShapeDtypeStruct((n, d), F32),
        scratch_shapes=[pltpu.VMEM((TOKEN_TILE, d), F32)],
        compiler_params=_cparams(1),
        name="tail",
    )(h, ya, yb, yc, p, norms, woa, wob, woc, *ffn_w, w_gate.astype(BF16), w_proj.astype(BF16))


def _overlap_t(s):
    ncp = s // CMP_STRIDE
    n_cmp = (s - CMP_LEN) // CMP_STRIDE + 1
    cs = jnp.arange(ncp) * CMP_STRIDE
    ss = jnp.arange(s // SEL_LEN) * SEL_LEN
    ov = jnp.clip(jnp.minimum(cs[None] + CMP_LEN, ss[:, None] + SEL_LEN)
                  - jnp.maximum(cs[None], ss[:, None]), 0, None).astype(F32) / CMP_LEN
    ov = jnp.where(jnp.arange(ncp)[None] < n_cmp, ov, 0.0).astype(BF16)
    return jnp.pad(ov, ((0, SEL_SLOTS - s // SEL_LEN), (0, 0)))


def kernel(x, p, rel_bias, norm_g, ffn_w_gate, ffn_w_up, ffn_w_down, w_in, w_out, sgu_norm_g, sgu_w, sgu_b,
           conv_w, conv_b, lru_wa, lru_ba, lru_wx, lru_bx, lru_lambda, cmp_pos, cmp_w1, cmp_b1, cmp_w2,
           cmp_b2, ple_w_gate, ple_w_proj):
    b, s, d = x.shape
    depth = norm_g.shape[0]
    assert s % TOKEN_TILE == 0 and s % SEL_KEYS == 0
    assert s % (TILES_PER_STEP * Q_TILE) == 0
    assert s >= WINDOW + Q_TILE and SEL_TOP <= s // SEL_LEN <= SEL_SLOTS
    n_tiles = s // Q_TILE

    rbx = jnp.repeat(rel_bias.reshape(N_BUCKETS, C_KV_GROUPS, C_HPG).transpose(1, 0, 2), Q_TILE, axis=2)
    no_limit = 1 << 30
    gsel = _bias_table(rbx, s + SEL_KEYS, 1, s - Q_TILE, no_limit)
    gwin = _bias_table(rbx, 2 * WINDOW + Q_TILE, 1, WINDOW, WINDOW)
    per_tile = Q_TILE // CMP_STRIDE
    gcmp = _bias_table(rbx, per_tile * (n_tiles - 1) + s // CMP_STRIDE, CMP_STRIDE,
                       CMP_STRIDE * per_tile * (n_tiles - 1) - (CMP_LEN - 1), no_limit)
    ovt = _overlap_t(s)

    ffn_w = (ffn_w_gate.astype(BF16), ffn_w_up.astype(BF16), ffn_w_down.astype(BF16))
    h = x
    flat = lambda a: a.reshape(b * s, -1)
    for i in range(depth):
        h, ya, yb, qt, raw, ks, kw, vst, vwt, gates = _head(
            h, norm_g[i], ffn_w, i, w_in[i], sgu_norm_g[i], sgu_w[i], sgu_b[i],
            (conv_w[i], conv_b[i], lru_wa[i], lru_ba[i], lru_wx[i], lru_bx[i], lru_lambda[i]))
        kc, kct = _compress(raw, cmp_pos[i], cmp_w1[i], cmp_b1[i], cmp_w2[i], cmp_b2[i])
        yc = _nsa(qt, kc[0], kct[1], ks, vst, kw, vwt, gates, gsel, gwin, gcmp, ovt)
        h = _tail(flat(h), flat(ya), flat(yb), flat(yc), p.reshape(depth, b * s, -1), norm_g[i], ffn_w, i,
                  w_out[i], ple_w_gate[i], ple_w_proj[i]).reshape(b, s, d)
    return h
```

```python
import functools
import math

import jax
import jax.numpy as jnp
from jax import lax
from jax.experimental import pallas as pl
from jax.experimental.pallas import tpu as pltpu

F32 = jnp.float32
BF16 = jnp.bfloat16

RMS_EPS = 1e-6
A_GROUPS = 4
A_WIDTH = 256
A_CHUNK = 128
B_WIDTH = 256
CONV_W = 4
LRU_C = 8.0
C_HEADS = 8
C_KV_GROUPS = 2
C_HPG = C_HEADS // C_KV_GROUPS
HEAD_DIM = 64
C_WIDTH = C_HEADS * HEAD_DIM
KV_W = C_KV_GROUPS * HEAD_DIM
CMP_LEN = 32
CMP_STRIDE = 16
SEL_LEN = 64
SEL_SHIFT = 6
SEL_SLOTS = 64
SEL_TOP = 16
WINDOW = 512
FORCE_SCORE = 1e4
NEG = -1e30
N_BUCKETS = 32
MAX_DISTANCE = 1024

LANES = 128
TOKEN_TILE = 512
SUB_TILES = 2
FFN_CHUNK = 256
SUBLANES = 8
Q_TILE = 128
TILES_PER_STEP = 4
SEL_KEYS = 512
SEL_HALF = 256
SEL_AHEAD = 2
SEL_NEAR_SPAN = -(-(TILES_PER_STEP * Q_TILE + MAX_DISTANCE + SEL_KEYS - 2) // Q_TILE) * Q_TILE
WIN_KEYS = 128
QL = C_HPG * Q_TILE
V_PAD_ROWS = 16
V_ROWS = HEAD_DIM + V_PAD_ROWS
LOG2E = math.log2(math.e)
TABLE_ROWS = 128
VMEM_LIMIT = 56 * 1024 * 1024


def _cparams(n_axes):
    return pltpu.CompilerParams(dimension_semantics=("arbitrary",) * n_axes,
                                vmem_limit_bytes=VMEM_LIMIT)


def _resident(shape):
    nd = len(shape)
    return pl.BlockSpec(shape, lambda *_: (0,) * nd, pipeline_mode=pl.Buffered(1))


def _rms(x, g):
    return x * lax.rsqrt(jnp.mean(x * x, axis=-1, keepdims=True) + RMS_EPS) * g


def _sigmoid(x):
    return 1.0 / (1.0 + jnp.exp(-x))


def _dot(a, b):
    return jnp.dot(a, b, preferred_element_type=F32)


def _bias_table_kernel(rbx_ref, o_ref, *, stride, offset, dmax):
    i = pl.program_id(1)
    shape = (TABLE_ROWS, QL)
    d_hi = (Q_TILE - 1) - stride * (i * TABLE_ROWS) + offset
    d_lo = -stride * (i * TABLE_ROWS + TABLE_ROWS - 1) + offset
    masked = (d_hi < 0) | (d_lo >= dmax)
    far = (d_lo >= MAX_DISTANCE) & (d_hi < dmax)

    @pl.when(masked)
    def _():
        o_ref[0] = jnp.full(shape, NEG, F32)

    @pl.when(far)
    def _():
        o_ref[0] = jnp.broadcast_to(rbx_ref[0, N_BUCKETS - 1:N_BUCKETS, :] * LOG2E, shape)

    @pl.when(jnp.logical_not(masked | far))
    def _():
        x = lax.broadcasted_iota(jnp.int32, shape, 0) + i * TABLE_ROWS
        t = lax.broadcasted_iota(jnp.int32, shape, 1) & (Q_TILE - 1)
        d = t - stride * x + offset
        n = jnp.maximum(d, 0)
        max_exact = N_BUCKETS // 2
        nf = jnp.maximum(n, max_exact).astype(F32)
        large = max_exact + (jnp.log(nf / max_exact) / math.log(MAX_DISTANCE / max_exact)
                             * (N_BUCKETS - max_exact)).astype(jnp.int32)
        large = jnp.minimum(large, N_BUCKETS - 1)
        bucket = jnp.where(n < max_exact, n, large)
        acc = jnp.zeros(shape, F32)
        for k in range(N_BUCKETS):
            acc = jnp.where(bucket == k, rbx_ref[0, k:k + 1, :], acc)
        o_ref[0] = jnp.where((d >= 0) & (d < dmax), acc * LOG2E, NEG)


def _bias_table(rbx, rows, stride, offset, dmax):
    rows_p = -(-rows // TABLE_ROWS) * TABLE_ROWS
    return pl.pallas_call(
        functools.partial(_bias_table_kernel, stride=stride, offset=offset, dmax=dmax),
        grid=(C_KV_GROUPS, rows_p // TABLE_ROWS),
        in_specs=[pl.BlockSpec((1, N_BUCKETS, QL), lambda g, i: (g, 0, 0))],
        out_specs=pl.BlockSpec((1, TABLE_ROWS, QL), lambda g, i: (g, i, 0)),
        out_shape=jax.ShapeDtypeStruct((C_KV_GROUPS, rows_p, QL), F32),
        compiler_params=_cparams(2),
        name="bias_table",
    )(rbx)


def _sub_tiles(n_rows):
    rows = n_rows // SUB_TILES
    return [slice(k * rows, (k + 1) * rows) for k in range(SUB_TILES)]


def _ffn_body(xs, g_pre, g_post, wg_ref, wu_ref, wd_ref, acc_ref):
    subs = _sub_tiles(acc_ref.shape[0])
    xn = [_rms(x, g_pre).astype(BF16) for x in xs]
    nch = wg_ref.shape[1] // FFN_CHUNK
    cols = lambda j: slice(j * FFN_CHUNK, (j + 1) * FFN_CHUNK)
    gate_up = [(_dot(x, wg_ref[:, cols(0)]), _dot(x, wu_ref[:, cols(0)])) for x in xn]
    for j in range(nch):
        if j + 1 < nch:
            nxt = [(_dot(x, wg_ref[:, cols(j + 1)]), _dot(x, wu_ref[:, cols(j + 1)])) for x in xn]
        for rows, (gate, up) in zip(subs, gate_up):
            hid = (gate * _sigmoid(gate) * up).astype(BF16)
            down = _dot(hid, wd_ref[cols(j), :])
            if j == 0:
                acc_ref[rows, :] = down
            else:
                acc_ref[rows, :] += down
        gate_up = nxt
    return [x + 0.5 * _rms(acc_ref[rows, :], g_post) for x, rows in zip(xs, subs)]


def _ffn_weight_spec(w, layer, which):
    return pl.BlockSpec((None, None) + w.shape[2:], lambda *_: (layer, which, 0, 0), pipeline_mode=pl.Buffered(1))


def _head_kernel(h_ref, norm_ref, wg_ref, wu_ref, wd_ref, wa_ref, wb_ref, wq_ref, wkv_ref, wgt_ref,
                 sgn_ref, sgw_ref, sgb_ref, cw_ref, cb_ref, lwa_ref, lba_ref, lwx_ref, lbx_ref, lam_ref,
                 h_out_ref, ya_ref, yb_ref, qt_ref, raw_ref, ks_ref, kw_ref, vst_ref, vwt_ref, gt_ref,
                 acc_ref, tail_ref, state_ref, a_ref, b_ref):
    @pl.when(pl.program_id(1) == 0)
    def _():
        tail_ref[...] = jnp.zeros_like(tail_ref)
        state_ref[...] = jnp.zeros_like(state_ref)

    h = _ffn_body([h_ref[0, r, :] for r in _sub_tiles(TOKEN_TILE)], norm_ref[0:1, :], norm_ref[1:2, :],
                  wg_ref, wu_ref, wd_ref, acc_ref)
    h = jnp.concatenate(h, axis=0)
    h_out_ref[0] = h
    xn = _rms(h, norm_ref[2:3, :]).astype(BF16)

    zb = _dot(xn, wb_ref[...])
    za = _dot(xn, wa_ref[...])
    zq = _dot(xn, wq_ref[...])
    zkv = _dot(xn, wkv_ref[...])
    zg = _dot(xn, wgt_ref[...])

    gate_b = _rglru_coefficients(zb, cw_ref, cb_ref, lwa_ref, lba_ref, lwx_ref, lbx_ref, lam_ref,
                                 tail_ref, state_ref, a_ref, b_ref)

    u = jax.nn.gelu(za[:, :A_WIDTH])
    v = _rms(jax.nn.gelu(za[:, A_WIDTH:]), sgn_ref[...]).astype(BF16)
    row = lax.broadcasted_iota(jnp.int32, (A_CHUNK, A_CHUNK), 0)
    col = lax.broadcasted_iota(jnp.int32, (A_CHUNK, A_CHUNK), 1)
    lane_group = lax.shift_right_logical(lax.broadcasted_iota(jnp.int32, (A_CHUNK, A_WIDTH), 1),
                                          (A_WIDTH // A_GROUPS).bit_length() - 1)
    w_tril = [jnp.where(row >= col, sgw_ref[g], 0.0).astype(BF16) for g in range(A_GROUPS)]
    for c in range(TOKEN_TILE // A_CHUNK):
        rows = slice(c * A_CHUNK, (c + 1) * A_CHUNK)
        mixed = jnp.zeros((A_CHUNK, A_WIDTH), F32)
        for g in range(A_GROUPS):
            mixed = jnp.where(lane_group == g, _dot(w_tril[g], v[rows]), mixed)
        ya_ref[0, rows, :] = (u[rows] * (mixed + sgb_ref[...])).astype(ya_ref.dtype)

    zq_t = (zq * (HEAD_DIM ** -0.5 * LOG2E)).T
    for g in range(C_KV_GROUPS):
        for c in range(TOKEN_TILE // Q_TILE):
            parts = []
            for r in range(C_HPG):
                base = (g * C_HPG + r) * HEAD_DIM
                parts.append(zq_t[base:base + HEAD_DIM, c * Q_TILE:(c + 1) * Q_TILE])
            qt_ref[0, g, c] = jnp.concatenate(parts, axis=1).astype(qt_ref.dtype)

    vs_t = zkv[:, 3 * KV_W:4 * KV_W].T
    vw_t = zkv[:, 5 * KV_W:6 * KV_W].T
    key_blk = lax.shift_right_logical(
        lax.broadcasted_iota(jnp.int32, (TOKEN_TILE, SEL_SLOTS), 0) + pl.program_id(1) * TOKEN_TILE, SEL_SHIFT)
    blk_onehot = jnp.where(key_blk == lax.broadcasted_iota(jnp.int32, (TOKEN_TILE, SEL_SLOTS), 1), 1.0, 0.0)
    ones_rows = jnp.where(lax.broadcasted_iota(jnp.int32, (V_PAD_ROWS, SEL_KEYS), 0) == 0, 1.0, 0.0)
    raw_ref[0, 0] = zkv[:, :KV_W]
    raw_ref[1, 0] = zkv[:, KV_W:2 * KV_W]
    for g in range(C_KV_GROUPS):
        lo, hi = g * HEAD_DIM, (g + 1) * HEAD_DIM
        ks_ref[0, g] = jnp.concatenate([zkv[:, 2 * KV_W + lo:2 * KV_W + hi], blk_onehot],
                                       axis=1).astype(ks_ref.dtype)
        kw_ref[0, g] = zkv[:, 4 * KV_W + lo:4 * KV_W + hi].astype(kw_ref.dtype)
        for c in range(TOKEN_TILE // SEL_KEYS):
            vst_ref[0, g, c] = jnp.concatenate(
                [vs_t[lo:hi, c * SEL_KEYS:(c + 1) * SEL_KEYS], ones_rows[:, :SEL_KEYS]], axis=0).astype(vst_ref.dtype)
        for c in range(TOKEN_TILE // WIN_KEYS):
            vwt_ref[0, g, c] = jnp.concatenate(
                [vw_t[lo:hi, c * WIN_KEYS:(c + 1) * WIN_KEYS], ones_rows[:, :WIN_KEYS]], axis=0).astype(vwt_ref.dtype)

    sg_t = _sigmoid(zg).T
    for br in range(3):
        for g in range(C_KV_GROUPS):
            base = br * C_HEADS + g * C_HPG
            gt_ref[0, br, g] = sg_t[base:base + C_HPG, :]

    yb_ref[0] = (_rglru_scan(state_ref, a_ref, b_ref) * jax.nn.gelu(gate_b)).astype(yb_ref.dtype)


def _head(h, norms, ffn_w, layer, w_in, sgu_norm_g, sgu_w, sgu_b, lru):
    b, s, d = h.shape
    assert ffn_w[0].shape[-1] % FFN_CHUNK == 0
    wb16 = w_in.astype(BF16)
    o = 0
    wa = wb16[:, o:o + 2 * A_WIDTH]; o += 2 * A_WIDTH
    wb = wb16[:, o:o + 2 * B_WIDTH]; o += 2 * B_WIDTH
    wq = wb16[:, o:o + C_WIDTH]; o += C_WIDTH
    wkv = wb16[:, o:o + 6 * KV_W]; o += 6 * KV_W
    wgt = jnp.pad(wb16[:, o:o + 3 * C_HEADS], ((0, 0), (0, LANES - 3 * C_HEADS)))
    sgb = jnp.repeat(sgu_b.T, A_WIDTH // A_GROUPS, axis=1)
    nt = s // TOKEN_TILE
    grid = (b, nt)
    G = C_KV_GROUPS
    out_shape = [
        jax.ShapeDtypeStruct((b, s, d), F32),
        jax.ShapeDtypeStruct((b, s, A_WIDTH), BF16),
        jax.ShapeDtypeStruct((b, s, B_WIDTH), BF16),
        jax.ShapeDtypeStruct((b, G, s // Q_TILE, HEAD_DIM, QL), BF16),
        jax.ShapeDtypeStruct((2, b, s, KV_W), F32),
        jax.ShapeDtypeStruct((b, G, s, HEAD_DIM + SEL_SLOTS), BF16),
        jax.ShapeDtypeStruct((b, G, s, HEAD_DIM), BF16),
        jax.ShapeDtypeStruct((b, G, s // SEL_KEYS, V_ROWS, SEL_KEYS), BF16),
        jax.ShapeDtypeStruct((b, G, s // WIN_KEYS, V_ROWS, WIN_KEYS), BF16),
        jax.ShapeDtypeStruct((b, 3, G, C_HPG, s), F32),
    ]
    out_specs = [
        pl.BlockSpec((1, TOKEN_TILE, d), lambda bi, i: (bi, i, 0)),
        pl.BlockSpec((1, TOKEN_TILE, A_WIDTH), lambda bi, i: (bi, i, 0)),
        pl.BlockSpec((1, TOKEN_TILE, B_WIDTH), lambda bi, i: (bi, i, 0)),
        pl.BlockSpec((1, G, TOKEN_TILE // Q_TILE, HEAD_DIM, QL), lambda bi, i: (bi, 0, i, 0, 0)),
        pl.BlockSpec((2, 1, TOKEN_TILE, KV_W), lambda bi, i: (0, bi, i, 0)),
        pl.BlockSpec((1, G, TOKEN_TILE, HEAD_DIM + SEL_SLOTS), lambda bi, i: (bi, 0, i, 0)),
        pl.BlockSpec((1, G, TOKEN_TILE, HEAD_DIM), lambda bi, i: (bi, 0, i, 0)),
        pl.BlockSpec((1, G, TOKEN_TILE // SEL_KEYS, V_ROWS, SEL_KEYS), lambda bi, i: (bi, 0, i, 0, 0)),
        pl.BlockSpec((1, G, TOKEN_TILE // WIN_KEYS, V_ROWS, WIN_KEYS), lambda bi, i: (bi, 0, i, 0, 0)),
        pl.BlockSpec((1, 3, G, C_HPG, TOKEN_TILE), lambda bi, i: (bi, 0, 0, 0, i)),
    ]
    in_specs = [
        pl.BlockSpec((1, TOKEN_TILE, d), lambda bi, i: (bi, i, 0)),
        _resident(norms.shape), *[_ffn_weight_spec(w, layer, 0) for w in ffn_w],
        _resident(wa.shape), _resident(wb.shape), _resident(wq.shape),
        _resident(wkv.shape), _resident(wgt.shape), _resident((1, A_WIDTH)),
        _resident(sgu_w.shape), _resident(sgb.shape),
        _resident((CONV_W, B_WIDTH)), _resident((1, B_WIDTH)), _resident((B_WIDTH, B_WIDTH)),
        _resident((1, B_WIDTH)), _resident((B_WIDTH, B_WIDTH)), _resident((1, B_WIDTH)), _resident((1, B_WIDTH)),
    ]
    conv_w, conv_b, lru_wa, lru_ba, lru_wx, lru_bx, lru_lambda = lru
    row = lambda v: v.reshape(1, B_WIDTH)
    return pl.pallas_call(
        _head_kernel, grid=grid, in_specs=in_specs, out_specs=out_specs, out_shape=out_shape,
        scratch_shapes=[pltpu.VMEM((TOKEN_TILE, d), F32),
                        pltpu.VMEM((SUBLANES, B_WIDTH), F32), pltpu.VMEM((SUBLANES, B_WIDTH), F32),
                        pltpu.VMEM((TOKEN_TILE, B_WIDTH), F32), pltpu.VMEM((TOKEN_TILE, B_WIDTH), F32)],
        compiler_params=_cparams(2), name="head",
    )(h, norms, *ffn_w, wa, wb, wq, wkv, wgt, sgu_norm_g.reshape(1, A_WIDTH), sgu_w, sgb,
      conv_w, row(conv_b), _block_diag(lru_wa).astype(BF16), row(lru_ba),
      _block_diag(lru_wx).astype(BF16), row(lru_bx), row(lru_lambda))


def _rglru_coefficients(zb, cw_ref, cb_ref, wa_ref, ba_ref, wx_ref, bxb_ref, lam_ref,
                        tail_ref, state_ref, a_ref, b_ref):
    xb = zb[:, :B_WIDTH]
    ext = jnp.concatenate([tail_ref[...], xb], axis=0)
    xc = cb_ref[...] + xb * cw_ref[CONV_W - 1:CONV_W, :]
    for k in range(CONV_W - 1):
        shift = CONV_W - 1 - k
        xc = xc + ext[SUBLANES - shift:SUBLANES - shift + TOKEN_TILE] * cw_ref[k:k + 1, :]
    tail_ref[...] = xb[TOKEN_TILE - SUBLANES:]

    xcb = xc.astype(BF16)
    r = _sigmoid(_dot(xcb, wa_ref[...]) + ba_ref[...])
    i = _sigmoid(_dot(xcb, wx_ref[...]) + bxb_ref[...])
    z = -lam_ref[...]
    e = jnp.exp(-jnp.abs(z))
    softplus = jnp.maximum(z, 0.0) + jnp.log1p(e)
    log_a = -LRU_C * r * softplus
    a = jnp.exp(log_a)
    b = jnp.sqrt(jnp.tanh(-log_a) * (a * a + 1.0)) * (i * xc)

    row = lax.broadcasted_iota(jnp.int32, a.shape, 0) & (SUBLANES - 1)
    for dist in (1, 2, 4):
        a_prev = jnp.where(row >= dist, pltpu.roll(a, dist, 0), 1.0)
        b_prev = jnp.where(row >= dist, pltpu.roll(b, dist, 0), 0.0)
        b = a * b_prev + b
        a = a * a_prev
    a_ref[...] = a
    b_ref[...] = b
    return zb[:, B_WIDTH:]


def _rglru_scan(state_ref, a_ref, b_ref):
    def body(k, h):
        off = pl.multiple_of(k * SUBLANES, SUBLANES)
        rows = pl.ds(off, SUBLANES)
        hs = b_ref[rows, :] + a_ref[rows, :] * h
        b_ref[rows, :] = hs
        return jnp.broadcast_to(hs[SUBLANES - 1:SUBLANES, :], hs.shape)

    state_ref[...] = lax.fori_loop(0, TOKEN_TILE // SUBLANES, body, state_ref[...])
    return b_ref[...]


def _block_diag(w):
    g, n, _ = w.shape
    out = jnp.zeros((g * n, g * n), w.dtype)
    for k in range(g):
        out = out.at[k * n:(k + 1) * n, k * n:(k + 1) * n].set(w[k])
    return out


def _compress_kernel(raw_ref, pos_ref, w1_ref, b1_ref, w2_ref, b2_ref, w2t_ref, b2c_ref, kc_ref, kct_ref):
    ncp = raw_ref.shape[2] // CMP_STRIDE
    groups = range(C_KV_GROUPS)
    hid_dim = w1_ref.shape[-1]
    u = [jnp.zeros((ncp, hid_dim), F32) for _ in groups]
    v = [jnp.zeros((ncp, hid_dim), F32) for _ in groups]
    for l in range(CMP_STRIDE):
        x = raw_ref[0, 0, pl.ds(l, ncp, stride=CMP_STRIDE), :]
        top = (x + pos_ref[0, l:l + 1, :]).astype(BF16)
        bot = (x + pos_ref[0, CMP_STRIDE + l:CMP_STRIDE + l + 1, :]).astype(BF16)
        for g in groups:
            u[g] = u[g] + _dot(top, w1_ref[0, g, l])
            v[g] = v[g] + _dot(bot, w1_ref[0, g, CMP_STRIDE + l])
    for g in groups:
        hid = jax.nn.gelu(u[g] + pltpu.roll(v[g], ncp - 1, 0) + b1_ref[0]).astype(BF16)
        kc_ref[0, 0, g] = (_dot(hid, w2_ref[0]) + b2_ref[0]).astype(kc_ref.dtype)
        t = lax.dot_general(w2t_ref[0], hid, (((1,), (1,)), ((), ())), preferred_element_type=F32)
        kct_ref[0, 0, g] = (t + b2c_ref[0]).astype(kct_ref.dtype)


def _compress(raw, cmp_pos, cmp_w1, cmp_b1, cmp_w2, cmp_b2):
    _, b, s, _ = raw.shape
    g, hd = C_KV_GROUPS, HEAD_DIM
    ncp = s // CMP_STRIDE
    hid = cmp_w1.shape[-1]
    w1 = cmp_w1.astype(BF16).reshape(2, CMP_LEN, hd, hid)
    zero = jnp.zeros_like(w1)
    w1g = jnp.stack([jnp.concatenate([w1 if k == gi else zero for k in range(g)], axis=2)
                     for gi in range(g)], axis=1)
    pos = jnp.tile(cmp_pos, (1, 1, g))
    sel = lambda *shape: pl.BlockSpec((1,) + shape, lambda kv, bi: (kv,) + (0,) * len(shape))
    return pl.pallas_call(
        _compress_kernel,
        grid=(2, b),
        in_specs=[pl.BlockSpec((1, 1, s, g * hd), lambda kv, bi: (kv, bi, 0, 0)),
                  sel(CMP_LEN, g * hd), sel(g, CMP_LEN, g * hd, hid), sel(1, hid), sel(hid, hd), sel(1, hd),
                  sel(hd, hid), sel(hd, 1)],
        out_specs=[pl.BlockSpec((1, 1, g, ncp, hd), lambda kv, bi: (kv, bi, 0, 0, 0)),
                   pl.BlockSpec((1, 1, g, hd, ncp), lambda kv, bi: (kv, bi, 0, 0, 0))],
        out_shape=[jax.ShapeDtypeStruct((2, b, g, ncp, hd), BF16),
                   jax.ShapeDtypeStruct((2, b, g, hd, ncp), BF16)],
        compiler_params=_cparams(2),
        name="compress",
    )(raw, pos, w1g, cmp_b1.reshape(2, 1, hid), cmp_w2.astype(BF16), cmp_b2.reshape(2, 1, hd),
      cmp_w2.astype(BF16).transpose(0, 2, 1), cmp_b2.reshape(2, hd, 1))


def _softmax_step(carry, s, v_t):
    m, acc = carry
    m_new = jnp.maximum(m, jnp.max(s, axis=0, keepdims=True))
    p = jnp.exp2(s - m_new).astype(BF16)
    acc = acc * jnp.exp2(m - m_new) + _dot(v_t, p)
    return m_new, acc


def _normalized(acc):
    return acc[:HEAD_DIM] / acc[HEAD_DIM:HEAD_DIM + 1]


def _nsa_kernel(qt_ref, kc_ref, vct_ref, ks_ref, vst_ref, kw_ref, vwt_ref, gate_ref,
                gsel_ref, gwin_ref, gcmp_ref, ovt_ref, o_ref, *, n_tiles):
    step = pl.program_id(1)
    ncp = kc_ref.shape[2]
    chains = [(t, g) for t in range(TILES_PER_STEP) for g in range(C_KV_GROUPS)]
    tile = [step * TILES_PER_STEP + t for t in range(TILES_PER_STEP)]
    qts = [qt_ref[0, g, t] for t, g in chains]

    back = WINDOW // WIN_KEYS
    n_win = WINDOW + Q_TILE
    first = [jnp.maximum(c - back, 0) for c in tile]
    s_cmp, s_win = [], []
    for k, (t, g) in enumerate(chains):
        y0 = pl.multiple_of((n_tiles - 1 - tile[t]) * (Q_TILE // CMP_STRIDE), SUBLANES)
        s_cmp.append(_dot(kc_ref[0, g], qts[k]) + gcmp_ref[g, pl.ds(y0, ncp), :])
    for k, (t, g) in enumerate(chains):
        start = pl.multiple_of(first[t] * WIN_KEYS, WIN_KEYS)
        rel = pl.multiple_of(jnp.maximum(back - tile[t], 0) * WIN_KEYS, WIN_KEYS)
        s_win.append(_dot(kw_ref[0, g, pl.ds(start, n_win), :], qts[k]) + gwin_ref[g, pl.ds(rel, n_win), :])

    tq = lax.broadcasted_iota(jnp.int32, (1, QL), 1) & (Q_TILE - 1)
    o_cmp, imp = [], []
    for k, (t, g) in enumerate(chains):
        e = jnp.exp2(s_cmp[k] - jnp.max(s_cmp[k], axis=0, keepdims=True))
        has_cmp = (tile[t] * Q_TILE + tq >= CMP_LEN - 1).astype(F32)
        p = e / jnp.sum(e, axis=0, keepdims=True) * has_cmp
        o_cmp.append(_dot(vct_ref[0, g], p.astype(BF16)))
        p_heads = p[:, 0:Q_TILE]
        for r in range(1, C_HPG):
            p_heads = p_heads + p[:, r * Q_TILE:(r + 1) * Q_TILE]
        p_hi = p_heads.astype(BF16)
        p_lo = (p_heads - p_hi.astype(F32)).astype(BF16)
        imp.append(_dot(ovt_ref[...], p_hi) + _dot(ovt_ref[...], p_lo))

    o_win = []
    for k, (t, g) in enumerate(chains):
        e = jnp.exp2(s_win[k] - jnp.max(s_win[k], axis=0, keepdims=True)).astype(BF16)
        acc = jnp.zeros((V_ROWS, QL), F32)
        for i in range(n_win // WIN_KEYS):
            acc = acc + _dot(vwt_ref[0, g, first[t] + i], e[i * WIN_KEYS:(i + 1) * WIN_KEYS])
        o_win.append(_normalized(acc))

    q_aug = _with_mask_rows(qts, [tile[t] for t, g in chains], imp)
    halves = range(SEL_KEYS // SEL_HALF)
    init = (jnp.full((1, QL), NEG, F32), jnp.zeros((V_ROWS, QL), F32))

    def sel_body(i, carry, far):
        off = pl.multiple_of(i * SEL_KEYS, SEL_KEYS)
        units = [(h, k) for h in halves for k in range(len(chains))]

        def logits_of(h, k):
            t, g = chains[k]
            logits = _dot(ks_ref[0, g, pl.ds(off + h * SEL_HALF, SEL_HALF), :], q_aug[k])
            if not far:
                x0 = SEL_NEAR_SPAN - tile[t] * Q_TILE
                logits = logits + gsel_ref[g, pl.ds(pl.multiple_of(x0 + off + h * SEL_HALF, Q_TILE), SEL_HALF), :]
            return logits

        carry = list(carry)
        s = [logits_of(*u) for u in units[:SEL_AHEAD]]
        for n, (h, k) in enumerate(units):
            if n + SEL_AHEAD < len(units):
                s.append(logits_of(*units[n + SEL_AHEAD]))
            t, g = chains[k]
            v_t = vst_ref[0, g, i][:, h * SEL_HALF:(h + 1) * SEL_HALF]
            carry[k] = _softmax_step(carry[k], s[n], v_t)
        return tuple(carry)

    n_far = jnp.maximum(tile[0] * Q_TILE - MAX_DISTANCE + 1, 0) // SEL_KEYS
    n_sel_steps = (tile[-1] * Q_TILE + Q_TILE + SEL_KEYS - 1) // SEL_KEYS
    sel = lax.fori_loop(0, n_far, functools.partial(sel_body, far=True), (init,) * len(chains))
    sel = tuple((m + gsel_ref[g, 0:1, :], acc) for (m, acc), (t, g) in zip(sel, chains))
    sel = lax.fori_loop(n_far, n_sel_steps, functools.partial(sel_body, far=False), sel)

    for t in range(TILES_PER_STEP):
        outs = []
        for k, (tk, g) in enumerate(chains):
            if tk != t:
                continue
            o_sel = _normalized(sel[k][1])
            tok = slice(t * Q_TILE, (t + 1) * Q_TILE)
            cols = []
            for r in range(C_HPG):
                ln = slice(r * Q_TILE, (r + 1) * Q_TILE)
                cols.append(gate_ref[0, 0, g, r:r + 1, tok] * o_cmp[k][:, ln]
                            + gate_ref[0, 1, g, r:r + 1, tok] * o_sel[:, ln]
                            + gate_ref[0, 2, g, r:r + 1, tok] * o_win[k][:, ln])
            outs += [jnp.concatenate(cols[2 * j:2 * j + 2], axis=0).T for j in range(C_HPG // 2)]
        o_ref[0, t * Q_TILE:(t + 1) * Q_TILE, :] = jnp.concatenate(outs, axis=1).astype(o_ref.dtype)


def _with_mask_rows(qts, tiles, imps):
    n_groups = SEL_SLOTS // SUBLANES
    j = lax.broadcasted_iota(jnp.int32, (SEL_SLOTS, Q_TILE), 0)
    half = lax.shift_right_logical(lax.broadcasted_iota(jnp.int32, (SEL_SLOTS, Q_TILE), 1), SEL_SHIFT)
    j_rows = j[:SUBLANES]
    scores = []
    for c, imp in zip(tiles, imps):
        blk = c * (Q_TILE // SEL_LEN) + half
        forced = (j == 0) | (j == blk) | (j == blk - 1)
        scores.append(jnp.where(j <= blk, jnp.where(forced, FORCE_SCORE, imp), -1.0))
    rows = [[sc[k * SUBLANES:(k + 1) * SUBLANES] for k in range(n_groups)] for sc in scores]

    def add_pair(ranks, jg, k):
        for ci, sc in enumerate(scores):
            acc = ranks[ci][k]
            for jp in range(jg * SUBLANES, (jg + 1) * SUBLANES):
                other = sc[jp:jp + 1, :]
                if k > jg:
                    beats = jnp.where(other >= rows[ci][k], 1, 0)
                elif k < jg:
                    beats = jnp.where(other > rows[ci][k], 1, 0)
                else:
                    beats = jnp.where(j_rows > jp - k * SUBLANES, jnp.where(other >= rows[ci][k], 1, 0),
                                      jnp.where(other > rows[ci][k], 1, 0))
                acc = acc + beats
            ranks[ci][k] = acc

    last_blk = tiles[-1] * (Q_TILE // SEL_LEN) + Q_TILE // SEL_LEN - 1
    ranks = [[jnp.zeros((SUBLANES, Q_TILE), jnp.int32) for _ in range(n_groups)] for _ in scores]
    for m in range(n_groups):
        def shell(ranks, m=m):
            ranks = [list(r) for r in ranks]
            for k in range(m + 1):
                add_pair(ranks, m, k)
            for jg in range(m):
                add_pair(ranks, jg, m)
            return ranks
        ranks = shell(ranks) if m == 0 else lax.cond(m * SUBLANES <= last_blk, shell, lambda r: r, ranks)

    out = []
    for qt, sc, rk in zip(qts, scores, ranks):
        rank = jnp.concatenate(rk, axis=0)
        mask_rows = jnp.where((rank < SEL_TOP) & (sc >= 0.0), 0.0, NEG).astype(BF16)
        out.append(jnp.concatenate([qt, jnp.concatenate([mask_rows] * C_HPG, axis=1)], axis=0))
    return out


def _nsa(qt, kc, vct, ks, vst, kw, vwt, gates, gsel, gwin, gcmp, ovt):
    b, G, n_tiles = qt.shape[:3]
    s = n_tiles * Q_TILE
    per_b = lambda a: pl.BlockSpec((1,) + a.shape[1:], lambda bi, c: (bi,) + (0,) * (a.ndim - 1))
    return pl.pallas_call(
        functools.partial(_nsa_kernel, n_tiles=n_tiles),
        grid=(b, n_tiles // TILES_PER_STEP),
        in_specs=[pl.BlockSpec((1, G, TILES_PER_STEP, HEAD_DIM, QL), lambda bi, c: (bi, 0, c, 0, 0)),
                  per_b(kc), per_b(vct), per_b(ks), per_b(vst), per_b(kw), per_b(vwt),
                  pl.BlockSpec((1, 3, G, C_HPG, TILES_PER_STEP * Q_TILE), lambda bi, c: (bi, 0, 0, 0, c)),
                  _resident(gsel.shape), _resident(gwin.shape), _resident(gcmp.shape),
                  _resident(ovt.shape)],
        out_specs=pl.BlockSpec((1, TILES_PER_STEP * Q_TILE, C_WIDTH), lambda bi, c: (bi, c, 0)),
        out_shape=jax.ShapeDtypeStruct((b, s, C_WIDTH), BF16),
        compiler_params=_cparams(2),
        name="nsa",
    )(qt, kc, vct, ks, vst, kw, vwt, gates, gsel, gwin, gcmp, ovt)


def _tail_kernel(h_ref, ya_ref, yb_ref, yc_ref, p_ref, norm_ref, woa_ref, wob_ref, woc_ref,
                 wg_ref, wu_ref, wd_ref, wpg_ref, wpp_ref, o_ref, acc_ref):
    subs = _sub_tiles(h_ref.shape[0])
    mix = [_dot(ya_ref[r, :], woa_ref[...]) + _dot(yb_ref[r, :], wob_ref[...]) + _dot(yc_ref[r, :], woc_ref[...])
           for r in subs]
    emb = [_dot(p_ref[r, :].astype(BF16), wpp_ref[...]) for r in subs]
    h = [h_ref[r, :] + _rms(m, norm_ref[3:4, :]) for r, m in zip(subs, mix)]
    h = _ffn_body(h, norm_ref[4:5, :], norm_ref[5:6, :], wg_ref, wu_ref, wd_ref, acc_ref)
    gate = [_dot(_rms(x, norm_ref[6:7, :]).astype(BF16), wpg_ref[...]) for x in h]
    for r, x, g, e in zip(subs, h, gate, emb):
        o_ref[r, :] = x + _rms(_sigmoid(g) * e, norm_ref[7:8, :])


def _tail(h, ya, yb, yc, p, norms, ffn_w, layer, w_out, w_gate, w_proj):
    n, d = h.shape
    dp = p.shape[-1]
    w = w_out.astype(BF16)
    woa, wob, woc = w[:A_WIDTH], w[A_WIDTH:A_WIDTH + B_WIDTH], w[A_WIDTH + B_WIDTH:]
    tile = lambda width: pl.BlockSpec((TOKEN_TILE, width), lambda i: (i, 0))
    return pl.pallas_call(
        _tail_kernel,
        grid=(n // TOKEN_TILE,),
        in_specs=[tile(d), tile(A_WIDTH), tile(B_WIDTH), tile(C_WIDTH),
                  pl.BlockSpec((None, TOKEN_TILE, dp), lambda i: (layer, i, 0)),
                  _resident(norms.shape), _resident(woa.shape), _resident(wob.shape), _resident(woc.shape),
                  *[_ffn_weight_spec(wt, layer, 1) for wt in ffn_w],
                  _resident((d, d)), _resident((dp, d))],
        out_specs=tile(d),
        out_shape=jax.ShapeDtypeStruct((n, d), F32),
        scratch_shapes=[pltpu.VMEM((TOKEN_TILE, d), F32)],
        compiler_params=_cparams(1),
        name="tail",
    )(h, ya, yb, yc, p, norms, woa, wob, woc, *ffn_w, w_gate.astype(BF16), w_proj.astype(BF16))


def _overlap_t(s):
    ncp = s // CMP_STRIDE
    n_cmp = (s - CMP_LEN) // CMP_STRIDE + 1
    cs = jnp.arange(ncp) * CMP_STRIDE
    ss = jnp.arange(s // SEL_LEN) * SEL_LEN
    ov = jnp.clip(jnp.minimum(cs[None] + CMP_LEN, ss[:, None] + SEL_LEN)
                  - jnp.maximum(cs[None], ss[:, None]), 0, None).astype(F32) / CMP_LEN
    ov = jnp.where(jnp.arange(ncp)[None] < n_cmp, ov, 0.0).astype(BF16)
    return jnp.pad(ov, ((0, SEL_SLOTS - s // SEL_LEN), (0, 0)))


def kernel(x, p, rel_bias, norm_g, ffn_w_gate, ffn_w_up, ffn_w_down, w_in, w_out, sgu_norm_g, sgu_w, sgu_b,
           conv_w, conv_b, lru_wa, lru_ba, lru_wx, lru_bx, lru_lambda, cmp_pos, cmp_w1, cmp_b1, cmp_w2,
           cmp_b2, ple_w_gate, ple_w_proj):
    b, s, d = x.shape
    depth = norm_g.shape[0]
    assert s % TOKEN_TILE == 0 and s % SEL_KEYS == 0
    assert s % (TILES_PER_STEP * Q_TILE) == 0
    assert s >= WINDOW + Q_TILE and SEL_TOP <= s // SEL_LEN <= SEL_SLOTS
    n_tiles = s // Q_TILE

    rbx = jnp.repeat(rel_bias.reshape(N_BUCKETS, C_KV_GROUPS, C_HPG).transpose(1, 0, 2), Q_TILE, axis=2)
    no_limit = 1 << 30
    gsel = _bias_table(rbx, SEL_NEAR_SPAN + Q_TILE + SEL_KEYS, 1, SEL_NEAR_SPAN, no_limit)
    gwin = _bias_table(rbx, 2 * WINDOW + Q_TILE, 1, WINDOW, WINDOW)
    per_tile = Q_TILE // CMP_STRIDE
    gcmp = _bias_table(rbx, per_tile * (n_tiles - 1) + s // CMP_STRIDE, CMP_STRIDE,
                       CMP_STRIDE * per_tile * (n_tiles - 1) - (CMP_LEN - 1), no_limit)
    ovt = _overlap_t(s)

    ffn_w = (ffn_w_gate.astype(BF16), ffn_w_up.astype(BF16), ffn_w_down.astype(BF16))
    h = x
    flat = lambda a: a.reshape(b * s, -1)
    for i in range(depth):
        h, ya, yb, qt, raw, ks, kw, vst, vwt, gates = _head(
            h, norm_g[i], ffn_w, i, w_in[i], sgu_norm_g[i], sgu_w[i], sgu_b[i],
            (conv_w[i], conv_b[i], lru_wa[i], lru_ba[i], lru_wx[i], lru_bx[i], lru_lambda[i]))
        kc, kct = _compress(raw, cmp_pos[i], cmp_w1[i], cmp_b1[i], cmp_w2[i], cmp_b2[i])
        yc = _nsa(qt, kc[0], kct[1], ks, vst, kw, vwt, gates, gsel, gwin, gcmp, ovt)
        h = _tail(flat(h), flat(ya), flat(yb), flat(yc), p.reshape(depth, b * s, -1), norm_g[i], ffn_w, i,
                  w_out[i], ple_w_gate[i], ple_w_proj[i]).reshape(b, s, d)
    return h
```

```python
import functools
import math

import jax
import jax.numpy as jnp
from jax import lax
from jax.experimental import pallas as pl
from jax.experimental.pallas import tpu as pltpu

F32 = jnp.float32
BF16 = jnp.bfloat16

RMS_EPS = 1e-6
A_GROUPS = 4
A_WIDTH = 256
A_CHUNK = 128
B_WIDTH = 256
CONV_W = 4
LRU_C = 8.0
C_HEADS = 8
C_KV_GROUPS = 2
C_HPG = C_HEADS // C_KV_GROUPS
HEAD_DIM = 64
C_WIDTH = C_HEADS * HEAD_DIM
KV_W = C_KV_GROUPS * HEAD_DIM
CMP_LEN = 32
CMP_STRIDE = 16
SEL_LEN = 64
SEL_SHIFT = 6
SEL_SLOTS = 64
SEL_TOP = 16
WINDOW = 512
FORCE_SCORE = 1e4
NEG = -1e30
N_BUCKETS = 32
MAX_DISTANCE = 1024

LANES = 128
TOKEN_TILE = 512
SUB_TILES = 2
FFN_CHUNK = 256
SUBLANES = 8
Q_TILE = 128
TILES_PER_STEP = 4
SEL_KEYS = 512
SEL_HALF = 256
SEL_AHEAD = 2
SEL_NEAR_SPAN = -(-(TILES_PER_STEP * Q_TILE + MAX_DISTANCE + SEL_KEYS - 2) // Q_TILE) * Q_TILE
WIN_KEYS = 128
QL = C_HPG * Q_TILE
V_PAD_ROWS = 16
V_ROWS = HEAD_DIM + V_PAD_ROWS
LOG2E = math.log2(math.e)
TABLE_ROWS = 128
VMEM_LIMIT = 56 * 1024 * 1024


def _cparams(n_axes):
    return pltpu.CompilerParams(dimension_semantics=("arbitrary",) * n_axes,
                                vmem_limit_bytes=VMEM_LIMIT)


def _resident(shape):
    nd = len(shape)
    return pl.BlockSpec(shape, lambda *_: (0,) * nd, pipeline_mode=pl.Buffered(1))


def _rms(x, g):
    return x * lax.rsqrt(jnp.mean(x * x, axis=-1, keepdims=True) + RMS_EPS) * g


def _sigmoid(x):
    return 1.0 / (1.0 + jnp.exp(-x))


def _dot(a, b):
    return jnp.dot(a, b, preferred_element_type=F32)


def _bias_table_kernel(rbx_ref, o_ref, *, stride, offset, dmax):
    i = pl.program_id(1)
    shape = (TABLE_ROWS, QL)
    d_hi = (Q_TILE - 1) - stride * (i * TABLE_ROWS) + offset
    d_lo = -stride * (i * TABLE_ROWS + TABLE_ROWS - 1) + offset
    masked = (d_hi < 0) | (d_lo >= dmax)
    far = (d_lo >= MAX_DISTANCE) & (d_hi < dmax)

    @pl.when(masked)
    def _():
        o_ref[0] = jnp.full(shape, NEG, F32)

    @pl.when(far)
    def _():
        o_ref[0] = jnp.broadcast_to(rbx_ref[0, N_BUCKETS - 1:N_BUCKETS, :] * LOG2E, shape)

    @pl.when(jnp.logical_not(masked | far))
    def _():
        x = lax.broadcasted_iota(jnp.int32, shape, 0) + i * TABLE_ROWS
        t = lax.broadcasted_iota(jnp.int32, shape, 1) & (Q_TILE - 1)
        d = t - stride * x + offset
        n = jnp.maximum(d, 0)
        max_exact = N_BUCKETS // 2
        nf = jnp.maximum(n, max_exact).astype(F32)
        large = max_exact + (jnp.log(nf / max_exact) / math.log(MAX_DISTANCE / max_exact)
                             * (N_BUCKETS - max_exact)).astype(jnp.int32)
        large = jnp.minimum(large, N_BUCKETS - 1)
        bucket = jnp.where(n < max_exact, n, large)
        acc = jnp.zeros(shape, F32)
        for k in range(N_BUCKETS):
            acc = jnp.where(bucket == k, rbx_ref[0, k:k + 1, :], acc)
        o_ref[0] = jnp.where((d >= 0) & (d < dmax), acc * LOG2E, NEG)


def _bias_table(rbx, rows, stride, offset, dmax):
    rows_p = -(-rows // TABLE_ROWS) * TABLE_ROWS
    return pl.pallas_call(
        functools.partial(_bias_table_kernel, stride=stride, offset=offset, dmax=dmax),
        grid=(C_KV_GROUPS, rows_p // TABLE_ROWS),
        in_specs=[pl.BlockSpec((1, N_BUCKETS, QL), lambda g, i: (g, 0, 0))],
        out_specs=pl.BlockSpec((1, TABLE_ROWS, QL), lambda g, i: (g, i, 0)),
        out_shape=jax.ShapeDtypeStruct((C_KV_GROUPS, rows_p, QL), F32),
        compiler_params=_cparams(2),
        name="bias_table",
    )(rbx)


def _sub_tiles(n_rows):
    rows = n_rows // SUB_TILES
    return [slice(k * rows, (k + 1) * rows) for k in range(SUB_TILES)]


def _ffn_body(xs, g_pre, g_post, wg_ref, wu_ref, wd_ref, acc_ref):
    subs = _sub_tiles(acc_ref.shape[0])
    xn = [_rms(x, g_pre).astype(BF16) for x in xs]
    nch = wg_ref.shape[1] // FFN_CHUNK
    cols = lambda j: slice(j * FFN_CHUNK, (j + 1) * FFN_CHUNK)
    gate_up = [(_dot(x, wg_ref[:, cols(0)]), _dot(x, wu_ref[:, cols(0)])) for x in xn]
    for j in range(nch):
        if j + 1 < nch:
            nxt = [(_dot(x, wg_ref[:, cols(j + 1)]), _dot(x, wu_ref[:, cols(j + 1)])) for x in xn]
        for rows, (gate, up) in zip(subs, gate_up):
            hid = (gate * _sigmoid(gate) * up).astype(BF16)
            down = _dot(hid, wd_ref[cols(j), :])
            if j == 0:
                acc_ref[rows, :] = down
            else:
                acc_ref[rows, :] += down
        gate_up = nxt
    return [x + 0.5 * _rms(acc_ref[rows, :], g_post) for x, rows in zip(xs, subs)]


def _ffn_weight_spec(w, layer, which):
    return pl.BlockSpec((None, None) + w.shape[2:], lambda *_: (layer, which, 0, 0), pipeline_mode=pl.Buffered(1))


def _head_kernel(h_ref, norm_ref, wg_ref, wu_ref, wd_ref, wa_ref, wb_ref, wq_ref, wkv_ref, wgt_ref,
                 sgn_ref, sgw_ref, sgb_ref, cw_ref, cb_ref, lwa_ref, lba_ref, lwx_ref, lbx_ref, lam_ref,
                 h_out_ref, ya_ref, yb_ref, qt_ref, raw_ref, ks_ref, kw_ref, vst_ref, vwt_ref, gt_ref,
                 acc_ref, tail_ref, state_ref, a_ref, b_ref):
    @pl.when(pl.program_id(1) == 0)
    def _():
        tail_ref[...] = jnp.zeros_like(tail_ref)
        state_ref[...] = jnp.zeros_like(state_ref)

    h = _ffn_body([h_ref[0, r, :] for r in _sub_tiles(TOKEN_TILE)], norm_ref[0:1, :], norm_ref[1:2, :],
                  wg_ref, wu_ref, wd_ref, acc_ref)
    h = jnp.concatenate(h, axis=0)
    h_out_ref[0] = h
    xn = _rms(h, norm_ref[2:3, :]).astype(BF16)

    zb = _dot(xn, wb_ref[...])
    za = _dot(xn, wa_ref[...])
    zq = _dot(xn, wq_ref[...])
    zkv = _dot(xn, wkv_ref[...])
    zg = _dot(xn, wgt_ref[...])

    gate_b = _rglru_coefficients(zb, cw_ref, cb_ref, lwa_ref, lba_ref, lwx_ref, lbx_ref, lam_ref,
                                 tail_ref, state_ref, a_ref, b_ref)

    u = jax.nn.gelu(za[:, :A_WIDTH])
    v = _rms(jax.nn.gelu(za[:, A_WIDTH:]), sgn_ref[...]).astype(BF16)
    row = lax.broadcasted_iota(jnp.int32, (A_CHUNK, A_CHUNK), 0)
    col = lax.broadcasted_iota(jnp.int32, (A_CHUNK, A_CHUNK), 1)
    lane_group = lax.shift_right_logical(lax.broadcasted_iota(jnp.int32, (A_CHUNK, A_WIDTH), 1),
                                          (A_WIDTH // A_GROUPS).bit_length() - 1)
    w_tril = [jnp.where(row >= col, sgw_ref[g], 0.0).astype(BF16) for g in range(A_GROUPS)]
    for c in range(TOKEN_TILE // A_CHUNK):
        rows = slice(c * A_CHUNK, (c + 1) * A_CHUNK)
        mixed = jnp.zeros((A_CHUNK, A_WIDTH), F32)
        for g in range(A_GROUPS):
            mixed = jnp.where(lane_group == g, _dot(w_tril[g], v[rows]), mixed)
        ya_ref[0, rows, :] = (u[rows] * (mixed + sgb_ref[...])).astype(ya_ref.dtype)

    zq_t = (zq * (HEAD_DIM ** -0.5 * LOG2E)).T
    for g in range(C_KV_GROUPS):
        for c in range(TOKEN_TILE // Q_TILE):
            parts = []
            for r in range(C_HPG):
                base = (g * C_HPG + r) * HEAD_DIM
                parts.append(zq_t[base:base + HEAD_DIM, c * Q_TILE:(c + 1) * Q_TILE])
            qt_ref[0, g, c] = jnp.concatenate(parts, axis=1).astype(qt_ref.dtype)

    vs_t = zkv[:, 3 * KV_W:4 * KV_W].T
    vw_t = zkv[:, 5 * KV_W:6 * KV_W].T
    key_blk = lax.shift_right_logical(
        lax.broadcasted_iota(jnp.int32, (TOKEN_TILE, SEL_SLOTS), 0) + pl.program_id(1) * TOKEN_TILE, SEL_SHIFT)
    blk_onehot = jnp.where(key_blk == lax.broadcasted_iota(jnp.int32, (TOKEN_TILE, SEL_SLOTS), 1), 1.0, 0.0)
    ones_rows = jnp.where(lax.broadcasted_iota(jnp.int32, (V_PAD_ROWS, SEL_KEYS), 0) == 0, 1.0, 0.0)
    raw_ref[0, 0] = zkv[:, :KV_W]
    raw_ref[1, 0] = zkv[:, KV_W:2 * KV_W]
    for g in range(C_KV_GROUPS):
        lo, hi = g * HEAD_DIM, (g + 1) * HEAD_DIM
        ks_ref[0, g] = jnp.concatenate([zkv[:, 2 * KV_W + lo:2 * KV_W + hi], blk_onehot],
                                       axis=1).astype(ks_ref.dtype)
        kw_ref[0, g] = zkv[:, 4 * KV_W + lo:4 * KV_W + hi].astype(kw_ref.dtype)
        for c in range(TOKEN_TILE // SEL_KEYS):
            vst_ref[0, g, c] = jnp.concatenate(
                [vs_t[lo:hi, c * SEL_KEYS:(c + 1) * SEL_KEYS], ones_rows[:, :SEL_KEYS]], axis=0).astype(vst_ref.dtype)
        for c in range(TOKEN_TILE // WIN_KEYS):
            vwt_ref[0, g, c] = jnp.concatenate(
                [vw_t[lo:hi, c * WIN_KEYS:(c + 1) * WIN_KEYS], ones_rows[:, :WIN_KEYS]], axis=0).astype(vwt_ref.dtype)

    sg_t = _sigmoid(zg).T
    for br in range(3):
        for g in range(C_KV_GROUPS):
            base = br * C_HEADS + g * C_HPG
            gt_ref[0, br, g] = sg_t[base:base + C_HPG, :]

    yb_ref[0] = (_rglru_scan(state_ref, a_ref, b_ref) * jax.nn.gelu(gate_b)).astype(yb_ref.dtype)


def _head(h, norms, ffn_w, layer, w_in, sgu_norm_g, sgu_w, sgu_b, lru):
    b, s, d = h.shape
    assert ffn_w[0].shape[-1] % FFN_CHUNK == 0
    wb16 = w_in.astype(BF16)
    o = 0
    wa = wb16[:, o:o + 2 * A_WIDTH]; o += 2 * A_WIDTH
    wb = wb16[:, o:o + 2 * B_WIDTH]; o += 2 * B_WIDTH
    wq = wb16[:, o:o + C_WIDTH]; o += C_WIDTH
    wkv = wb16[:, o:o + 6 * KV_W]; o += 6 * KV_W
    wgt = jnp.pad(wb16[:, o:o + 3 * C_HEADS], ((0, 0), (0, LANES - 3 * C_HEADS)))
    sgb = jnp.repeat(sgu_b.T, A_WIDTH // A_GROUPS, axis=1)
    nt = s // TOKEN_TILE
    grid = (b, nt)
    G = C_KV_GROUPS
    out_shape = [
        jax.ShapeDtypeStruct((b, s, d), F32),
        jax.ShapeDtypeStruct((b, s, A_WIDTH), BF16),
        jax.ShapeDtypeStruct((b, s, B_WIDTH), BF16),
        jax.ShapeDtypeStruct((b, G, s // Q_TILE, HEAD_DIM, QL), BF16),
        jax.ShapeDtypeStruct((2, b, s, KV_W), F32),
        jax.ShapeDtypeStruct((b, G, s, HEAD_DIM + SEL_SLOTS), BF16),
        jax.ShapeDtypeStruct((b, G, s, HEAD_DIM), BF16),
        jax.ShapeDtypeStruct((b, G, s // SEL_KEYS, V_ROWS, SEL_KEYS), BF16),
        jax.ShapeDtypeStruct((b, G, s // WIN_KEYS, V_ROWS, WIN_KEYS), BF16),
        jax.ShapeDtypeStruct((b, 3, G, C_HPG, s), F32),
    ]
    out_specs = [
        pl.BlockSpec((1, TOKEN_TILE, d), lambda bi, i: (bi, i, 0)),
        pl.BlockSpec((1, TOKEN_TILE, A_WIDTH), lambda bi, i: (bi, i, 0)),
        pl.BlockSpec((1, TOKEN_TILE, B_WIDTH), lambda bi, i: (bi, i, 0)),
        pl.BlockSpec((1, G, TOKEN_TILE // Q_TILE, HEAD_DIM, QL), lambda bi, i: (bi, 0, i, 0, 0)),
        pl.BlockSpec((2, 1, TOKEN_TILE, KV_W), lambda bi, i: (0, bi, i, 0)),
        pl.BlockSpec((1, G, TOKEN_TILE, HEAD_DIM + SEL_SLOTS), lambda bi, i: (bi, 0, i, 0)),
        pl.BlockSpec((1, G, TOKEN_TILE, HEAD_DIM), lambda bi, i: (bi, 0, i, 0)),
        pl.BlockSpec((1, G, TOKEN_TILE // SEL_KEYS, V_ROWS, SEL_KEYS), lambda bi, i: (bi, 0, i, 0, 0)),
        pl.BlockSpec((1, G, TOKEN_TILE // WIN_KEYS, V_ROWS, WIN_KEYS), lambda bi, i: (bi, 0, i, 0, 0)),
        pl.BlockSpec((1, 3, G, C_HPG, TOKEN_TILE), lambda bi, i: (bi, 0, 0, 0, i)),
    ]
    in_specs = [
        pl.BlockSpec((1, TOKEN_TILE, d), lambda bi, i: (bi, i, 0)),
        _resident(norms.shape), *[_ffn_weight_spec(w, layer, 0) for w in ffn_w],
        _resident(wa.shape), _resident(wb.shape), _resident(wq.shape),
        _resident(wkv.shape), _resident(wgt.shape), _resident((1, A_WIDTH)),
        _resident(sgu_w.shape), _resident(sgb.shape),
        _resident((CONV_W, B_WIDTH)), _resident((1, B_WIDTH)), _resident((B_WIDTH, B_WIDTH)),
        _resident((1, B_WIDTH)), _resident((B_WIDTH, B_WIDTH)), _resident((1, B_WIDTH)), _resident((1, B_WIDTH)),
    ]
    conv_w, conv_b, lru_wa, lru_ba, lru_wx, lru_bx, lru_lambda = lru
    row = lambda v: v.reshape(1, B_WIDTH)
    return pl.pallas_call(
        _head_kernel, grid=grid, in_specs=in_specs, out_specs=out_specs, out_shape=out_shape,
        scratch_shapes=[pltpu.VMEM((TOKEN_TILE, d), F32),
                        pltpu.VMEM((SUBLANES, B_WIDTH), F32), pltpu.VMEM((SUBLANES, B_WIDTH), F32),
                        pltpu.VMEM((TOKEN_TILE, B_WIDTH), F32), pltpu.VMEM((TOKEN_TILE, B_WIDTH), F32)],
        compiler_params=_cparams(2), name="head",
    )(h, norms, *ffn_w, wa, wb, wq, wkv, wgt, sgu_norm_g.reshape(1, A_WIDTH), sgu_w, sgb,
      conv_w, row(conv_b), _block_diag(lru_wa).astype(BF16), row(lru_ba),
      _block_diag(lru_wx).astype(BF16), row(lru_bx), row(lru_lambda))


def _rglru_coefficients(zb, cw_ref, cb_ref, wa_ref, ba_ref, wx_ref, bxb_ref, lam_ref,
                        tail_ref, state_ref, a_ref, b_ref):
    xb = zb[:, :B_WIDTH]
    ext = jnp.concatenate([tail_ref[...], xb], axis=0)
    xc = cb_ref[...] + xb * cw_ref[CONV_W - 1:CONV_W, :]
    for k in range(CONV_W - 1):
        shift = CONV_W - 1 - k
        xc = xc + ext[SUBLANES - shift:SUBLANES - shift + TOKEN_TILE] * cw_ref[k:k + 1, :]
    tail_ref[...] = xb[TOKEN_TILE - SUBLANES:]

    xcb = xc.astype(BF16)
    r = _sigmoid(_dot(xcb, wa_ref[...]) + ba_ref[...])
    i = _sigmoid(_dot(xcb, wx_ref[...]) + bxb_ref[...])
    z = -lam_ref[...]
    e = jnp.exp(-jnp.abs(z))
    softplus = jnp.maximum(z, 0.0) + jnp.log1p(e)
    log_a = -LRU_C * r * softplus
    a = jnp.exp(log_a)
    b = jnp.sqrt(jnp.tanh(-log_a) * (a * a + 1.0)) * (i * xc)

    row = lax.broadcasted_iota(jnp.int32, a.shape, 0) & (SUBLANES - 1)
    for dist in (1, 2, 4):
        a_prev = jnp.where(row >= dist, pltpu.roll(a, dist, 0), 1.0)
        b_prev = jnp.where(row >= dist, pltpu.roll(b, dist, 0), 0.0)
        b = a * b_prev + b
        a = a * a_prev
    a_ref[...] = a
    b_ref[...] = b
    return zb[:, B_WIDTH:]


def _rglru_scan(state_ref, a_ref, b_ref):
    def body(k, h):
        off = pl.multiple_of(k * SUBLANES, SUBLANES)
        rows = pl.ds(off, SUBLANES)
        hs = b_ref[rows, :] + a_ref[rows, :] * h
        b_ref[rows, :] = hs
        return jnp.broadcast_to(hs[SUBLANES - 1:SUBLANES, :], hs.shape)

    state_ref[...] = lax.fori_loop(0, TOKEN_TILE // SUBLANES, body, state_ref[...])
    return b_ref[...]


def _block_diag(w):
    g, n, _ = w.shape
    out = jnp.zeros((g * n, g * n), w.dtype)
    for k in range(g):
        out = out.at[k * n:(k + 1) * n, k * n:(k + 1) * n].set(w[k])
    return out


def _compress_kernel(raw_ref, pos_ref, w1_ref, b1_ref, w2_ref, b2_ref, w2t_ref, b2c_ref, kc_ref, kct_ref):
    ncp = raw_ref.shape[2] // CMP_STRIDE
    hid_dim = w1_ref.shape[-1] // C_KV_GROUPS
    u = jnp.zeros((ncp, w1_ref.shape[-1]), F32)
    v = jnp.zeros((ncp, w1_ref.shape[-1]), F32)
    for l in range(CMP_STRIDE):
        x = raw_ref[0, 0, pl.ds(l, ncp, stride=CMP_STRIDE), :]
        top = (x + pos_ref[0, l:l + 1, :]).astype(BF16)
        bot = (x + pos_ref[0, CMP_STRIDE + l:CMP_STRIDE + l + 1, :]).astype(BF16)
        u = u + _dot(top, w1_ref[0, l])
        v = v + _dot(bot, w1_ref[0, CMP_STRIDE + l])
    pre = u + pltpu.roll(v, ncp - 1, 0)
    for g in range(C_KV_GROUPS):
        hid = jax.nn.gelu(pre[:, g * hid_dim:(g + 1) * hid_dim] + b1_ref[0]).astype(BF16)
        kc_ref[0, 0, g] = (_dot(hid, w2_ref[0]) + b2_ref[0]).astype(kc_ref.dtype)
        t = lax.dot_general(w2t_ref[0], hid, (((1,), (1,)), ((), ())), preferred_element_type=F32)
        kct_ref[0, 0, g] = (t + b2c_ref[0]).astype(kct_ref.dtype)


def _compress(raw, cmp_pos, cmp_w1, cmp_b1, cmp_w2, cmp_b2):
    _, b, s, _ = raw.shape
    g, hd = C_KV_GROUPS, HEAD_DIM
    ncp = s // CMP_STRIDE
    hid = cmp_w1.shape[-1]
    w1 = cmp_w1.astype(BF16).reshape(2, CMP_LEN, hd, hid)
    zero = jnp.zeros_like(w1)
    w1g = jnp.concatenate([jnp.concatenate([w1 if k == gi else zero for k in range(g)], axis=2)
                           for gi in range(g)], axis=3)
    pos = jnp.tile(cmp_pos, (1, 1, g))
    sel = lambda *shape: pl.BlockSpec((1,) + shape, lambda kv, bi: (kv,) + (0,) * len(shape))
    return pl.pallas_call(
        _compress_kernel,
        grid=(2, b),
        in_specs=[pl.BlockSpec((1, 1, s, g * hd), lambda kv, bi: (kv, bi, 0, 0)),
                  sel(CMP_LEN, g * hd), sel(CMP_LEN, g * hd, g * hid), sel(1, hid), sel(hid, hd), sel(1, hd),
                  sel(hd, hid), sel(hd, 1)],
        out_specs=[pl.BlockSpec((1, 1, g, ncp, hd), lambda kv, bi: (kv, bi, 0, 0, 0)),
                   pl.BlockSpec((1, 1, g, hd, ncp), lambda kv, bi: (kv, bi, 0, 0, 0))],
        out_shape=[jax.ShapeDtypeStruct((2, b, g, ncp, hd), BF16),
                   jax.ShapeDtypeStruct((2, b, g, hd, ncp), BF16)],
        compiler_params=_cparams(2),
        name="compress",
    )(raw, pos, w1g, cmp_b1.reshape(2, 1, hid), cmp_w2.astype(BF16), cmp_b2.reshape(2, 1, hd),
      cmp_w2.astype(BF16).transpose(0, 2, 1), cmp_b2.reshape(2, hd, 1))


def _softmax_step(carry, s, v_t):
    m, acc = carry
    m_new = jnp.maximum(m, jnp.max(s, axis=0, keepdims=True))
    p = jnp.exp2(s - m_new).astype(BF16)
    acc = acc * jnp.exp2(m - m_new) + _dot(v_t, p)
    return m_new, acc


def _normalized(acc):
    return acc[:HEAD_DIM] / acc[HEAD_DIM:HEAD_DIM + 1]


def _nsa_kernel(qt_ref, kc_ref, vct_ref, ks_ref, vst_ref, kw_ref, vwt_ref, gate_ref,
                gsel_ref, gwin_ref, gcmp_ref, ovt_ref, o_ref, *, n_tiles):
    step = pl.program_id(1)
    ncp = kc_ref.shape[2]
    chains = [(t, g) for t in range(TILES_PER_STEP) for g in range(C_KV_GROUPS)]
    tile = [step * TILES_PER_STEP + t for t in range(TILES_PER_STEP)]
    qts = [qt_ref[0, g, t] for t, g in chains]

    back = WINDOW // WIN_KEYS
    n_win = WINDOW + Q_TILE
    first = [jnp.maximum(c - back, 0) for c in tile]
    s_cmp, s_win = [], []
    for k, (t, g) in enumerate(chains):
        y0 = pl.multiple_of((n_tiles - 1 - tile[t]) * (Q_TILE // CMP_STRIDE), SUBLANES)
        s_cmp.append(_dot(kc_ref[0, g], qts[k]) + gcmp_ref[g, pl.ds(y0, ncp), :])
    for k, (t, g) in enumerate(chains):
        start = pl.multiple_of(first[t] * WIN_KEYS, WIN_KEYS)
        rel = pl.multiple_of(jnp.maximum(back - tile[t], 0) * WIN_KEYS, WIN_KEYS)
        s_win.append(_dot(kw_ref[0, g, pl.ds(start, n_win), :], qts[k]) + gwin_ref[g, pl.ds(rel, n_win), :])

    tq = lax.broadcasted_iota(jnp.int32, (1, QL), 1) & (Q_TILE - 1)
    o_cmp, imp = [], []
    for k, (t, g) in enumerate(chains):
        e = jnp.exp2(s_cmp[k] - jnp.max(s_cmp[k], axis=0, keepdims=True))
        has_cmp = (tile[t] * Q_TILE + tq >= CMP_LEN - 1).astype(F32)
        p = e / jnp.sum(e, axis=0, keepdims=True) * has_cmp
        o_cmp.append(_dot(vct_ref[0, g], p.astype(BF16)))
        p_heads = p[:, 0:Q_TILE]
        for r in range(1, C_HPG):
            p_heads = p_heads + p[:, r * Q_TILE:(r + 1) * Q_TILE]
        p_hi = p_heads.astype(BF16)
        p_lo = (p_heads - p_hi.astype(F32)).astype(BF16)
        imp.append(_dot(ovt_ref[...], p_hi) + _dot(ovt_ref[...], p_lo))

    o_win = []
    for k, (t, g) in enumerate(chains):
        e = jnp.exp2(s_win[k] - jnp.max(s_win[k], axis=0, keepdims=True)).astype(BF16)
        acc = jnp.zeros((V_ROWS, QL), F32)
        for i in range(n_win // WIN_KEYS):
            acc = acc + _dot(vwt_ref[0, g, first[t] + i], e[i * WIN_KEYS:(i + 1) * WIN_KEYS])
        o_win.append(_normalized(acc))

    q_aug = _with_mask_rows(qts, [tile[t] for t, g in chains], imp)
    halves = range(SEL_KEYS // SEL_HALF)
    init = (jnp.full((1, QL), NEG, F32), jnp.zeros((V_ROWS, QL), F32))

    def sel_body(i, carry, far):
        off = pl.multiple_of(i * SEL_KEYS, SEL_KEYS)
        units = [(h, k) for h in halves for k in range(len(chains))]

        def logits_of(h, k):
            t, g = chains[k]
            logits = _dot(ks_ref[0, g, pl.ds(off + h * SEL_HALF, SEL_HALF), :], q_aug[k])
            if not far:
                x0 = SEL_NEAR_SPAN - tile[t] * Q_TILE
                logits = logits + gsel_ref[g, pl.ds(pl.multiple_of(x0 + off + h * SEL_HALF, Q_TILE), SEL_HALF), :]
            return logits

        carry = list(carry)
        s = [logits_of(*u) for u in units[:SEL_AHEAD]]
        for n, (h, k) in enumerate(units):
            if n + SEL_AHEAD < len(units):
                s.append(logits_of(*units[n + SEL_AHEAD]))
            t, g = chains[k]
            v_t = vst_ref[0, g, i][:, h * SEL_HALF:(h + 1) * SEL_HALF]
            carry[k] = _softmax_step(carry[k], s[n], v_t)
        return tuple(carry)

    n_far = jnp.maximum(tile[0] * Q_TILE - MAX_DISTANCE + 1, 0) // SEL_KEYS
    n_sel_steps = (tile[-1] * Q_TILE + Q_TILE + SEL_KEYS - 1) // SEL_KEYS
    sel = lax.fori_loop(0, n_far, functools.partial(sel_body, far=True), (init,) * len(chains))
    sel = tuple((m + gsel_ref[g, 0:1, :], acc) for (m, acc), (t, g) in zip(sel, chains))
    sel = lax.fori_loop(n_far, n_sel_steps, functools.partial(sel_body, far=False), sel)

    for t in range(TILES_PER_STEP):
        outs = []
        for k, (tk, g) in enumerate(chains):
            if tk != t:
                continue
            o_sel = _normalized(sel[k][1])
            tok = slice(t * Q_TILE, (t + 1) * Q_TILE)
            cols = []
            for r in range(C_HPG):
                ln = slice(r * Q_TILE, (r + 1) * Q_TILE)
                cols.append(gate_ref[0, 0, g, r:r + 1, tok] * o_cmp[k][:, ln]
                            + gate_ref[0, 1, g, r:r + 1, tok] * o_sel[:, ln]
                            + gate_ref[0, 2, g, r:r + 1, tok] * o_win[k][:, ln])
            outs += [jnp.concatenate(cols[2 * j:2 * j + 2], axis=0).T for j in range(C_HPG // 2)]
        o_ref[0, t * Q_TILE:(t + 1) * Q_TILE, :] = jnp.concatenate(outs, axis=1).astype(o_ref.dtype)


def _with_mask_rows(qts, tiles, imps):
    n_groups = SEL_SLOTS // SUBLANES
    j = lax.broadcasted_iota(jnp.int32, (SEL_SLOTS, Q_TILE), 0)
    half = lax.shift_right_logical(lax.broadcasted_iota(jnp.int32, (SEL_SLOTS, Q_TILE), 1), SEL_SHIFT)
    j_rows = j[:SUBLANES]
    scores = []
    for c, imp in zip(tiles, imps):
        blk = c * (Q_TILE // SEL_LEN) + half
        forced = (j == 0) | (j == blk) | (j == blk - 1)
        scores.append(jnp.where(j <= blk, jnp.where(forced, FORCE_SCORE, imp), -1.0))
    rows = [[sc[k * SUBLANES:(k + 1) * SUBLANES] for k in range(n_groups)] for sc in scores]

    def add_pair(ranks, jg, k):
        for ci, sc in enumerate(scores):
            acc = ranks[ci][k]
            for jp in range(jg * SUBLANES, (jg + 1) * SUBLANES):
                other = sc[jp:jp + 1, :]
                if k > jg:
                    beats = jnp.where(other >= rows[ci][k], 1, 0)
                elif k < jg:
                    beats = jnp.where(other > rows[ci][k], 1, 0)
                else:
                    beats = jnp.where(j_rows > jp - k * SUBLANES, jnp.where(other >= rows[ci][k], 1, 0),
                                      jnp.where(other > rows[ci][k], 1, 0))
                acc = acc + beats
            ranks[ci][k] = acc

    last_blk = tiles[-1] * (Q_TILE // SEL_LEN) + Q_TILE // SEL_LEN - 1
    ranks = [[jnp.zeros((SUBLANES, Q_TILE), jnp.int32) for _ in range(n_groups)] for _ in scores]
    for m in range(n_groups):
        def shell(ranks, m=m):
            ranks = [list(r) for r in ranks]
            for k in range(m + 1):
                add_pair(ranks, m, k)
            for jg in range(m):
                add_pair(ranks, jg, m)
            return ranks
        ranks = shell(ranks) if m == 0 else lax.cond(m * SUBLANES <= last_blk, shell, lambda r: r, ranks)

    out = []
    for qt, sc, rk in zip(qts, scores, ranks):
        rank = jnp.concatenate(rk, axis=0)
        mask_rows = jnp.where((rank < SEL_TOP) & (sc >= 0.0), 0.0, NEG).astype(BF16)
        out.append(jnp.concatenate([qt, jnp.concatenate([mask_rows] * C_HPG, axis=1)], axis=0))
    return out


def _nsa(qt, kc, vct, ks, vst, kw, vwt, gates, gsel, gwin, gcmp, ovt):
    b, G, n_tiles = qt.shape[:3]
    s = n_tiles * Q_TILE
    per_b = lambda a: pl.BlockSpec((1,) + a.shape[1:], lambda bi, c: (bi,) + (0,) * (a.ndim - 1))
    return pl.pallas_call(
        functools.partial(_nsa_kernel, n_tiles=n_tiles),
        grid=(b, n_tiles // TILES_PER_STEP),
        in_specs=[pl.BlockSpec((1, G, TILES_PER_STEP, HEAD_DIM, QL), lambda bi, c: (bi, 0, c, 0, 0)),
                  per_b(kc), per_b(vct), per_b(ks), per_b(vst), per_b(kw), per_b(vwt),
                  pl.BlockSpec((1, 3, G, C_HPG, TILES_PER_STEP * Q_TILE), lambda bi, c: (bi, 0, 0, 0, c)),
                  _resident(gsel.shape), _resident(gwin.shape), _resident(gcmp.shape),
                  _resident(ovt.shape)],
        out_specs=pl.BlockSpec((1, TILES_PER_STEP * Q_TILE, C_WIDTH), lambda bi, c: (bi, c, 0)),
        out_shape=jax.ShapeDtypeStruct((b, s, C_WIDTH), BF16),
        compiler_params=_cparams(2),
        name="nsa",
    )(qt, kc, vct, ks, vst, kw, vwt, gates, gsel, gwin, gcmp, ovt)


def _tail_kernel(h_ref, ya_ref, yb_ref, yc_ref, p_ref, norm_ref, woa_ref, wob_ref, woc_ref,
                 wg_ref, wu_ref, wd_ref, wpg_ref, wpp_ref, o_ref, acc_ref):
    subs = _sub_tiles(h_ref.shape[0])
    mix = [_dot(ya_ref[r, :], woa_ref[...]) + _dot(yb_ref[r, :], wob_ref[...]) + _dot(yc_ref[r, :], woc_ref[...])
           for r in subs]
    emb = [_dot(p_ref[r, :].astype(BF16), wpp_ref[...]) for r in subs]
    h = [h_ref[r, :] + _rms(m, norm_ref[3:4, :]) for r, m in zip(subs, mix)]
    h = _ffn_body(h, norm_ref[4:5, :], norm_ref[5:6, :], wg_ref, wu_ref, wd_ref, acc_ref)
    gate = [_dot(_rms(x, norm_ref[6:7, :]).astype(BF16), wpg_ref[...]) for x in h]
    for r, x, g, e in zip(subs, h, gate, emb):
        o_ref[r, :] = x + _rms(_sigmoid(g) * e, norm_ref[7:8, :])


def _tail(h, ya, yb, yc, p, norms, ffn_w, layer, w_out, w_gate, w_proj):
    n, d = h.shape
    dp = p.shape[-1]
    w = w_out.astype(BF16)
    woa, wob, woc = w[:A_WIDTH], w[A_WIDTH:A_WIDTH + B_WIDTH], w[A_WIDTH + B_WIDTH:]
    tile = lambda width: pl.BlockSpec((TOKEN_TILE, width), lambda i: (i, 0))
    return pl.pallas_call(
        _tail_kernel,
        grid=(n // TOKEN_TILE,),
        in_specs=[tile(d), tile(A_WIDTH), tile(B_WIDTH), tile(C_WIDTH),
                  pl.BlockSpec((None, TOKEN_TILE, dp), lambda i: (layer, i, 0)),
                  _resident(norms.shape), _resident(woa.shape), _resident(wob.shape), _resident(woc.shape),
                  *[_ffn_weight_spec(wt, layer, 1) for wt in ffn_w],
                  _resident((d, d)), _resident((dp, d))],
        out_specs=tile(d),
        out_shape=jax.ShapeDtypeStruct((n, d), F32),
        scratch_shapes=[pltpu.VMEM((TOKEN_TILE, d), F32)],
        compiler_params=_cparams(1),
        name="tail",
    )(h, ya, yb, yc, p, norms, woa, wob, woc, *ffn_w, w_gate.astype(BF16), w_proj.astype(BF16))


def _overlap_t(s):
    ncp = s // CMP_STRIDE
    n_cmp = (s - CMP_LEN) // CMP_STRIDE + 1
    cs = jnp.arange(ncp) * CMP_STRIDE
    ss = jnp.arange(s // SEL_LEN) * SEL_LEN
    ov = jnp.clip(jnp.minimum(cs[None] + CMP_LEN, ss[:, None] + SEL_LEN)
                  - jnp.maximum(cs[None], ss[:, None]), 0, None).astype(F32) / CMP_LEN
    ov = jnp.where(jnp.arange(ncp)[None] < n_cmp, ov, 0.0).astype(BF16)
    return jnp.pad(ov, ((0, SEL_SLOTS - s // SEL_LEN), (0, 0)))


def kernel(x, p, rel_bias, norm_g, ffn_w_gate, ffn_w_up, ffn_w_down, w_in, w_out, sgu_norm_g, sgu_w, sgu_b,
           conv_w, conv_b, lru_wa, lru_ba, lru_wx, lru_bx, lru_lambda, cmp_pos, cmp_w1, cmp_b1, cmp_w2,
           cmp_b2, ple_w_gate, ple_w_proj):
    b, s, d = x.shape
    depth = norm_g.shape[0]
    assert s % TOKEN_TILE == 0 and s % SEL_KEYS == 0
    assert s % (TILES_PER_STEP * Q_TILE) == 0
    assert s >= WINDOW + Q_TILE and SEL_TOP <= s // SEL_LEN <= SEL_SLOTS
    n_tiles = s // Q_TILE

    rbx = jnp.repeat(rel_bias.reshape(N_BUCKETS, C_KV_GROUPS, C_HPG).transpose(1, 0, 2), Q_TILE, axis=2)
    no_limit = 1 << 30
    gsel = _bias_table(rbx, SEL_NEAR_SPAN + Q_TILE + SEL_KEYS, 1, SEL_NEAR_SPAN, no_limit)
    gwin = _bias_table(rbx, 2 * WINDOW + Q_TILE, 1, WINDOW, WINDOW)
    per_tile = Q_TILE // CMP_STRIDE
    gcmp = _bias_table(rbx, per_tile * (n_tiles - 1) + s // CMP_STRIDE, CMP_STRIDE,
                       CMP_STRIDE * per_tile * (n_tiles - 1) - (CMP_LEN - 1), no_limit)
    ovt = _overlap_t(s)

    ffn_w = (ffn_w_gate.astype(BF16), ffn_w_up.astype(BF16), ffn_w_down.astype(BF16))
    h = x
    flat = lambda a: a.reshape(b * s, -1)
    for i in range(depth):
        h, ya, yb, qt, raw, ks, kw, vst, vwt, gates = _head(
            h, norm_g[i], ffn_w, i, w_in[i], sgu_norm_g[i], sgu_w[i], sgu_b[i],
            (conv_w[i], conv_b[i], lru_wa[i], lru_ba[i], lru_wx[i], lru_bx[i], lru_lambda[i]))
        kc, kct = _compress(raw, cmp_pos[i], cmp_w1[i], cmp_b1[i], cmp_w2[i], cmp_b2[i])
        yc = _nsa(qt, kc[0], kct[1], ks, vst, kw, vwt, gates, gsel, gwin, gcmp, ovt)
        h = _tail(flat(h), flat(ya), flat(yb), flat(yc), p.reshape(depth, b * s, -1), norm_g[i], ffn_w, i,
                  w_out[i], ple_w_gate[i], ple_w_proj[i]).reshape(b, s, d)
    return h
```

```python
import functools
import math

import jax
import jax.numpy as jnp
from jax import lax
from jax.experimental import pallas as pl
from jax.experimental.pallas import tpu as pltpu

F32 = jnp.float32
BF16 = jnp.bfloat16

RMS_EPS = 1e-6
A_GROUPS = 4
A_WIDTH = 256
A_CHUNK = 128
B_WIDTH = 256
CONV_W = 4
LRU_C = 8.0
C_HEADS = 8
C_KV_GROUPS = 2
C_HPG = C_HEADS // C_KV_GROUPS
HEAD_DIM = 64
C_WIDTH = C_HEADS * HEAD_DIM
KV_W = C_KV_GROUPS * HEAD_DIM
CMP_LEN = 32
CMP_STRIDE = 16
SEL_LEN = 64
SEL_SHIFT = 6
SEL_SLOTS = 64
SEL_TOP = 16
WINDOW = 512
FORCE_SCORE = 1e4
NEG = -1e30
N_BUCKETS = 32
MAX_DISTANCE = 1024

LANES = 128
TOKEN_TILE = 512
SUB_TILES = 2
FFN_CHUNK = 256
SUBLANES = 8
Q_TILE = 128
TILES_PER_STEP = 4
SEL_KEYS = 512
SEL_HALF = 256
SEL_AHEAD = 2
SEL_NEAR_SPAN = -(-(TILES_PER_STEP * Q_TILE + MAX_DISTANCE + SEL_KEYS - 2) // Q_TILE) * Q_TILE
WIN_KEYS = 128
QL = C_HPG * Q_TILE
V_PAD_ROWS = 16
V_ROWS = HEAD_DIM + V_PAD_ROWS
LOG2E = math.log2(math.e)
TABLE_ROWS = 128
VMEM_LIMIT = 56 * 1024 * 1024


def _cparams(n_axes):
    return pltpu.CompilerParams(dimension_semantics=("arbitrary",) * n_axes,
                                vmem_limit_bytes=VMEM_LIMIT)


def _resident(shape):
    nd = len(shape)
    return pl.BlockSpec(shape, lambda *_: (0,) * nd, pipeline_mode=pl.Buffered(1))


def _rms(x, g):
    return x * lax.rsqrt(jnp.mean(x * x, axis=-1, keepdims=True) + RMS_EPS) * g


def _sigmoid(x):
    return 1.0 / (1.0 + jnp.exp(-x))


def _dot(a, b):
    return jnp.dot(a, b, preferred_element_type=F32)


def _bias_table_kernel(rbx_ref, o_ref, *, stride, offset, dmax):
    i = pl.program_id(1)
    shape = (TABLE_ROWS, QL)
    d_hi = (Q_TILE - 1) - stride * (i * TABLE_ROWS) + offset
    d_lo = -stride * (i * TABLE_ROWS + TABLE_ROWS - 1) + offset
    masked = (d_hi < 0) | (d_lo >= dmax)
    far = (d_lo >= MAX_DISTANCE) & (d_hi < dmax)

    @pl.when(masked)
    def _():
        o_ref[0] = jnp.full(shape, NEG, F32)

    @pl.when(far)
    def _():
        o_ref[0] = jnp.broadcast_to(rbx_ref[0, N_BUCKETS - 1:N_BUCKETS, :] * LOG2E, shape)

    @pl.when(jnp.logical_not(masked | far))
    def _():
        x = lax.broadcasted_iota(jnp.int32, shape, 0) + i * TABLE_ROWS
        t = lax.broadcasted_iota(jnp.int32, shape, 1) & (Q_TILE - 1)
        d = t - stride * x + offset
        n = jnp.maximum(d, 0)
        max_exact = N_BUCKETS // 2
        nf = jnp.maximum(n, max_exact).astype(F32)
        large = max_exact + (jnp.log(nf / max_exact) / math.log(MAX_DISTANCE / max_exact)
                             * (N_BUCKETS - max_exact)).astype(jnp.int32)
        large = jnp.minimum(large, N_BUCKETS - 1)
        bucket = jnp.where(n < max_exact, n, large)
        acc = jnp.zeros(shape, F32)
        for k in range(N_BUCKETS):
            acc = jnp.where(bucket == k, rbx_ref[0, k:k + 1, :], acc)
        o_ref[0] = jnp.where((d >= 0) & (d < dmax), acc * LOG2E, NEG)


def _bias_table(rbx, rows, stride, offset, dmax):
    rows_p = -(-rows // TABLE_ROWS) * TABLE_ROWS
    return pl.pallas_call(
        functools.partial(_bias_table_kernel, stride=stride, offset=offset, dmax=dmax),
        grid=(C_KV_GROUPS, rows_p // TABLE_ROWS),
        in_specs=[pl.BlockSpec((1, N_BUCKETS, QL), lambda g, i: (g, 0, 0))],
        out_specs=pl.BlockSpec((1, TABLE_ROWS, QL), lambda g, i: (g, i, 0)),
        out_shape=jax.ShapeDtypeStruct((C_KV_GROUPS, rows_p, QL), F32),
        compiler_params=_cparams(2),
        name="bias_table",
    )(rbx)


def _sub_tiles(n_rows):
    rows = n_rows // SUB_TILES
    return [slice(k * rows, (k + 1) * rows) for k in range(SUB_TILES)]


def _ffn_body(xs, g_pre, g_post, wg_ref, wu_ref, wd_ref, acc_ref):
    subs = _sub_tiles(acc_ref.shape[0])
    xn = [_rms(x, g_pre).astype(BF16) for x in xs]
    nch = wg_ref.shape[1] // FFN_CHUNK
    cols = lambda j: slice(j * FFN_CHUNK, (j + 1) * FFN_CHUNK)
    gate_up = [(_dot(x, wg_ref[:, cols(0)]), _dot(x, wu_ref[:, cols(0)])) for x in xn]
    for j in range(nch):
        if j + 1 < nch:
            nxt = [(_dot(x, wg_ref[:, cols(j + 1)]), _dot(x, wu_ref[:, cols(j + 1)])) for x in xn]
        for rows, (gate, up) in zip(subs, gate_up):
            hid = (gate * _sigmoid(gate) * up).astype(BF16)
            down = _dot(hid, wd_ref[cols(j), :])
            if j == 0:
                acc_ref[rows, :] = down
            else:
                acc_ref[rows, :] += down
        gate_up = nxt
    return [x + 0.5 * _rms(acc_ref[rows, :], g_post) for x, rows in zip(xs, subs)]


def _ffn_weight_spec(w, layer, which):
    return pl.BlockSpec((None, None) + w.shape[2:], lambda *_: (layer, which, 0, 0), pipeline_mode=pl.Buffered(1))


def _head_kernel(h_ref, norm_ref, wg_ref, wu_ref, wd_ref, wa_ref, wb_ref, wq_ref, wkv_ref, wgt_ref,
                 sgn_ref, sgw_ref, sgb_ref, cw_ref, cb_ref, lwa_ref, lba_ref, lwx_ref, lbx_ref, lam_ref,
                 h_out_ref, ya_ref, yb_ref, qt_ref, raw_ref, ks_ref, kw_ref, vst_ref, vwt_ref, gt_ref,
                 acc_ref, tail_ref, state_ref, a_ref, b_ref):
    @pl.when(pl.program_id(1) == 0)
    def _():
        tail_ref[...] = jnp.zeros_like(tail_ref)
        state_ref[...] = jnp.zeros_like(state_ref)

    h = _ffn_body([h_ref[0, r, :] for r in _sub_tiles(TOKEN_TILE)], norm_ref[0:1, :], norm_ref[1:2, :],
                  wg_ref, wu_ref, wd_ref, acc_ref)
    h = jnp.concatenate(h, axis=0)
    h_out_ref[0] = h
    xn = _rms(h, norm_ref[2:3, :]).astype(BF16)

    zb = _dot(xn, wb_ref[...])
    za = _dot(xn, wa_ref[...])
    zq = _dot(xn, wq_ref[...])
    zkv = _dot(xn, wkv_ref[...])
    zg = _dot(xn, wgt_ref[...])

    gate_b = _rglru_coefficients(zb, cw_ref, cb_ref, lwa_ref, lba_ref, lwx_ref, lbx_ref, lam_ref,
                                 tail_ref, state_ref, a_ref, b_ref)

    u = jax.nn.gelu(za[:, :A_WIDTH])
    v = _rms(jax.nn.gelu(za[:, A_WIDTH:]), sgn_ref[...]).astype(BF16)
    row = lax.broadcasted_iota(jnp.int32, (A_CHUNK, A_CHUNK), 0)
    col = lax.broadcasted_iota(jnp.int32, (A_CHUNK, A_CHUNK), 1)
    lane_group = lax.shift_right_logical(lax.broadcasted_iota(jnp.int32, (A_CHUNK, A_WIDTH), 1),
                                          (A_WIDTH // A_GROUPS).bit_length() - 1)
    w_tril = [jnp.where(row >= col, sgw_ref[g], 0.0).astype(BF16) for g in range(A_GROUPS)]
    for c in range(TOKEN_TILE // A_CHUNK):
        rows = slice(c * A_CHUNK, (c + 1) * A_CHUNK)
        mixed = jnp.zeros((A_CHUNK, A_WIDTH), F32)
        for g in range(A_GROUPS):
            mixed = jnp.where(lane_group == g, _dot(w_tril[g], v[rows]), mixed)
        ya_ref[0, rows, :] = (u[rows] * (mixed + sgb_ref[...])).astype(ya_ref.dtype)

    zq_t = (zq * (HEAD_DIM ** -0.5 * LOG2E)).T
    for g in range(C_KV_GROUPS):
        for c in range(TOKEN_TILE // Q_TILE):
            parts = []
            for r in range(C_HPG):
                base = (g * C_HPG + r) * HEAD_DIM
                parts.append(zq_t[base:base + HEAD_DIM, c * Q_TILE:(c + 1) * Q_TILE])
            qt_ref[0, g, c] = jnp.concatenate(parts, axis=1).astype(qt_ref.dtype)

    vs_t = zkv[:, 3 * KV_W:4 * KV_W].T
    vw_t = zkv[:, 5 * KV_W:6 * KV_W].T
    key_blk = lax.shift_right_logical(
        lax.broadcasted_iota(jnp.int32, (TOKEN_TILE, SEL_SLOTS), 0) + pl.program_id(1) * TOKEN_TILE, SEL_SHIFT)
    blk_onehot = jnp.where(key_blk == lax.broadcasted_iota(jnp.int32, (TOKEN_TILE, SEL_SLOTS), 1), 1.0, 0.0)
    ones_rows = jnp.where(lax.broadcasted_iota(jnp.int32, (V_PAD_ROWS, SEL_KEYS), 0) == 0, 1.0, 0.0)
    raw_ref[0, 0] = zkv[:, :KV_W]
    raw_ref[1, 0] = zkv[:, KV_W:2 * KV_W]
    for g in range(C_KV_GROUPS):
        lo, hi = g * HEAD_DIM, (g + 1) * HEAD_DIM
        ks_ref[0, g] = jnp.concatenate([zkv[:, 2 * KV_W + lo:2 * KV_W + hi], blk_onehot],
                                       axis=1).astype(ks_ref.dtype)
        kw_ref[0, g] = zkv[:, 4 * KV_W + lo:4 * KV_W + hi].astype(kw_ref.dtype)
        for c in range(TOKEN_TILE // SEL_KEYS):
            vst_ref[0, g, c] = jnp.concatenate(
                [vs_t[lo:hi, c * SEL_KEYS:(c + 1) * SEL_KEYS], ones_rows[:, :SEL_KEYS]], axis=0).astype(vst_ref.dtype)
        for c in range(TOKEN_TILE // WIN_KEYS):
            vwt_ref[0, g, c] = jnp.concatenate(
                [vw_t[lo:hi, c * WIN_KEYS:(c + 1) * WIN_KEYS], ones_rows[:, :WIN_KEYS]], axis=0).astype(vwt_ref.dtype)

    sg_t = _sigmoid(zg).T
    for br in range(3):
        for g in range(C_KV_GROUPS):
            base = br * C_HEADS + g * C_HPG
            gt_ref[0, br, g] = sg_t[base:base + C_HPG, :]

    yb_ref[0] = (_rglru_scan(state_ref, a_ref, b_ref) * jax.nn.gelu(gate_b)).astype(yb_ref.dtype)


def _head(h, norms, ffn_w, layer, w_in, sgu_norm_g, sgu_w, sgu_b, lru):
    b, s, d = h.shape
    assert ffn_w[0].shape[-1] % FFN_CHUNK == 0
    wb16 = w_in.astype(BF16)
    o = 0
    wa = wb16[:, o:o + 2 * A_WIDTH]; o += 2 * A_WIDTH
    wb = wb16[:, o:o + 2 * B_WIDTH]; o += 2 * B_WIDTH
    wq = wb16[:, o:o + C_WIDTH]; o += C_WIDTH
    wkv = wb16[:, o:o + 6 * KV_W]; o += 6 * KV_W
    wgt = jnp.pad(wb16[:, o:o + 3 * C_HEADS], ((0, 0), (0, LANES - 3 * C_HEADS)))
    sgb = jnp.repeat(sgu_b.T, A_WIDTH // A_GROUPS, axis=1)
    nt = s // TOKEN_TILE
    grid = (b, nt)
    G = C_KV_GROUPS
    out_shape = [
        jax.ShapeDtypeStruct((b, s, d), F32),
        jax.ShapeDtypeStruct((b, s, A_WIDTH), BF16),
        jax.ShapeDtypeStruct((b, s, B_WIDTH), BF16),
        jax.ShapeDtypeStruct((b, G, s // Q_TILE, HEAD_DIM, QL), BF16),
        jax.ShapeDtypeStruct((2, b, s, KV_W), F32),
        jax.ShapeDtypeStruct((b, G, s, HEAD_DIM + SEL_SLOTS), BF16),
        jax.ShapeDtypeStruct((b, G, s, HEAD_DIM), BF16),
        jax.ShapeDtypeStruct((b, G, s // SEL_KEYS, V_ROWS, SEL_KEYS), BF16),
        jax.ShapeDtypeStruct((b, G, s // WIN_KEYS, V_ROWS, WIN_KEYS), BF16),
        jax.ShapeDtypeStruct((b, 3, G, C_HPG, s), F32),
    ]
    out_specs = [
        pl.BlockSpec((1, TOKEN_TILE, d), lambda bi, i: (bi, i, 0)),
        pl.BlockSpec((1, TOKEN_TILE, A_WIDTH), lambda bi, i: (bi, i, 0)),
        pl.BlockSpec((1, TOKEN_TILE, B_WIDTH), lambda bi, i: (bi, i, 0)),
        pl.BlockSpec((1, G, TOKEN_TILE // Q_TILE, HEAD_DIM, QL), lambda bi, i: (bi, 0, i, 0, 0)),
        pl.BlockSpec((2, 1, TOKEN_TILE, KV_W), lambda bi, i: (0, bi, i, 0)),
        pl.BlockSpec((1, G, TOKEN_TILE, HEAD_DIM + SEL_SLOTS), lambda bi, i: (bi, 0, i, 0)),
        pl.BlockSpec((1, G, TOKEN_TILE, HEAD_DIM), lambda bi, i: (bi, 0, i, 0)),
        pl.BlockSpec((1, G, TOKEN_TILE // SEL_KEYS, V_ROWS, SEL_KEYS), lambda bi, i: (bi, 0, i, 0, 0)),
        pl.BlockSpec((1, G, TOKEN_TILE // WIN_KEYS, V_ROWS, WIN_KEYS), lambda bi, i: (bi, 0, i, 0, 0)),
        pl.BlockSpec((1, 3, G, C_HPG, TOKEN_TILE), lambda bi, i: (bi, 0, 0, 0, i)),
    ]
    in_specs = [
        pl.BlockSpec((1, TOKEN_TILE, d), lambda bi, i: (bi, i, 0)),
        _resident(norms.shape), *[_ffn_weight_spec(w, layer, 0) for w in ffn_w],
        _resident(wa.shape), _resident(wb.shape), _resident(wq.shape),
        _resident(wkv.shape), _resident(wgt.shape), _resident((1, A_WIDTH)),
        _resident(sgu_w.shape), _resident(sgb.shape),
        _resident((CONV_W, B_WIDTH)), _resident((1, B_WIDTH)), _resident((B_WIDTH, B_WIDTH)),
        _resident((1, B_WIDTH)), _resident((B_WIDTH, B_WIDTH)), _resident((1, B_WIDTH)), _resident((1, B_WIDTH)),
    ]
    conv_w, conv_b, lru_wa, lru_ba, lru_wx, lru_bx, lru_lambda = lru
    row = lambda v: v.reshape(1, B_WIDTH)
    return pl.pallas_call(
        _head_kernel, grid=grid, in_specs=in_specs, out_specs=out_specs, out_shape=out_shape,
        scratch_shapes=[pltpu.VMEM((TOKEN_TILE, d), F32),
                        pltpu.VMEM((SUBLANES, B_WIDTH), F32), pltpu.VMEM((SUBLANES, B_WIDTH), F32),
                        pltpu.VMEM((TOKEN_TILE, B_WIDTH), F32), pltpu.VMEM((TOKEN_TILE, B_WIDTH), F32)],
        compiler_params=_cparams(2), name="head",
    )(h, norms, *ffn_w, wa, wb, wq, wkv, wgt, sgu_norm_g.reshape(1, A_WIDTH), sgu_w, sgb,
      conv_w, row(conv_b), _block_diag(lru_wa).astype(BF16), row(lru_ba),
      _block_diag(lru_wx).astype(BF16), row(lru_bx), row(lru_lambda))


def _rglru_coefficients(zb, cw_ref, cb_ref, wa_ref, ba_ref, wx_ref, bxb_ref, lam_ref,
                        tail_ref, state_ref, a_ref, b_ref):
    xb = zb[:, :B_WIDTH]
    ext = jnp.concatenate([tail_ref[...], xb], axis=0)
    xc = cb_ref[...] + xb * cw_ref[CONV_W - 1:CONV_W, :]
    for k in range(CONV_W - 1):
        shift = CONV_W - 1 - k
        xc = xc + ext[SUBLANES - shift:SUBLANES - shift + TOKEN_TILE] * cw_ref[k:k + 1, :]
    tail_ref[...] = xb[TOKEN_TILE - SUBLANES:]

    xcb = xc.astype(BF16)
    r = _sigmoid(_dot(xcb, wa_ref[...]) + ba_ref[...])
    i = _sigmoid(_dot(xcb, wx_ref[...]) + bxb_ref[...])
    z = -lam_ref[...]
    e = jnp.exp(-jnp.abs(z))
    softplus = jnp.maximum(z, 0.0) + jnp.log1p(e)
    log_a = -LRU_C * r * softplus
    a = jnp.exp(log_a)
    b = jnp.sqrt(jnp.tanh(-log_a) * (a * a + 1.0)) * (i * xc)

    row = lax.broadcasted_iota(jnp.int32, a.shape, 0) & (SUBLANES - 1)
    for dist in (1, 2, 4):
        a_prev = jnp.where(row >= dist, pltpu.roll(a, dist, 0), 1.0)
        b_prev = jnp.where(row >= dist, pltpu.roll(b, dist, 0), 0.0)
        b = a * b_prev + b
        a = a * a_prev
    a_ref[...] = a
    b_ref[...] = b
    return zb[:, B_WIDTH:]


def _rglru_scan(state_ref, a_ref, b_ref):
    def body(k, h):
        off = pl.multiple_of(k * SUBLANES, SUBLANES)
        rows = pl.ds(off, SUBLANES)
        hs = b_ref[rows, :] + a_ref[rows, :] * h
        b_ref[rows, :] = hs
        return jnp.broadcast_to(hs[SUBLANES - 1:SUBLANES, :], hs.shape)

    state_ref[...] = lax.fori_loop(0, TOKEN_TILE // SUBLANES, body, state_ref[...])
    return b_ref[...]


def _block_diag(w):
    g, n, _ = w.shape
    out = jnp.zeros((g * n, g * n), w.dtype)
    for k in range(g):
        out = out.at[k * n:(k + 1) * n, k * n:(k + 1) * n].set(w[k])
    return out


def _compress_kernel(raw_ref, pos_ref, w1_ref, b1_ref, w2_ref, b2_ref, w2t_ref, b2c_ref, kc_ref, kct_ref):
    ncp = raw_ref.shape[2] // CMP_STRIDE
    hid_dim = w1_ref.shape[-1] // C_KV_GROUPS
    u = jnp.zeros((ncp, w1_ref.shape[-1]), F32)
    v = jnp.zeros((ncp, w1_ref.shape[-1]), F32)
    for l in range(CMP_STRIDE):
        x = raw_ref[0, 0, pl.ds(l, ncp, stride=CMP_STRIDE), :]
        top = (x + pos_ref[0, l:l + 1, :]).astype(BF16)
        bot = (x + pos_ref[0, CMP_STRIDE + l:CMP_STRIDE + l + 1, :]).astype(BF16)
        u = u + _dot(top, w1_ref[0, l])
        v = v + _dot(bot, w1_ref[0, CMP_STRIDE + l])
    pre = u + pltpu.roll(v, ncp - 1, 0)
    for g in range(C_KV_GROUPS):
        hid = jax.nn.gelu(pre[:, g * hid_dim:(g + 1) * hid_dim] + b1_ref[0]).astype(BF16)
        kc_ref[0, 0, g] = (_dot(hid, w2_ref[0]) + b2_ref[0]).astype(kc_ref.dtype)
        t = lax.dot_general(w2t_ref[0], hid, (((1,), (1,)), ((), ())), preferred_element_type=F32)
        kct_ref[0, 0, g] = (t + b2c_ref[0]).astype(kct_ref.dtype)


def _compress(raw, cmp_pos, cmp_w1, cmp_b1, cmp_w2, cmp_b2):
    _, b, s, _ = raw.shape
    g, hd = C_KV_GROUPS, HEAD_DIM
    ncp = s // CMP_STRIDE
    hid = cmp_w1.shape[-1]
    w1 = cmp_w1.astype(BF16).reshape(2, CMP_LEN, hd, hid)
    zero = jnp.zeros_like(w1)
    w1g = jnp.concatenate([jnp.concatenate([w1 if k == gi else zero for k in range(g)], axis=2)
                           for gi in range(g)], axis=3)
    pos = jnp.tile(cmp_pos, (1, 1, g))
    sel = lambda *shape: pl.BlockSpec((1,) + shape, lambda kv, bi: (kv,) + (0,) * len(shape))
    return pl.pallas_call(
        _compress_kernel,
        grid=(2, b),
        in_specs=[pl.BlockSpec((1, 1, s, g * hd), lambda kv, bi: (kv, bi, 0, 0)),
                  sel(CMP_LEN, g * hd), sel(CMP_LEN, g * hd, g * hid), sel(1, hid), sel(hid, hd), sel(1, hd),
                  sel(hd, hid), sel(hd, 1)],
        out_specs=[pl.BlockSpec((1, 1, g, ncp, hd), lambda kv, bi: (kv, bi, 0, 0, 0)),
                   pl.BlockSpec((1, 1, g, hd, ncp), lambda kv, bi: (kv, bi, 0, 0, 0))],
        out_shape=[jax.ShapeDtypeStruct((2, b, g, ncp, hd), BF16),
                   jax.ShapeDtypeStruct((2, b, g, hd, ncp), BF16)],
        compiler_params=_cparams(2),
        name="compress",
    )(raw, pos, w1g, cmp_b1.reshape(2, 1, hid), cmp_w2.astype(BF16), cmp_b2.reshape(2, 1, hd),
      cmp_w2.astype(BF16).transpose(0, 2, 1), cmp_b2.reshape(2, hd, 1))


def _softmax_step(carry, s, v_t):
    m, acc = carry
    m_new = jnp.maximum(m, jnp.max(s, axis=0, keepdims=True))
    p = jnp.exp2(s - m_new).astype(BF16)
    acc = acc * jnp.exp2(m - m_new) + _dot(v_t, p)
    return m_new, acc


def _normalized(acc):
    return acc[:HEAD_DIM] / acc[HEAD_DIM:HEAD_DIM + 1]


def _nsa_kernel(qt_ref, kc_ref, vct_ref, ks_ref, vst_ref, kw_ref, vwt_ref, gate_ref,
                gsel_ref, gwin_ref, gcmp_ref, ovt_ref, o_ref, *, n_tiles):
    step = pl.program_id(1)
    ncp = kc_ref.shape[2]
    chains = [(t, g) for t in range(TILES_PER_STEP) for g in range(C_KV_GROUPS)]
    tile = [step * TILES_PER_STEP + t for t in range(TILES_PER_STEP)]
    qts = [qt_ref[0, g, t] for t, g in chains]

    back = WINDOW // WIN_KEYS
    n_win = WINDOW + Q_TILE
    first = [jnp.maximum(c - back, 0) for c in tile]
    s_win = []
    for k, (t, g) in enumerate(chains):
        start = pl.multiple_of(first[t] * WIN_KEYS, WIN_KEYS)
        rel = pl.multiple_of(jnp.maximum(back - tile[t], 0) * WIN_KEYS, WIN_KEYS)
        s_win.append(_dot(kw_ref[0, g, pl.ds(start, n_win), :], qts[k]) + gwin_ref[g, pl.ds(rel, n_win), :])

    tq = lax.broadcasted_iota(jnp.int32, (1, QL), 1) & (Q_TILE - 1)

    def cmp_branch(rows):
        s_cmp = []
        for k, (t, g) in enumerate(chains):
            y0 = pl.multiple_of((n_tiles - 1 - tile[t]) * (Q_TILE // CMP_STRIDE), SUBLANES)
            s_cmp.append(_dot(kc_ref[0, g, :rows, :], qts[k]) + gcmp_ref[g, pl.ds(y0, rows), :])
        outs, imps = [], []
        for k, (t, g) in enumerate(chains):
            e = jnp.exp2(s_cmp[k] - jnp.max(s_cmp[k], axis=0, keepdims=True))
            has_cmp = (tile[t] * Q_TILE + tq >= CMP_LEN - 1).astype(F32)
            p = e / jnp.sum(e, axis=0, keepdims=True) * has_cmp
            outs.append(_dot(vct_ref[0, g][:, :rows], p.astype(BF16)))
            p_heads = p[:, 0:Q_TILE]
            for r in range(1, C_HPG):
                p_heads = p_heads + p[:, r * Q_TILE:(r + 1) * Q_TILE]
            p_hi = p_heads.astype(BF16)
            p_lo = (p_heads - p_hi.astype(F32)).astype(BF16)
            imps.append(_dot(ovt_ref[:, :rows], p_hi) + _dot(ovt_ref[:, :rows], p_lo))
        return tuple(outs), tuple(imps)

    reachable = (tile[-1] + 1) * (Q_TILE // CMP_STRIDE)
    o_cmp, imp = lax.cond(reachable <= ncp // 2, lambda: cmp_branch(ncp // 2), lambda: cmp_branch(ncp))

    o_win = []
    for k, (t, g) in enumerate(chains):
        e = jnp.exp2(s_win[k] - jnp.max(s_win[k], axis=0, keepdims=True)).astype(BF16)
        acc = jnp.zeros((V_ROWS, QL), F32)
        for i in range(n_win // WIN_KEYS):
            acc = acc + _dot(vwt_ref[0, g, first[t] + i], e[i * WIN_KEYS:(i + 1) * WIN_KEYS])
        o_win.append(_normalized(acc))

    q_aug = _with_mask_rows(qts, [tile[t] for t, g in chains], imp)
    halves = range(SEL_KEYS // SEL_HALF)
    init = (jnp.full((1, QL), NEG, F32), jnp.zeros((V_ROWS, QL), F32))

    def sel_body(i, carry, far):
        off = pl.multiple_of(i * SEL_KEYS, SEL_KEYS)
        units = [(h, k) for h in halves for k in range(len(chains))]

        def logits_of(h, k):
            t, g = chains[k]
            logits = _dot(ks_ref[0, g, pl.ds(off + h * SEL_HALF, SEL_HALF), :], q_aug[k])
            if not far:
                x0 = SEL_NEAR_SPAN - tile[t] * Q_TILE
                logits = logits + gsel_ref[g, pl.ds(pl.multiple_of(x0 + off + h * SEL_HALF, Q_TILE), SEL_HALF), :]
            return logits

        carry = list(carry)
        s = [logits_of(*u) for u in units[:SEL_AHEAD]]
        for n, (h, k) in enumerate(units):
            if n + SEL_AHEAD < len(units):
                s.append(logits_of(*units[n + SEL_AHEAD]))
            t, g = chains[k]
            v_t = vst_ref[0, g, i][:, h * SEL_HALF:(h + 1) * SEL_HALF]
            carry[k] = _softmax_step(carry[k], s[n], v_t)
        return tuple(carry)

    n_far = jnp.maximum(tile[0] * Q_TILE - MAX_DISTANCE + 1, 0) // SEL_KEYS
    n_sel_steps = (tile[-1] * Q_TILE + Q_TILE + SEL_KEYS - 1) // SEL_KEYS
    sel = lax.fori_loop(0, n_far, functools.partial(sel_body, far=True), (init,) * len(chains))
    sel = tuple((m + gsel_ref[g, 0:1, :], acc) for (m, acc), (t, g) in zip(sel, chains))
    sel = lax.fori_loop(n_far, n_sel_steps, functools.partial(sel_body, far=False), sel)

    for t in range(TILES_PER_STEP):
        outs = []
        for k, (tk, g) in enumerate(chains):
            if tk != t:
                continue
            o_sel = _normalized(sel[k][1])
            tok = slice(t * Q_TILE, (t + 1) * Q_TILE)
            cols = []
            for r in range(C_HPG):
                ln = slice(r * Q_TILE, (r + 1) * Q_TILE)
                cols.append(gate_ref[0, 0, g, r:r + 1, tok] * o_cmp[k][:, ln]
                            + gate_ref[0, 1, g, r:r + 1, tok] * o_sel[:, ln]
                            + gate_ref[0, 2, g, r:r + 1, tok] * o_win[k][:, ln])
            outs += [jnp.concatenate(cols[2 * j:2 * j + 2], axis=0).T for j in range(C_HPG // 2)]
        o_ref[0, t * Q_TILE:(t + 1) * Q_TILE, :] = jnp.concatenate(outs, axis=1).astype(o_ref.dtype)


def _with_mask_rows(qts, tiles, imps):
    n_groups = SEL_SLOTS // SUBLANES
    j = lax.broadcasted_iota(jnp.int32, (SEL_SLOTS, Q_TILE), 0)
    half = lax.shift_right_logical(lax.broadcasted_iota(jnp.int32, (SEL_SLOTS, Q_TILE), 1), SEL_SHIFT)
    j_rows = j[:SUBLANES]
    scores = []
    for c, imp in zip(tiles, imps):
        blk = c * (Q_TILE // SEL_LEN) + half
        forced = (j == 0) | (j == blk) | (j == blk - 1)
        scores.append(jnp.where(j <= blk, jnp.where(forced, FORCE_SCORE, imp), -1.0))
    rows = [[sc[k * SUBLANES:(k + 1) * SUBLANES] for k in range(n_groups)] for sc in scores]

    def add_pair(ranks, jg, k):
        for ci, sc in enumerate(scores):
            acc = ranks[ci][k]
            for jp in range(jg * SUBLANES, (jg + 1) * SUBLANES):
                other = sc[jp:jp + 1, :]
                if k > jg:
                    beats = jnp.where(other >= rows[ci][k], 1, 0)
                elif k < jg:
                    beats = jnp.where(other > rows[ci][k], 1, 0)
                else:
                    beats = jnp.where(j_rows > jp - k * SUBLANES, jnp.where(other >= rows[ci][k], 1, 0),
                                      jnp.where(other > rows[ci][k], 1, 0))
                acc = acc + beats
            ranks[ci][k] = acc

    last_blk = tiles[-1] * (Q_TILE // SEL_LEN) + Q_TILE // SEL_LEN - 1
    ranks = [[jnp.zeros((SUBLANES, Q_TILE), jnp.int32) for _ in range(n_groups)] for _ in scores]
    for m in range(n_groups):
        def shell(ranks, m=m):
            ranks = [list(r) for r in ranks]
            for k in range(m + 1):
                add_pair(ranks, m, k)
            for jg in range(m):
                add_pair(ranks, jg, m)
            return ranks
        ranks = shell(ranks) if m == 0 else lax.cond(m * SUBLANES <= last_blk, shell, lambda r: r, ranks)

    out = []
    for qt, sc, rk in zip(qts, scores, ranks):
        rank = jnp.concatenate(rk, axis=0)
        mask_rows = jnp.where((rank < SEL_TOP) & (sc >= 0.0), 0.0, NEG).astype(BF16)
        out.append(jnp.concatenate([qt, jnp.concatenate([mask_rows] * C_HPG, axis=1)], axis=0))
    return out


def _nsa(qt, kc, vct, ks, vst, kw, vwt, gates, gsel, gwin, gcmp, ovt):
    b, G, n_tiles = qt.shape[:3]
    s = n_tiles * Q_TILE
    per_b = lambda a: pl.BlockSpec((1,) + a.shape[1:], lambda bi, c: (bi,) + (0,) * (a.ndim - 1))
    return pl.pallas_call(
        functools.partial(_nsa_kernel, n_tiles=n_tiles),
        grid=(b, n_tiles // TILES_PER_STEP),
        in_specs=[pl.BlockSpec((1, G, TILES_PER_STEP, HEAD_DIM, QL), lambda bi, c: (bi, 0, c, 0, 0)),
                  per_b(kc), per_b(vct), per_b(ks), per_b(vst), per_b(kw), per_b(vwt),
                  pl.BlockSpec((1, 3, G, C_HPG, TILES_PER_STEP * Q_TILE), lambda bi, c: (bi, 0, 0, 0, c)),
                  _resident(gsel.shape), _resident(gwin.shape), _resident(gcmp.shape),
                  _resident(ovt.shape)],
        out_specs=pl.BlockSpec((1, TILES_PER_STEP * Q_TILE, C_WIDTH), lambda bi, c: (bi, c, 0)),
        out_shape=jax.ShapeDtypeStruct((b, s, C_WIDTH), BF16),
        compiler_params=_cparams(2),
        name="nsa",
    )(qt, kc, vct, ks, vst, kw, vwt, gates, gsel, gwin, gcmp, ovt)


def _tail_kernel(h_ref, ya_ref, yb_ref, yc_ref, p_ref, norm_ref, woa_ref, wob_ref, woc_ref,
                 wg_ref, wu_ref, wd_ref, wpg_ref, wpp_ref, o_ref, acc_ref):
    subs = _sub_tiles(h_ref.shape[0])
    mix = [_dot(ya_ref[r, :], woa_ref[...]) + _dot(yb_ref[r, :], wob_ref[...]) + _dot(yc_ref[r, :], woc_ref[...])
           for r in subs]
    emb = [_dot(p_ref[r, :].astype(BF16), wpp_ref[...]) for r in subs]
    h = [h_ref[r, :] + _rms(m, norm_ref[3:4, :]) for r, m in zip(subs, mix)]
    h = _ffn_body(h, norm_ref[4:5, :], norm_ref[5:6, :], wg_ref, wu_ref, wd_ref, acc_ref)
    gate = [_dot(_rms(x, norm_ref[6:7, :]).astype(BF16), wpg_ref[...]) for x in h]
    for r, x, g, e in zip(subs, h, gate, emb):
        o_ref[r, :] = x + _rms(_sigmoid(g) * e, norm_ref[7:8, :])


def _tail(h, ya, yb, yc, p, norms, ffn_w, layer, w_out, w_gate, w_proj):
    n, d = h.shape
    dp = p.shape[-1]
    w = w_out.astype(BF16)
    woa, wob, woc = w[:A_WIDTH], w[A_WIDTH:A_WIDTH + B_WIDTH], w[A_WIDTH + B_WIDTH:]
    tile = lambda width: pl.BlockSpec((TOKEN_TILE, width), lambda i: (i, 0))
    return pl.pallas_call(
        _tail_kernel,
        grid=(n // TOKEN_TILE,),
        in_specs=[tile(d), tile(A_WIDTH), tile(B_WIDTH), tile(C_WIDTH),
                  pl.BlockSpec((None, TOKEN_TILE, dp), lambda i: (layer, i, 0)),
                  _resident(norms.shape), _resident(woa.shape), _resident(wob.shape), _resident(woc.shape),
                  *[_ffn_weight_spec(wt, layer, 1) for wt in ffn_w],
                  _resident((d, d)), _resident((dp, d))],
        out_specs=tile(d),
        out_shape=jax.ShapeDtypeStruct((n, d), F32),
        scratch_shapes=[pltpu.VMEM((TOKEN_TILE, d), F32)],
        compiler_params=_cparams(1),
        name="tail",
    )(h, ya, yb, yc, p, norms, woa, wob, woc, *ffn_w, w_gate.astype(BF16), w_proj.astype(BF16))


def _overlap_t(s):
    ncp = s // CMP_STRIDE
    n_cmp = (s - CMP_LEN) // CMP_STRIDE + 1
    cs = jnp.arange(ncp) * CMP_STRIDE
    ss = jnp.arange(s // SEL_LEN) * SEL_LEN
    ov = jnp.clip(jnp.minimum(cs[None] + CMP_LEN, ss[:, None] + SEL_LEN)
                  - jnp.maximum(cs[None], ss[:, None]), 0, None).astype(F32) / CMP_LEN
    ov = jnp.where(jnp.arange(ncp)[None] < n_cmp, ov, 0.0).astype(BF16)
    return jnp.pad(ov, ((0, SEL_SLOTS - s // SEL_LEN), (0, 0)))


def kernel(x, p, rel_bias, norm_g, ffn_w_gate, ffn_w_up, ffn_w_down, w_in, w_out, sgu_norm_g, sgu_w, sgu_b,
           conv_w, conv_b, lru_wa, lru_ba, lru_wx, lru_bx, lru_lambda, cmp_pos, cmp_w1, cmp_b1, cmp_w2,
           cmp_b2, ple_w_gate, ple_w_proj):
    b, s, d = x.shape
    depth = norm_g.shape[0]
    assert s % TOKEN_TILE == 0 and s % SEL_KEYS == 0
    assert s % (TILES_PER_STEP * Q_TILE) == 0
    assert s >= WINDOW + Q_TILE and SEL_TOP <= s // SEL_LEN <= SEL_SLOTS
    n_tiles = s // Q_TILE

    rbx = jnp.repeat(rel_bias.reshape(N_BUCKETS, C_KV_GROUPS, C_HPG).transpose(1, 0, 2), Q_TILE, axis=2)
    no_limit = 1 << 30
    gsel = _bias_table(rbx, SEL_NEAR_SPAN + Q_TILE + SEL_KEYS, 1, SEL_NEAR_SPAN, no_limit)
    gwin = _bias_table(rbx, 2 * WINDOW + Q_TILE, 1, WINDOW, WINDOW)
    per_tile = Q_TILE // CMP_STRIDE
    gcmp = _bias_table(rbx, per_tile * (n_tiles - 1) + s // CMP_STRIDE, CMP_STRIDE,
                       CMP_STRIDE * per_tile * (n_tiles - 1) - (CMP_LEN - 1), no_limit)
    ovt = _overlap_t(s)

    ffn_w = (ffn_w_gate.astype(BF16), ffn_w_up.astype(BF16), ffn_w_down.astype(BF16))
    h = x
    flat = lambda a: a.reshape(b * s, -1)
    for i in range(depth):
        h, ya, yb, qt, raw, ks, kw, vst, vwt, gates = _head(
            h, norm_g[i], ffn_w, i, w_in[i], sgu_norm_g[i], sgu_w[i], sgu_b[i],
            (conv_w[i], conv_b[i], lru_wa[i], lru_ba[i], lru_wx[i], lru_bx[i], lru_lambda[i]))
        kc, kct = _compress(raw, cmp_pos[i], cmp_w1[i], cmp_b1[i], cmp_w2[i], cmp_b2[i])
        yc = _nsa(qt, kc[0], kct[1], ks, vst, kw, vwt, gates, gsel, gwin, gcmp, ovt)
        h = _tail(flat(h), flat(ya), flat(yb), flat(yc), p.reshape(depth, b * s, -1), norm_g[i], ffn_w, i,
                  w_out[i], ple_w_gate[i], ple_w_proj[i]).reshape(b, s, d)
    return h
```
